```python
import jax, jax.numpy as jnp
from jax import lax
import numpy as np

D_MODEL = 1024
BATCH = 4
SEQ = 4096
DEPTH = 1

D_MIX = D_MODEL
SB_HEADS = 8
SB_HEAD_DIM = 64
SB_WIDTH = SB_HEADS * SB_HEAD_DIM
GLA_HEADS = 4
GLA_DK = 64
GLA_DV = 128
GLA_KW = GLA_HEADS * GLA_DK
GLA_VW = GLA_HEADS * GLA_DV
GLA_GATE_RANK = 16
GLA_GATE_TAU = 16.0
GLA_CHUNK = 64
Q_BLOCK = 128
D_IN = 3 * SB_WIDTH + 2 * GLA_KW + 2 * GLA_VW + GLA_GATE_RANK
N_GROUPS = 4
EXPERTS_PER_GROUP = 8
N_EXPERTS = N_GROUPS * EXPERTS_PER_GROUP
TOP_K = 2
D_FF_EXPERT = 512
MOE_BLOCK = 128
RMS_EPS = 1e-6

kernel_name = "hymba_stickbreak_gla_hiermoe"


def rmsnorm(x, g):
    xf = x.astype(jnp.float32)
    inv = lax.rsqrt(jnp.mean(xf * xf, axis=-1, keepdims=True) + RMS_EPS)
    return (xf * inv).astype(x.dtype) * g


def stick_breaking_attention(q, k, v):
    B, H, S, Dh = q.shape
    nb = S // Q_BLOCK
    scale = Dh ** -0.5
    key_pos = jnp.arange(S)
    qb = q.reshape(B, H, nb, Q_BLOCK, Dh).transpose(2, 0, 1, 3, 4)

    def block(args):
        i, q_i = args
        q_pos = i * Q_BLOCK + jnp.arange(Q_BLOCK)
        z = jnp.einsum('bhqd,bhkd->bhqk', q_i, k).astype(jnp.float32) * scale
        mask = key_pos[None, :] < q_pos[:, None]
        log_beta = jax.nn.log_sigmoid(z)
        log_one_minus = jnp.where(mask, jax.nn.log_sigmoid(-z), 0.0)
        later = lax.cumsum(log_one_minus, axis=3, reverse=True) - log_one_minus
        weights = jnp.where(mask, jnp.exp(log_beta + later), 0.0)
        return jnp.einsum('bhqk,bhkd->bhqd', weights.astype(v.dtype), v)

    out = lax.map(block, (jnp.arange(nb), qb))
    return out.transpose(1, 2, 0, 3, 4).reshape(B, H, S, Dh)


def gla_chunked(q, k, v, log_a):
    B, H, S, Dk = q.shape
    Dv = v.shape[-1]
    nc = S // GLA_CHUNK

    def to_chunks(t):
        return t.astype(jnp.float32).reshape(B, H, nc, GLA_CHUNK, t.shape[-1]).transpose(2, 0, 1, 3, 4)

    causal = jnp.tril(jnp.ones((GLA_CHUNK, GLA_CHUNK), dtype=bool))

    def step(state, inp):
        q_c, k_c, v_c, g_c = inp
        b = jnp.cumsum(g_c, axis=2)
        o_inter = jnp.einsum('bhtk,bhkv->bhtv', q_c * jnp.exp(b), state)
        diff = b[:, :, :, None, :] - b[:, :, None, :, :]
        decay = jnp.where(causal[:, :, None], jnp.exp(jnp.minimum(diff, 0.0)), 0.0)
        scores = jnp.einsum('bhtk,bhsk,bhtsk->bhts', q_c, k_c, decay)
        o_intra = jnp.einsum('bhts,bhsv->bhtv', scores, v_c)
        b_last = b[:, :, -1:, :]
        k_dec = k_c * jnp.exp(b_last - b)
        state = jnp.exp(b_last)[:, :, 0, :, None] * state + jnp.einsum('bhsk,bhsv->bhkv', k_dec, v_c)
        return state, o_inter + o_intra

    state0 = jnp.zeros((B, H, Dk, Dv), jnp.float32)
    _, out = lax.scan(step, state0, (to_chunks(q), to_chunks(k), to_chunks(v), to_chunks(log_a)))
    return out.transpose(1, 2, 0, 3, 4).reshape(B, H, S, Dv).astype(v.dtype)


def hybrid_mixer(h, w_in, gla_gate_w2, gla_gate_b, gla_norm_g, w_out):
    B, S, _ = h.shape
    proj = h @ w_in
    sizes = (SB_WIDTH, SB_WIDTH, SB_WIDTH, GLA_KW, GLA_KW, GLA_VW, GLA_VW, GLA_GATE_RANK)
    points = [int(p) for p in np.cumsum(sizes)[:-1]]
    sb_q, sb_k, sb_v, g_q, g_k, g_v, g_r, g_lr = jnp.split(proj, points, axis=-1)

    def heads(t, n):
        return t.reshape(B, S, n, -1).transpose(0, 2, 1, 3)

    o_sb = stick_breaking_attention(heads(sb_q, SB_HEADS), heads(sb_k, SB_HEADS), heads(sb_v, SB_HEADS))
    o_sb = o_sb.transpose(0, 2, 1, 3).reshape(B, S, SB_WIDTH)

    log_a = jax.nn.log_sigmoid((g_lr @ gla_gate_w2 + gla_gate_b).astype(jnp.float32)) / GLA_GATE_TAU
    o_gla = gla_chunked(heads(g_q * (GLA_DK ** -0.5), GLA_HEADS), heads(g_k, GLA_HEADS),
                        heads(g_v, GLA_HEADS), heads(log_a, GLA_HEADS))
    o_gla = rmsnorm(o_gla.transpose(0, 2, 1, 3), gla_norm_g.reshape(GLA_HEADS, GLA_DV))
    o_gla = o_gla.reshape(B, S, GLA_VW) * jax.nn.silu(g_r)

    return jnp.concatenate([o_sb, o_gla], axis=-1) @ w_out


def hierarchical_moe(h, w_group, b_group, w_expert, b_expert, exp_w_gate, exp_w_up, exp_w_down):
    B, S, D = h.shape
    T = B * S
    xt = h.reshape(T, D)
    tok = jnp.arange(T)
    group_logits = (xt @ w_group).astype(jnp.float32) + b_group
    group_prob = jax.nn.softmax(group_logits, axis=-1)
    g_idx = jnp.argmax(group_logits, axis=-1)
    g_p = group_prob[tok, g_idx][:, None]
    exp_logits = ((xt @ w_expert).astype(jnp.float32) + b_expert).reshape(T, N_GROUPS, EXPERTS_PER_GROUP)
    sel = exp_logits[tok, g_idx]
    top_w, top_i = lax.top_k(jax.nn.softmax(sel, axis=-1), TOP_K)
    top_w = top_w / jnp.sum(top_w, axis=-1, keepdims=True)
    gate = g_p * top_w
    expert_id = g_idx[:, None] * EXPERTS_PER_GROUP + top_i

    N = T * TOP_K
    e_flat = expert_id.reshape(N)
    tok_flat = jnp.repeat(jnp.arange(T, dtype=jnp.int32), TOP_K)
    gate_flat = gate.reshape(N)
    order = jnp.argsort(e_flat)
    e_sorted = e_flat[order]
    counts = jnp.bincount(e_flat, length=N_EXPERTS)
    starts = jnp.cumsum(counts) - counts
    padded = (counts + MOE_BLOCK - 1) // MOE_BLOCK * MOE_BLOCK
    padded_ends = jnp.cumsum(padded)
    padded_starts = padded_ends - padded
    dest = padded_starts[e_sorted] + (jnp.arange(N) - starts[e_sorted])
    n_slots = N + N_EXPERTS * MOE_BLOCK
    n_blocks = n_slots // MOE_BLOCK
    slot_tok = jnp.full((n_slots,), T, jnp.int32).at[dest].set(tok_flat[order])
    slot_gate = jnp.zeros((n_slots,), jnp.float32).at[dest].set(gate_flat[order])
    block_start = jnp.arange(n_blocks) * MOE_BLOCK
    block_expert = jnp.minimum(jnp.searchsorted(padded_ends, block_start, side='right'), N_EXPERTS - 1)

    x_pad = jnp.concatenate([xt, jnp.zeros((1, D), xt.dtype)], axis=0)
    xb = x_pad[slot_tok].reshape(n_blocks, MOE_BLOCK, D)

    def expert_block(args):
        e, x_blk = args
        hidden = jax.nn.silu(x_blk @ exp_w_gate[e]) * (x_blk @ exp_w_up[e])
        return hidden @ exp_w_down[e]

    yb = lax.map(expert_block, (block_expert, xb))
    y = yb.reshape(n_slots, D) * slot_gate[:, None].astype(yb.dtype)
    out = jnp.zeros((T + 1, D), y.dtype).at[slot_tok].add(y)[:T]
    return out.reshape(B, S, D)


def setup_inputs(seed: int = 0) -> dict:
    key = jax.random.key(seed)
    ks = jax.random.split(key, 16)
    f32 = jnp.float32
    nrm = lambda k, shape, s: jax.random.normal(k, shape, f32) * s
    return {
        "x": nrm(ks[0], (BATCH, SEQ, D_MODEL), 1.0),
        "ln1_g": 1.0 + nrm(ks[1], (DEPTH, D_MODEL), 0.02),
        "w_in": nrm(ks[2], (DEPTH, D_MODEL, D_IN), D_MODEL ** -0.5),
        "gla_gate_w2": nrm(ks[3], (DEPTH, GLA_GATE_RANK, GLA_KW), GLA_GATE_RANK ** -0.5),
        "gla_gate_b": nrm(ks[4], (DEPTH, GLA_KW), 0.1),
        "gla_norm_g": 1.0 + nrm(ks[5], (DEPTH, GLA_VW), 0.02),
        "w_out": nrm(ks[6], (DEPTH, D_MIX, D_MODEL), D_MIX ** -0.5),
        "ln2_g": 1.0 + nrm(ks[7], (DEPTH, D_MODEL), 0.02),
        "w_group": nrm(ks[8], (DEPTH, D_MODEL, N_GROUPS), D_MODEL ** -0.5),
        "b_group": nrm(ks[9], (DEPTH, N_GROUPS), 0.01),
        "w_expert": nrm(ks[10], (DEPTH, D_MODEL, N_EXPERTS), D_MODEL ** -0.5),
        "b_expert": nrm(ks[11], (DEPTH, N_EXPERTS), 0.01),
        "exp_w_gate": nrm(ks[12], (DEPTH, N_EXPERTS, D_MODEL, D_FF_EXPERT), D_MODEL ** -0.5),
        "exp_w_up": nrm(ks[13], (DEPTH, N_EXPERTS, D_MODEL, D_FF_EXPERT), D_MODEL ** -0.5),
        "exp_w_down": nrm(ks[14], (DEPTH, N_EXPERTS, D_FF_EXPERT, D_MODEL), D_FF_EXPERT ** -0.5),
        "ln_f_g": 1.0 + nrm(ks[15], (D_MODEL,), 0.02),
    }


def reference(x, ln1_g, w_in, gla_gate_w2, gla_gate_b, gla_norm_g, w_out, ln2_g,
              w_group, b_group, w_expert, b_expert, exp_w_gate, exp_w_up, exp_w_down, ln_f_g):
    for l in range(DEPTH):
        h = rmsnorm(x, ln1_g[l])
        x = x + hybrid_mixer(h, w_in[l], gla_gate_w2[l], gla_gate_b[l], gla_norm_g[l], w_out[l])
        h = rmsnorm(x, ln2_g[l])
        x = x + hierarchical_moe(h, w_group[l], b_group[l], w_expert[l], b_expert[l],
                                 exp_w_gate[l], exp_w_up[l], exp_w_down[l])
    return rmsnorm(x, ln_f_g)
```

```python
import functools

import numpy as np
import jax
import jax.numpy as jnp
from jax import lax
from jax.experimental import pallas as pl
from jax.experimental.pallas import tpu as pltpu

SB_HEADS = 8
SB_HEAD_DIM = 64
SB_WIDTH = SB_HEADS * SB_HEAD_DIM
GLA_HEADS = 4
GLA_DK = 64
GLA_DV = 128
GLA_KW = GLA_HEADS * GLA_DK
GLA_VW = GLA_HEADS * GLA_DV
GLA_GATE_RANK = 16
GLA_GATE_TAU = 16.0
N_GROUPS = 4
EXPERTS_PER_GROUP = 8
N_EXPERTS = N_GROUPS * EXPERTS_PER_GROUP
TOP_K = 2
RMS_EPS = 1e-6

LANES = 128
VMEM_LIMIT_BYTES = 56 * 1024 * 1024

F32 = jnp.float32
BF16 = jnp.bfloat16


def _dot(a, b):
    return jnp.dot(a, b, preferred_element_type=F32)


def _dot_nt(a, b):
    return lax.dot_general(a, b, (((1,), (1,)), ((), ())), preferred_element_type=F32)


def _softplus(z):
    return jnp.maximum(z, 0.0) + jnp.log(1.0 + jnp.exp(-jnp.abs(z)))


def _params(*sem):
    return pltpu.CompilerParams(dimension_semantics=sem, vmem_limit_bytes=VMEM_LIMIT_BYTES)


def _in_proj_kernel(x_ref, g_ref, wq, wk, wv, wgq, wgk, wgv, wgr, wlr,
                    oq, ok, ov, ogq, ogk, ogv, ogr, olr):
    x = x_ref[...]
    inv = lax.rsqrt(jnp.mean(x * x, axis=-1, keepdims=True) + RMS_EPS)
    h = ((x * inv) * g_ref[...]).astype(BF16)
    oq[...] = (_dot(h, wq[...]) * (SB_HEAD_DIM ** -0.5)).astype(oq.dtype)
    ok[...] = _dot(h, wk[...]).astype(ok.dtype)
    ov[...] = _dot(h, wv[...]).astype(ov.dtype)
    ogq[...] = (_dot(h, wgq[...]) * (GLA_DK ** -0.5)).astype(ogq.dtype)
    ogk[...] = _dot(h, wgk[...]).astype(ogk.dtype)
    ogv[...] = _dot(h, wgv[...]).astype(ogv.dtype)
    ogr[...] = _dot(h, wgr[...]).astype(ogr.dtype)
    olr[...] = _dot(h, wlr[...]).astype(olr.dtype)


def _in_proj(x2, ln_g, w_in, tm):
    T, D = x2.shape
    sizes = (SB_WIDTH, SB_WIDTH, SB_WIDTH, GLA_KW, GLA_KW, GLA_VW, GLA_VW, GLA_GATE_RANK)
    offs = np.concatenate([[0], np.cumsum(sizes)])
    ws = [w_in[:, int(offs[i]):int(offs[i + 1])].astype(BF16) for i in range(len(sizes))]
    out_dtypes = (BF16, BF16, BF16, F32, F32, BF16, F32, F32)
    row = lambda n: pl.BlockSpec((tm, n), lambda i: (i, 0))
    full = lambda n: pl.BlockSpec((D, n), lambda i: (0, 0))
    return pl.pallas_call(
        _in_proj_kernel,
        grid=(T // tm,),
        in_specs=[row(D), pl.BlockSpec((1, D), lambda i: (0, 0))] + [full(n) for n in sizes],
        out_specs=[row(n) for n in sizes],
        out_shape=[jax.ShapeDtypeStruct((T, n), dt) for n, dt in zip(sizes, out_dtypes)],
        compiler_params=_params("parallel"),
        name="in_proj",
    )(x2, ln_g.reshape(1, D), *ws)


def _sb_attn_kernel(q_ref, k_ref, v_ref, o_ref, *, tq, n_pairs):
    qi = pl.program_id(1)
    r = lax.broadcasted_iota(jnp.int32, (tq, tq), 0)
    c = lax.broadcasted_iota(jnp.int32, (tq, tq), 1)
    causal = c < r
    suffix = (r >= c).astype(BF16)
    head0 = lax.broadcasted_iota(jnp.int32, (1, LANES), 1) < SB_HEAD_DIM

    for p in range(n_pairs):
        cols = pl.ds(p * LANES, LANES)
        q2 = q_ref[:, cols]
        zero = jnp.zeros_like(q2)
        q_heads = (jnp.where(head0, q2, zero), jnp.where(head0, zero, q2))

        def tile(kb, state, diag):
            acc, carries = state
            rows = pl.ds(pl.multiple_of(kb * tq, tq), tq)
            k2 = k_ref[rows, cols]
            v2 = v_ref[rows, cols]
            vzero = jnp.zeros_like(v2)
            v_heads = (jnp.where(head0, v2, vzero), jnp.where(head0, vzero, v2))
            new_carries = []
            for qh, vh, carry in zip(q_heads, v_heads, carries):
                z = _dot_nt(qh, k2)
                sp = _softplus(z)
                if diag:
                    sp = jnp.where(causal, sp, 0.0)
                run = _dot(sp.astype(BF16), suffix)
                w = jnp.exp(z - run - carry)
                if diag:
                    w = jnp.where(causal, w, 0.0)
                acc = acc + _dot(w.astype(BF16), vh)
                new_carries.append(carry + run[:, 0:1])
            return acc, tuple(new_carries)

        state = (jnp.zeros((tq, LANES), F32),
                 (jnp.zeros((tq, 1), F32), jnp.zeros((tq, 1), F32)))
        state = tile(qi, state, True)
        state = lax.fori_loop(0, qi, lambda i, s: tile(qi - 1 - i, s, False), state)
        o_ref[:, cols] = state[0].astype(o_ref.dtype)


def _sb_attn(q, k, v, B, S, tq):
    W = q.shape[-1]
    n_pairs = W // LANES
    q3, k3, v3 = (a.reshape(B, S, W) for a in (q, k, v))
    blk = pl.BlockSpec((None, tq, W), lambda b, i: (b, i, 0))
    seq = pl.BlockSpec((None, S, W), lambda b, i: (b, 0, 0))
    out = pl.pallas_call(
        functools.partial(_sb_attn_kernel, tq=tq, n_pairs=n_pairs),
        grid=(B, S // tq),
        in_specs=[blk, seq, seq],
        out_specs=blk,
        out_shape=jax.ShapeDtypeStruct((B, S, W), BF16),
        compiler_params=_params("parallel", "arbitrary"),
        name="sb_attn",
    )(q3, k3, v3)
    return out.reshape(B * S, W)


GLA_CHUNK = 64
GLA_LEVELS = (32, 16, 8, 4, 2, 1)


def _gla_tables(C=GLA_CHUNK):
    t = np.arange(C)[:, None]
    j = np.arange(C)[None, :]
    blocks = [(j <= t), (j > t)]
    q_rows, k_rows, masks = [], [], []
    for half in GLA_LEVELS:
        blk = 2 * half
        u = (t // blk) * blk + half - 1
        second = (t % blk) >= half
        q_rows.append(second & (j > u) & (j <= t))
        k_rows.append((~second) & (j > t) & (j <= u))
        s = j
        masks.append(second & ((s % blk) < half) & ((s // blk) == (t // blk)))
    blocks += q_rows + k_rows
    blocks.append(np.ones((8, C), bool))
    table = np.concatenate(blocks, axis=0).astype(np.float32)
    table = np.concatenate([table, table], axis=1)
    return table, np.concatenate(masks, axis=0).astype(np.float32)


def _split_bf16(a):
    hi = a.astype(BF16)
    lo = (a - hi.astype(F32)).astype(BF16)
    return hi, lo


def _gla_kernel(gq_ref, gk_ref, gv_ref, gr_ref, lr_ref, w2_ref, gb_ref, ng_ref, tab_ref, msk_ref,
                o_ref, state_ref, *, n_chunks):
    C = GLA_CHUNK
    n_lv = len(GLA_LEVELS)

    @pl.when(pl.program_id(1) == 0)
    def _():
        state_ref[...] = jnp.zeros_like(state_ref)

    head0 = lax.broadcasted_iota(jnp.int32, (1, LANES), 1) < GLA_DK
    head_masks = (head0, jnp.logical_not(head0))
    eye = (lax.broadcasted_iota(jnp.int32, (C, C), 0) == lax.broadcasted_iota(jnp.int32, (C, C), 1))
    w2_hi, w2_lo = _split_bf16(w2_ref[...])

    def chunk(ci, _):
        rows = pl.ds(pl.multiple_of(ci * C, C), C)
        lr_hi, lr_lo = _split_bf16(lr_ref[rows, :])
        for p in range(GLA_HEADS // 2):
            cols = pl.ds(p * LANES, LANES)
            pre = (_dot(lr_hi, w2_hi[:, p * LANES:(p + 1) * LANES])
                   + _dot(lr_hi, w2_lo[:, p * LANES:(p + 1) * LANES])
                   + _dot(lr_lo, w2_hi[:, p * LANES:(p + 1) * LANES])) + gb_ref[:, cols]
            log_a = -_softplus(-pre) * (1.0 / GLA_GATE_TAU)
            la_hi, la_lo = _split_bf16(log_a)
            expo = _dot(tab_ref[...], jnp.concatenate([la_hi, la_lo], axis=0))
            ex = jnp.exp(expo)
            q2 = gq_ref[rows, cols]
            k2 = gk_ref[rows, cols]
            q_int = q2 * ex[0:C]
            k_dec = (k2 * ex[C:2 * C]).astype(BF16)
            dec_row = ex[(2 + 2 * n_lv) * C:(2 + 2 * n_lv) * C + 1]
            st = state_ref[p]
            st_bf = st.astype(BF16)
            qk = q2 * k2
            scores = [jnp.where(eye, jnp.sum(jnp.where(hm, qk, 0.0), axis=1, keepdims=True), 0.0)
                      for hm in head_masks]
            for l in range(n_lv):
                q_l = q2 * ex[(2 + l) * C:(3 + l) * C]
                k_l = (k2 * ex[(2 + n_lv + l) * C:(3 + n_lv + l) * C]).astype(BF16)
                m_l = msk_ref[l * C:(l + 1) * C, :] > 0.5
                for h, hm in enumerate(head_masks):
                    a = _dot_nt(jnp.where(hm, q_l, 0.0).astype(BF16), k_l)
                    scores[h] = scores[h] + jnp.where(m_l, a, 0.0)
            upd = []
            for h, hm in enumerate(head_masks):
                vcols = pl.ds((2 * p + h) * GLA_DV, GLA_DV)
                v_h = gv_ref[rows, vcols]
                o = (_dot_nt(jnp.where(hm, q_int, 0.0).astype(BF16), st_bf)
                     + _dot(scores[h].astype(BF16), v_h))
                inv = lax.rsqrt(jnp.mean(o * o, axis=-1, keepdims=True) + RMS_EPS)
                y = (o * inv) * ng_ref[:, vcols]
                g = gr_ref[rows, vcols]
                y = y * (g / (1.0 + jnp.exp(-g)))
                o_ref[rows, vcols] = y.astype(o_ref.dtype)
                v_t = v_h.astype(F32).T.astype(BF16)
                upd.append(_dot(v_t, k_dec))
            state_ref[p] = st * dec_row + jnp.where(head0, upd[0], upd[1])
        return 0

    lax.fori_loop(0, n_chunks, chunk, 0)


def _gla(gq, gk, gv, gr, lr, w2, gb, ng, B, S, tb):
    table, masks = _gla_tables()
    table = jnp.asarray(table, BF16)
    masks = jnp.asarray(masks, F32)
    T = B * S
    nblk = S // tb
    row = lambda n: pl.BlockSpec((tb, n), lambda b, i: (b * nblk + i, 0))
    const = lambda a: pl.BlockSpec(a.shape, lambda b, i: (0,) * a.ndim)
    w2 = w2.astype(F32)
    gb2 = gb.reshape(1, GLA_KW)
    ng2 = ng.reshape(1, GLA_VW)
    return pl.pallas_call(
        functools.partial(_gla_kernel, n_chunks=tb // GLA_CHUNK),
        grid=(B, nblk),
        in_specs=[row(GLA_KW), row(GLA_KW), row(GLA_VW), row(GLA_VW), row(GLA_GATE_RANK),
                  const(w2), const(gb2), const(ng2), const(table), const(masks)],
        out_specs=row(GLA_VW),
        out_shape=jax.ShapeDtypeStruct((T, GLA_VW), BF16),
        scratch_shapes=[pltpu.VMEM((GLA_HEADS // 2, GLA_DV, 2 * GLA_DK), F32)],
        compiler_params=_params("parallel", "arbitrary"),
        name="gla",
    )(gq, gk, gv, gr, lr, w2, gb2, ng2, table, masks)


def _out_proj_kernel(x_ref, oa_ref, ob_ref, wa_ref, wb_ref, g_ref, wr_ref, br_ref,
                     x1_ref, h2_ref, ri_ref, rf_ref):
    x1 = x_ref[...] + _dot(oa_ref[...], wa_ref[...]) + _dot(ob_ref[...], wb_ref[...])
    x1_ref[...] = x1
    inv = lax.rsqrt(jnp.mean(x1 * x1, axis=-1, keepdims=True) + RMS_EPS)
    h2 = (x1 * inv) * g_ref[...]
    h2_ref[...] = h2

    h_hi, h_lo = _split_bf16(h2)
    w_hi, w_lo = _split_bf16(wr_ref[...])
    logits = _dot(h_hi, w_hi) + _dot(h_hi, w_lo) + _dot(h_lo, w_hi) + br_ref[...]
    lane = lax.broadcasted_iota(jnp.int32, logits.shape, 1)
    neg = -jnp.inf

    def first_max(vals):
        m = jnp.max(vals, axis=1, keepdims=True)
        idx = jnp.min(jnp.where(vals == m, lane, LANES), axis=1, keepdims=True)
        return m, idx

    is_group = lane < N_GROUPS
    g_max, g_idx = first_max(jnp.where(is_group, logits, neg))
    g_p = 1.0 / jnp.sum(jnp.where(is_group, jnp.exp(logits - g_max), 0.0), axis=1, keepdims=True)
    lo_lane = N_GROUPS + EXPERTS_PER_GROUP * g_idx
    sel = jnp.where((lane >= lo_lane) & (lane < lo_lane + EXPERTS_PER_GROUP), logits, neg)
    m1, i1 = first_max(sel)
    m2, i2 = first_max(jnp.where(lane == i1, neg, sel))
    e = jnp.exp(m2 - m1)
    gate1 = g_p / (1.0 + e)
    gate2 = g_p * e / (1.0 + e)
    ri_ref[...] = jnp.where(lane == 0, i1 - N_GROUPS, jnp.where(lane == 1, i2 - N_GROUPS, 0))
    rf_ref[...] = jnp.where(lane == 0, gate1, jnp.where(lane == 1, gate2, 0.0))


def _out_proj(x2, o_sb, o_gla, w_out, ln_g, w_group, b_group, w_expert, b_expert, tm):
    T, D = x2.shape
    wa = w_out[:SB_WIDTH].astype(BF16)
    wb = w_out[SB_WIDTH:].astype(BF16)
    n_r = N_GROUPS + N_EXPERTS
    wr = jnp.zeros((D, LANES), F32).at[:, :n_r].set(jnp.concatenate([w_group, w_expert], axis=1))
    br = jnp.zeros((1, LANES), F32).at[0, :n_r].set(jnp.concatenate([b_group, b_expert]))
    row = lambda n: pl.BlockSpec((tm, n), lambda i: (i, 0))
    const = lambda a: pl.BlockSpec(a.shape, lambda i: (0,) * a.ndim)
    g2 = ln_g.reshape(1, D)
    return pl.pallas_call(
        _out_proj_kernel,
        grid=(T // tm,),
        in_specs=[row(D), row(SB_WIDTH), row(GLA_VW), const(wa), const(wb), const(g2), const(wr), const(br)],
        out_specs=[row(D), row(D), row(LANES), row(LANES)],
        out_shape=[jax.ShapeDtypeStruct((T, D), F32), jax.ShapeDtypeStruct((T, D), F32),
                   jax.ShapeDtypeStruct((T, LANES), jnp.int32), jax.ShapeDtypeStruct((T, LANES), F32)],
        compiler_params=_params("parallel"),
        name="out_proj",
    )(x2, o_sb, o_gla, wa, wb, g2, wr, br)


def _route_tables(expert_id, n_rows, tm):
    e_flat = expert_id.reshape(-1)
    onehot = (e_flat[:, None] == jnp.arange(N_EXPERTS, dtype=jnp.int32)[None, :]).astype(jnp.int32)
    cum = jnp.cumsum(onehot, axis=0)
    counts = cum[-1]
    rank = jnp.take_along_axis(cum, e_flat[:, None], axis=1)[:, 0] - 1
    ends = jnp.cumsum(counts)
    starts = ends - counts
    dest = (starts[e_flat] + rank).astype(jnp.int32)

    n_blocks = n_rows // tm
    n_items = n_blocks + N_EXPERTS - 1
    first_blk = starts // tm
    last_blk = jnp.maximum(ends - 1, starts) // tm
    items_e = jnp.where(counts > 0, last_blk - first_blk + 1, 0)
    item_end = jnp.cumsum(items_e)
    item_start = item_end - items_e
    w = jnp.arange(n_items, dtype=jnp.int32)
    total = item_end[-1]
    valid = w < total
    wc = jnp.minimum(w, total - 1)
    it_e = jnp.sum(item_end[None, :] <= wc[:, None], axis=1).astype(jnp.int32)
    it_b = (first_blk[it_e] + (wc - item_start[it_e])).astype(jnp.int32)
    lo = jnp.clip(starts[it_e] - it_b * tm, 0, tm)
    hi = jnp.clip(ends[it_e] - it_b * tm, 0, tm)
    hi = jnp.where(valid, hi, lo)
    prev_b = jnp.concatenate([jnp.full((1,), -1, jnp.int32), it_b[:-1]])
    first = (it_b != prev_b).astype(jnp.int32)
    return dest, it_b, it_e, lo.astype(jnp.int32), hi.astype(jnp.int32), first


def _row_copy(src_ref, dst_ref, sem, i, j):
    return pltpu.make_async_copy(src_ref.at[pl.ds(i, 1), :], dst_ref.at[pl.ds(j, 1), :], sem)


def _dispatch_kernel(dest_ref, h_ref, xs_ref, sem, *, tb):
    base = pl.program_id(0) * tb

    def issue(r, _):
        t = base + r
        for k in range(TOP_K):
            _row_copy(h_ref, xs_ref, sem, t, dest_ref[TOP_K * t + k]).start()
        return 0

    def drain(r, _):
        for k in range(TOP_K):
            _row_copy(h_ref, xs_ref, sem, 0, 0).wait()
        return 0

    lax.fori_loop(0, tb, issue, 0)
    lax.fori_loop(0, tb, drain, 0)


def _dispatch(h2, dest, tb):
    T, D = h2.shape
    return pl.pallas_call(
        functools.partial(_dispatch_kernel, tb=tb),
        grid_spec=pltpu.PrefetchScalarGridSpec(
            num_scalar_prefetch=1,
            grid=(T // tb,),
            in_specs=[pl.BlockSpec(memory_space=pl.ANY)],
            out_specs=pl.BlockSpec(memory_space=pl.ANY),
            scratch_shapes=[pltpu.SemaphoreType.DMA(())],
        ),
        out_shape=jax.ShapeDtypeStruct((TOP_K * T, D), h2.dtype),
        compiler_params=_params("arbitrary"),
        name="dispatch",
    )(dest, h2)


def _expert_kernel(blk_ref, exp_ref, lo_ref, hi_ref, first_ref, x_ref, wg_ref, wu_ref, wd_ref, y_ref):
    w = pl.program_id(0)
    xb = x_ref[...].astype(BF16)
    g = _dot(xb, wg_ref[...].astype(BF16))
    u = _dot(xb, wu_ref[...].astype(BF16))
    hidden = (g / (1.0 + jnp.exp(-g))) * u
    y = _dot(hidden.astype(BF16), wd_ref[...].astype(BF16))
    r = lax.broadcasted_iota(jnp.int32, y.shape, 0)
    y = jnp.where((r >= lo_ref[w]) & (r < hi_ref[w]), y, 0.0)

    @pl.when(first_ref[w] == 1)
    def _():
        y_ref[...] = y

    @pl.when(first_ref[w] == 0)
    def _():
        y_ref[...] += y


def _expert_ffn(xs, tables, w_gate, w_up, w_down, tm):
    N, D = xs.shape
    F = w_gate.shape[-1]
    it_b, it_e, lo, hi, first = tables
    n_items = it_b.shape[0]
    return pl.pallas_call(
        _expert_kernel,
        grid_spec=pltpu.PrefetchScalarGridSpec(
            num_scalar_prefetch=5,
            grid=(n_items,),
            in_specs=[pl.BlockSpec((tm, D), lambda w, b, e, lo, hi, f: (b[w], 0)),
                      pl.BlockSpec((None, D, F), lambda w, b, e, lo, hi, f: (e[w], 0, 0)),
                      pl.BlockSpec((None, D, F), lambda w, b, e, lo, hi, f: (e[w], 0, 0)),
                      pl.BlockSpec((None, F, D), lambda w, b, e, lo, hi, f: (e[w], 0, 0))],
            out_specs=pl.BlockSpec((tm, D), lambda w, b, e, lo, hi, f: (b[w], 0)),
        ),
        out_shape=jax.ShapeDtypeStruct((N, D), F32),
        compiler_params=_params("arbitrary"),
        name="expert_ffn",
    )(it_b, it_e, lo, hi, first, xs, w_gate, w_up, w_down)


def _combine_kernel(dest_ref, x1_ref, rf_ref, g_ref, y_ref, o_ref, buf, sem, *, tb):
    base = pl.program_id(0) * tb

    def issue(r, _):
        for k in range(TOP_K):
            _row_copy(y_ref, buf.at[k], sem, dest_ref[TOP_K * (base + r) + k], r).start()
        return 0

    def drain(r, _):
        for k in range(TOP_K):
            _row_copy(y_ref, buf.at[k], sem, 0, 0).wait()
        return 0

    lax.fori_loop(0, tb, issue, 0)
    lax.fori_loop(0, tb, drain, 0)
    rf = rf_ref[...]
    x = x1_ref[...] + rf[:, 0:1] * buf[0] + rf[:, 1:2] * buf[1]
    inv = lax.rsqrt(jnp.mean(x * x, axis=-1, keepdims=True) + RMS_EPS)
    o_ref[...] = (x * inv) * g_ref[...]


def _combine(x1, route_f, y, dest, ln_g, tb):
    T, D = x1.shape
    g2 = ln_g.reshape(1, D)
    return pl.pallas_call(
        functools.partial(_combine_kernel, tb=tb),
        grid_spec=pltpu.PrefetchScalarGridSpec(
            num_scalar_prefetch=1,
            grid=(T // tb,),
            in_specs=[pl.BlockSpec((tb, D), lambda i, d: (i, 0)),
                      pl.BlockSpec((tb, LANES), lambda i, d: (i, 0)),
                      pl.BlockSpec((1, D), lambda i, d: (0, 0)),
                      pl.BlockSpec(memory_space=pl.ANY)],
            out_specs=pl.BlockSpec((tb, D), lambda i, d: (i, 0)),
            scratch_shapes=[pltpu.VMEM((TOP_K, tb, D), F32), pltpu.SemaphoreType.DMA(())],
        ),
        out_shape=jax.ShapeDtypeStruct((T, D), F32),
        compiler_params=_params("arbitrary"),
        name="combine",
    )(dest, x1, route_f, g2, y)


def _layer(x2, B, S, ln1_g, w_in, w2, gb, ng, w_out, ln2_g, w_group, b_group, w_expert, b_expert,
           w_gate, w_up, w_down, ln_f_g, *, tm, tq, tb_gla, tb_row, tm_e):
    q, k, v, gq, gk, gv, gr, lr = _in_proj(x2, ln1_g, w_in, tm)
    o_sb = _sb_attn(q, k, v, B, S, tq)
    o_gla = _gla(gq, gk, gv, gr, lr, w2, gb, ng, B, S, tb_gla)
    x1, h2, route_i, route_f = _out_proj(x2, o_sb, o_gla, w_out, ln2_g, w_group, b_group,
                                         w_expert, b_expert, tm)
    T = x2.shape[0]
    dest, *tables = _route_tables(route_i[:, :TOP_K], TOP_K * T, tm_e)
    xs = _dispatch(h2, dest, tb_row)
    y = _expert_ffn(xs, tables, w_gate, w_up, w_down, tm_e)
    return _combine(x1, route_f, y, dest, ln_f_g, tb_row)


def kernel(x, ln1_g, w_in, gla_gate_w2, gla_gate_b, gla_norm_g, w_out, ln2_g, w_group, b_group, w_expert, b_expert, exp_w_gate, exp_w_up, exp_w_down, ln_f_g):
    B, S, D = x.shape
    assert ln1_g.shape[0] == 1, "single-layer stack"
    out = _layer(x.reshape(B * S, D), B, S, ln1_g[0], w_in[0], gla_gate_w2[0], gla_gate_b[0],
                 gla_norm_g[0], w_out[0], ln2_g[0], w_group[0], b_group[0], w_expert[0], b_expert[0],
                 exp_w_gate[0], exp_w_up[0], exp_w_down[0], ln_f_g,
                 tm=512, tq=256, tb_gla=512, tb_row=256, tm_e=256)
    return out.reshape(B, S, D)
```

```python
import functools

import numpy as np
import jax
import jax.numpy as jnp
from jax import lax
from jax.experimental import pallas as pl
from jax.experimental.pallas import tpu as pltpu

SB_HEADS = 8
SB_HEAD_DIM = 64
SB_WIDTH = SB_HEADS * SB_HEAD_DIM
GLA_HEADS = 4
GLA_DK = 64
GLA_DV = 128
GLA_KW = GLA_HEADS * GLA_DK
GLA_VW = GLA_HEADS * GLA_DV
GLA_GATE_RANK = 16
GLA_GATE_TAU = 16.0
N_GROUPS = 4
EXPERTS_PER_GROUP = 8
N_EXPERTS = N_GROUPS * EXPERTS_PER_GROUP
TOP_K = 2
RMS_EPS = 1e-6

LANES = 128
VMEM_LIMIT_BYTES = 56 * 1024 * 1024

F32 = jnp.float32
BF16 = jnp.bfloat16


def _dot(a, b):
    return jnp.dot(a, b, preferred_element_type=F32)


def _dot_nt(a, b):
    return lax.dot_general(a, b, (((1,), (1,)), ((), ())), preferred_element_type=F32)


def _softplus(z):
    return jnp.maximum(z, 0.0) + jnp.log(1.0 + jnp.exp(-jnp.abs(z)))


def _params(*sem):
    return pltpu.CompilerParams(dimension_semantics=sem, vmem_limit_bytes=VMEM_LIMIT_BYTES)


def _in_proj_kernel(x_ref, g_ref, wq, wk, wv, wgq, wgk, wgv, wgr, wlr,
                    oq, ok, ov, ogq, ogk, ogv, ogr, olr):
    x = x_ref[...]
    inv = lax.rsqrt(jnp.mean(x * x, axis=-1, keepdims=True) + RMS_EPS)
    h = ((x * inv) * g_ref[...]).astype(BF16)
    oq[...] = (_dot(h, wq[...]) * (SB_HEAD_DIM ** -0.5)).astype(oq.dtype)
    ok[...] = _dot(h, wk[...]).astype(ok.dtype)
    ov[...] = _dot(h, wv[...]).astype(ov.dtype)
    ogq[...] = (_dot(h, wgq[...]) * (GLA_DK ** -0.5)).astype(ogq.dtype)
    ogk[...] = _dot(h, wgk[...]).astype(ogk.dtype)
    ogv[...] = _dot(h, wgv[...]).astype(ogv.dtype)
    ogr[...] = _dot(h, wgr[...]).astype(ogr.dtype)
    olr[...] = _dot(h, wlr[...]).astype(olr.dtype)


def _in_proj(x2, ln_g, w_in, tm):
    T, D = x2.shape
    sizes = (SB_WIDTH, SB_WIDTH, SB_WIDTH, GLA_KW, GLA_KW, GLA_VW, GLA_VW, GLA_GATE_RANK)
    offs = np.concatenate([[0], np.cumsum(sizes)])
    ws = [w_in[:, int(offs[i]):int(offs[i + 1])].astype(BF16) for i in range(len(sizes))]
    out_dtypes = (BF16, BF16, BF16, F32, F32, BF16, F32, F32)
    row = lambda n: pl.BlockSpec((tm, n), lambda i: (i, 0))
    full = lambda n: pl.BlockSpec((D, n), lambda i: (0, 0))
    return pl.pallas_call(
        _in_proj_kernel,
        grid=(T // tm,),
        in_specs=[row(D), pl.BlockSpec((1, D), lambda i: (0, 0))] + [full(n) for n in sizes],
        out_specs=[row(n) for n in sizes],
        out_shape=[jax.ShapeDtypeStruct((T, n), dt) for n, dt in zip(sizes, out_dtypes)],
        compiler_params=_params("parallel"),
        name="in_proj",
    )(x2, ln_g.reshape(1, D), *ws)


def _sb_attn_kernel(q_ref, k_ref, v_ref, o_ref, *, tq, n_pairs):
    qi = pl.program_id(1)
    r = lax.broadcasted_iota(jnp.int32, (tq, tq), 0)
    c = lax.broadcasted_iota(jnp.int32, (tq, tq), 1)
    causal = c < r
    suffix = (r >= c).astype(BF16)
    head0 = lax.broadcasted_iota(jnp.int32, (1, LANES), 1) < SB_HEAD_DIM

    for p in range(n_pairs):
        cols = pl.ds(p * LANES, LANES)
        q2 = q_ref[:, cols]
        zero = jnp.zeros_like(q2)
        q_heads = (jnp.where(head0, q2, zero), jnp.where(head0, zero, q2))

        def tile(kb, state, diag):
            acc, carries = state
            rows = pl.ds(pl.multiple_of(kb * tq, tq), tq)
            k2 = k_ref[rows, cols]
            v2 = v_ref[rows, cols]
            vzero = jnp.zeros_like(v2)
            v_heads = (jnp.where(head0, v2, vzero), jnp.where(head0, vzero, v2))
            new_carries = []
            for qh, vh, carry in zip(q_heads, v_heads, carries):
                z = _dot_nt(qh, k2)
                sp = _softplus(z)
                if diag:
                    sp = jnp.where(causal, sp, 0.0)
                run = _dot(sp.astype(BF16), suffix)
                w = jnp.exp(z - run - carry)
                if diag:
                    w = jnp.where(causal, w, 0.0)
                acc = acc + _dot(w.astype(BF16), vh)
                new_carries.append(carry + run[:, 0:1])
            return acc, tuple(new_carries)

        state = (jnp.zeros((tq, LANES), F32),
                 (jnp.zeros((tq, 1), F32), jnp.zeros((tq, 1), F32)))
        state = tile(qi, state, True)
        state = lax.fori_loop(0, qi, lambda i, s: tile(qi - 1 - i, s, False), state)
        o_ref[:, cols] = state[0].astype(o_ref.dtype)


def _sb_attn(q, k, v, B, S, tq):
    W = q.shape[-1]
    n_pairs = W // LANES
    q3, k3, v3 = (a.reshape(B, S, W) for a in (q, k, v))
    blk = pl.BlockSpec((None, tq, W), lambda b, i: (b, i, 0))
    seq = pl.BlockSpec((None, S, W), lambda b, i: (b, 0, 0))
    out = pl.pallas_call(
        functools.partial(_sb_attn_kernel, tq=tq, n_pairs=n_pairs),
        grid=(B, S // tq),
        in_specs=[blk, seq, seq],
        out_specs=blk,
        out_shape=jax.ShapeDtypeStruct((B, S, W), BF16),
        compiler_params=_params("parallel", "arbitrary"),
        name="sb_attn",
    )(q3, k3, v3)
    return out.reshape(B * S, W)


GLA_CHUNK = 64
GLA_LEVELS = (32, 16, 8, 4, 2, 1)


def _gla_tables(C=GLA_CHUNK):
    t = np.arange(C)[:, None]
    j = np.arange(C)[None, :]
    blocks = [(j <= t), (j > t)]
    q_rows, k_rows, masks = [], [], []
    for half in GLA_LEVELS:
        blk = 2 * half
        u = (t // blk) * blk + half - 1
        second = (t % blk) >= half
        q_rows.append(second & (j > u) & (j <= t))
        k_rows.append((~second) & (j > t) & (j <= u))
        s = j
        masks.append(second & ((s % blk) < half) & ((s // blk) == (t // blk)))
    blocks += q_rows + k_rows
    blocks.append(np.ones((8, C), bool))
    table = np.concatenate(blocks, axis=0).astype(np.float32)
    table = np.concatenate([table, table], axis=1)
    return table, np.concatenate(masks, axis=0).astype(np.float32)


def _split_bf16(a):
    hi = a.astype(BF16)
    lo = (a - hi.astype(F32)).astype(BF16)
    return hi, lo


def _gla_kernel(gq_ref, gk_ref, gv_ref, gr_ref, lr_ref, w2_ref, gb_ref, ng_ref, tab_ref, msk_ref,
                o_ref, state_ref, *, n_chunks):
    C = GLA_CHUNK
    n_lv = len(GLA_LEVELS)

    @pl.when(pl.program_id(1) == 0)
    def _():
        state_ref[...] = jnp.zeros_like(state_ref)

    head0 = lax.broadcasted_iota(jnp.int32, (1, LANES), 1) < GLA_DK
    head_masks = (head0, jnp.logical_not(head0))
    eye = (lax.broadcasted_iota(jnp.int32, (C, C), 0) == lax.broadcasted_iota(jnp.int32, (C, C), 1))
    w2_hi, w2_lo = _split_bf16(w2_ref[...])

    def chunk(ci, _):
        rows = pl.ds(pl.multiple_of(ci * C, C), C)
        lr_hi, lr_lo = _split_bf16(lr_ref[rows, :])
        for p in range(GLA_HEADS // 2):
            cols = pl.ds(p * LANES, LANES)
            pre = (_dot(lr_hi, w2_hi[:, p * LANES:(p + 1) * LANES])
                   + _dot(lr_hi, w2_lo[:, p * LANES:(p + 1) * LANES])
                   + _dot(lr_lo, w2_hi[:, p * LANES:(p + 1) * LANES])) + gb_ref[:, cols]
            log_a = -_softplus(-pre) * (1.0 / GLA_GATE_TAU)
            la_hi, la_lo = _split_bf16(log_a)
            expo = _dot(tab_ref[...], jnp.concatenate([la_hi, la_lo], axis=0))
            ex = jnp.exp(expo)
            q2 = gq_ref[rows, cols]
            k2 = gk_ref[rows, cols]
            q_int = q2 * ex[0:C]
            k_dec = (k2 * ex[C:2 * C]).astype(BF16)
            dec_row = ex[(2 + 2 * n_lv) * C:(2 + 2 * n_lv) * C + 1]
            st = state_ref[p]
            st_bf = st.astype(BF16)
            qk = q2 * k2
            scores = [jnp.where(eye, jnp.sum(jnp.where(hm, qk, 0.0), axis=1, keepdims=True), 0.0)
                      for hm in head_masks]
            for l in range(n_lv):
                q_l = q2 * ex[(2 + l) * C:(3 + l) * C]
                k_l = (k2 * ex[(2 + n_lv + l) * C:(3 + n_lv + l) * C]).astype(BF16)
                m_l = msk_ref[l * C:(l + 1) * C, :] > 0.5
                for h, hm in enumerate(head_masks):
                    a = _dot_nt(jnp.where(hm, q_l, 0.0).astype(BF16), k_l)
                    scores[h] = scores[h] + jnp.where(m_l, a, 0.0)
            upd = []
            for h, hm in enumerate(head_masks):
                vcols = pl.ds((2 * p + h) * GLA_DV, GLA_DV)
                v_h = gv_ref[rows, vcols]
                o = (_dot_nt(jnp.where(hm, q_int, 0.0).astype(BF16), st_bf)
                     + _dot(scores[h].astype(BF16), v_h))
                inv = lax.rsqrt(jnp.mean(o * o, axis=-1, keepdims=True) + RMS_EPS)
                y = (o * inv) * ng_ref[:, vcols]
                g = gr_ref[rows, vcols]
                y = y * (g / (1.0 + jnp.exp(-g)))
                o_ref[rows, vcols] = y.astype(o_ref.dtype)
                v_t = v_h.astype(F32).T.astype(BF16)
                upd.append(_dot(v_t, k_dec))
            state_ref[p] = st * dec_row + jnp.where(head0, upd[0], upd[1])
        return 0

    lax.fori_loop(0, n_chunks, chunk, 0)


def _gla(gq, gk, gv, gr, lr, w2, gb, ng, B, S, tb):
    table, masks = _gla_tables()
    table = jnp.asarray(table, BF16)
    masks = jnp.asarray(masks, F32)
    T = B * S
    nblk = S // tb
    row = lambda n: pl.BlockSpec((tb, n), lambda b, i: (b * nblk + i, 0))
    const = lambda a: pl.BlockSpec(a.shape, lambda b, i: (0,) * a.ndim)
    w2 = w2.astype(F32)
    gb2 = gb.reshape(1, GLA_KW)
    ng2 = ng.reshape(1, GLA_VW)
    return pl.pallas_call(
        functools.partial(_gla_kernel, n_chunks=tb // GLA_CHUNK),
        grid=(B, nblk),
        in_specs=[row(GLA_KW), row(GLA_KW), row(GLA_VW), row(GLA_VW), row(GLA_GATE_RANK),
                  const(w2), const(gb2), const(ng2), const(table), const(masks)],
        out_specs=row(GLA_VW),
        out_shape=jax.ShapeDtypeStruct((T, GLA_VW), BF16),
        scratch_shapes=[pltpu.VMEM((GLA_HEADS // 2, GLA_DV, 2 * GLA_DK), F32)],
        compiler_params=_params("parallel", "arbitrary"),
        name="gla",
    )(gq, gk, gv, gr, lr, w2, gb2, ng2, table, masks)


def _out_proj_kernel(x_ref, oa_ref, ob_ref, wa_ref, wb_ref, g_ref, wr_ref, br_ref,
                     x1_ref, h2_ref, ri_ref, rf_ref):
    x1 = x_ref[...] + _dot(oa_ref[...], wa_ref[...]) + _dot(ob_ref[...], wb_ref[...])
    x1_ref[...] = x1
    inv = lax.rsqrt(jnp.mean(x1 * x1, axis=-1, keepdims=True) + RMS_EPS)
    h2 = (x1 * inv) * g_ref[...]
    h2_ref[...] = h2

    h_hi, h_lo = _split_bf16(h2)
    w_hi, w_lo = _split_bf16(wr_ref[...])
    logits = _dot(h_hi, w_hi) + _dot(h_hi, w_lo) + _dot(h_lo, w_hi) + br_ref[...]
    lane = lax.broadcasted_iota(jnp.int32, logits.shape, 1)
    neg = -jnp.inf

    def first_max(vals):
        m = jnp.max(vals, axis=1, keepdims=True)
        idx = jnp.min(jnp.where(vals == m, lane, LANES), axis=1, keepdims=True)
        return m, idx

    is_group = lane < N_GROUPS
    g_max, g_idx = first_max(jnp.where(is_group, logits, neg))
    g_p = 1.0 / jnp.sum(jnp.where(is_group, jnp.exp(logits - g_max), 0.0), axis=1, keepdims=True)
    lo_lane = N_GROUPS + EXPERTS_PER_GROUP * g_idx
    sel = jnp.where((lane >= lo_lane) & (lane < lo_lane + EXPERTS_PER_GROUP), logits, neg)
    m1, i1 = first_max(sel)
    m2, i2 = first_max(jnp.where(lane == i1, neg, sel))
    e = jnp.exp(m2 - m1)
    gate1 = g_p / (1.0 + e)
    gate2 = g_p * e / (1.0 + e)
    ri_ref[...] = jnp.where(lane == 0, i1 - N_GROUPS, jnp.where(lane == 1, i2 - N_GROUPS, 0))
    rf_ref[...] = jnp.where(lane == 0, gate1, jnp.where(lane == 1, gate2, 0.0))


def _out_proj(x2, o_sb, o_gla, w_out, ln_g, w_group, b_group, w_expert, b_expert, tm):
    T, D = x2.shape
    wa = w_out[:SB_WIDTH].astype(BF16)
    wb = w_out[SB_WIDTH:].astype(BF16)
    n_r = N_GROUPS + N_EXPERTS
    wr = jnp.zeros((D, LANES), F32).at[:, :n_r].set(jnp.concatenate([w_group, w_expert], axis=1))
    br = jnp.zeros((1, LANES), F32).at[0, :n_r].set(jnp.concatenate([b_group, b_expert]))
    row = lambda n: pl.BlockSpec((tm, n), lambda i: (i, 0))
    const = lambda a: pl.BlockSpec(a.shape, lambda i: (0,) * a.ndim)
    g2 = ln_g.reshape(1, D)
    return pl.pallas_call(
        _out_proj_kernel,
        grid=(T // tm,),
        in_specs=[row(D), row(SB_WIDTH), row(GLA_VW), const(wa), const(wb), const(g2), const(wr), const(br)],
        out_specs=[row(D), row(D), row(LANES), row(LANES)],
        out_shape=[jax.ShapeDtypeStruct((T, D), F32), jax.ShapeDtypeStruct((T, D), F32),
                   jax.ShapeDtypeStruct((T, LANES), jnp.int32), jax.ShapeDtypeStruct((T, LANES), F32)],
        compiler_params=_params("parallel"),
        name="out_proj",
    )(x2, o_sb, o_gla, wa, wb, g2, wr, br)


def _route_tables(expert_id, n_rows, tm):
    e_flat = expert_id.reshape(-1)
    onehot = (e_flat[:, None] == jnp.arange(N_EXPERTS, dtype=jnp.int32)[None, :]).astype(jnp.int32)
    cum = jnp.cumsum(onehot, axis=0)
    counts = cum[-1]
    rank = jnp.take_along_axis(cum, e_flat[:, None], axis=1)[:, 0] - 1
    ends = jnp.cumsum(counts)
    starts = ends - counts
    dest = (starts[e_flat] + rank).astype(jnp.int32)

    n_blocks = n_rows // tm
    n_items = n_blocks + N_EXPERTS - 1
    first_blk = starts // tm
    last_blk = jnp.maximum(ends - 1, starts) // tm
    items_e = jnp.where(counts > 0, last_blk - first_blk + 1, 0)
    item_end = jnp.cumsum(items_e)
    item_start = item_end - items_e
    w = jnp.arange(n_items, dtype=jnp.int32)
    total = item_end[-1]
    valid = w < total
    wc = jnp.minimum(w, total - 1)
    it_e = jnp.sum(item_end[None, :] <= wc[:, None], axis=1).astype(jnp.int32)
    it_b = (first_blk[it_e] + (wc - item_start[it_e])).astype(jnp.int32)
    lo = jnp.clip(starts[it_e] - it_b * tm, 0, tm)
    hi = jnp.clip(ends[it_e] - it_b * tm, 0, tm)
    hi = jnp.where(valid, hi, lo)
    prev_b = jnp.concatenate([jnp.full((1,), -1, jnp.int32), it_b[:-1]])
    first = (it_b != prev_b).astype(jnp.int32)
    return dest, it_b, it_e, lo.astype(jnp.int32), hi.astype(jnp.int32), first


def _row_copy(src_ref, dst_ref, sem, i, j):
    return pltpu.make_async_copy(src_ref.at[pl.ds(i, 1), :], dst_ref.at[pl.ds(j, 1), :], sem)


ROW_ISSUE_UNROLL = 8


def _start_row_gather(src_ref, idx_ref, idx_base, idx_stride, dst_ref, sem):
    def body(r, _):
        _row_copy(src_ref, dst_ref, sem, idx_ref[idx_base + r * idx_stride], r).start()
        return 0
    lax.fori_loop(0, dst_ref.shape[0], body, 0, unroll=ROW_ISSUE_UNROLL)


def _wait_row_gather(src_ref, dst_ref, sem):
    pltpu.make_async_copy(src_ref.at[pl.ds(0, dst_ref.shape[0]), :], dst_ref, sem).wait()


def _expert_kernel(blk_ref, exp_ref, lo_ref, hi_ref, first_ref, tok_ref,
                   h_ref, wg_ref, wu_ref, wd_ref, y_ref, xbuf, sem, *, tm):
    w = pl.program_id(0)
    n_items = pl.num_programs(0)
    slot = w % 2

    def gather(item, s):
        _start_row_gather(h_ref, tok_ref, blk_ref[item] * tm, 1, xbuf.at[s], sem.at[s])

    @pl.when(w == 0)
    def _():
        gather(0, 0)

    @pl.when(w + 1 < n_items)
    def _():
        gather(w + 1, 1 - slot)

    _wait_row_gather(h_ref, xbuf.at[slot], sem.at[slot])
    xb = xbuf[slot].astype(BF16)
    g = _dot(xb, wg_ref[...].astype(BF16))
    u = _dot(xb, wu_ref[...].astype(BF16))
    hidden = (g / (1.0 + jnp.exp(-g))) * u
    y = _dot(hidden.astype(BF16), wd_ref[...].astype(BF16))
    r = lax.broadcasted_iota(jnp.int32, y.shape, 0)
    y = jnp.where((r >= lo_ref[w]) & (r < hi_ref[w]), y, 0.0)

    @pl.when(first_ref[w] == 1)
    def _():
        y_ref[...] = y

    @pl.when(first_ref[w] == 0)
    def _():
        y_ref[...] += y


def _expert_ffn(h2, slot_tok, tables, w_gate, w_up, w_down, tm):
    T, D = h2.shape
    N = slot_tok.shape[0]
    F = w_gate.shape[-1]
    it_b, it_e, lo, hi, first = tables
    n_items = it_b.shape[0]
    wspec = lambda shape: pl.BlockSpec((None,) + shape, lambda w, b, e, *_: (e[w], 0, 0))
    return pl.pallas_call(
        functools.partial(_expert_kernel, tm=tm),
        grid_spec=pltpu.PrefetchScalarGridSpec(
            num_scalar_prefetch=6,
            grid=(n_items,),
            in_specs=[pl.BlockSpec(memory_space=pl.ANY), wspec((D, F)), wspec((D, F)), wspec((F, D))],
            out_specs=pl.BlockSpec((tm, D), lambda w, b, *_: (b[w], 0)),
            scratch_shapes=[pltpu.VMEM((2, tm, D), h2.dtype), pltpu.SemaphoreType.DMA((2,))],
        ),
        out_shape=jax.ShapeDtypeStruct((N, D), F32),
        compiler_params=_params("arbitrary"),
        name="expert_ffn",
    )(it_b, it_e, lo, hi, first, slot_tok, h2, w_gate, w_up, w_down)


def _combine_kernel(dest_ref, x1_ref, rf_ref, g_ref, y_ref, o_ref, buf, sem, *, tb):
    i = pl.program_id(0)
    n_blk = pl.num_programs(0)
    slot = i % 2

    def gather(blk, s):
        for k in range(TOP_K):
            _start_row_gather(y_ref, dest_ref, TOP_K * blk * tb + k, TOP_K, buf.at[s, k], sem.at[s])

    @pl.when(i == 0)
    def _():
        gather(0, 0)

    @pl.when(i + 1 < n_blk)
    def _():
        gather(i + 1, 1 - slot)

    for k in range(TOP_K):
        _wait_row_gather(y_ref, buf.at[slot, k], sem.at[slot])
    rf = rf_ref[...]
    x = x1_ref[...] + rf[:, 0:1] * buf[slot, 0] + rf[:, 1:2] * buf[slot, 1]
    inv = lax.rsqrt(jnp.mean(x * x, axis=-1, keepdims=True) + RMS_EPS)
    o_ref[...] = (x * inv) * g_ref[...]


def _combine(x1, route_f, y, dest, ln_g, tb):
    T, D = x1.shape
    g2 = ln_g.reshape(1, D)
    return pl.pallas_call(
        functools.partial(_combine_kernel, tb=tb),
        grid_spec=pltpu.PrefetchScalarGridSpec(
            num_scalar_prefetch=1,
            grid=(T // tb,),
            in_specs=[pl.BlockSpec((tb, D), lambda i, d: (i, 0)),
                      pl.BlockSpec((tb, LANES), lambda i, d: (i, 0)),
                      pl.BlockSpec((1, D), lambda i, d: (0, 0)),
                      pl.BlockSpec(memory_space=pl.ANY)],
            out_specs=pl.BlockSpec((tb, D), lambda i, d: (i, 0)),
            scratch_shapes=[pltpu.VMEM((2, TOP_K, tb, D), F32), pltpu.SemaphoreType.DMA((2,))],
        ),
        out_shape=jax.ShapeDtypeStruct((T, D), F32),
        compiler_params=_params("arbitrary"),
        name="combine",
    )(dest, x1, route_f, g2, y)


def _layer(x2, B, S, ln1_g, w_in, w2, gb, ng, w_out, ln2_g, w_group, b_group, w_expert, b_expert,
           w_gate, w_up, w_down, ln_f_g, *, tm, tq, tb_gla, tb_row, tm_e):
    q, k, v, gq, gk, gv, gr, lr = _in_proj(x2, ln1_g, w_in, tm)
    o_sb = _sb_attn(q, k, v, B, S, tq)
    o_gla = _gla(gq, gk, gv, gr, lr, w2, gb, ng, B, S, tb_gla)
    x1, h2, route_i, route_f = _out_proj(x2, o_sb, o_gla, w_out, ln2_g, w_group, b_group,
                                         w_expert, b_expert, tm)
    T = x2.shape[0]
    dest, *tables = _route_tables(route_i[:, :TOP_K], TOP_K * T, tm_e)
    slot_tok = jnp.zeros((TOP_K * T,), jnp.int32).at[dest].set(
        jnp.arange(TOP_K * T, dtype=jnp.int32) // TOP_K)
    y = _expert_ffn(h2, slot_tok, tables, w_gate, w_up, w_down, tm_e)
    return _combine(x1, route_f, y, dest, ln_f_g, tb_row)


def kernel(x, ln1_g, w_in, gla_gate_w2, gla_gate_b, gla_norm_g, w_out, ln2_g, w_group, b_group, w_expert, b_expert, exp_w_gate, exp_w_up, exp_w_down, ln_f_g):
    B, S, D = x.shape
    assert ln1_g.shape[0] == 1, "single-layer stack"
    out = _layer(x.reshape(B * S, D), B, S, ln1_g[0], w_in[0], gla_gate_w2[0], gla_gate_b[0],
                 gla_norm_g[0], w_out[0], ln2_g[0], w_group[0], b_group[0], w_expert[0], b_expert[0],
                 exp_w_gate[0], exp_w_up[0], exp_w_down[0], ln_f_g,
                 tm=512, tq=256, tb_gla=512, tb_row=256, tm_e=256)
    return out.reshape(B, S, D)
```

```python
import functools

import numpy as np
import jax
import jax.numpy as jnp
from jax import lax
from jax.experimental import pallas as pl
from jax.experimental.pallas import tpu as pltpu

SB_HEADS = 8
SB_HEAD_DIM = 64
SB_WIDTH = SB_HEADS * SB_HEAD_DIM
GLA_HEADS = 4
GLA_DK = 64
GLA_DV = 128
GLA_KW = GLA_HEADS * GLA_DK
GLA_VW = GLA_HEADS * GLA_DV
GLA_GATE_RANK = 16
GLA_GATE_TAU = 16.0
N_GROUPS = 4
EXPERTS_PER_GROUP = 8
N_EXPERTS = N_GROUPS * EXPERTS_PER_GROUP
TOP_K = 2
RMS_EPS = 1e-6
LOG2_E = 1.4426950408889634

LANES = 128
VMEM_LIMIT_BYTES = 56 * 1024 * 1024

F32 = jnp.float32
BF16 = jnp.bfloat16


def _dot(a, b):
    return jnp.dot(a, b, preferred_element_type=F32)


def _dot_nt(a, b):
    return lax.dot_general(a, b, (((1,), (1,)), ((), ())), preferred_element_type=F32)


def _softplus(z):
    return jnp.maximum(z, 0.0) + jnp.log(1.0 + jnp.exp(-jnp.abs(z)))


def _params(*sem):
    return pltpu.CompilerParams(dimension_semantics=sem, vmem_limit_bytes=VMEM_LIMIT_BYTES)


def _in_proj_kernel(x_ref, g_ref, wq, wk, wv, wgq, wgk, wgv, wgr, wlr,
                    oq, ok, ov, ogq, ogk, ogv, ogr, olr):
    x = x_ref[...]
    inv = lax.rsqrt(jnp.mean(x * x, axis=-1, keepdims=True) + RMS_EPS)
    h = ((x * inv) * g_ref[...]).astype(BF16)
    oq[...] = (_dot(h, wq[...]) * (SB_HEAD_DIM ** -0.5 * LOG2_E)).astype(oq.dtype)
    ok[...] = _dot(h, wk[...]).astype(ok.dtype)
    ov[...] = _dot(h, wv[...]).astype(ov.dtype)
    ogq[...] = (_dot(h, wgq[...]) * (GLA_DK ** -0.5)).astype(ogq.dtype)
    ogk[...] = _dot(h, wgk[...]).astype(ogk.dtype)
    ogv[...] = _dot(h, wgv[...]).astype(ogv.dtype)
    ogr[...] = _dot(h, wgr[...]).astype(ogr.dtype)
    olr[...] = _dot(h, wlr[...]).astype(olr.dtype)


def _in_proj(x2, ln_g, w_in, tm):
    T, D = x2.shape
    sizes = (SB_WIDTH, SB_WIDTH, SB_WIDTH, GLA_KW, GLA_KW, GLA_VW, GLA_VW, GLA_GATE_RANK)
    offs = np.concatenate([[0], np.cumsum(sizes)])
    ws = [w_in[:, int(offs[i]):int(offs[i + 1])].astype(BF16) for i in range(len(sizes))]
    out_dtypes = (BF16, BF16, BF16, F32, F32, BF16, F32, F32)
    row = lambda n: pl.BlockSpec((tm, n), lambda i: (i, 0))
    full = lambda n: pl.BlockSpec((D, n), lambda i: (0, 0))
    return pl.pallas_call(
        _in_proj_kernel,
        grid=(T // tm,),
        in_specs=[row(D), pl.BlockSpec((1, D), lambda i: (0, 0))] + [full(n) for n in sizes],
        out_specs=[row(n) for n in sizes],
        out_shape=[jax.ShapeDtypeStruct((T, n), dt) for n, dt in zip(sizes, out_dtypes)],
        compiler_params=_params("parallel"),
        name="in_proj",
    )(x2, ln_g.reshape(1, D), *ws)


def _softplus2(z):
    return jnp.maximum(z, 0.0) + jnp.log2(1.0 + jnp.exp2(-jnp.abs(z)))


def _sb_attn_kernel(q_ref, k_ref, v_ref, o_ref, qh_ref, acc_ref, carry_ref, *, tq, n_pairs):
    qi = pl.program_id(1)
    r = lax.broadcasted_iota(jnp.int32, (tq, tq), 0)
    c = lax.broadcasted_iota(jnp.int32, (tq, tq), 1)
    causal = c < r
    suffix = (r >= c).astype(BF16)
    head0 = lax.broadcasted_iota(jnp.int32, (1, LANES), 1) < SB_HEAD_DIM

    for p in range(n_pairs):
        q2 = q_ref[:, p * LANES:(p + 1) * LANES]
        zero = jnp.zeros_like(q2)
        qh_ref[2 * p] = jnp.where(head0, q2, zero)
        qh_ref[2 * p + 1] = jnp.where(head0, zero, q2)
    acc_ref[...] = jnp.zeros_like(acc_ref)
    carry_ref[...] = jnp.zeros_like(carry_ref)

    def key_block(kb, diag):
        rows = pl.ds(pl.multiple_of(kb * tq, tq), tq)
        for p in range(n_pairs):
            cols = pl.ds(p * LANES, LANES)
            k2 = k_ref[rows, cols]
            v2 = v_ref[rows, cols]
            vzero = jnp.zeros_like(v2)
            v_heads = (jnp.where(head0, v2, vzero), jnp.where(head0, vzero, v2))
            pv = None
            for h in range(2):
                z = _dot_nt(qh_ref[2 * p + h], k2)
                sp = _softplus2(z)
                if diag:
                    sp = jnp.where(causal, sp, 0.0)
                run = _dot(sp.astype(BF16), suffix)
                carry = carry_ref[2 * p + h]
                w = jnp.exp2(z - run - jnp.concatenate([carry] * (tq // LANES), axis=1))
                if diag:
                    w = jnp.where(causal, w, 0.0)
                d = _dot(w.astype(BF16), v_heads[h])
                pv = d if pv is None else pv + d
                carry_ref[2 * p + h] = carry + jnp.broadcast_to(run[:, 0:1], (tq, LANES))
            acc_ref[p] += pv

    key_block(qi, True)

    def body(i, _):
        key_block(qi - 1 - i, False)
        return 0

    lax.fori_loop(0, qi, body, 0)
    for p in range(n_pairs):
        o_ref[:, p * LANES:(p + 1) * LANES] = acc_ref[p].astype(o_ref.dtype)


def _sb_attn(q, k, v, B, S, tq):
    W = q.shape[-1]
    n_pairs = W // LANES
    q3, k3, v3 = (a.reshape(B, S, W) for a in (q, k, v))
    blk = pl.BlockSpec((None, tq, W), lambda b, i: (b, i, 0))
    seq = pl.BlockSpec((None, S, W), lambda b, i: (b, 0, 0))
    out = pl.pallas_call(
        functools.partial(_sb_attn_kernel, tq=tq, n_pairs=n_pairs),
        grid=(B, S // tq),
        in_specs=[blk, seq, seq],
        out_specs=blk,
        out_shape=jax.ShapeDtypeStruct((B, S, W), BF16),
        scratch_shapes=[pltpu.VMEM((2 * n_pairs, tq, LANES), BF16),
                        pltpu.VMEM((n_pairs, tq, LANES), F32),
                        pltpu.VMEM((2 * n_pairs, tq, LANES), F32)],
        compiler_params=_params("parallel", "arbitrary"),
        name="sb_attn",
    )(q3, k3, v3)
    return out.reshape(B * S, W)


GLA_CHUNK = 64
GLA_LEVELS = (32, 16, 8, 4, 2, 1)


def _gla_tables(C=GLA_CHUNK):
    t = np.arange(C)[:, None]
    j = np.arange(C)[None, :]
    blocks = [(j <= t), (j > t)]
    q_rows, k_rows, masks = [], [], []
    for half in GLA_LEVELS:
        blk = 2 * half
        u = (t // blk) * blk + half - 1
        second = (t % blk) >= half
        q_rows.append(second & (j > u) & (j <= t))
        k_rows.append((~second) & (j > t) & (j <= u))
        s = j
        masks.append(second & ((s % blk) < half) & ((s // blk) == (t // blk)))
    blocks += q_rows + k_rows
    blocks.append(np.ones((8, C), bool))
    table = np.concatenate(blocks, axis=0).astype(np.float32)
    table = np.concatenate([table, table], axis=1)
    return table, np.concatenate(masks, axis=0).astype(np.float32)


def _split_bf16(a):
    hi = a.astype(BF16)
    lo = (a - hi.astype(F32)).astype(BF16)
    return hi, lo


def _gla_kernel(gq_ref, gk_ref, gv_ref, gr_ref, lr_ref, w2_ref, gb_ref, ng_ref, tab_ref, msk_ref,
                o_ref, state_ref, *, n_chunks):
    C = GLA_CHUNK
    n_lv = len(GLA_LEVELS)

    @pl.when(pl.program_id(1) == 0)
    def _():
        state_ref[...] = jnp.zeros_like(state_ref)

    head0 = lax.broadcasted_iota(jnp.int32, (1, LANES), 1) < GLA_DK
    head_masks = (head0, jnp.logical_not(head0))
    eye = (lax.broadcasted_iota(jnp.int32, (C, C), 0) == lax.broadcasted_iota(jnp.int32, (C, C), 1))
    w2_hi, w2_lo = _split_bf16(w2_ref[...])

    def chunk(ci, _):
        rows = pl.ds(pl.multiple_of(ci * C, C), C)
        lr_hi, lr_lo = _split_bf16(lr_ref[rows, :])
        for p in range(GLA_HEADS // 2):
            cols = pl.ds(p * LANES, LANES)
            pre = (_dot(lr_hi, w2_hi[:, p * LANES:(p + 1) * LANES])
                   + _dot(lr_hi, w2_lo[:, p * LANES:(p + 1) * LANES])
                   + _dot(lr_lo, w2_hi[:, p * LANES:(p + 1) * LANES])) + gb_ref[:, cols]
            log_a = -_softplus(-pre) * (1.0 / GLA_GATE_TAU)
            la_hi, la_lo = _split_bf16(log_a)
            expo = _dot(tab_ref[...], jnp.concatenate([la_hi, la_lo], axis=0))
            ex = jnp.exp(expo)
            q2 = gq_ref[rows, cols]
            k2 = gk_ref[rows, cols]
            q_int = q2 * ex[0:C]
            k_dec = (k2 * ex[C:2 * C]).astype(BF16)
            dec_row = ex[(2 + 2 * n_lv) * C:(2 + 2 * n_lv) * C + 1]
            st = state_ref[p]
            st_bf = st.astype(BF16)
            qk = q2 * k2
            scores = [jnp.where(eye, jnp.sum(jnp.where(hm, qk, 0.0), axis=1, keepdims=True), 0.0)
                      for hm in head_masks]
            for l in range(n_lv):
                q_l = q2 * ex[(2 + l) * C:(3 + l) * C]
                k_l = (k2 * ex[(2 + n_lv + l) * C:(3 + n_lv + l) * C]).astype(BF16)
                m_l = msk_ref[l * C:(l + 1) * C, :] > 0.5
                for h, hm in enumerate(head_masks):
                    a = _dot_nt(jnp.where(hm, q_l, 0.0).astype(BF16), k_l)
                    scores[h] = scores[h] + jnp.where(m_l, a, 0.0)
            upd = []
            for h, hm in enumerate(head_masks):
                vcols = pl.ds((2 * p + h) * GLA_DV, GLA_DV)
                v_h = gv_ref[rows, vcols]
                o = (_dot_nt(jnp.where(hm, q_int, 0.0).astype(BF16), st_bf)
                     + _dot(scores[h].astype(BF16), v_h))
                inv = lax.rsqrt(jnp.mean(o * o, axis=-1, keepdims=True) + RMS_EPS)
                y = (o * inv) * ng_ref[:, vcols]
                g = gr_ref[rows, vcols]
                y = y * (g / (1.0 + jnp.exp(-g)))
                o_ref[rows, vcols] = y.astype(o_ref.dtype)
                v_t = v_h.astype(F32).T.astype(BF16)
                upd.append(_dot(v_t, k_dec))
            state_ref[p] = st * dec_row + jnp.where(head0, upd[0], upd[1])
        return 0

    lax.fori_loop(0, n_chunks, chunk, 0)


def _gla(gq, gk, gv, gr, lr, w2, gb, ng, B, S, tb):
    table, masks = _gla_tables()
    table = jnp.asarray(table, BF16)
    masks = jnp.asarray(masks, F32)
    T = B * S
    nblk = S // tb
    row = lambda n: pl.BlockSpec((tb, n), lambda b, i: (b * nblk + i, 0))
    const = lambda a: pl.BlockSpec(a.shape, lambda b, i: (0,) * a.ndim)
    w2 = w2.astype(F32)
    gb2 = gb.reshape(1, GLA_KW)
    ng2 = ng.reshape(1, GLA_VW)
    return pl.pallas_call(
        functools.partial(_gla_kernel, n_chunks=tb // GLA_CHUNK),
        grid=(B, nblk),
        in_specs=[row(GLA_KW), row(GLA_KW), row(GLA_VW), row(GLA_VW), row(GLA_GATE_RANK),
                  const(w2), const(gb2), const(ng2), const(table), const(masks)],
        out_specs=row(GLA_VW),
        out_shape=jax.ShapeDtypeStruct((T, GLA_VW), BF16),
        scratch_shapes=[pltpu.VMEM((GLA_HEADS // 2, GLA_DV, 2 * GLA_DK), F32)],
        compiler_params=_params("parallel", "arbitrary"),
        name="gla",
    )(gq, gk, gv, gr, lr, w2, gb2, ng2, table, masks)


def _out_proj_kernel(x_ref, oa_ref, ob_ref, wa_ref, wb_ref, g_ref, wr_ref, br_ref,
                     x1_ref, h2_ref, ri_ref, rf_ref):
    x1 = x_ref[...] + _dot(oa_ref[...], wa_ref[...]) + _dot(ob_ref[...], wb_ref[...])
    x1_ref[...] = x1
    inv = lax.rsqrt(jnp.mean(x1 * x1, axis=-1, keepdims=True) + RMS_EPS)
    h2 = (x1 * inv) * g_ref[...]
    h2_ref[...] = h2

    h_hi, h_lo = _split_bf16(h2)
    w_hi, w_lo = _split_bf16(wr_ref[...])
    logits = _dot(h_hi, w_hi) + _dot(h_hi, w_lo) + _dot(h_lo, w_hi) + br_ref[...]
    lane = lax.broadcasted_iota(jnp.int32, logits.shape, 1)
    neg = -jnp.inf

    def first_max(vals):
        m = jnp.max(vals, axis=1, keepdims=True)
        idx = jnp.min(jnp.where(vals == m, lane, LANES), axis=1, keepdims=True)
        return m, idx

    is_group = lane < N_GROUPS
    g_max, g_idx = first_max(jnp.where(is_group, logits, neg))
    g_p = 1.0 / jnp.sum(jnp.where(is_group, jnp.exp(logits - g_max), 0.0), axis=1, keepdims=True)
    lo_lane = N_GROUPS + EXPERTS_PER_GROUP * g_idx
    sel = jnp.where((lane >= lo_lane) & (lane < lo_lane + EXPERTS_PER_GROUP), logits, neg)
    m1, i1 = first_max(sel)
    m2, i2 = first_max(jnp.where(lane == i1, neg, sel))
    e = jnp.exp(m2 - m1)
    gate1 = g_p / (1.0 + e)
    gate2 = g_p * e / (1.0 + e)
    ri_ref[...] = jnp.where(lane == 0, i1 - N_GROUPS, jnp.where(lane == 1, i2 - N_GROUPS, 0))
    rf_ref[...] = jnp.where(lane == 0, gate1, jnp.where(lane == 1, gate2, 0.0))


def _out_proj(x2, o_sb, o_gla, w_out, ln_g, w_group, b_group, w_expert, b_expert, tm):
    T, D = x2.shape
    wa = w_out[:SB_WIDTH].astype(BF16)
    wb = w_out[SB_WIDTH:].astype(BF16)
    n_r = N_GROUPS + N_EXPERTS
    wr = jnp.zeros((D, LANES), F32).at[:, :n_r].set(jnp.concatenate([w_group, w_expert], axis=1))
    br = jnp.zeros((1, LANES), F32).at[0, :n_r].set(jnp.concatenate([b_group, b_expert]))
    row = lambda n: pl.BlockSpec((tm, n), lambda i: (i, 0))
    const = lambda a: pl.BlockSpec(a.shape, lambda i: (0,) * a.ndim)
    g2 = ln_g.reshape(1, D)
    return pl.pallas_call(
        _out_proj_kernel,
        grid=(T // tm,),
        in_specs=[row(D), row(SB_WIDTH), row(GLA_VW), const(wa), const(wb), const(g2), const(wr), const(br)],
        out_specs=[row(D), row(D), row(LANES), row(LANES)],
        out_shape=[jax.ShapeDtypeStruct((T, D), F32), jax.ShapeDtypeStruct((T, D), F32),
                   jax.ShapeDtypeStruct((T, LANES), jnp.int32), jax.ShapeDtypeStruct((T, LANES), F32)],
        compiler_params=_params("parallel"),
        name="out_proj",
    )(x2, o_sb, o_gla, wa, wb, g2, wr, br)


def _route_tables(expert_id, n_rows, tm):
    e_flat = expert_id.reshape(-1)
    onehot = (e_flat[:, None] == jnp.arange(N_EXPERTS, dtype=jnp.int32)[None, :]).astype(jnp.int32)
    cum = jnp.cumsum(onehot, axis=0)
    counts = cum[-1]
    rank = jnp.take_along_axis(cum, e_flat[:, None], axis=1)[:, 0] - 1
    ends = jnp.cumsum(counts)
    starts = ends - counts
    dest = (starts[e_flat] + rank).astype(jnp.int32)

    n_blocks = n_rows // tm
    n_items = n_blocks + N_EXPERTS - 1
    first_blk = starts // tm
    last_blk = jnp.maximum(ends - 1, starts) // tm
    items_e = jnp.where(counts > 0, last_blk - first_blk + 1, 0)
    item_end = jnp.cumsum(items_e)
    item_start = item_end - items_e
    w = jnp.arange(n_items, dtype=jnp.int32)
    total = item_end[-1]
    valid = w < total
    wc = jnp.minimum(w, total - 1)
    it_e = jnp.sum(item_end[None, :] <= wc[:, None], axis=1).astype(jnp.int32)
    it_b = (first_blk[it_e] + (wc - item_start[it_e])).astype(jnp.int32)
    lo = jnp.clip(starts[it_e] - it_b * tm, 0, tm)
    hi = jnp.clip(ends[it_e] - it_b * tm, 0, tm)
    hi = jnp.where(valid, hi, lo)
    prev_b = jnp.concatenate([jnp.full((1,), -1, jnp.int32), it_b[:-1]])
    first = (it_b != prev_b).astype(jnp.int32)
    return dest, it_b, it_e, lo.astype(jnp.int32), hi.astype(jnp.int32), first


def _row_copy(src_ref, dst_ref, sem, i, j):
    return pltpu.make_async_copy(src_ref.at[pl.ds(i, 1), :], dst_ref.at[pl.ds(j, 1), :], sem)


ROW_ISSUE_UNROLL = 8


def _start_row_gather(src_ref, idx_ref, idx_base, idx_stride, dst_ref, sem):
    def body(r, _):
        _row_copy(src_ref, dst_ref, sem, idx_ref[idx_base + r * idx_stride], r).start()
        return 0
    lax.fori_loop(0, dst_ref.shape[0], body, 0, unroll=ROW_ISSUE_UNROLL)


def _wait_row_gather(src_ref, dst_ref, sem):
    pltpu.make_async_copy(src_ref.at[pl.ds(0, dst_ref.shape[0]), :], dst_ref, sem).wait()


def _expert_kernel(blk_ref, exp_ref, lo_ref, hi_ref, first_ref, tok_ref,
                   h_ref, wg_ref, wu_ref, wd_ref, y_ref, xbuf, sem, *, tm):
    w = pl.program_id(0)
    n_items = pl.num_programs(0)
    slot = w % 2

    def gather(item, s):
        _start_row_gather(h_ref, tok_ref, blk_ref[item] * tm, 1, xbuf.at[s], sem.at[s])

    @pl.when(w == 0)
    def _():
        gather(0, 0)

    @pl.when(w + 1 < n_items)
    def _():
        gather(w + 1, 1 - slot)

    _wait_row_gather(h_ref, xbuf.at[slot], sem.at[slot])
    xb = xbuf[slot].astype(BF16)
    g = _dot(xb, wg_ref[...].astype(BF16))
    u = _dot(xb, wu_ref[...].astype(BF16))
    hidden = (g / (1.0 + jnp.exp(-g))) * u
    y = _dot(hidden.astype(BF16), wd_ref[...].astype(BF16))
    r = lax.broadcasted_iota(jnp.int32, y.shape, 0)
    y = jnp.where((r >= lo_ref[w]) & (r < hi_ref[w]), y, 0.0)

    @pl.when(first_ref[w] == 1)
    def _():
        y_ref[...] = y

    @pl.when(first_ref[w] == 0)
    def _():
        y_ref[...] += y


def _expert_ffn(h2, slot_tok, tables, w_gate, w_up, w_down, tm):
    T, D = h2.shape
    N = slot_tok.shape[0]
    F = w_gate.shape[-1]
    it_b, it_e, lo, hi, first = tables
    n_items = it_b.shape[0]
    wspec = lambda shape: pl.BlockSpec((None,) + shape, lambda w, b, e, *_: (e[w], 0, 0))
    return pl.pallas_call(
        functools.partial(_expert_kernel, tm=tm),
        grid_spec=pltpu.PrefetchScalarGridSpec(
            num_scalar_prefetch=6,
            grid=(n_items,),
            in_specs=[pl.BlockSpec(memory_space=pl.ANY), wspec((D, F)), wspec((D, F)), wspec((F, D))],
            out_specs=pl.BlockSpec((tm, D), lambda w, b, *_: (b[w], 0)),
            scratch_shapes=[pltpu.VMEM((2, tm, D), h2.dtype), pltpu.SemaphoreType.DMA((2,))],
        ),
        out_shape=jax.ShapeDtypeStruct((N, D), F32),
        compiler_params=_params("arbitrary"),
        name="expert_ffn",
    )(it_b, it_e, lo, hi, first, slot_tok, h2, w_gate, w_up, w_down)


def _combine_kernel(dest_ref, x1_ref, rf_ref, g_ref, y_ref, o_ref, buf, sem, *, tb):
    i = pl.program_id(0)
    n_blk = pl.num_programs(0)
    slot = i % 2

    def gather(blk, s):
        for k in range(TOP_K):
            _start_row_gather(y_ref, dest_ref, TOP_K * blk * tb + k, TOP_K, buf.at[s, k], sem.at[s])

    @pl.when(i == 0)
    def _():
        gather(0, 0)

    @pl.when(i + 1 < n_blk)
    def _():
        gather(i + 1, 1 - slot)

    for k in range(TOP_K):
        _wait_row_gather(y_ref, buf.at[slot, k], sem.at[slot])
    rf = rf_ref[...]
    x = x1_ref[...] + rf[:, 0:1] * buf[slot, 0] + rf[:, 1:2] * buf[slot, 1]
    inv = lax.rsqrt(jnp.mean(x * x, axis=-1, keepdims=True) + RMS_EPS)
    o_ref[...] = (x * inv) * g_ref[...]


def _combine(x1, route_f, y, dest, ln_g, tb):
    T, D = x1.shape
    g2 = ln_g.reshape(1, D)
    return pl.pallas_call(
        functools.partial(_combine_kernel, tb=tb),
        grid_spec=pltpu.PrefetchScalarGridSpec(
            num_scalar_prefetch=1,
            grid=(T // tb,),
            in_specs=[pl.BlockSpec((tb, D), lambda i, d: (i, 0)),
                      pl.BlockSpec((tb, LANES), lambda i, d: (i, 0)),
                      pl.BlockSpec((1, D), lambda i, d: (0, 0)),
                      pl.BlockSpec(memory_space=pl.ANY)],
            out_specs=pl.BlockSpec((tb, D), lambda i, d: (i, 0)),
            scratch_shapes=[pltpu.VMEM((2, TOP_K, tb, D), F32), pltpu.SemaphoreType.DMA((2,))],
        ),
        out_shape=jax.ShapeDtypeStruct((T, D), F32),
        compiler_params=_params("arbitrary"),
        name="combine",
    )(dest, x1, route_f, g2, y)


def _layer(x2, B, S, ln1_g, w_in, w2, gb, ng, w_out, ln2_g, w_group, b_group, w_expert, b_expert,
           w_gate, w_up, w_down, ln_f_g, *, tm, tq, tb_gla, tb_row, tm_e):
    q, k, v, gq, gk, gv, gr, lr = _in_proj(x2, ln1_g, w_in, tm)
    o_sb = _sb_attn(q, k, v, B, S, tq)
    o_gla = _gla(gq, gk, gv, gr, lr, w2, gb, ng, B, S, tb_gla)
    x1, h2, route_i, route_f = _out_proj(x2, o_sb, o_gla, w_out, ln2_g, w_group, b_group,
                                         w_expert, b_expert, tm)
    T = x2.shape[0]
    dest, *tables = _route_tables(route_i[:, :TOP_K], TOP_K * T, tm_e)
    slot_tok = jnp.zeros((TOP_K * T,), jnp.int32).at[dest].set(
        jnp.arange(TOP_K * T, dtype=jnp.int32) // TOP_K)
    y = _expert_ffn(h2, slot_tok, tables, w_gate, w_up, w_down, tm_e)
    return _combine(x1, route_f, y, dest, ln_f_g, tb_row)


def kernel(x, ln1_g, w_in, gla_gate_w2, gla_gate_b, gla_norm_g, w_out, ln2_g, w_group, b_group, w_expert, b_expert, exp_w_gate, exp_w_up, exp_w_down, ln_f_g):
    B, S, D = x.shape
    assert ln1_g.shape[0] == 1, "single-layer stack"
    out = _layer(x.reshape(B * S, D), B, S, ln1_g[0], w_in[0], gla_gate_w2[0], gla_gate_b[0],
                 gla_norm_g[0], w_out[0], ln2_g[0], w_group[0], b_group[0], w_expert[0], b_expert[0],
                 exp_w_gate[0], exp_w_up[0], exp_w_down[0], ln_f_g,
                 tm=512, tq=256, tb_gla=512, tb_row=256, tm_e=256)
    return out.reshape(B, S, D)
```

```python
import functools

import numpy as np
import jax
import jax.numpy as jnp
from jax import lax
from jax.experimental import pallas as pl
from jax.experimental.pallas import tpu as pltpu

SB_HEADS = 8
SB_HEAD_DIM = 64
SB_WIDTH = SB_HEADS * SB_HEAD_DIM
GLA_HEADS = 4
GLA_DK = 64
GLA_DV = 128
GLA_KW = GLA_HEADS * GLA_DK
GLA_VW = GLA_HEADS * GLA_DV
GLA_GATE_RANK = 16
GLA_GATE_TAU = 16.0
N_GROUPS = 4
EXPERTS_PER_GROUP = 8
N_EXPERTS = N_GROUPS * EXPERTS_PER_GROUP
TOP_K = 2
RMS_EPS = 1e-6
LOG2_E = 1.4426950408889634

LANES = 128
VMEM_LIMIT_BYTES = 56 * 1024 * 1024

F32 = jnp.float32
BF16 = jnp.bfloat16


def _dot(a, b):
    return jnp.dot(a, b, preferred_element_type=F32)


def _dot_nt(a, b):
    return lax.dot_general(a, b, (((1,), (1,)), ((), ())), preferred_element_type=F32)


def _softplus(z):
    return jnp.maximum(z, 0.0) + jnp.log(1.0 + jnp.exp(-jnp.abs(z)))


def _params(*sem):
    return pltpu.CompilerParams(dimension_semantics=sem, vmem_limit_bytes=VMEM_LIMIT_BYTES)


def _in_proj_kernel(x_ref, g_ref, wq, wk, wv, wgq, wgk, wgv, wgr, wlr,
                    oq, ok, ov, ogq, ogk, ogv, ogr, olr):
    x = x_ref[...]
    inv = lax.rsqrt(jnp.mean(x * x, axis=-1, keepdims=True) + RMS_EPS)
    h = ((x * inv) * g_ref[...]).astype(BF16)
    oq[...] = (_dot(h, wq[...]) * (SB_HEAD_DIM ** -0.5 * LOG2_E)).astype(oq.dtype)
    ok[...] = _dot(h, wk[...]).astype(ok.dtype)
    ov[...] = _dot(h, wv[...]).astype(ov.dtype)
    ogq[...] = (_dot(h, wgq[...]) * (GLA_DK ** -0.5)).astype(ogq.dtype)
    ogk[...] = _dot(h, wgk[...]).astype(ogk.dtype)
    ogv[...] = _dot(h, wgv[...]).astype(ogv.dtype)
    ogr[...] = _dot(h, wgr[...]).astype(ogr.dtype)
    olr[...] = _dot(h, wlr[...]).astype(olr.dtype)


def _in_proj(x2, ln_g, w_in, tm):
    T, D = x2.shape
    sizes = (SB_WIDTH, SB_WIDTH, SB_WIDTH, GLA_KW, GLA_KW, GLA_VW, GLA_VW, GLA_GATE_RANK)
    offs = np.concatenate([[0], np.cumsum(sizes)])
    ws = [w_in[:, int(offs[i]):int(offs[i + 1])].astype(BF16) for i in range(len(sizes))]
    out_dtypes = (BF16, BF16, BF16, F32, F32, BF16, F32, F32)
    row = lambda n: pl.BlockSpec((tm, n), lambda i: (i, 0))
    full = lambda n: pl.BlockSpec((D, n), lambda i: (0, 0))
    return pl.pallas_call(
        _in_proj_kernel,
        grid=(T // tm,),
        in_specs=[row(D), pl.BlockSpec((1, D), lambda i: (0, 0))] + [full(n) for n in sizes],
        out_specs=[row(n) for n in sizes],
        out_shape=[jax.ShapeDtypeStruct((T, n), dt) for n, dt in zip(sizes, out_dtypes)],
        compiler_params=_params("parallel"),
        name="in_proj",
    )(x2, ln_g.reshape(1, D), *ws)


def _softplus2(z):
    return jnp.maximum(z, 0.0) + jnp.log2(1.0 + jnp.exp2(-jnp.abs(z)))


def _sb_attn_kernel(q_ref, k_ref, v_ref, o_ref, qh_ref, acc_ref, carry_ref, *, tq, n_pairs):
    qi = pl.program_id(1)
    r = lax.broadcasted_iota(jnp.int32, (tq, tq), 0)
    c = lax.broadcasted_iota(jnp.int32, (tq, tq), 1)
    causal = c < r
    suffix = (r >= c).astype(BF16)
    head0 = lax.broadcasted_iota(jnp.int32, (1, LANES), 1) < SB_HEAD_DIM

    for p in range(n_pairs):
        q2 = q_ref[:, p * LANES:(p + 1) * LANES]
        zero = jnp.zeros_like(q2)
        qh_ref[2 * p] = jnp.where(head0, q2, zero)
        qh_ref[2 * p + 1] = jnp.where(head0, zero, q2)
    acc_ref[...] = jnp.zeros_like(acc_ref)
    carry_ref[...] = jnp.zeros_like(carry_ref)

    def key_block(kb, diag):
        rows = pl.ds(pl.multiple_of(kb * tq, tq), tq)
        for p in range(n_pairs):
            cols = pl.ds(p * LANES, LANES)
            k2 = k_ref[rows, cols]
            v2 = v_ref[rows, cols]
            vzero = jnp.zeros_like(v2)
            v_heads = (jnp.where(head0, v2, vzero), jnp.where(head0, vzero, v2))
            pv = None
            for h in range(2):
                z = _dot_nt(qh_ref[2 * p + h], k2)
                sp = _softplus2(z)
                if diag:
                    sp = jnp.where(causal, sp, 0.0)
                run = _dot(sp.astype(BF16), suffix)
                carry = carry_ref[2 * p + h]
                w = jnp.exp2(z - run - jnp.concatenate([carry] * (tq // LANES), axis=1))
                if diag:
                    w = jnp.where(causal, w, 0.0)
                d = _dot(w.astype(BF16), v_heads[h])
                pv = d if pv is None else pv + d
                carry_ref[2 * p + h] = carry + jnp.broadcast_to(run[:, 0:1], (tq, LANES))
            acc_ref[p] += pv

    key_block(qi, True)

    def body(i, _):
        key_block(qi - 1 - i, False)
        return 0

    lax.fori_loop(0, qi, body, 0)
    for p in range(n_pairs):
        o_ref[:, p * LANES:(p + 1) * LANES] = acc_ref[p].astype(o_ref.dtype)


def _sb_attn(q, k, v, B, S, tq):
    W = q.shape[-1]
    n_pairs = W // LANES
    q3, k3, v3 = (a.reshape(B, S, W) for a in (q, k, v))
    blk = pl.BlockSpec((None, tq, W), lambda b, i: (b, i, 0))
    seq = pl.BlockSpec((None, S, W), lambda b, i: (b, 0, 0))
    out = pl.pallas_call(
        functools.partial(_sb_attn_kernel, tq=tq, n_pairs=n_pairs),
        grid=(B, S // tq),
        in_specs=[blk, seq, seq],
        out_specs=blk,
        out_shape=jax.ShapeDtypeStruct((B, S, W), BF16),
        scratch_shapes=[pltpu.VMEM((2 * n_pairs, tq, LANES), BF16),
                        pltpu.VMEM((n_pairs, tq, LANES), F32),
                        pltpu.VMEM((2 * n_pairs, tq, LANES), F32)],
        compiler_params=_params("parallel", "arbitrary"),
        name="sb_attn",
    )(q3, k3, v3)
    return out.reshape(B * S, W)


GLA_CHUNK = 64
GLA_LEVELS = (32, 16, 8, 4, 2, 1)


def _gla_tables(C=GLA_CHUNK):
    t = np.arange(C)[:, None]
    j = np.arange(C)[None, :]
    blocks = [(j <= t), (j > t)]
    q_rows, k_rows, masks = [], [], []
    for half in GLA_LEVELS:
        blk = 2 * half
        u = (t // blk) * blk + half - 1
        second = (t % blk) >= half
        q_rows.append(second & (j > u) & (j <= t))
        k_rows.append((~second) & (j > t) & (j <= u))
        s = j
        masks.append(second & ((s % blk) < half) & ((s // blk) == (t // blk)))
    blocks += q_rows + k_rows
    blocks.append(np.ones((8, C), bool))
    table = np.concatenate(blocks, axis=0).astype(np.float32)
    table = np.concatenate([table, table], axis=1)
    return table, np.concatenate(masks, axis=0).astype(np.float32)


def _split_bf16(a):
    hi = a.astype(BF16)
    lo = (a - hi.astype(F32)).astype(BF16)
    return hi, lo


def _gla_kernel(gq_ref, gk_ref, gv_ref, gr_ref, lr_ref, w2_ref, gb_ref, ng_ref, tab_ref, msk_ref,
                o_ref, state_ref, *, n_chunks):
    C = GLA_CHUNK
    n_lv = len(GLA_LEVELS)

    @pl.when(pl.program_id(1) == 0)
    def _():
        state_ref[...] = jnp.zeros_like(state_ref)

    head0 = lax.broadcasted_iota(jnp.int32, (1, LANES), 1) < GLA_DK
    head_masks = (head0, jnp.logical_not(head0))
    eye = (lax.broadcasted_iota(jnp.int32, (C, C), 0) == lax.broadcasted_iota(jnp.int32, (C, C), 1))
    w2_hi, w2_lo = _split_bf16(w2_ref[...])

    def chunk(ci, _):
        rows = pl.ds(pl.multiple_of(ci * C, C), C)
        lr_hi, lr_lo = _split_bf16(lr_ref[rows, :])
        for p in range(GLA_HEADS // 2):
            cols = pl.ds(p * LANES, LANES)
            pre = (_dot(lr_hi, w2_hi[:, p * LANES:(p + 1) * LANES])
                   + _dot(lr_hi, w2_lo[:, p * LANES:(p + 1) * LANES])
                   + _dot(lr_lo, w2_hi[:, p * LANES:(p + 1) * LANES])) + gb_ref[:, cols]
            log_a = -_softplus(-pre) * (1.0 / GLA_GATE_TAU)
            la_hi, la_lo = _split_bf16(log_a)
            expo = _dot(tab_ref[...], jnp.concatenate([la_hi, la_lo], axis=0))
            ex = jnp.exp(expo)
            q2 = gq_ref[rows, cols]
            k2 = gk_ref[rows, cols]
            q_int = q2 * ex[0:C]
            k_dec = (k2 * ex[C:2 * C]).astype(BF16)
            dec_row = ex[(2 + 2 * n_lv) * C:(2 + 2 * n_lv) * C + 1]
            st = state_ref[p]
            st_bf = st.astype(BF16)
            qk = q2 * k2
            scores = [jnp.where(eye, jnp.sum(jnp.where(hm, qk, 0.0), axis=1, keepdims=True), 0.0)
                      for hm in head_masks]
            for l in range(n_lv):
                q_l = q2 * ex[(2 + l) * C:(3 + l) * C]
                k_l = (k2 * ex[(2 + n_lv + l) * C:(3 + n_lv + l) * C]).astype(BF16)
                m_l = msk_ref[l * C:(l + 1) * C, :] > 0.5
                for h, hm in enumerate(head_masks):
                    a = _dot_nt(jnp.where(hm, q_l, 0.0).astype(BF16), k_l)
                    scores[h] = scores[h] + jnp.where(m_l, a, 0.0)
            upd = []
            for h, hm in enumerate(head_masks):
                vcols = pl.ds((2 * p + h) * GLA_DV, GLA_DV)
                v_h = gv_ref[rows, vcols]
                o = (_dot_nt(jnp.where(hm, q_int, 0.0).astype(BF16), st_bf)
                     + _dot(scores[h].astype(BF16), v_h))
                inv = lax.rsqrt(jnp.mean(o * o, axis=-1, keepdims=True) + RMS_EPS)
                y = (o * inv) * ng_ref[:, vcols]
                g = gr_ref[rows, vcols]
                y = y * (g / (1.0 + jnp.exp(-g)))
                o_ref[rows, vcols] = y.astype(o_ref.dtype)
                v_t = v_h.astype(F32).T.astype(BF16)
                upd.append(_dot(v_t, k_dec))
            state_ref[p] = st * dec_row + jnp.where(head0, upd[0], upd[1])
        return 0

    lax.fori_loop(0, n_chunks, chunk, 0)


def _gla(gq, gk, gv, gr, lr, w2, gb, ng, B, S, tb):
    table, masks = _gla_tables()
    table = jnp.asarray(table, BF16)
    masks = jnp.asarray(masks, F32)
    T = B * S
    nblk = S // tb
    row = lambda n: pl.BlockSpec((tb, n), lambda b, i: (b * nblk + i, 0))
    const = lambda a: pl.BlockSpec(a.shape, lambda b, i: (0,) * a.ndim)
    w2 = w2.astype(F32)
    gb2 = gb.reshape(1, GLA_KW)
    ng2 = ng.reshape(1, GLA_VW)
    return pl.pallas_call(
        functools.partial(_gla_kernel, n_chunks=tb // GLA_CHUNK),
        grid=(B, nblk),
        in_specs=[row(GLA_KW), row(GLA_KW), row(GLA_VW), row(GLA_VW), row(GLA_GATE_RANK),
                  const(w2), const(gb2), const(ng2), const(table), const(masks)],
        out_specs=row(GLA_VW),
        out_shape=jax.ShapeDtypeStruct((T, GLA_VW), BF16),
        scratch_shapes=[pltpu.VMEM((GLA_HEADS // 2, GLA_DV, 2 * GLA_DK), F32)],
        compiler_params=_params("parallel", "arbitrary"),
        name="gla",
    )(gq, gk, gv, gr, lr, w2, gb2, ng2, table, masks)


SUBLANES = 8


def _store_token_tiles(ref, value, accumulate=False):
    n = value.shape[0]
    for s in range(SUBLANES):
        rows = pl.ds(s, n, stride=SUBLANES)
        chunk = value[:, s * LANES:(s + 1) * LANES]
        ref[rows, :] = ref[rows, :] + chunk if accumulate else chunk


def _load_token_tiles(ref, n):
    return jnp.concatenate([ref[pl.ds(s, n, stride=SUBLANES), :] for s in range(SUBLANES)], axis=1)


def _out_proj_kernel(x_ref, oa_ref, ob_ref, wa_ref, wb_ref, g_ref, wr_ref, br_ref,
                     x1_ref, h2_ref, ri_ref, rf_ref):
    x1 = x_ref[...] + _dot(oa_ref[...], wa_ref[...]) + _dot(ob_ref[...], wb_ref[...])
    x1_ref[...] = x1
    inv = lax.rsqrt(jnp.mean(x1 * x1, axis=-1, keepdims=True) + RMS_EPS)
    h2 = (x1 * inv) * g_ref[...]
    _store_token_tiles(h2_ref, h2)

    h_hi, h_lo = _split_bf16(h2)
    w_hi, w_lo = _split_bf16(wr_ref[...])
    logits = _dot(h_hi, w_hi) + _dot(h_hi, w_lo) + _dot(h_lo, w_hi) + br_ref[...]
    lane = lax.broadcasted_iota(jnp.int32, logits.shape, 1)
    neg = -jnp.inf

    def first_max(vals):
        m = jnp.max(vals, axis=1, keepdims=True)
        idx = jnp.min(jnp.where(vals == m, lane, LANES), axis=1, keepdims=True)
        return m, idx

    is_group = lane < N_GROUPS
    g_max, g_idx = first_max(jnp.where(is_group, logits, neg))
    g_p = 1.0 / jnp.sum(jnp.where(is_group, jnp.exp(logits - g_max), 0.0), axis=1, keepdims=True)
    lo_lane = N_GROUPS + EXPERTS_PER_GROUP * g_idx
    sel = jnp.where((lane >= lo_lane) & (lane < lo_lane + EXPERTS_PER_GROUP), logits, neg)
    m1, i1 = first_max(sel)
    m2, i2 = first_max(jnp.where(lane == i1, neg, sel))
    e = jnp.exp(m2 - m1)
    gate1 = g_p / (1.0 + e)
    gate2 = g_p * e / (1.0 + e)
    ri_ref[...] = jnp.where(lane == 0, i1 - N_GROUPS, jnp.where(lane == 1, i2 - N_GROUPS, 0))
    rf_ref[...] = jnp.where(lane == 0, gate1, jnp.where(lane == 1, gate2, 0.0))


def _out_proj(x2, o_sb, o_gla, w_out, ln_g, w_group, b_group, w_expert, b_expert, tm):
    T, D = x2.shape
    wa = w_out[:SB_WIDTH].astype(BF16)
    wb = w_out[SB_WIDTH:].astype(BF16)
    n_r = N_GROUPS + N_EXPERTS
    wr = jnp.zeros((D, LANES), F32).at[:, :n_r].set(jnp.concatenate([w_group, w_expert], axis=1))
    br = jnp.zeros((1, LANES), F32).at[0, :n_r].set(jnp.concatenate([b_group, b_expert]))
    row = lambda n: pl.BlockSpec((tm, n), lambda i: (i, 0))
    const = lambda a: pl.BlockSpec(a.shape, lambda i: (0,) * a.ndim)
    g2 = ln_g.reshape(1, D)
    return pl.pallas_call(
        _out_proj_kernel,
        grid=(T // tm,),
        in_specs=[row(D), row(SB_WIDTH), row(GLA_VW), const(wa), const(wb), const(g2), const(wr), const(br)],
        out_specs=[row(D), pl.BlockSpec((tm * SUBLANES, LANES), lambda i: (i, 0)), row(LANES), row(LANES)],
        out_shape=[jax.ShapeDtypeStruct((T, D), F32), jax.ShapeDtypeStruct((T * SUBLANES, LANES), F32),
                   jax.ShapeDtypeStruct((T, LANES), jnp.int32), jax.ShapeDtypeStruct((T, LANES), F32)],
        compiler_params=_params("parallel"),
        name="out_proj",
    )(x2, o_sb, o_gla, wa, wb, g2, wr, br)


def _route_tables(expert_id, n_rows, tm):
    e_flat = expert_id.reshape(-1)
    onehot = (e_flat[:, None] == jnp.arange(N_EXPERTS, dtype=jnp.int32)[None, :]).astype(jnp.int32)
    cum = jnp.cumsum(onehot, axis=0)
    counts = cum[-1]
    rank = jnp.take_along_axis(cum, e_flat[:, None], axis=1)[:, 0] - 1
    ends = jnp.cumsum(counts)
    starts = ends - counts
    dest = (starts[e_flat] + rank).astype(jnp.int32)

    n_blocks = n_rows // tm
    n_items = n_blocks + N_EXPERTS - 1
    first_blk = starts // tm
    last_blk = jnp.maximum(ends - 1, starts) // tm
    items_e = jnp.where(counts > 0, last_blk - first_blk + 1, 0)
    item_end = jnp.cumsum(items_e)
    item_start = item_end - items_e
    w = jnp.arange(n_items, dtype=jnp.int32)
    total = item_end[-1]
    valid = w < total
    wc = jnp.minimum(w, total - 1)
    it_e = jnp.sum(item_end[None, :] <= wc[:, None], axis=1).astype(jnp.int32)
    it_b = (first_blk[it_e] + (wc - item_start[it_e])).astype(jnp.int32)
    lo = jnp.clip(starts[it_e] - it_b * tm, 0, tm)
    hi = jnp.clip(ends[it_e] - it_b * tm, 0, tm)
    hi = jnp.where(valid, hi, lo)
    prev_b = jnp.concatenate([jnp.full((1,), -1, jnp.int32), it_b[:-1]])
    first = (it_b != prev_b).astype(jnp.int32)
    return dest, it_b, it_e, lo.astype(jnp.int32), hi.astype(jnp.int32), first


ROW_ISSUE_UNROLL = 8


def _token_tile(ref, t):
    return ref.at[pl.ds(pl.multiple_of(t * SUBLANES, SUBLANES), SUBLANES), :]


def _start_token_gather(src_ref, idx_ref, idx_base, idx_stride, dst_ref, sem, n):
    def body(r, _):
        t = idx_ref[idx_base + r * idx_stride]
        pltpu.make_async_copy(_token_tile(src_ref, t), _token_tile(dst_ref, r), sem).start()
        return 0
    lax.fori_loop(0, n, body, 0, unroll=ROW_ISSUE_UNROLL)


def _wait_token_gather(src_ref, dst_ref, sem):
    pltpu.make_async_copy(src_ref.at[pl.ds(0, dst_ref.shape[0]), :], dst_ref, sem).wait()


def _expert_kernel(blk_ref, exp_ref, lo_ref, hi_ref, first_ref, dest_ref,
                   h_ref, wg_ref, wu_ref, wd_ref, y_ref, xbuf, tok_ref, sem, *, tm):
    w = pl.program_id(0)
    n_items = pl.num_programs(0)
    blk = blk_ref[w]

    def gather(b):
        _start_token_gather(h_ref, tok_ref, b * tm, 1, xbuf.at[b % 2], sem.at[b % 2], tm)

    @pl.when(w == 0)
    def _():
        def invert(t, _):
            for k in range(TOP_K):
                tok_ref[dest_ref[TOP_K * t + k]] = t
            return 0
        lax.fori_loop(0, tok_ref.shape[0] // TOP_K, invert, 0, unroll=ROW_ISSUE_UNROLL)
        gather(blk)

    nxt = jnp.minimum(w + 1, n_items - 1)

    @pl.when((w + 1 < n_items) & (first_ref[nxt] == 1))
    def _():
        gather(blk_ref[nxt])

    @pl.when(first_ref[w] == 1)
    def _():
        _wait_token_gather(h_ref, xbuf.at[blk % 2], sem.at[blk % 2])

    xb = _load_token_tiles(xbuf.at[blk % 2], tm).astype(BF16)
    g = _dot(xb, wg_ref[...].astype(BF16))
    u = _dot(xb, wu_ref[...].astype(BF16))
    hidden = (g / (1.0 + jnp.exp(-g))) * u
    y = _dot(hidden.astype(BF16), wd_ref[...].astype(BF16))
    r = lax.broadcasted_iota(jnp.int32, y.shape, 0)
    y = jnp.where((r >= lo_ref[w]) & (r < hi_ref[w]), y, 0.0)

    @pl.when(first_ref[w] == 1)
    def _():
        _store_token_tiles(y_ref, y)

    @pl.when(first_ref[w] == 0)
    def _():
        _store_token_tiles(y_ref, y, accumulate=True)


def _expert_ffn(h2_tiles, dest, tables, w_gate, w_up, w_down, tm):
    N = dest.shape[0]
    _, D, F = w_gate.shape
    assert D == SUBLANES * LANES
    it_b, it_e, lo, hi, first = tables
    n_items = it_b.shape[0]
    wspec = lambda shape: pl.BlockSpec((None,) + shape, lambda w, b, e, *_: (e[w], 0, 0))
    return pl.pallas_call(
        functools.partial(_expert_kernel, tm=tm),
        grid_spec=pltpu.PrefetchScalarGridSpec(
            num_scalar_prefetch=6,
            grid=(n_items,),
            in_specs=[pl.BlockSpec(memory_space=pl.ANY), wspec((D, F)), wspec((D, F)), wspec((F, D))],
            out_specs=pl.BlockSpec((tm * SUBLANES, LANES), lambda w, b, *_: (b[w], 0)),
            scratch_shapes=[pltpu.VMEM((2, tm * SUBLANES, LANES), F32),
                            pltpu.SMEM((N,), jnp.int32),
                            pltpu.SemaphoreType.DMA((2,))],
        ),
        out_shape=jax.ShapeDtypeStruct((N * SUBLANES, LANES), F32),
        compiler_params=_params("arbitrary"),
        name="expert_ffn",
    )(it_b, it_e, lo, hi, first, dest, h2_tiles, w_gate, w_up, w_down)


def _combine_kernel(dest_ref, x1_ref, rf_ref, g_ref, y_ref, o_ref, buf, sem, *, tb):
    i = pl.program_id(0)
    n_blk = pl.num_programs(0)
    slot = i % 2

    def gather(blk, s):
        for k in range(TOP_K):
            _start_token_gather(y_ref, dest_ref, TOP_K * blk * tb + k, TOP_K, buf.at[s, k], sem.at[s], tb)

    @pl.when(i == 0)
    def _():
        gather(0, 0)

    @pl.when(i + 1 < n_blk)
    def _():
        gather(i + 1, 1 - slot)

    for k in range(TOP_K):
        _wait_token_gather(y_ref, buf.at[slot, k], sem.at[slot])
    rf = rf_ref[...]
    x = (x1_ref[...] + rf[:, 0:1] * _load_token_tiles(buf.at[slot, 0], tb)
         + rf[:, 1:2] * _load_token_tiles(buf.at[slot, 1], tb))
    inv = lax.rsqrt(jnp.mean(x * x, axis=-1, keepdims=True) + RMS_EPS)
    o_ref[...] = (x * inv) * g_ref[...]


def _combine(x1, route_f, y_tiles, dest, ln_g, tb):
    T, D = x1.shape
    g2 = ln_g.reshape(1, D)
    return pl.pallas_call(
        functools.partial(_combine_kernel, tb=tb),
        grid_spec=pltpu.PrefetchScalarGridSpec(
            num_scalar_prefetch=1,
            grid=(T // tb,),
            in_specs=[pl.BlockSpec((tb, D), lambda i, d: (i, 0)),
                      pl.BlockSpec((tb, LANES), lambda i, d: (i, 0)),
                      pl.BlockSpec((1, D), lambda i, d: (0, 0)),
                      pl.BlockSpec(memory_space=pl.ANY)],
            out_specs=pl.BlockSpec((tb, D), lambda i, d: (i, 0)),
            scratch_shapes=[pltpu.VMEM((2, TOP_K, tb * SUBLANES, LANES), F32),
                            pltpu.SemaphoreType.DMA((2,))],
        ),
        out_shape=jax.ShapeDtypeStruct((T, D), F32),
        compiler_params=_params("arbitrary"),
        name="combine",
    )(dest, x1, route_f, g2, y_tiles)


def _layer(x2, B, S, ln1_g, w_in, w2, gb, ng, w_out, ln2_g, w_group, b_group, w_expert, b_expert,
           w_gate, w_up, w_down, ln_f_g, *, tm, tq, tb_gla, tb_row, tm_e):
    q, k, v, gq, gk, gv, gr, lr = _in_proj(x2, ln1_g, w_in, tm)
    o_sb = _sb_attn(q, k, v, B, S, tq)
    o_gla = _gla(gq, gk, gv, gr, lr, w2, gb, ng, B, S, tb_gla)
    x1, h2_tiles, route_i, route_f = _out_proj(x2, o_sb, o_gla, w_out, ln2_g, w_group, b_group,
                                               w_expert, b_expert, tm)
    T = x2.shape[0]
    dest, *tables = _route_tables(route_i[:, :TOP_K], TOP_K * T, tm_e)
    y_tiles = _expert_ffn(h2_tiles, dest, tables, w_gate, w_up, w_down, tm_e)
    return _combine(x1, route_f, y_tiles, dest, ln_f_g, tb_row)


def kernel(x, ln1_g, w_in, gla_gate_w2, gla_gate_b, gla_norm_g, w_out, ln2_g, w_group, b_group, w_expert, b_expert, exp_w_gate, exp_w_up, exp_w_down, ln_f_g):
    B, S, D = x.shape
    assert ln1_g.shape[0] == 1, "single-layer stack"
    out = _layer(x.reshape(B * S, D), B, S, ln1_g[0], w_in[0], gla_gate_w2[0], gla_gate_b[0],
                 gla_norm_g[0], w_out[0], ln2_g[0], w_group[0], b_group[0], w_expert[0], b_expert[0],
                 exp_w_gate[0], exp_w_up[0], exp_w_down[0], ln_f_g,
                 tm=512, tq=256, tb_gla=512, tb_row=256, tm_e=256)
    return out.reshape(B, S, D)
```

```python
import functools

import numpy as np
import jax
import jax.numpy as jnp
from jax import lax
from jax.experimental import pallas as pl
from jax.experimental.pallas import tpu as pltpu

SB_HEADS = 8
SB_HEAD_DIM = 64
SB_WIDTH = SB_HEADS * SB_HEAD_DIM
GLA_HEADS = 4
GLA_DK = 64
GLA_DV = 128
GLA_KW = GLA_HEADS * GLA_DK
GLA_VW = GLA_HEADS * GLA_DV
GLA_GATE_RANK = 16
GLA_GATE_TAU = 16.0
N_GROUPS = 4
EXPERTS_PER_GROUP = 8
N_EXPERTS = N_GROUPS * EXPERTS_PER_GROUP
TOP_K = 2
RMS_EPS = 1e-6
LOG2_E = 1.4426950408889634

LANES = 128
VMEM_LIMIT_BYTES = 56 * 1024 * 1024

F32 = jnp.float32
BF16 = jnp.bfloat16


def _dot(a, b):
    return jnp.dot(a, b, preferred_element_type=F32)


def _dot_nt(a, b):
    return lax.dot_general(a, b, (((1,), (1,)), ((), ())), preferred_element_type=F32)


def _softplus(z):
    return jnp.maximum(z, 0.0) + jnp.log(1.0 + jnp.exp(-jnp.abs(z)))


def _params(*sem, flags=None):
    return pltpu.CompilerParams(dimension_semantics=sem, vmem_limit_bytes=VMEM_LIMIT_BYTES, flags=flags)


def _in_proj_kernel(x_ref, g_ref, wq, wk, wv, wgq, wgk, wgv, wgr, wlr,
                    oq, ok, ov, ogq, ogk, ogv, ogr, olr):
    x = x_ref[...]
    inv = lax.rsqrt(jnp.mean(x * x, axis=-1, keepdims=True) + RMS_EPS)
    h = ((x * inv) * g_ref[...]).astype(BF16)
    oq[...] = (_dot(h, wq[...]) * (SB_HEAD_DIM ** -0.5 * LOG2_E)).astype(oq.dtype)
    ok[...] = _dot(h, wk[...]).astype(ok.dtype)
    ov[...] = _dot(h, wv[...]).astype(ov.dtype)
    ogq[...] = (_dot(h, wgq[...]) * (GLA_DK ** -0.5)).astype(ogq.dtype)
    ogk[...] = _dot(h, wgk[...]).astype(ogk.dtype)
    ogv[...] = _dot(h, wgv[...]).astype(ogv.dtype)
    ogr[...] = _dot(h, wgr[...]).astype(ogr.dtype)
    olr[...] = _dot(h, wlr[...]).astype(olr.dtype)


def _in_proj(x2, ln_g, w_in, tm):
    T, D = x2.shape
    sizes = (SB_WIDTH, SB_WIDTH, SB_WIDTH, GLA_KW, GLA_KW, GLA_VW, GLA_VW, GLA_GATE_RANK)
    offs = np.concatenate([[0], np.cumsum(sizes)])
    ws = [w_in[:, int(offs[i]):int(offs[i + 1])].astype(BF16) for i in range(len(sizes))]
    out_dtypes = (BF16, BF16, BF16, F32, F32, BF16, F32, F32)
    row = lambda n: pl.BlockSpec((tm, n), lambda i: (i, 0))
    full = lambda n: pl.BlockSpec((D, n), lambda i: (0, 0))
    return pl.pallas_call(
        _in_proj_kernel,
        grid=(T // tm,),
        in_specs=[row(D), pl.BlockSpec((1, D), lambda i: (0, 0))] + [full(n) for n in sizes],
        out_specs=[row(n) for n in sizes],
        out_shape=[jax.ShapeDtypeStruct((T, n), dt) for n, dt in zip(sizes, out_dtypes)],
        compiler_params=_params("parallel"),
        name="in_proj",
    )(x2, ln_g.reshape(1, D), *ws)


SB_ZERO_WEIGHT_BITS = 160.0


def _softplus2(z):
    one = jnp.ones((), z.dtype)
    return jnp.maximum(z, jnp.zeros((), z.dtype)) + jnp.log2(one + jnp.exp2(-jnp.abs(z)))


def _sb_attn_kernel(q_ref, k_ref, v_ref, o_ref, qh_ref, acc_ref, carry_ref, *, tq, n_pairs):
    qi = pl.program_id(1)
    r = lax.broadcasted_iota(jnp.int32, (tq, tq), 0)
    c = lax.broadcasted_iota(jnp.int32, (tq, tq), 1)
    causal = c < r
    suffix = (r >= c).astype(BF16)
    head0 = lax.broadcasted_iota(jnp.int32, (1, LANES), 1) < SB_HEAD_DIM

    for p in range(n_pairs):
        q2 = q_ref[:, p * LANES:(p + 1) * LANES]
        zero = jnp.zeros_like(q2)
        qh_ref[p, 0:tq] = jnp.where(head0, q2, zero)
        qh_ref[p, tq:2 * tq] = jnp.where(head0, zero, q2)
    acc_ref[...] = jnp.zeros_like(acc_ref)
    carry_ref[...] = jnp.zeros_like(carry_ref)
    causal2 = jnp.concatenate([causal, causal], axis=0)

    def key_block(kb, diag):
        rows = pl.ds(pl.multiple_of(kb * tq, tq), tq)
        for p in range(n_pairs):
            cols = pl.ds(p * LANES, LANES)
            k2 = k_ref[rows, cols]
            v2 = v_ref[rows, cols]
            vzero = jnp.zeros_like(v2)
            v_stack = jnp.concatenate([jnp.where(head0, v2, vzero), jnp.where(head0, vzero, v2)], axis=0)
            z = _dot_nt(qh_ref[p], k2)
            sp = _softplus2(z)
            if diag:
                sp = jnp.where(causal2, sp, 0.0)
            run = _dot(sp.astype(BF16), suffix)
            carry = carry_ref[p]
            w = jnp.exp2(z - run - jnp.concatenate([carry] * (tq // LANES), axis=1))
            if diag:
                w = jnp.where(causal2, w, 0.0)
            w = w.astype(BF16)
            acc_ref[p] += _dot(jnp.concatenate([w[0:tq], w[tq:2 * tq]], axis=1), v_stack)
            carry_ref[p] = carry + jnp.broadcast_to(run[:, 0:1], (2 * tq, LANES))

    def min_carry():
        m = carry_ref[0]
        for p in range(1, n_pairs):
            m = jnp.minimum(m, carry_ref[p])
        return jnp.min(m)

    def cond(state):
        kb, m = state
        return jnp.logical_and(kb >= 0, m < SB_ZERO_WEIGHT_BITS)

    def body(state):
        kb, _ = state
        key_block(kb, False)
        return kb - 1, min_carry()

    key_block(qi, True)
    lax.while_loop(cond, body, (qi - 1, min_carry()))
    for p in range(n_pairs):
        o_ref[:, p * LANES:(p + 1) * LANES] = acc_ref[p].astype(o_ref.dtype)


def _sb_attn(q, k, v, B, S, tq):
    W = q.shape[-1]
    n_pairs = W // LANES
    q3, k3, v3 = (a.reshape(B, S, W) for a in (q, k, v))
    blk = pl.BlockSpec((None, tq, W), lambda b, i: (b, i, 0))
    seq = pl.BlockSpec((None, S, W), lambda b, i: (b, 0, 0))
    out = pl.pallas_call(
        functools.partial(_sb_attn_kernel, tq=tq, n_pairs=n_pairs),
        grid=(B, S // tq),
        in_specs=[blk, seq, seq],
        out_specs=blk,
        out_shape=jax.ShapeDtypeStruct((B, S, W), BF16),
        scratch_shapes=[pltpu.VMEM((n_pairs, 2 * tq, LANES), BF16),
                        pltpu.VMEM((n_pairs, tq, LANES), F32),
                        pltpu.VMEM((n_pairs, 2 * tq, LANES), F32)],
        compiler_params=_params("parallel", "arbitrary"),
        name="sb_attn",
    )(q3, k3, v3)
    return out.reshape(B * S, W)


GLA_CHUNK = 64
GLA_LEVELS = (32, 16, 8, 4, 2, 1)


def _gla_tables(C=GLA_CHUNK):
    t = np.arange(C)[:, None]
    j = np.arange(C)[None, :]
    blocks = [(j <= t), (j > t)]
    q_rows, k_rows, masks = [], [], []
    for half in GLA_LEVELS:
        blk = 2 * half
        u = (t // blk) * blk + half - 1
        second = (t % blk) >= half
        q_rows.append(second & (j > u) & (j <= t))
        k_rows.append((~second) & (j > t) & (j <= u))
        s = j
        masks.append(second & ((s % blk) < half) & ((s // blk) == (t // blk)))
    blocks += q_rows + k_rows
    blocks.append(np.ones((8, C), bool))
    table = np.concatenate(blocks, axis=0).astype(np.float32)
    table = np.concatenate([table, table], axis=1)
    return table, np.concatenate(masks, axis=0).astype(np.float32)


def _split_bf16(a):
    hi = a.astype(BF16)
    lo = (a - hi.astype(F32)).astype(BF16)
    return hi, lo


def _gla_kernel(gq_ref, gk_ref, gv_ref, gr_ref, lr_ref, w2_ref, gb_ref, ng_ref, tab_ref, msk_ref,
                o_ref, state_ref, *, n_chunks):
    C = GLA_CHUNK
    n_lv = len(GLA_LEVELS)

    @pl.when(pl.program_id(1) == 0)
    def _():
        state_ref[...] = jnp.zeros_like(state_ref)

    head0 = lax.broadcasted_iota(jnp.int32, (1, LANES), 1) < GLA_DK
    head_masks = (head0, jnp.logical_not(head0))
    eye = (lax.broadcasted_iota(jnp.int32, (C, C), 0) == lax.broadcasted_iota(jnp.int32, (C, C), 1))
    w2_hi, w2_lo = _split_bf16(w2_ref[...])

    def chunk(ci, _):
        rows = pl.ds(pl.multiple_of(ci * C, C), C)
        lr_hi, lr_lo = _split_bf16(lr_ref[rows, :])
        for p in range(GLA_HEADS // 2):
            cols = pl.ds(p * LANES, LANES)
            pre = (_dot(lr_hi, w2_hi[:, p * LANES:(p + 1) * LANES])
                   + _dot(lr_hi, w2_lo[:, p * LANES:(p + 1) * LANES])
                   + _dot(lr_lo, w2_hi[:, p * LANES:(p + 1) * LANES])) + gb_ref[:, cols]
            log_a = -_softplus(-pre) * (1.0 / GLA_GATE_TAU)
            la_hi, la_lo = _split_bf16(log_a)
            expo = _dot(tab_ref[...], jnp.concatenate([la_hi, la_lo], axis=0))
            ex = jnp.exp(expo)
            q2 = gq_ref[rows, cols]
            k2 = gk_ref[rows, cols]
            q_int = q2 * ex[0:C]
            k_dec = (k2 * ex[C:2 * C]).astype(BF16)
            dec_row = ex[(2 + 2 * n_lv) * C:(2 + 2 * n_lv) * C + 1]
            st = state_ref[p]
            st_bf = st.astype(BF16)
            qk = q2 * k2
            scores = [jnp.where(eye, jnp.sum(jnp.where(hm, qk, 0.0), axis=1, keepdims=True), 0.0)
                      for hm in head_masks]
            for l in range(n_lv):
                q_l = q2 * ex[(2 + l) * C:(3 + l) * C]
                k_l = (k2 * ex[(2 + n_lv + l) * C:(3 + n_lv + l) * C]).astype(BF16)
                m_l = msk_ref[l * C:(l + 1) * C, :] > 0.5
                for h, hm in enumerate(head_masks):
                    a = _dot_nt(jnp.where(hm, q_l, 0.0).astype(BF16), k_l)
                    scores[h] = scores[h] + jnp.where(m_l, a, 0.0)
            upd = []
            for h, hm in enumerate(head_masks):
                vcols = pl.ds((2 * p + h) * GLA_DV, GLA_DV)
                v_h = gv_ref[rows, vcols]
                o = (_dot_nt(jnp.where(hm, q_int, 0.0).astype(BF16), st_bf)
                     + _dot(scores[h].astype(BF16), v_h))
                inv = lax.rsqrt(jnp.mean(o * o, axis=-1, keepdims=True) + RMS_EPS)
                y = (o * inv) * ng_ref[:, vcols]
                g = gr_ref[rows, vcols]
                y = y * (g / (1.0 + jnp.exp(-g)))
                o_ref[rows, vcols] = y.astype(o_ref.dtype)
                v_t = v_h.astype(F32).T.astype(BF16)
                upd.append(_dot(v_t, k_dec))
            state_ref[p] = st * dec_row + jnp.where(head0, upd[0], upd[1])
        return 0

    lax.fori_loop(0, n_chunks, chunk, 0)


def _gla(gq, gk, gv, gr, lr, w2, gb, ng, B, S, tb):
    table, masks = _gla_tables()
    table = jnp.asarray(table, BF16)
    masks = jnp.asarray(masks, F32)
    T = B * S
    nblk = S // tb
    row = lambda n: pl.BlockSpec((tb, n), lambda b, i: (b * nblk + i, 0))
    const = lambda a: pl.BlockSpec(a.shape, lambda b, i: (0,) * a.ndim)
    w2 = w2.astype(F32)
    gb2 = gb.reshape(1, GLA_KW)
    ng2 = ng.reshape(1, GLA_VW)
    return pl.pallas_call(
        functools.partial(_gla_kernel, n_chunks=tb // GLA_CHUNK),
        grid=(B, nblk),
        in_specs=[row(GLA_KW), row(GLA_KW), row(GLA_VW), row(GLA_VW), row(GLA_GATE_RANK),
                  const(w2), const(gb2), const(ng2), const(table), const(masks)],
        out_specs=row(GLA_VW),
        out_shape=jax.ShapeDtypeStruct((T, GLA_VW), BF16),
        scratch_shapes=[pltpu.VMEM((GLA_HEADS // 2, GLA_DV, 2 * GLA_DK), F32)],
        compiler_params=_params("parallel", "arbitrary"),
        name="gla",
    )(gq, gk, gv, gr, lr, w2, gb2, ng2, table, masks)


SUBLANES = 8


def _store_token_tiles(ref, value, accumulate=False):
    n = value.shape[0]
    for s in range(SUBLANES):
        rows = pl.ds(s, n, stride=SUBLANES)
        chunk = value[:, s * LANES:(s + 1) * LANES]
        ref[rows, :] = ref[rows, :] + chunk if accumulate else chunk


def _load_token_tiles(ref, n):
    return jnp.concatenate([ref[pl.ds(s, n, stride=SUBLANES), :] for s in range(SUBLANES)], axis=1)


def _out_proj_kernel(x_ref, oa_ref, ob_ref, wa_ref, wb_ref, g_ref, wr_ref, br_ref,
                     x1_ref, h2_ref, ri_ref, rf_ref):
    x1 = x_ref[...] + _dot(oa_ref[...], wa_ref[...]) + _dot(ob_ref[...], wb_ref[...])
    x1_ref[...] = x1
    inv = lax.rsqrt(jnp.mean(x1 * x1, axis=-1, keepdims=True) + RMS_EPS)
    h2 = (x1 * inv) * g_ref[...]
    _store_token_tiles(h2_ref, h2)

    h_hi, h_lo = _split_bf16(h2)
    w_hi, w_lo = _split_bf16(wr_ref[...])
    logits = _dot(h_hi, w_hi) + _dot(h_hi, w_lo) + _dot(h_lo, w_hi) + br_ref[...]
    lane = lax.broadcasted_iota(jnp.int32, logits.shape, 1)
    neg = -jnp.inf

    def first_max(vals):
        m = jnp.max(vals, axis=1, keepdims=True)
        idx = jnp.min(jnp.where(vals == m, lane, LANES), axis=1, keepdims=True)
        return m, idx

    is_group = lane < N_GROUPS
    g_max, g_idx = first_max(jnp.where(is_group, logits, neg))
    g_p = 1.0 / jnp.sum(jnp.where(is_group, jnp.exp(logits - g_max), 0.0), axis=1, keepdims=True)
    lo_lane = N_GROUPS + EXPERTS_PER_GROUP * g_idx
    sel = jnp.where((lane >= lo_lane) & (lane < lo_lane + EXPERTS_PER_GROUP), logits, neg)
    m1, i1 = first_max(sel)
    m2, i2 = first_max(jnp.where(lane == i1, neg, sel))
    e = jnp.exp(m2 - m1)
    gate1 = g_p / (1.0 + e)
    gate2 = g_p * e / (1.0 + e)
    ri_ref[...] = jnp.where(lane == 0, i1 - N_GROUPS, jnp.where(lane == 1, i2 - N_GROUPS, 0))
    rf_ref[...] = jnp.where(lane == 0, gate1, jnp.where(lane == 1, gate2, 0.0))


def _out_proj(x2, o_sb, o_gla, w_out, ln_g, w_group, b_group, w_expert, b_expert, tm):
    T, D = x2.shape
    wa = w_out[:SB_WIDTH].astype(BF16)
    wb = w_out[SB_WIDTH:].astype(BF16)
    n_r = N_GROUPS + N_EXPERTS
    wr = jnp.zeros((D, LANES), F32).at[:, :n_r].set(jnp.concatenate([w_group, w_expert], axis=1))
    br = jnp.zeros((1, LANES), F32).at[0, :n_r].set(jnp.concatenate([b_group, b_expert]))
    row = lambda n: pl.BlockSpec((tm, n), lambda i: (i, 0))
    const = lambda a: pl.BlockSpec(a.shape, lambda i: (0,) * a.ndim)
    g2 = ln_g.reshape(1, D)
    return pl.pallas_call(
        _out_proj_kernel,
        grid=(T // tm,),
        in_specs=[row(D), row(SB_WIDTH), row(GLA_VW), const(wa), const(wb), const(g2), const(wr), const(br)],
        out_specs=[row(D), pl.BlockSpec((tm * SUBLANES, LANES), lambda i: (i, 0)), row(LANES), row(LANES)],
        out_shape=[jax.ShapeDtypeStruct((T, D), F32), jax.ShapeDtypeStruct((T * SUBLANES, LANES), F32),
                   jax.ShapeDtypeStruct((T, LANES), jnp.int32), jax.ShapeDtypeStruct((T, LANES), F32)],
        compiler_params=_params("parallel"),
        name="out_proj",
    )(x2, o_sb, o_gla, wa, wb, g2, wr, br)


def _route_tables(expert_id, n_rows, tm):
    e_flat = expert_id.reshape(-1)
    onehot = (e_flat[:, None] == jnp.arange(N_EXPERTS, dtype=jnp.int32)[None, :]).astype(jnp.int32)
    cum = jnp.cumsum(onehot, axis=0)
    counts = cum[-1]
    rank = jnp.take_along_axis(cum, e_flat[:, None], axis=1)[:, 0] - 1
    ends = jnp.cumsum(counts)
    starts = ends - counts
    dest = (starts[e_flat] + rank).astype(jnp.int32)

    n_blocks = n_rows // tm
    n_items = n_blocks + N_EXPERTS - 1
    first_blk = starts // tm
    last_blk = jnp.maximum(ends - 1, starts) // tm
    items_e = jnp.where(counts > 0, last_blk - first_blk + 1, 0)
    item_end = jnp.cumsum(items_e)
    item_start = item_end - items_e
    w = jnp.arange(n_items, dtype=jnp.int32)
    total = item_end[-1]
    valid = w < total
    wc = jnp.minimum(w, total - 1)
    it_e = jnp.sum(item_end[None, :] <= wc[:, None], axis=1).astype(jnp.int32)
    it_b = (first_blk[it_e] + (wc - item_start[it_e])).astype(jnp.int32)
    lo = jnp.clip(starts[it_e] - it_b * tm, 0, tm)
    hi = jnp.clip(ends[it_e] - it_b * tm, 0, tm)
    hi = jnp.where(valid, hi, lo)
    prev_b = jnp.concatenate([jnp.full((1,), -1, jnp.int32), it_b[:-1]])
    first = (it_b != prev_b).astype(jnp.int32)
    return dest, it_b, it_e, lo.astype(jnp.int32), hi.astype(jnp.int32), first


ROW_ISSUE_UNROLL = 8


def _token_tile(ref, t):
    return ref.at[pl.ds(pl.multiple_of(t * SUBLANES, SUBLANES), SUBLANES), :]


def _start_token_gather(src_ref, idx_ref, idx_base, idx_stride, dst_ref, sem, n):
    def body(r, _):
        t = idx_ref[idx_base + r * idx_stride]
        pltpu.make_async_copy(_token_tile(src_ref, t), _token_tile(dst_ref, r), sem).start()
        return 0
    lax.fori_loop(0, n, body, 0, unroll=ROW_ISSUE_UNROLL)


def _wait_token_gather(src_ref, dst_ref, sem):
    pltpu.make_async_copy(src_ref.at[pl.ds(0, dst_ref.shape[0]), :], dst_ref, sem).wait()


def _expert_kernel(blk_ref, exp_ref, lo_ref, hi_ref, first_ref, dest_ref,
                   h_ref, wg_ref, wu_ref, wd_ref, y_ref, xbuf, tok_ref, sem, *, tm):
    w = pl.program_id(0)
    n_items = pl.num_programs(0)
    blk = blk_ref[w]

    def gather(b):
        _start_token_gather(h_ref, tok_ref, b * tm, 1, xbuf.at[b % 2], sem.at[b % 2], tm)

    @pl.when(w == 0)
    def _():
        def invert(t, _):
            for k in range(TOP_K):
                tok_ref[dest_ref[TOP_K * t + k]] = t
            return 0
        lax.fori_loop(0, tok_ref.shape[0] // TOP_K, invert, 0, unroll=ROW_ISSUE_UNROLL)
        gather(blk)

    nxt = jnp.minimum(w + 1, n_items - 1)

    @pl.when((w + 1 < n_items) & (first_ref[nxt] == 1))
    def _():
        gather(blk_ref[nxt])

    @pl.when(first_ref[w] == 1)
    def _():
        _wait_token_gather(h_ref, xbuf.at[blk % 2], sem.at[blk % 2])

    xb = _load_token_tiles(xbuf.at[blk % 2], tm).astype(BF16)
    g = _dot(xb, wg_ref[...].astype(BF16))
    u = _dot(xb, wu_ref[...].astype(BF16))
    hidden = (g / (1.0 + jnp.exp(-g))) * u
    y = _dot(hidden.astype(BF16), wd_ref[...].astype(BF16))
    r = lax.broadcasted_iota(jnp.int32, y.shape, 0)
    y = jnp.where((r >= lo_ref[w]) & (r < hi_ref[w]), y, 0.0)

    @pl.when(first_ref[w] == 1)
    def _():
        _store_token_tiles(y_ref, y)

    @pl.when(first_ref[w] == 0)
    def _():
        _store_token_tiles(y_ref, y, accumulate=True)


def _expert_ffn(h2_tiles, dest, tables, w_gate, w_up, w_down, tm):
    N = dest.shape[0]
    _, D, F = w_gate.shape
    assert D == SUBLANES * LANES
    it_b, it_e, lo, hi, first = tables
    n_items = it_b.shape[0]
    wspec = lambda shape: pl.BlockSpec((None,) + shape, lambda w, b, e, *_: (e[w], 0, 0))
    return pl.pallas_call(
        functools.partial(_expert_kernel, tm=tm),
        grid_spec=pltpu.PrefetchScalarGridSpec(
            num_scalar_prefetch=6,
            grid=(n_items,),
            in_specs=[pl.BlockSpec(memory_space=pl.ANY), wspec((D, F)), wspec((D, F)), wspec((F, D))],
            out_specs=pl.BlockSpec((tm * SUBLANES, LANES), lambda w, b, *_: (b[w], 0)),
            scratch_shapes=[pltpu.VMEM((2, tm * SUBLANES, LANES), F32),
                            pltpu.SMEM((N,), jnp.int32),
                            pltpu.SemaphoreType.DMA((2,))],
        ),
        out_shape=jax.ShapeDtypeStruct((N * SUBLANES, LANES), F32),
        compiler_params=_params("arbitrary"),
        name="expert_ffn",
    )(it_b, it_e, lo, hi, first, dest, h2_tiles, w_gate, w_up, w_down)


def _combine_kernel(dest_ref, x1_ref, rf_ref, g_ref, y_ref, o_ref, buf, sem, *, tb):
    i = pl.program_id(0)
    n_blk = pl.num_programs(0)
    slot = i % 2

    def gather(blk, s):
        for k in range(TOP_K):
            _start_token_gather(y_ref, dest_ref, TOP_K * blk * tb + k, TOP_K, buf.at[s, k], sem.at[s], tb)

    @pl.when(i == 0)
    def _():
        gather(0, 0)

    @pl.when(i + 1 < n_blk)
    def _():
        gather(i + 1, 1 - slot)

    for k in range(TOP_K):
        _wait_token_gather(y_ref, buf.at[slot, k], sem.at[slot])
    rf = rf_ref[...]
    x = (x1_ref[...] + rf[:, 0:1] * _load_token_tiles(buf.at[slot, 0], tb)
         + rf[:, 1:2] * _load_token_tiles(buf.at[slot, 1], tb))
    inv = lax.rsqrt(jnp.mean(x * x, axis=-1, keepdims=True) + RMS_EPS)
    o_ref[...] = (x * inv) * g_ref[...]


def _combine(x1, route_f, y_tiles, dest, ln_g, tb):
    T, D = x1.shape
    g2 = ln_g.reshape(1, D)
    return pl.pallas_call(
        functools.partial(_combine_kernel, tb=tb),
        grid_spec=pltpu.PrefetchScalarGridSpec(
            num_scalar_prefetch=1,
            grid=(T // tb,),
            in_specs=[pl.BlockSpec((tb, D), lambda i, d: (i, 0)),
                      pl.BlockSpec((tb, LANES), lambda i, d: (i, 0)),
                      pl.BlockSpec((1, D), lambda i, d: (0, 0)),
                      pl.BlockSpec(memory_space=pl.ANY)],
            out_specs=pl.BlockSpec((tb, D), lambda i, d: (i, 0)),
            scratch_shapes=[pltpu.VMEM((2, TOP_K, tb * SUBLANES, LANES), F32),
                            pltpu.SemaphoreType.DMA((2,))],
        ),
        out_shape=jax.ShapeDtypeStruct((T, D), F32),
        compiler_params=_params("arbitrary"),
        name="combine",
    )(dest, x1, route_f, g2, y_tiles)


def _layer(x2, B, S, ln1_g, w_in, w2, gb, ng, w_out, ln2_g, w_group, b_group, w_expert, b_expert,
           w_gate, w_up, w_down, ln_f_g, *, tm, tq, tb_gla, tb_row, tm_e):
    q, k, v, gq, gk, gv, gr, lr = _in_proj(x2, ln1_g, w_in, tm)
    o_sb = _sb_attn(q, k, v, B, S, tq)
    o_gla = _gla(gq, gk, gv, gr, lr, w2, gb, ng, B, S, tb_gla)
    x1, h2_tiles, route_i, route_f = _out_proj(x2, o_sb, o_gla, w_out, ln2_g, w_group, b_group,
                                               w_expert, b_expert, tm)
    T = x2.shape[0]
    dest, *tables = _route_tables(route_i[:, :TOP_K], TOP_K * T, tm_e)
    y_tiles = _expert_ffn(h2_tiles, dest, tables, w_gate, w_up, w_down, tm_e)
    return _combine(x1, route_f, y_tiles, dest, ln_f_g, tb_row)


def kernel(x, ln1_g, w_in, gla_gate_w2, gla_gate_b, gla_norm_g, w_out, ln2_g, w_group, b_group, w_expert, b_expert, exp_w_gate, exp_w_up, exp_w_down, ln_f_g):
    B, S, D = x.shape
    assert ln1_g.shape[0] == 1, "single-layer stack"
    out = _layer(x.reshape(B * S, D), B, S, ln1_g[0], w_in[0], gla_gate_w2[0], gla_gate_b[0],
                 gla_norm_g[0], w_out[0], ln2_g[0], w_group[0], b_group[0], w_expert[0], b_expert[0],
                 exp_w_gate[0], exp_w_up[0], exp_w_down[0], ln_f_g,
                 tm=512, tq=256, tb_gla=512, tb_row=256, tm_e=256)
    return out.reshape(B, S, D)
```

```python
import functools

import numpy as np
import jax
import jax.numpy as jnp
from jax import lax
from jax.experimental import pallas as pl
from jax.experimental.pallas import tpu as pltpu

SB_HEADS = 8
SB_HEAD_DIM = 64
SB_WIDTH = SB_HEADS * SB_HEAD_DIM
GLA_HEADS = 4
GLA_DK = 64
GLA_DV = 128
GLA_KW = GLA_HEADS * GLA_DK
GLA_VW = GLA_HEADS * GLA_DV
GLA_GATE_RANK = 16
GLA_GATE_TAU = 16.0
N_GROUPS = 4
EXPERTS_PER_GROUP = 8
N_EXPERTS = N_GROUPS * EXPERTS_PER_GROUP
TOP_K = 2
RMS_EPS = 1e-6
LOG2_E = 1.4426950408889634

LANES = 128
VMEM_LIMIT_BYTES = 56 * 1024 * 1024

F32 = jnp.float32
BF16 = jnp.bfloat16


def _dot(a, b):
    return jnp.dot(a, b, preferred_element_type=F32)


def _dot_nt(a, b):
    return lax.dot_general(a, b, (((1,), (1,)), ((), ())), preferred_element_type=F32)


def _softplus(z):
    return jnp.maximum(z, 0.0) + jnp.log(1.0 + jnp.exp(-jnp.abs(z)))


def _params(*sem, flags=None):
    return pltpu.CompilerParams(dimension_semantics=sem, vmem_limit_bytes=VMEM_LIMIT_BYTES, flags=flags)


def _in_proj_kernel(x_ref, g_ref, wq, wk, wv, wgq, wgk, wgv, wgr, wlr,
                    oq, ok, ov, ogq, ogk, ogv, ogr, olr):
    x = x_ref[...]
    inv = lax.rsqrt(jnp.mean(x * x, axis=-1, keepdims=True) + RMS_EPS)
    h = ((x * inv) * g_ref[...]).astype(BF16)
    oq[...] = (_dot(h, wq[...]) * (SB_HEAD_DIM ** -0.5 * LOG2_E)).astype(oq.dtype)
    ok[...] = _dot(h, wk[...]).astype(ok.dtype)
    ov[...] = _dot(h, wv[...]).astype(ov.dtype)
    ogq[...] = (_dot(h, wgq[...]) * (GLA_DK ** -0.5)).astype(ogq.dtype)
    ogk[...] = _dot(h, wgk[...]).astype(ogk.dtype)
    ogv[...] = _dot(h, wgv[...]).astype(ogv.dtype)
    ogr[...] = _dot(h, wgr[...]).astype(ogr.dtype)
    olr[...] = _dot(h, wlr[...]).astype(olr.dtype)


def _in_proj(x2, ln_g, w_in, tm):
    T, D = x2.shape
    sizes = (SB_WIDTH, SB_WIDTH, SB_WIDTH, GLA_KW, GLA_KW, GLA_VW, GLA_VW, GLA_GATE_RANK)
    offs = np.concatenate([[0], np.cumsum(sizes)])
    ws = [w_in[:, int(offs[i]):int(offs[i + 1])].astype(BF16) for i in range(len(sizes))]
    out_dtypes = (BF16, BF16, BF16, F32, F32, BF16, F32, F32)
    row = lambda n: pl.BlockSpec((tm, n), lambda i: (i, 0))
    full = lambda n: pl.BlockSpec((D, n), lambda i: (0, 0))
    return pl.pallas_call(
        _in_proj_kernel,
        grid=(T // tm,),
        in_specs=[row(D), pl.BlockSpec((1, D), lambda i: (0, 0))] + [full(n) for n in sizes],
        out_specs=[row(n) for n in sizes],
        out_shape=[jax.ShapeDtypeStruct((T, n), dt) for n, dt in zip(sizes, out_dtypes)],
        compiler_params=_params("parallel"),
        name="in_proj",
    )(x2, ln_g.reshape(1, D), *ws)


SB_ZERO_WEIGHT_BITS = 160.0


def _softplus2(z):
    one = jnp.ones((), z.dtype)
    return jnp.maximum(z, jnp.zeros((), z.dtype)) + jnp.log2(one + jnp.exp2(-jnp.abs(z)))


def _sb_attn_kernel(q_ref, k_ref, v_ref, o_ref, qh_ref, acc_ref, carry_ref, *, tq, n_pairs):
    qi = pl.program_id(1)
    r = lax.broadcasted_iota(jnp.int32, (tq, tq), 0)
    c = lax.broadcasted_iota(jnp.int32, (tq, tq), 1)
    causal = c < r
    suffix = (r >= c).astype(BF16)
    head0 = lax.broadcasted_iota(jnp.int32, (1, LANES), 1) < SB_HEAD_DIM

    for p in range(n_pairs):
        q2 = q_ref[:, p * LANES:(p + 1) * LANES]
        zero = jnp.zeros_like(q2)
        qh_ref[p, 0:tq] = jnp.where(head0, q2, zero)
        qh_ref[p, tq:2 * tq] = jnp.where(head0, zero, q2)
    acc_ref[...] = jnp.zeros_like(acc_ref)
    carry_ref[...] = jnp.zeros_like(carry_ref)
    causal2 = jnp.concatenate([causal, causal], axis=0)

    def key_block(kb, diag):
        rows = pl.ds(pl.multiple_of(kb * tq, tq), tq)
        for p in range(n_pairs):
            cols = pl.ds(p * LANES, LANES)
            k2 = k_ref[rows, cols]
            v2 = v_ref[rows, cols]
            vzero = jnp.zeros_like(v2)
            v_stack = jnp.concatenate([jnp.where(head0, v2, vzero), jnp.where(head0, vzero, v2)], axis=0)
            z = _dot_nt(qh_ref[p], k2)
            sp = _softplus2(z)
            if diag:
                sp = jnp.where(causal2, sp, 0.0)
            run = _dot(sp.astype(BF16), suffix)
            carry = carry_ref[p]
            w = jnp.exp2(z - run - jnp.concatenate([carry] * (tq // LANES), axis=1))
            if diag:
                w = jnp.where(causal2, w, 0.0)
            w = w.astype(BF16)
            acc_ref[p] += _dot(jnp.concatenate([w[0:tq], w[tq:2 * tq]], axis=1), v_stack)
            carry_ref[p] = carry + jnp.broadcast_to(run[:, 0:1], (2 * tq, LANES))

    def min_carry():
        m = carry_ref[0]
        for p in range(1, n_pairs):
            m = jnp.minimum(m, carry_ref[p])
        return jnp.min(m)

    def cond(state):
        kb, m = state
        return jnp.logical_and(kb >= 0, m < SB_ZERO_WEIGHT_BITS)

    def body(state):
        kb, _ = state
        key_block(kb, False)
        return kb - 1, min_carry()

    key_block(qi, True)
    lax.while_loop(cond, body, (qi - 1, min_carry()))
    for p in range(n_pairs):
        o_ref[:, p * LANES:(p + 1) * LANES] = acc_ref[p].astype(o_ref.dtype)


def _sb_attn(q, k, v, B, S, tq):
    W = q.shape[-1]
    n_pairs = W // LANES
    q3, k3, v3 = (a.reshape(B, S, W) for a in (q, k, v))
    blk = pl.BlockSpec((None, tq, W), lambda b, i: (b, i, 0))
    seq = pl.BlockSpec((None, S, W), lambda b, i: (b, 0, 0))
    out = pl.pallas_call(
        functools.partial(_sb_attn_kernel, tq=tq, n_pairs=n_pairs),
        grid=(B, S // tq),
        in_specs=[blk, seq, seq],
        out_specs=blk,
        out_shape=jax.ShapeDtypeStruct((B, S, W), BF16),
        scratch_shapes=[pltpu.VMEM((n_pairs, 2 * tq, LANES), BF16),
                        pltpu.VMEM((n_pairs, tq, LANES), F32),
                        pltpu.VMEM((n_pairs, 2 * tq, LANES), F32)],
        compiler_params=_params("parallel", "arbitrary"),
        name="sb_attn",
    )(q3, k3, v3)
    return out.reshape(B * S, W)


GLA_CHUNK = 64
GLA_LEVELS = (32, 16, 8, 4, 2, 1)


def _gla_tables(C=GLA_CHUNK):
    t = np.arange(C)[:, None]
    j = np.arange(C)[None, :]
    blocks = [(j <= t), (j > t)]
    q_rows, k_rows, masks = [], [], []
    for half in GLA_LEVELS:
        blk = 2 * half
        u = (t // blk) * blk + half - 1
        second = (t % blk) >= half
        q_rows.append(second & (j > u) & (j <= t))
        k_rows.append((~second) & (j > t) & (j <= u))
        s = j
        masks.append(second & ((s % blk) < half) & ((s // blk) == (t // blk)))
    blocks += q_rows + k_rows
    blocks.append(np.ones((8, C), bool))
    table = np.concatenate(blocks, axis=0).astype(np.float32)
    table = np.concatenate([table, table], axis=1)
    return table, np.concatenate(masks, axis=0).astype(np.float32)


def _split_bf16(a):
    hi = a.astype(BF16)
    lo = (a - hi.astype(F32)).astype(BF16)
    return hi, lo


def _gla_kernel(gq_ref, gk_ref, gv_ref, gr_ref, lr_ref, w2_ref, gb_ref, ng_ref, tab_ref, msk_ref,
                o_ref, state_ref, *, n_chunks):
    C = GLA_CHUNK
    n_lv = len(GLA_LEVELS)

    @pl.when(pl.program_id(1) == 0)
    def _():
        state_ref[...] = jnp.zeros_like(state_ref)

    head0 = lax.broadcasted_iota(jnp.int32, (1, LANES), 1) < GLA_DK
    eye = (lax.broadcasted_iota(jnp.int32, (C, C), 0) == lax.broadcasted_iota(jnp.int32, (C, C), 1))
    w2_hi, w2_lo = _split_bf16(w2_ref[...])
    lr_hi, lr_lo = _split_bf16(lr_ref[...])

    def stack_heads(a):
        return jnp.concatenate([jnp.where(head0, a, 0.0), jnp.where(head0, 0.0, a)], axis=0)

    for p in range(GLA_HEADS // 2):
        cols = slice(p * LANES, (p + 1) * LANES)
        pre = (_dot(lr_hi, w2_hi[:, cols]) + _dot(lr_hi, w2_lo[:, cols])
               + _dot(lr_lo, w2_hi[:, cols])) + gb_ref[:, cols]
        log_a = -_softplus(-pre) * (1.0 / GLA_GATE_TAU)
        la_hi, la_lo = _split_bf16(log_a)
        la = jnp.concatenate(
            [jnp.concatenate([la_hi[c * C:(c + 1) * C], la_lo[c * C:(c + 1) * C]], axis=0)
             for c in range(n_chunks)], axis=1)
        ex_all = jnp.exp(_dot(tab_ref[...], la))
        st = state_ref[p]
        for c in range(n_chunks):
            rows = slice(c * C, (c + 1) * C)
            ex = ex_all[:, c * LANES:(c + 1) * LANES]
            q2 = gq_ref[rows, cols]
            k2 = gk_ref[rows, cols]
            q_int = q2 * ex[0:C]
            k_dec = (k2 * ex[C:2 * C]).astype(BF16)
            dec_row = ex[(2 + 2 * n_lv) * C:(2 + 2 * n_lv) * C + 1]
            diag = jnp.sum(stack_heads(q2 * k2), axis=1, keepdims=True)
            scores = jnp.where(jnp.concatenate([eye, eye], axis=0), diag, 0.0)
            for l in range(n_lv):
                q_l = stack_heads(q2 * ex[(2 + l) * C:(3 + l) * C]).astype(BF16)
                k_l = (k2 * ex[(2 + n_lv + l) * C:(3 + n_lv + l) * C]).astype(BF16)
                m_l = msk_ref[l * C:(l + 1) * C, :] > 0.5
                scores = scores + jnp.where(jnp.concatenate([m_l, m_l], axis=0), _dot_nt(q_l, k_l), 0.0)
            scores = scores.astype(BF16)
            o_inter = _dot_nt(stack_heads(q_int).astype(BF16), st.astype(BF16))
            upd = []
            for h in range(2):
                vcols = slice((2 * p + h) * GLA_DV, (2 * p + h + 1) * GLA_DV)
                v_h = gv_ref[rows, vcols]
                o = o_inter[h * C:(h + 1) * C] + _dot(scores[h * C:(h + 1) * C], v_h)
                inv = lax.rsqrt(jnp.mean(o * o, axis=-1, keepdims=True) + RMS_EPS)
                y = (o * inv) * ng_ref[:, vcols]
                g = gr_ref[rows, vcols]
                y = y * (g / (1.0 + jnp.exp(-g)))
                o_ref[rows, vcols] = y.astype(o_ref.dtype)
                v_t = v_h.astype(F32).T.astype(BF16)
                upd.append(_dot(v_t, k_dec))
            st = st * dec_row + jnp.where(head0, upd[0], upd[1])
        state_ref[p] = st


def _gla(gq, gk, gv, gr, lr, w2, gb, ng, B, S, tb):
    table, masks = _gla_tables()
    table = jnp.asarray(table, BF16)
    masks = jnp.asarray(masks, F32)
    T = B * S
    nblk = S // tb
    row = lambda n: pl.BlockSpec((tb, n), lambda b, i: (b * nblk + i, 0))
    const = lambda a: pl.BlockSpec(a.shape, lambda b, i: (0,) * a.ndim)
    w2 = w2.astype(F32)
    gb2 = gb.reshape(1, GLA_KW)
    ng2 = ng.reshape(1, GLA_VW)
    return pl.pallas_call(
        functools.partial(_gla_kernel, n_chunks=tb // GLA_CHUNK),
        grid=(B, nblk),
        in_specs=[row(GLA_KW), row(GLA_KW), row(GLA_VW), row(GLA_VW), row(GLA_GATE_RANK),
                  const(w2), const(gb2), const(ng2), const(table), const(masks)],
        out_specs=row(GLA_VW),
        out_shape=jax.ShapeDtypeStruct((T, GLA_VW), BF16),
        scratch_shapes=[pltpu.VMEM((GLA_HEADS // 2, GLA_DV, 2 * GLA_DK), F32)],
        compiler_params=_params("parallel", "arbitrary"),
        name="gla",
    )(gq, gk, gv, gr, lr, w2, gb2, ng2, table, masks)


SUBLANES = 8


def _store_token_tiles(ref, value, accumulate=False):
    n = value.shape[0]
    for s in range(SUBLANES):
        rows = pl.ds(s, n, stride=SUBLANES)
        chunk = value[:, s * LANES:(s + 1) * LANES]
        ref[rows, :] = ref[rows, :] + chunk if accumulate else chunk


def _load_token_tiles(ref, n):
    return jnp.concatenate([ref[pl.ds(s, n, stride=SUBLANES), :] for s in range(SUBLANES)], axis=1)


def _out_proj_kernel(x_ref, oa_ref, ob_ref, wa_ref, wb_ref, g_ref, wr_ref, br_ref,
                     x1_ref, h2_ref, ri_ref, rf_ref):
    x1 = x_ref[...] + _dot(oa_ref[...], wa_ref[...]) + _dot(ob_ref[...], wb_ref[...])
    x1_ref[...] = x1
    inv = lax.rsqrt(jnp.mean(x1 * x1, axis=-1, keepdims=True) + RMS_EPS)
    h2 = (x1 * inv) * g_ref[...]
    _store_token_tiles(h2_ref, h2)

    h_hi, h_lo = _split_bf16(h2)
    w_hi, w_lo = _split_bf16(wr_ref[...])
    logits = _dot(h_hi, w_hi) + _dot(h_hi, w_lo) + _dot(h_lo, w_hi) + br_ref[...]
    lane = lax.broadcasted_iota(jnp.int32, logits.shape, 1)
    neg = -jnp.inf

    def first_max(vals):
        m = jnp.max(vals, axis=1, keepdims=True)
        idx = jnp.min(jnp.where(vals == m, lane, LANES), axis=1, keepdims=True)
        return m, idx

    is_group = lane < N_GROUPS
    g_max, g_idx = first_max(jnp.where(is_group, logits, neg))
    g_p = 1.0 / jnp.sum(jnp.where(is_group, jnp.exp(logits - g_max), 0.0), axis=1, keepdims=True)
    lo_lane = N_GROUPS + EXPERTS_PER_GROUP * g_idx
    sel = jnp.where((lane >= lo_lane) & (lane < lo_lane + EXPERTS_PER_GROUP), logits, neg)
    m1, i1 = first_max(sel)
    m2, i2 = first_max(jnp.where(lane == i1, neg, sel))
    e = jnp.exp(m2 - m1)
    gate1 = g_p / (1.0 + e)
    gate2 = g_p * e / (1.0 + e)
    ri_ref[...] = jnp.where(lane == 0, i1 - N_GROUPS, jnp.where(lane == 1, i2 - N_GROUPS, 0))
    rf_ref[...] = jnp.where(lane == 0, gate1, jnp.where(lane == 1, gate2, 0.0))


def _out_proj(x2, o_sb, o_gla, w_out, ln_g, w_group, b_group, w_expert, b_expert, tm):
    T, D = x2.shape
    wa = w_out[:SB_WIDTH].astype(BF16)
    wb = w_out[SB_WIDTH:].astype(BF16)
    n_r = N_GROUPS + N_EXPERTS
    wr = jnp.zeros((D, LANES), F32).at[:, :n_r].set(jnp.concatenate([w_group, w_expert], axis=1))
    br = jnp.zeros((1, LANES), F32).at[0, :n_r].set(jnp.concatenate([b_group, b_expert]))
    row = lambda n: pl.BlockSpec((tm, n), lambda i: (i, 0))
    const = lambda a: pl.BlockSpec(a.shape, lambda i: (0,) * a.ndim)
    g2 = ln_g.reshape(1, D)
    return pl.pallas_call(
        _out_proj_kernel,
        grid=(T // tm,),
        in_specs=[row(D), row(SB_WIDTH), row(GLA_VW), const(wa), const(wb), const(g2), const(wr), const(br)],
        out_specs=[row(D), pl.BlockSpec((tm * SUBLANES, LANES), lambda i: (i, 0)), row(LANES), row(LANES)],
        out_shape=[jax.ShapeDtypeStruct((T, D), F32), jax.ShapeDtypeStruct((T * SUBLANES, LANES), F32),
                   jax.ShapeDtypeStruct((T, LANES), jnp.int32), jax.ShapeDtypeStruct((T, LANES), F32)],
        compiler_params=_params("parallel"),
        name="out_proj",
    )(x2, o_sb, o_gla, wa, wb, g2, wr, br)


def _route_tables(expert_id, n_rows, tm):
    e_flat = expert_id.reshape(-1)
    onehot = (e_flat[:, None] == jnp.arange(N_EXPERTS, dtype=jnp.int32)[None, :]).astype(jnp.int32)
    cum = jnp.cumsum(onehot, axis=0)
    counts = cum[-1]
    rank = jnp.take_along_axis(cum, e_flat[:, None], axis=1)[:, 0] - 1
    ends = jnp.cumsum(counts)
    starts = ends - counts
    dest = (starts[e_flat] + rank).astype(jnp.int32)

    n_blocks = n_rows // tm
    n_items = n_blocks + N_EXPERTS - 1
    first_blk = starts // tm
    last_blk = jnp.maximum(ends - 1, starts) // tm
    items_e = jnp.where(counts > 0, last_blk - first_blk + 1, 0)
    item_end = jnp.cumsum(items_e)
    item_start = item_end - items_e
    w = jnp.arange(n_items, dtype=jnp.int32)
    total = item_end[-1]
    valid = w < total
    wc = jnp.minimum(w, total - 1)
    it_e = jnp.sum(item_end[None, :] <= wc[:, None], axis=1).astype(jnp.int32)
    it_b = (first_blk[it_e] + (wc - item_start[it_e])).astype(jnp.int32)
    lo = jnp.clip(starts[it_e] - it_b * tm, 0, tm)
    hi = jnp.clip(ends[it_e] - it_b * tm, 0, tm)
    hi = jnp.where(valid, hi, lo)
    prev_b = jnp.concatenate([jnp.full((1,), -1, jnp.int32), it_b[:-1]])
    first = (it_b != prev_b).astype(jnp.int32)
    return dest, it_b, it_e, lo.astype(jnp.int32), hi.astype(jnp.int32), first


ROW_ISSUE_UNROLL = 8


def _token_tile(ref, t):
    return ref.at[pl.ds(pl.multiple_of(t * SUBLANES, SUBLANES), SUBLANES), :]


def _start_token_gather(src_ref, idx_ref, idx_base, idx_stride, dst_ref, sem, n):
    def body(r, _):
        t = idx_ref[idx_base + r * idx_stride]
        pltpu.make_async_copy(_token_tile(src_ref, t), _token_tile(dst_ref, r), sem).start()
        return 0
    lax.fori_loop(0, n, body, 0, unroll=ROW_ISSUE_UNROLL)


def _wait_token_gather(src_ref, dst_ref, sem):
    pltpu.make_async_copy(src_ref.at[pl.ds(0, dst_ref.shape[0]), :], dst_ref, sem).wait()


def _expert_kernel(blk_ref, exp_ref, lo_ref, hi_ref, first_ref, dest_ref,
                   h_ref, wg_ref, wu_ref, wd_ref, y_ref, xbuf, tok_ref, wg_bf, wu_bf, wd_bf, sem, *, tm):
    w = pl.program_id(0)
    n_items = pl.num_programs(0)
    blk = blk_ref[w]

    def gather(b):
        _start_token_gather(h_ref, tok_ref, b * tm, 1, xbuf.at[b % 2], sem.at[b % 2], tm)

    @pl.when(w == 0)
    def _():
        def invert(t, _):
            for k in range(TOP_K):
                tok_ref[dest_ref[TOP_K * t + k]] = t
            return 0
        lax.fori_loop(0, tok_ref.shape[0] // TOP_K, invert, 0, unroll=ROW_ISSUE_UNROLL)
        gather(blk)

    nxt = jnp.minimum(w + 1, n_items - 1)

    @pl.when((w + 1 < n_items) & (first_ref[nxt] == 1))
    def _():
        gather(blk_ref[nxt])

    @pl.when(first_ref[w] == 1)
    def _():
        _wait_token_gather(h_ref, xbuf.at[blk % 2], sem.at[blk % 2])

    @pl.when((w == 0) | (exp_ref[w] != exp_ref[jnp.maximum(w - 1, 0)]))
    def _():
        wg_bf[...] = wg_ref[...].astype(BF16)
        wu_bf[...] = wu_ref[...].astype(BF16)
        wd_bf[...] = wd_ref[...].astype(BF16)

    xb = _load_token_tiles(xbuf.at[blk % 2], tm).astype(BF16)
    g = _dot(xb, wg_bf[...])
    u = _dot(xb, wu_bf[...])
    hidden = (g / (1.0 + jnp.exp(-g))) * u
    y = _dot(hidden.astype(BF16), wd_bf[...])
    r = lax.broadcasted_iota(jnp.int32, y.shape, 0)
    y = jnp.where((r >= lo_ref[w]) & (r < hi_ref[w]), y, 0.0)

    @pl.when(first_ref[w] == 1)
    def _():
        _store_token_tiles(y_ref, y)

    @pl.when(first_ref[w] == 0)
    def _():
        _store_token_tiles(y_ref, y, accumulate=True)


def _expert_ffn(h2_tiles, dest, tables, w_gate, w_up, w_down, tm):
    N = dest.shape[0]
    _, D, F = w_gate.shape
    assert D == SUBLANES * LANES
    it_b, it_e, lo, hi, first = tables
    n_items = it_b.shape[0]
    wspec = lambda shape: pl.BlockSpec((None,) + shape, lambda w, b, e, *_: (e[w], 0, 0))
    return pl.pallas_call(
        functools.partial(_expert_kernel, tm=tm),
        grid_spec=pltpu.PrefetchScalarGridSpec(
            num_scalar_prefetch=6,
            grid=(n_items,),
            in_specs=[pl.BlockSpec(memory_space=pl.ANY), wspec((D, F)), wspec((D, F)), wspec((F, D))],
            out_specs=pl.BlockSpec((tm * SUBLANES, LANES), lambda w, b, *_: (b[w], 0)),
            scratch_shapes=[pltpu.VMEM((2, tm * SUBLANES, LANES), F32),
                            pltpu.SMEM((N,), jnp.int32),
                            pltpu.VMEM((D, F), BF16), pltpu.VMEM((D, F), BF16), pltpu.VMEM((F, D), BF16),
                            pltpu.SemaphoreType.DMA((2,))],
        ),
        out_shape=jax.ShapeDtypeStruct((N * SUBLANES, LANES), F32),
        compiler_params=_params("arbitrary"),
        name="expert_ffn",
    )(it_b, it_e, lo, hi, first, dest, h2_tiles, w_gate, w_up, w_down)


def _combine_kernel(dest_ref, x1_ref, rf_ref, g_ref, y_ref, o_ref, buf, sem, *, tb):
    i = pl.program_id(0)
    n_blk = pl.num_programs(0)
    slot = i % 2

    def gather(blk, s):
        for k in range(TOP_K):
            _start_token_gather(y_ref, dest_ref, TOP_K * blk * tb + k, TOP_K, buf.at[s, k], sem.at[s], tb)

    @pl.when(i == 0)
    def _():
        gather(0, 0)

    @pl.when(i + 1 < n_blk)
    def _():
        gather(i + 1, 1 - slot)

    for k in range(TOP_K):
        _wait_token_gather(y_ref, buf.at[slot, k], sem.at[slot])
    rf = rf_ref[...]
    x = (x1_ref[...] + rf[:, 0:1] * _load_token_tiles(buf.at[slot, 0], tb)
         + rf[:, 1:2] * _load_token_tiles(buf.at[slot, 1], tb))
    inv = lax.rsqrt(jnp.mean(x * x, axis=-1, keepdims=True) + RMS_EPS)
    o_ref[...] = (x * inv) * g_ref[...]


def _combine(x1, route_f, y_tiles, dest, ln_g, tb):
    T, D = x1.shape
    g2 = ln_g.reshape(1, D)
    return pl.pallas_call(
        functools.partial(_combine_kernel, tb=tb),
        grid_spec=pltpu.PrefetchScalarGridSpec(
            num_scalar_prefetch=1,
            grid=(T // tb,),
            in_specs=[pl.BlockSpec((tb, D), lambda i, d: (i, 0)),
                      pl.BlockSpec((tb, LANES), lambda i, d: (i, 0)),
                      pl.BlockSpec((1, D), lambda i, d: (0, 0)),
                      pl.BlockSpec(memory_space=pl.ANY)],
            out_specs=pl.BlockSpec((tb, D), lambda i, d: (i, 0)),
            scratch_shapes=[pltpu.VMEM((2, TOP_K, tb * SUBLANES, LANES), F32),
                            pltpu.SemaphoreType.DMA((2,))],
        ),
        out_shape=jax.ShapeDtypeStruct((T, D), F32),
        compiler_params=_params("arbitrary"),
        name="combine",
    )(dest, x1, route_f, g2, y_tiles)


def _layer(x2, B, S, ln1_g, w_in, w2, gb, ng, w_out, ln2_g, w_group, b_group, w_expert, b_expert,
           w_gate, w_up, w_down, ln_f_g, *, tm, tq, tb_gla, tb_row, tm_e):
    q, k, v, gq, gk, gv, gr, lr = _in_proj(x2, ln1_g, w_in, tm)
    o_sb = _sb_attn(q, k, v, B, S, tq)
    o_gla = _gla(gq, gk, gv, gr, lr, w2, gb, ng, B, S, tb_gla)
    x1, h2_tiles, route_i, route_f = _out_proj(x2, o_sb, o_gla, w_out, ln2_g, w_group, b_group,
                                               w_expert, b_expert, tm)
    T = x2.shape[0]
    dest, *tables = _route_tables(route_i[:, :TOP_K], TOP_K * T, tm_e)
    y_tiles = _expert_ffn(h2_tiles, dest, tables, w_gate, w_up, w_down, tm_e)
    return _combine(x1, route_f, y_tiles, dest, ln_f_g, tb_row)


def kernel(x, ln1_g, w_in, gla_gate_w2, gla_gate_b, gla_norm_g, w_out, ln2_g, w_group, b_group, w_expert, b_expert, exp_w_gate, exp_w_up, exp_w_down, ln_f_g):
    B, S, D = x.shape
    assert ln1_g.shape[0] == 1, "single-layer stack"
    out = _layer(x.reshape(B * S, D), B, S, ln1_g[0], w_in[0], gla_gate_w2[0], gla_gate_b[0],
                 gla_norm_g[0], w_out[0], ln2_g[0], w_group[0], b_group[0], w_expert[0], b_expert[0],
                 exp_w_gate[0], exp_w_up[0], exp_w_down[0], ln_f_g,
                 tm=512, tq=256, tb_gla=512, tb_row=256, tm_e=256)
    return out.reshape(B, S, D)
```

```python
import functools

import numpy as np
import jax
import jax.numpy as jnp
from jax import lax
from jax.experimental import pallas as pl
from jax.experimental.pallas import tpu as pltpu

SB_HEADS = 8
SB_HEAD_DIM = 64
SB_WIDTH = SB_HEADS * SB_HEAD_DIM
GLA_HEADS = 4
GLA_DK = 64
GLA_DV = 128
GLA_KW = GLA_HEADS * GLA_DK
GLA_VW = GLA_HEADS * GLA_DV
GLA_GATE_RANK = 16
GLA_GATE_TAU = 16.0
N_GROUPS = 4
EXPERTS_PER_GROUP = 8
N_EXPERTS = N_GROUPS * EXPERTS_PER_GROUP
TOP_K = 2
RMS_EPS = 1e-6
LOG2_E = 1.4426950408889634

LANES = 128
VMEM_LIMIT_BYTES = 56 * 1024 * 1024

F32 = jnp.float32
BF16 = jnp.bfloat16


def _dot(a, b):
    return jnp.dot(a, b, preferred_element_type=F32)


def _dot_nt(a, b):
    return lax.dot_general(a, b, (((1,), (1,)), ((), ())), preferred_element_type=F32)


def _softplus(z):
    return jnp.maximum(z, 0.0) + jnp.log(1.0 + jnp.exp(-jnp.abs(z)))


def _params(*sem, flags=None):
    return pltpu.CompilerParams(dimension_semantics=sem, vmem_limit_bytes=VMEM_LIMIT_BYTES, flags=flags)


def _in_proj_kernel(x_ref, g_ref, wq, wk, wv, wgq, wgk, wgv, wgr, wlr,
                    oq, ok, ov, ogq, ogk, ogv, ogr, olr):
    x = x_ref[...]
    inv = lax.rsqrt(jnp.mean(x * x, axis=-1, keepdims=True) + RMS_EPS)
    h = ((x * inv) * g_ref[...]).astype(BF16)
    oq[...] = (_dot(h, wq[...]) * (SB_HEAD_DIM ** -0.5 * LOG2_E)).astype(oq.dtype)
    ok[...] = _dot(h, wk[...]).astype(ok.dtype)
    ov[...] = _dot(h, wv[...]).astype(ov.dtype)
    ogq[...] = (_dot(h, wgq[...]) * (GLA_DK ** -0.5)).astype(ogq.dtype)
    ogk[...] = _dot(h, wgk[...]).astype(ogk.dtype)
    ogv[...] = _dot(h, wgv[...]).astype(ogv.dtype)
    ogr[...] = _dot(h, wgr[...]).astype(ogr.dtype)
    olr[...] = _dot(h, wlr[...]).astype(olr.dtype)


def _in_proj(x2, ln_g, w_in, tm):
    T, D = x2.shape
    sizes = (SB_WIDTH, SB_WIDTH, SB_WIDTH, GLA_KW, GLA_KW, GLA_VW, GLA_VW, GLA_GATE_RANK)
    offs = np.concatenate([[0], np.cumsum(sizes)])
    ws = [w_in[:, int(offs[i]):int(offs[i + 1])].astype(BF16) for i in range(len(sizes))]
    out_dtypes = (BF16, BF16, BF16, F32, F32, BF16, F32, F32)
    row = lambda n: pl.BlockSpec((tm, n), lambda i: (i, 0))
    full = lambda n: pl.BlockSpec((D, n), lambda i: (0, 0))
    return pl.pallas_call(
        _in_proj_kernel,
        grid=(T // tm,),
        in_specs=[row(D), pl.BlockSpec((1, D), lambda i: (0, 0))] + [full(n) for n in sizes],
        out_specs=[row(n) for n in sizes],
        out_shape=[jax.ShapeDtypeStruct((T, n), dt) for n, dt in zip(sizes, out_dtypes)],
        compiler_params=_params("parallel"),
        name="in_proj",
    )(x2, ln_g.reshape(1, D), *ws)


SB_ZERO_WEIGHT_BITS = 160.0


def _softplus2(z):
    return jnp.maximum(z, 0.0) + jnp.log2(1.0 + jnp.exp2(-jnp.abs(z)))


def _sb_attn_kernel(q_ref, k_ref, v_ref, o_ref, qh_ref, acc_ref, carry_ref, *, tq, n_pairs):
    qi = pl.program_id(1)
    r = lax.broadcasted_iota(jnp.int32, (tq, tq), 0)
    c = lax.broadcasted_iota(jnp.int32, (tq, tq), 1)
    causal = c < r
    suffix = (r >= c).astype(BF16)
    head0 = lax.broadcasted_iota(jnp.int32, (1, LANES), 1) < SB_HEAD_DIM

    for p in range(n_pairs):
        q2 = q_ref[:, p * LANES:(p + 1) * LANES]
        zero = jnp.zeros_like(q2)
        qh_ref[p, 0:tq] = jnp.where(head0, q2, zero)
        qh_ref[p, tq:2 * tq] = jnp.where(head0, zero, q2)
    acc_ref[...] = jnp.zeros_like(acc_ref)
    carry_ref[...] = jnp.zeros_like(carry_ref)
    causal2 = jnp.concatenate([causal, causal], axis=0)

    def key_block(kb, diag):
        rows = pl.ds(pl.multiple_of(kb * tq, tq), tq)
        for p in range(n_pairs):
            cols = pl.ds(p * LANES, LANES)
            k2 = k_ref[rows, cols]
            v2 = v_ref[rows, cols]
            vzero = jnp.zeros_like(v2)
            v_stack = jnp.concatenate([jnp.where(head0, v2, vzero), jnp.where(head0, vzero, v2)], axis=0)
            z = _dot_nt(qh_ref[p], k2)
            sp = _softplus2(z)
            if diag:
                sp = jnp.where(causal2, sp, 0.0)
            run = _dot(sp.astype(BF16), suffix)
            carry = carry_ref[p]
            w = jnp.exp2(z - run - jnp.concatenate([carry] * (tq // LANES), axis=1))
            if diag:
                w = jnp.where(causal2, w, 0.0)
            w = w.astype(BF16)
            acc_ref[p] += _dot(jnp.concatenate([w[0:tq], w[tq:2 * tq]], axis=1), v_stack)
            carry_ref[p] = carry + jnp.broadcast_to(run[:, 0:1], (2 * tq, LANES))

    def min_carry():
        m = carry_ref[0]
        for p in range(1, n_pairs):
            m = jnp.minimum(m, carry_ref[p])
        return jnp.min(m)

    def cond(state):
        kb, m = state
        return jnp.logical_and(kb >= 0, m < SB_ZERO_WEIGHT_BITS)

    def body(state):
        kb, _ = state
        key_block(kb, False)
        return kb - 1, min_carry()

    key_block(qi, True)
    lax.while_loop(cond, body, (qi - 1, min_carry()))
    for p in range(n_pairs):
        o_ref[:, p * LANES:(p + 1) * LANES] = acc_ref[p].astype(o_ref.dtype)


def _sb_attn(q, k, v, B, S, tq):
    W = q.shape[-1]
    n_pairs = W // LANES
    q3, k3, v3 = (a.reshape(B, S, W) for a in (q, k, v))
    blk = pl.BlockSpec((None, tq, W), lambda b, i: (b, i, 0))
    seq = pl.BlockSpec((None, S, W), lambda b, i: (b, 0, 0))
    out = pl.pallas_call(
        functools.partial(_sb_attn_kernel, tq=tq, n_pairs=n_pairs),
        grid=(B, S // tq),
        in_specs=[blk, seq, seq],
        out_specs=blk,
        out_shape=jax.ShapeDtypeStruct((B, S, W), BF16),
        scratch_shapes=[pltpu.VMEM((n_pairs, 2 * tq, LANES), BF16),
                        pltpu.VMEM((n_pairs, tq, LANES), F32),
                        pltpu.VMEM((n_pairs, 2 * tq, LANES), F32)],
        compiler_params=_params("parallel", "arbitrary"),
        name="sb_attn",
    )(q3, k3, v3)
    return out.reshape(B * S, W)


GLA_CHUNK = 64
GLA_LEVELS = (32, 16, 8, 4, 2, 1)


def _gla_tables(C=GLA_CHUNK):
    t = np.arange(C)[:, None]
    j = np.arange(C)[None, :]
    blocks = [(j <= t), (j > t)]
    q_rows, k_rows, masks = [], [], []
    for half in GLA_LEVELS:
        blk = 2 * half
        u = (t // blk) * blk + half - 1
        second = (t % blk) >= half
        q_rows.append(second & (j > u) & (j <= t))
        k_rows.append((~second) & (j > t) & (j <= u))
        s = j
        masks.append(second & ((s % blk) < half) & ((s // blk) == (t // blk)))
    blocks += q_rows + k_rows
    blocks.append(np.ones((8, C), bool))
    table = np.concatenate(blocks, axis=0).astype(np.float32)
    table = np.concatenate([table, table], axis=1)
    return table, np.concatenate(masks, axis=0).astype(np.float32)


def _split_bf16(a):
    hi = a.astype(BF16)
    lo = (a - hi.astype(F32)).astype(BF16)
    return hi, lo


def _gla_kernel(gq_ref, gk_ref, gv_ref, gr_ref, lr_ref, w2_ref, gb_ref, ng_ref, tab_ref, msk_ref,
                o_ref, state_ref, *, n_chunks):
    C = GLA_CHUNK
    n_lv = len(GLA_LEVELS)

    @pl.when(pl.program_id(1) == 0)
    def _():
        state_ref[...] = jnp.zeros_like(state_ref)

    head0 = lax.broadcasted_iota(jnp.int32, (1, LANES), 1) < GLA_DK
    eye = (lax.broadcasted_iota(jnp.int32, (C, C), 0) == lax.broadcasted_iota(jnp.int32, (C, C), 1))
    w2_hi, w2_lo = _split_bf16(w2_ref[...])
    lr_hi, lr_lo = _split_bf16(lr_ref[...])

    def stack_heads(a):
        return jnp.concatenate([jnp.where(head0, a, 0.0), jnp.where(head0, 0.0, a)], axis=0)

    for p in range(GLA_HEADS // 2):
        cols = slice(p * LANES, (p + 1) * LANES)
        pre = (_dot(lr_hi, w2_hi[:, cols]) + _dot(lr_hi, w2_lo[:, cols])
               + _dot(lr_lo, w2_hi[:, cols])) + gb_ref[:, cols]
        log_a = -_softplus(-pre) * (1.0 / GLA_GATE_TAU)
        la_hi, la_lo = _split_bf16(log_a)
        la = jnp.concatenate(
            [jnp.concatenate([la_hi[c * C:(c + 1) * C], la_lo[c * C:(c + 1) * C]], axis=0)
             for c in range(n_chunks)], axis=1)
        ex_all = jnp.exp(_dot(tab_ref[...], la))
        st = state_ref[p]
        for c in range(n_chunks):
            rows = slice(c * C, (c + 1) * C)
            ex = ex_all[:, c * LANES:(c + 1) * LANES]
            q2 = gq_ref[rows, cols]
            k2 = gk_ref[rows, cols]
            q_int = q2 * ex[0:C]
            k_dec = (k2 * ex[C:2 * C]).astype(BF16)
            dec_row = ex[(2 + 2 * n_lv) * C:(2 + 2 * n_lv) * C + 1]
            diag = jnp.sum(stack_heads(q2 * k2), axis=1, keepdims=True)
            scores = jnp.where(jnp.concatenate([eye, eye], axis=0), diag, 0.0)
            for l in range(n_lv):
                q_l = stack_heads(q2 * ex[(2 + l) * C:(3 + l) * C]).astype(BF16)
                k_l = (k2 * ex[(2 + n_lv + l) * C:(3 + n_lv + l) * C]).astype(BF16)
                m_l = msk_ref[l * C:(l + 1) * C, :] > 0.5
                scores = scores + jnp.where(jnp.concatenate([m_l, m_l], axis=0), _dot_nt(q_l, k_l), 0.0)
            scores = scores.astype(BF16)
            o_inter = _dot_nt(stack_heads(q_int).astype(BF16), st.astype(BF16))
            upd = []
            for h in range(2):
                vcols = slice((2 * p + h) * GLA_DV, (2 * p + h + 1) * GLA_DV)
                v_h = gv_ref[rows, vcols]
                o = o_inter[h * C:(h + 1) * C] + _dot(scores[h * C:(h + 1) * C], v_h)
                inv = lax.rsqrt(jnp.mean(o * o, axis=-1, keepdims=True) + RMS_EPS)
                y = (o * inv) * ng_ref[:, vcols]
                g = gr_ref[rows, vcols]
                y = y * (g / (1.0 + jnp.exp(-g)))
                o_ref[rows, vcols] = y.astype(o_ref.dtype)
                v_t = v_h.astype(F32).T.astype(BF16)
                upd.append(_dot(v_t, k_dec))
            st = st * dec_row + jnp.where(head0, upd[0], upd[1])
        state_ref[p] = st


def _gla(gq, gk, gv, gr, lr, w2, gb, ng, B, S, tb):
    table, masks = _gla_tables()
    table = jnp.asarray(table, BF16)
    masks = jnp.asarray(masks, F32)
    T = B * S
    nblk = S // tb
    row = lambda n: pl.BlockSpec((tb, n), lambda b, i: (b * nblk + i, 0))
    const = lambda a: pl.BlockSpec(a.shape, lambda b, i: (0,) * a.ndim)
    w2 = w2.astype(F32)
    gb2 = gb.reshape(1, GLA_KW)
    ng2 = ng.reshape(1, GLA_VW)
    return pl.pallas_call(
        functools.partial(_gla_kernel, n_chunks=tb // GLA_CHUNK),
        grid=(B, nblk),
        in_specs=[row(GLA_KW), row(GLA_KW), row(GLA_VW), row(GLA_VW), row(GLA_GATE_RANK),
                  const(w2), const(gb2), const(ng2), const(table), const(masks)],
        out_specs=row(GLA_VW),
        out_shape=jax.ShapeDtypeStruct((T, GLA_VW), BF16),
        scratch_shapes=[pltpu.VMEM((GLA_HEADS // 2, GLA_DV, 2 * GLA_DK), F32)],
        compiler_params=_params("parallel", "arbitrary"),
        name="gla",
    )(gq, gk, gv, gr, lr, w2, gb2, ng2, table, masks)


SUBLANES = 8


def _store_token_tiles(ref, value, accumulate=False):
    n = value.shape[0]
    for s in range(SUBLANES):
        rows = pl.ds(s, n, stride=SUBLANES)
        chunk = value[:, s * LANES:(s + 1) * LANES]
        ref[rows, :] = ref[rows, :] + chunk if accumulate else chunk


def _load_token_tiles(ref, n):
    return jnp.concatenate([ref[pl.ds(s, n, stride=SUBLANES), :] for s in range(SUBLANES)], axis=1)


def _out_proj_kernel(x_ref, oa_ref, ob_ref, wa_ref, wb_ref, g_ref, wr_ref, br_ref, tri_ref,
                     x1_ref, h2_ref, ri_ref, rf_ref, cnt_ref):
    @pl.when(pl.program_id(0) == 0)
    def _():
        cnt_ref[...] = jnp.zeros_like(cnt_ref)

    x1 = x_ref[...] + _dot(oa_ref[...], wa_ref[...]) + _dot(ob_ref[...], wb_ref[...])
    x1_ref[...] = x1
    inv = lax.rsqrt(jnp.mean(x1 * x1, axis=-1, keepdims=True) + RMS_EPS)
    h2 = (x1 * inv) * g_ref[...]
    _store_token_tiles(h2_ref, h2)

    h_hi, h_lo = _split_bf16(h2)
    w_hi, w_lo = _split_bf16(wr_ref[...])
    logits = _dot(h_hi, w_hi) + _dot(h_hi, w_lo) + _dot(h_lo, w_hi) + br_ref[...]
    lane = lax.broadcasted_iota(jnp.int32, logits.shape, 1)
    neg = -jnp.inf

    def first_max(vals):
        m = jnp.max(vals, axis=1, keepdims=True)
        idx = jnp.min(jnp.where(vals == m, lane, LANES), axis=1, keepdims=True)
        return m, idx

    is_group = lane < N_GROUPS
    g_max, g_idx = first_max(jnp.where(is_group, logits, neg))
    g_p = 1.0 / jnp.sum(jnp.where(is_group, jnp.exp(logits - g_max), 0.0), axis=1, keepdims=True)
    lo_lane = N_GROUPS + EXPERTS_PER_GROUP * g_idx
    sel = jnp.where((lane >= lo_lane) & (lane < lo_lane + EXPERTS_PER_GROUP), logits, neg)
    m1, i1 = first_max(sel)
    m2, i2 = first_max(jnp.where(lane == i1, neg, sel))
    e = jnp.exp(m2 - m1)
    gate1 = g_p / (1.0 + e)
    gate2 = g_p * e / (1.0 + e)
    rf_ref[...] = jnp.where(lane == 0, gate1, jnp.where(lane == 1, gate2, 0.0))

    e1 = i1 - N_GROUPS
    e2 = i2 - N_GROUPS
    hits = (lane == e1).astype(F32) + (lane == e2).astype(F32)
    before = _dot(tri_ref[...], hits.astype(BF16)) + cnt_ref[...]
    r1 = jnp.sum(jnp.where(lane == e1, before, 0.0), axis=1, keepdims=True)
    r2 = jnp.sum(jnp.where(lane == e2, before, 0.0), axis=1, keepdims=True)
    cnt_ref[...] += jnp.sum(hits, axis=0, keepdims=True)
    packed = jnp.where(lane == 0, e1.astype(F32), jnp.where(lane == 1, e2.astype(F32),
                       jnp.where(lane == 2, r1, jnp.where(lane == 3, r2, 0.0))))
    for j in range(packed.shape[0] // LANES):
        rows_t = packed[j * LANES:(j + 1) * LANES, :].T
        for k in range(2 * TOP_K):
            ri_ref[k, j:j + 1, :] = rows_t[k:k + 1, :].astype(jnp.int32)


def _out_proj(x2, o_sb, o_gla, w_out, ln_g, w_group, b_group, w_expert, b_expert, tm):
    T, D = x2.shape
    wa = w_out[:SB_WIDTH].astype(BF16)
    wb = w_out[SB_WIDTH:].astype(BF16)
    n_r = N_GROUPS + N_EXPERTS
    wr = jnp.zeros((D, LANES), F32).at[:, :n_r].set(jnp.concatenate([w_group, w_expert], axis=1))
    br = jnp.zeros((1, LANES), F32).at[0, :n_r].set(jnp.concatenate([b_group, b_expert]))
    row = lambda n: pl.BlockSpec((tm, n), lambda i: (i, 0))
    const = lambda a: pl.BlockSpec(a.shape, lambda i: (0,) * a.ndim)
    g2 = ln_g.reshape(1, D)
    tri = jnp.asarray(np.tril(np.ones((tm, tm), np.float32), -1), BF16)
    return pl.pallas_call(
        _out_proj_kernel,
        grid=(T // tm,),
        in_specs=[row(D), row(SB_WIDTH), row(GLA_VW), const(wa), const(wb), const(g2), const(wr), const(br),
                  const(tri)],
        out_specs=[row(D), pl.BlockSpec((tm * SUBLANES, LANES), lambda i: (i, 0)),
                   pl.BlockSpec((2 * TOP_K, tm // LANES, LANES), lambda i: (0, i, 0)), row(LANES),
                   pl.BlockSpec((1, LANES), lambda i: (0, 0))],
        out_shape=[jax.ShapeDtypeStruct((T, D), F32), jax.ShapeDtypeStruct((T * SUBLANES, LANES), F32),
                   jax.ShapeDtypeStruct((2 * TOP_K, T // LANES, LANES), jnp.int32),
                   jax.ShapeDtypeStruct((T, LANES), F32), jax.ShapeDtypeStruct((1, LANES), F32)],
        compiler_params=_params("arbitrary"),
        name="out_proj",
    )(x2, o_sb, o_gla, wa, wb, g2, wr, br, tri)


def _route_tables(route, counts, n_rows, tm):
    counts = counts[0, :N_EXPERTS].astype(jnp.int32)
    ends = jnp.cumsum(counts)
    starts = ends - counts
    route = route.reshape(2 * TOP_K, -1)
    expert = route[:TOP_K, :, None] == jnp.arange(N_EXPERTS, dtype=jnp.int32)
    dest = (jnp.sum(jnp.where(expert, starts, 0), axis=-1) + route[TOP_K:]).reshape(-1)

    n_blocks = n_rows // tm
    n_items = n_blocks + N_EXPERTS - 1
    first_blk = starts // tm
    last_blk = jnp.maximum(ends - 1, starts) // tm
    items_e = jnp.where(counts > 0, last_blk - first_blk + 1, 0)
    item_end = jnp.cumsum(items_e)
    item_start = item_end - items_e
    w = jnp.arange(n_items, dtype=jnp.int32)
    total = item_end[-1]
    valid = w < total
    wc = jnp.minimum(w, total - 1)
    it_e = jnp.sum(item_end[None, :] <= wc[:, None], axis=1).astype(jnp.int32)
    it_b = (first_blk[it_e] + (wc - item_start[it_e])).astype(jnp.int32)
    lo = jnp.clip(starts[it_e] - it_b * tm, 0, tm)
    hi = jnp.clip(ends[it_e] - it_b * tm, 0, tm)
    hi = jnp.where(valid, hi, lo)
    prev_b = jnp.concatenate([jnp.full((1,), -1, jnp.int32), it_b[:-1]])
    first = (it_b != prev_b).astype(jnp.int32)
    return dest, it_b, it_e, lo.astype(jnp.int32), hi.astype(jnp.int32), first


ROW_ISSUE_UNROLL = 8


def _token_tile(ref, t):
    return ref.at[pl.ds(pl.multiple_of(t * SUBLANES, SUBLANES), SUBLANES), :]


def _start_token_gather(src_ref, idx_ref, idx_base, idx_stride, dst_ref, sem, n):
    def body(i, _):
        for j in range(ROW_ISSUE_UNROLL):
            r = i * ROW_ISSUE_UNROLL + j
            t = idx_ref[idx_base + r * idx_stride]
            pltpu.make_async_copy(_token_tile(src_ref, t), _token_tile(dst_ref, r), sem).start(priority=j % 2)
        return 0
    lax.fori_loop(0, n // ROW_ISSUE_UNROLL, body, 0)


def _wait_token_gather(src_ref, dst_ref, sem):
    pltpu.make_async_copy(src_ref.at[pl.ds(0, dst_ref.shape[0]), :], dst_ref, sem).wait()


def _expert_kernel(blk_ref, exp_ref, lo_ref, hi_ref, first_ref, dest_ref,
                   h_ref, wg_ref, wu_ref, wd_ref, y_ref, xbuf, tok_ref, wg_bf, wu_bf, wd_bf,
                   sem, *, tm):
    w = pl.program_id(0)
    n_items = pl.num_programs(0)
    blk = blk_ref[w]

    def gather(b):
        _start_token_gather(h_ref, tok_ref, b * tm, 1, xbuf.at[b % 2], sem.at[b % 2], tm)

    @pl.when(w == 0)
    def _():
        n_tok = tok_ref.shape[0] // TOP_K

        def invert(t, _):
            for k in range(TOP_K):
                tok_ref[dest_ref[k * n_tok + t]] = t
            return 0
        lax.fori_loop(0, n_tok, invert, 0, unroll=ROW_ISSUE_UNROLL)
        gather(blk)

    nxt = jnp.minimum(w + 1, n_items - 1)

    @pl.when((w + 1 < n_items) & (first_ref[nxt] == 1))
    def _():
        gather(blk_ref[nxt])

    @pl.when(first_ref[w] == 1)
    def _():
        _wait_token_gather(h_ref, xbuf.at[blk % 2], sem.at[blk % 2])

    @pl.when((w == 0) | (exp_ref[w] != exp_ref[jnp.maximum(w - 1, 0)]))
    def _():
        wg_bf[...] = wg_ref[...].astype(BF16)
        wu_bf[...] = wu_ref[...].astype(BF16)
        wd_bf[...] = wd_ref[...].astype(BF16)

    xb = _load_token_tiles(xbuf.at[blk % 2], tm).astype(BF16)
    g = _dot(xb, wg_bf[...])
    u = _dot(xb, wu_bf[...])
    hidden = (g / (1.0 + jnp.exp(-g))) * u
    y = _dot(hidden.astype(BF16), wd_bf[...])
    r = lax.broadcasted_iota(jnp.int32, y.shape, 0)
    y = jnp.where((r >= lo_ref[w]) & (r < hi_ref[w]), y, 0.0)

    @pl.when(first_ref[w] == 1)
    def _():
        _store_token_tiles(y_ref, y)

    @pl.when(first_ref[w] == 0)
    def _():
        _store_token_tiles(y_ref, y, accumulate=True)


def _expert_ffn(h2_tiles, dest, tables, w_gate, w_up, w_down, tm):
    N = dest.shape[0]
    _, D, F = w_gate.shape
    assert D == SUBLANES * LANES
    it_b, it_e, lo, hi, first = tables
    n_items = it_b.shape[0]
    wspec = lambda shape: pl.BlockSpec((None,) + shape, lambda w, b, e, *_: (e[w], 0, 0))
    return pl.pallas_call(
        functools.partial(_expert_kernel, tm=tm),
        grid_spec=pltpu.PrefetchScalarGridSpec(
            num_scalar_prefetch=6,
            grid=(n_items,),
            in_specs=[pl.BlockSpec(memory_space=pl.ANY), wspec((D, F)), wspec((D, F)), wspec((F, D))],
            out_specs=pl.BlockSpec((tm * SUBLANES, LANES), lambda w, b, *_: (b[w], 0)),
            scratch_shapes=[pltpu.VMEM((2, tm * SUBLANES, LANES), F32),
                            pltpu.SMEM((N,), jnp.int32),
                            pltpu.VMEM((D, F), BF16), pltpu.VMEM((D, F), BF16), pltpu.VMEM((F, D), BF16),
                            pltpu.SemaphoreType.DMA((2,))],
        ),
        out_shape=jax.ShapeDtypeStruct((N * SUBLANES, LANES), F32),
        compiler_params=_params("arbitrary"),
        name="expert_ffn",
    )(it_b, it_e, lo, hi, first, dest, h2_tiles, w_gate, w_up, w_down)


def _combine_kernel(dest_ref, x1_ref, rf_ref, g_ref, y_ref, o_ref, buf, sem, *, tb):
    i = pl.program_id(0)
    n_blk = pl.num_programs(0)
    slot = i % 2

    def gather(blk, s):
        for k in range(TOP_K):
            _start_token_gather(y_ref, dest_ref, k * n_blk * tb + blk * tb, 1, buf.at[s, k], sem.at[s], tb)

    @pl.when(i == 0)
    def _():
        gather(0, 0)

    @pl.when(i + 1 < n_blk)
    def _():
        gather(i + 1, 1 - slot)

    for k in range(TOP_K):
        _wait_token_gather(y_ref, buf.at[slot, k], sem.at[slot])
    rf = rf_ref[...]
    x = (x1_ref[...] + rf[:, 0:1] * _load_token_tiles(buf.at[slot, 0], tb)
         + rf[:, 1:2] * _load_token_tiles(buf.at[slot, 1], tb))
    inv = lax.rsqrt(jnp.mean(x * x, axis=-1, keepdims=True) + RMS_EPS)
    o_ref[...] = (x * inv) * g_ref[...]


def _combine(x1, route_f, y_tiles, dest, ln_g, tb):
    T, D = x1.shape
    g2 = ln_g.reshape(1, D)
    return pl.pallas_call(
        functools.partial(_combine_kernel, tb=tb),
        grid_spec=pltpu.PrefetchScalarGridSpec(
            num_scalar_prefetch=1,
            grid=(T // tb,),
            in_specs=[pl.BlockSpec((tb, D), lambda i, d: (i, 0)),
                      pl.BlockSpec((tb, LANES), lambda i, d: (i, 0)),
                      pl.BlockSpec((1, D), lambda i, d: (0, 0)),
                      pl.BlockSpec(memory_space=pl.ANY)],
            out_specs=pl.BlockSpec((tb, D), lambda i, d: (i, 0)),
            scratch_shapes=[pltpu.VMEM((2, TOP_K, tb * SUBLANES, LANES), F32),
                            pltpu.SemaphoreType.DMA((2,))],
        ),
        out_shape=jax.ShapeDtypeStruct((T, D), F32),
        compiler_params=_params("arbitrary"),
        name="combine",
    )(dest, x1, route_f, g2, y_tiles)


def _layer(x2, B, S, ln1_g, w_in, w2, gb, ng, w_out, ln2_g, w_group, b_group, w_expert, b_expert,
           w_gate, w_up, w_down, ln_f_g, *, tm, tm_out, tq, tb_gla, tb_row, tm_e):
    q, k, v, gq, gk, gv, gr, lr = _in_proj(x2, ln1_g, w_in, tm)
    o_sb = _sb_attn(q, k, v, B, S, tq)
    o_gla = _gla(gq, gk, gv, gr, lr, w2, gb, ng, B, S, tb_gla)
    x1, h2_tiles, route_i, route_f, counts = _out_proj(x2, o_sb, o_gla, w_out, ln2_g, w_group, b_group,
                                                       w_expert, b_expert, tm_out)
    T = x2.shape[0]
    dest, *tables = _route_tables(route_i, counts, TOP_K * T, tm_e)
    y_tiles = _expert_ffn(h2_tiles, dest, tables, w_gate, w_up, w_down, tm_e)
    return _combine(x1, route_f, y_tiles, dest, ln_f_g, tb_row)


def kernel(x, ln1_g, w_in, gla_gate_w2, gla_gate_b, gla_norm_g, w_out, ln2_g, w_group, b_group, w_expert, b_expert, exp_w_gate, exp_w_up, exp_w_down, ln_f_g):
    B, S, D = x.shape
    assert ln1_g.shape[0] == 1, "single-layer stack"
    out = _layer(x.reshape(B * S, D), B, S, ln1_g[0], w_in[0], gla_gate_w2[0], gla_gate_b[0],
                 gla_norm_g[0], w_out[0], ln2_g[0], w_group[0], b_group[0], w_expert[0], b_expert[0],
                 exp_w_gate[0], exp_w_up[0], exp_w_down[0], ln_f_g,
                 tm=512, tm_out=1024, tq=256, tb_gla=512, tb_row=256, tm_e=256)
    return out.reshape(B, S, D)
```

```python
import functools

import numpy as np
import jax
import jax.numpy as jnp
from jax import lax
from jax.experimental import pallas as pl
from jax.experimental.pallas import tpu as pltpu

SB_HEADS = 8
SB_HEAD_DIM = 64
SB_WIDTH = SB_HEADS * SB_HEAD_DIM
GLA_HEADS = 4
GLA_DK = 64
GLA_DV = 128
GLA_KW = GLA_HEADS * GLA_DK
GLA_VW = GLA_HEADS * GLA_DV
GLA_GATE_RANK = 16
GLA_GATE_TAU = 16.0
N_GROUPS = 4
EXPERTS_PER_GROUP = 8
N_EXPERTS = N_GROUPS * EXPERTS_PER_GROUP
TOP_K = 2
RMS_EPS = 1e-6
LOG2_E = 1.4426950408889634

LANES = 128
VMEM_LIMIT_BYTES = 56 * 1024 * 1024

F32 = jnp.float32
BF16 = jnp.bfloat16


def _dot(a, b):
    return jnp.dot(a, b, preferred_element_type=F32)


def _dot_nt(a, b):
    return lax.dot_general(a, b, (((1,), (1,)), ((), ())), preferred_element_type=F32)


def _softplus(z):
    return jnp.maximum(z, 0.0) + jnp.log(1.0 + jnp.exp(-jnp.abs(z)))


def _params(*sem, flags=None):
    return pltpu.CompilerParams(dimension_semantics=sem, vmem_limit_bytes=VMEM_LIMIT_BYTES, flags=flags)


def _in_proj_kernel(x_ref, g_ref, wq, wk, wv, wgq, wgk, wgv, wgr, wlr,
                    oq, ok, ov, ogq, ogk, ogv, ogr, olr):
    x = x_ref[...]
    inv = lax.rsqrt(jnp.mean(x * x, axis=-1, keepdims=True) + RMS_EPS)
    h = ((x * inv) * g_ref[...]).astype(BF16)
    oq[...] = (_dot(h, wq[...]) * (SB_HEAD_DIM ** -0.5 * LOG2_E)).astype(oq.dtype)
    ok[...] = _dot(h, wk[...]).astype(ok.dtype)
    ov[...] = _dot(h, wv[...]).astype(ov.dtype)
    ogq[...] = (_dot(h, wgq[...]) * (GLA_DK ** -0.5)).astype(ogq.dtype)
    ogk[...] = _dot(h, wgk[...]).astype(ogk.dtype)
    ogv[...] = _dot(h, wgv[...]).astype(ogv.dtype)
    ogr[...] = _dot(h, wgr[...]).astype(ogr.dtype)
    olr[...] = _dot(h, wlr[...]).astype(olr.dtype)


def _in_proj(x2, ln_g, w_in, tm):
    T, D = x2.shape
    sizes = (SB_WIDTH, SB_WIDTH, SB_WIDTH, GLA_KW, GLA_KW, GLA_VW, GLA_VW, GLA_GATE_RANK)
    offs = np.concatenate([[0], np.cumsum(sizes)])
    ws = [w_in[:, int(offs[i]):int(offs[i + 1])].astype(BF16) for i in range(len(sizes))]
    out_dtypes = (BF16, BF16, BF16, F32, F32, BF16, F32, F32)
    row = lambda n: pl.BlockSpec((tm, n), lambda i: (i, 0))
    full = lambda n: pl.BlockSpec((D, n), lambda i: (0, 0))
    return pl.pallas_call(
        _in_proj_kernel,
        grid=(T // tm,),
        in_specs=[row(D), pl.BlockSpec((1, D), lambda i: (0, 0))] + [full(n) for n in sizes],
        out_specs=[row(n) for n in sizes],
        out_shape=[jax.ShapeDtypeStruct((T, n), dt) for n, dt in zip(sizes, out_dtypes)],
        compiler_params=_params("parallel"),
        name="in_proj",
    )(x2, ln_g.reshape(1, D), *ws)


SB_ZERO_WEIGHT_BITS = 160.0


def _softplus2(z):
    return jnp.maximum(z, 0.0) + jnp.log2(1.0 + jnp.exp2(-jnp.abs(z)))


def _sb_attn_kernel(q_ref, k_ref, v_ref, o_ref, qh_ref, acc_ref, carry_ref, *, tq, n_pairs):
    qi = pl.program_id(1)
    r = lax.broadcasted_iota(jnp.int32, (tq, tq), 0)
    c = lax.broadcasted_iota(jnp.int32, (tq, tq), 1)
    causal = c < r
    suffix = (r >= c).astype(BF16)
    head0 = lax.broadcasted_iota(jnp.int32, (1, LANES), 1) < SB_HEAD_DIM

    for p in range(n_pairs):
        q2 = q_ref[:, p * LANES:(p + 1) * LANES]
        zero = jnp.zeros_like(q2)
        qh_ref[p, 0:tq] = jnp.where(head0, q2, zero)
        qh_ref[p, tq:2 * tq] = jnp.where(head0, zero, q2)
    acc_ref[...] = jnp.zeros_like(acc_ref)
    carry_ref[...] = jnp.zeros_like(carry_ref)
    causal2 = jnp.concatenate([causal, causal], axis=0)

    def key_block(kb, diag):
        rows = pl.ds(pl.multiple_of(kb * tq, tq), tq)
        for p in range(n_pairs):
            cols = pl.ds(p * LANES, LANES)
            k2 = k_ref[rows, cols]
            v2 = v_ref[rows, cols]
            vzero = jnp.zeros_like(v2)
            v_stack = jnp.concatenate([jnp.where(head0, v2, vzero), jnp.where(head0, vzero, v2)], axis=0)
            z = _dot_nt(qh_ref[p], k2)
            sp = _softplus2(z)
            if diag:
                sp = jnp.where(causal2, sp, 0.0)
            run = _dot(sp.astype(BF16), suffix)
            carry = carry_ref[p]
            w = jnp.exp2(z - run - jnp.concatenate([carry] * (tq // LANES), axis=1))
            if diag:
                w = jnp.where(causal2, w, 0.0)
            w = w.astype(BF16)
            acc_ref[p] += _dot(jnp.concatenate([w[0:tq], w[tq:2 * tq]], axis=1), v_stack)
            carry_ref[p] = carry + jnp.broadcast_to(run[:, 0:1], (2 * tq, LANES))

    def min_carry():
        m = carry_ref[0]
        for p in range(1, n_pairs):
            m = jnp.minimum(m, carry_ref[p])
        return jnp.min(m)

    def cond(state):
        kb, m = state
        return jnp.logical_and(kb >= 0, m < SB_ZERO_WEIGHT_BITS)

    def body(state):
        kb, _ = state
        key_block(kb, False)
        return kb - 1, min_carry()

    key_block(qi, True)
    lax.while_loop(cond, body, (qi - 1, min_carry()))
    for p in range(n_pairs):
        o_ref[:, p * LANES:(p + 1) * LANES] = acc_ref[p].astype(o_ref.dtype)


def _sb_attn(q, k, v, B, S, tq):
    W = q.shape[-1]
    n_pairs = W // LANES
    q3, k3, v3 = (a.reshape(B, S, W) for a in (q, k, v))
    blk = pl.BlockSpec((None, tq, W), lambda b, i: (b, i, 0))
    seq = pl.BlockSpec((None, S, W), lambda b, i: (b, 0, 0))
    out = pl.pallas_call(
        functools.partial(_sb_attn_kernel, tq=tq, n_pairs=n_pairs),
        grid=(B, S // tq),
        in_specs=[blk, seq, seq],
        out_specs=blk,
        out_shape=jax.ShapeDtypeStruct((B, S, W), BF16),
        scratch_shapes=[pltpu.VMEM((n_pairs, 2 * tq, LANES), BF16),
                        pltpu.VMEM((n_pairs, tq, LANES), F32),
                        pltpu.VMEM((n_pairs, 2 * tq, LANES), F32)],
        compiler_params=_params("parallel", "arbitrary"),
        name="sb_attn",
    )(q3, k3, v3)
    return out.reshape(B * S, W)


GLA_CHUNK = 64
GLA_LEVELS = (32, 16, 8, 4, 2, 1)


def _gla_tables(C=GLA_CHUNK):
    t = np.arange(C)[:, None]
    j = np.arange(C)[None, :]
    blocks = [(j <= t), (j > t)]
    q_rows, k_rows, masks = [], [], []
    for half in GLA_LEVELS:
        blk = 2 * half
        u = (t // blk) * blk + half - 1
        second = (t % blk) >= half
        q_rows.append(second & (j > u) & (j <= t))
        k_rows.append((~second) & (j > t) & (j <= u))
        s = j
        masks.append(second & ((s % blk) < half) & ((s // blk) == (t // blk)))
    blocks += q_rows + k_rows
    blocks.append(np.ones((8, C), bool))
    table = np.concatenate(blocks, axis=0).astype(np.float32)
    table = np.concatenate([table, table], axis=1)
    return table, np.concatenate(masks, axis=0).astype(np.float32)


def _split_bf16(a):
    hi = a.astype(BF16)
    lo = (a - hi.astype(F32)).astype(BF16)
    return hi, lo


def _gla_kernel(gq_ref, gk_ref, gv_ref, gr_ref, lr_ref, w2_ref, gb_ref, ng_ref, tab_ref, msk_ref,
                o_ref, state_ref, *, n_chunks):
    C = GLA_CHUNK
    n_lv = len(GLA_LEVELS)

    @pl.when(pl.program_id(1) == 0)
    def _():
        state_ref[...] = jnp.zeros_like(state_ref)

    head0 = lax.broadcasted_iota(jnp.int32, (1, LANES), 1) < GLA_DK
    eye = (lax.broadcasted_iota(jnp.int32, (C, C), 0) == lax.broadcasted_iota(jnp.int32, (C, C), 1))
    w2_hi, w2_lo = _split_bf16(w2_ref[...])
    lr_hi, lr_lo = _split_bf16(lr_ref[...])

    def stack_heads(a):
        return jnp.concatenate([jnp.where(head0, a, 0.0), jnp.where(head0, 0.0, a)], axis=0)

    for p in range(GLA_HEADS // 2):
        cols = slice(p * LANES, (p + 1) * LANES)
        pre = (_dot(lr_hi, w2_hi[:, cols]) + _dot(lr_hi, w2_lo[:, cols])
               + _dot(lr_lo, w2_hi[:, cols])) + gb_ref[:, cols]
        log_a = -_softplus(-pre) * (1.0 / GLA_GATE_TAU)
        la_hi, la_lo = _split_bf16(log_a)
        la = jnp.concatenate(
            [jnp.concatenate([la_hi[c * C:(c + 1) * C], la_lo[c * C:(c + 1) * C]], axis=0)
             for c in range(n_chunks)], axis=1)
        ex_all = jnp.exp(_dot(tab_ref[...], la))
        st = state_ref[p]
        for c in range(n_chunks):
            rows = slice(c * C, (c + 1) * C)
            ex = ex_all[:, c * LANES:(c + 1) * LANES]
            q2 = gq_ref[rows, cols]
            k2 = gk_ref[rows, cols]
            q_int = q2 * ex[0:C]
            k_dec = (k2 * ex[C:2 * C]).astype(BF16)
            dec_row = ex[(2 + 2 * n_lv) * C:(2 + 2 * n_lv) * C + 1]
            diag = jnp.sum(stack_heads(q2 * k2), axis=1, keepdims=True)
            scores = jnp.where(jnp.concatenate([eye, eye], axis=0), diag, 0.0)
            for l in range(n_lv):
                q_l = stack_heads(q2 * ex[(2 + l) * C:(3 + l) * C]).astype(BF16)
                k_l = (k2 * ex[(2 + n_lv + l) * C:(3 + n_lv + l) * C]).astype(BF16)
                m_l = msk_ref[l * C:(l + 1) * C, :] > 0.5
                scores = scores + jnp.where(jnp.concatenate([m_l, m_l], axis=0), _dot_nt(q_l, k_l), 0.0)
            scores = scores.astype(BF16)
            o_inter = _dot_nt(stack_heads(q_int).astype(BF16), st.astype(BF16))
            upd = []
            for h in range(2):
                vcols = slice((2 * p + h) * GLA_DV, (2 * p + h + 1) * GLA_DV)
                v_h = gv_ref[rows, vcols]
                o = o_inter[h * C:(h + 1) * C] + _dot(scores[h * C:(h + 1) * C], v_h)
                inv = lax.rsqrt(jnp.mean(o * o, axis=-1, keepdims=True) + RMS_EPS)
                y = (o * inv) * ng_ref[:, vcols]
                g = gr_ref[rows, vcols]
                y = y * (g / (1.0 + jnp.exp(-g)))
                o_ref[rows, vcols] = y.astype(o_ref.dtype)
                v_t = v_h.astype(F32).T.astype(BF16)
                upd.append(_dot(v_t, k_dec))
            st = st * dec_row + jnp.where(head0, upd[0], upd[1])
        state_ref[p] = st


def _gla(gq, gk, gv, gr, lr, w2, gb, ng, B, S, tb):
    table, masks = _gla_tables()
    table = jnp.asarray(table, BF16)
    masks = jnp.asarray(masks, F32)
    T = B * S
    nblk = S // tb
    row = lambda n: pl.BlockSpec((tb, n), lambda b, i: (b * nblk + i, 0))
    const = lambda a: pl.BlockSpec(a.shape, lambda b, i: (0,) * a.ndim)
    w2 = w2.astype(F32)
    gb2 = gb.reshape(1, GLA_KW)
    ng2 = ng.reshape(1, GLA_VW)
    return pl.pallas_call(
        functools.partial(_gla_kernel, n_chunks=tb // GLA_CHUNK),
        grid=(B, nblk),
        in_specs=[row(GLA_KW), row(GLA_KW), row(GLA_VW), row(GLA_VW), row(GLA_GATE_RANK),
                  const(w2), const(gb2), const(ng2), const(table), const(masks)],
        out_specs=row(GLA_VW),
        out_shape=jax.ShapeDtypeStruct((T, GLA_VW), BF16),
        scratch_shapes=[pltpu.VMEM((GLA_HEADS // 2, GLA_DV, 2 * GLA_DK), F32)],
        compiler_params=_params("parallel", "arbitrary"),
        name="gla",
    )(gq, gk, gv, gr, lr, w2, gb2, ng2, table, masks)


SUBLANES = 8


def _store_token_tiles(ref, value, accumulate=False):
    n = value.shape[0]
    for s in range(SUBLANES):
        rows = pl.ds(s, n, stride=SUBLANES)
        chunk = value[:, s * LANES:(s + 1) * LANES]
        ref[rows, :] = ref[rows, :] + chunk if accumulate else chunk


def _load_token_tiles(ref, n):
    return jnp.concatenate([ref[pl.ds(s, n, stride=SUBLANES), :] for s in range(SUBLANES)], axis=1)


def _out_proj_kernel(x_ref, oa_ref, ob_ref, wa_ref, wb_ref, g_ref, wr_ref, br_ref, tri_ref,
                     x1_ref, h2_ref, ri_ref, rf_ref, cnt_ref):
    @pl.when(pl.program_id(0) == 0)
    def _():
        cnt_ref[...] = jnp.zeros_like(cnt_ref)

    x1 = x_ref[...] + _dot(oa_ref[...], wa_ref[...]) + _dot(ob_ref[...], wb_ref[...])
    x1_ref[...] = x1
    inv = lax.rsqrt(jnp.mean(x1 * x1, axis=-1, keepdims=True) + RMS_EPS)
    h2 = (x1 * inv) * g_ref[...]
    _store_token_tiles(h2_ref, h2)

    h_hi, h_lo = _split_bf16(h2)
    w_hi, w_lo = _split_bf16(wr_ref[...])
    logits = _dot(h_hi, w_hi) + _dot(h_hi, w_lo) + _dot(h_lo, w_hi) + br_ref[...]
    lane = lax.broadcasted_iota(jnp.int32, logits.shape, 1)
    neg = -jnp.inf

    def first_max(vals):
        m = jnp.max(vals, axis=1, keepdims=True)
        idx = jnp.min(jnp.where(vals == m, lane, LANES), axis=1, keepdims=True)
        return m, idx

    is_group = lane < N_GROUPS
    g_max, g_idx = first_max(jnp.where(is_group, logits, neg))
    g_p = 1.0 / jnp.sum(jnp.where(is_group, jnp.exp(logits - g_max), 0.0), axis=1, keepdims=True)
    lo_lane = N_GROUPS + EXPERTS_PER_GROUP * g_idx
    sel = jnp.where((lane >= lo_lane) & (lane < lo_lane + EXPERTS_PER_GROUP), logits, neg)
    m1, i1 = first_max(sel)
    m2, i2 = first_max(jnp.where(lane == i1, neg, sel))
    e = jnp.exp(m2 - m1)
    gate1 = g_p / (1.0 + e)
    gate2 = g_p * e / (1.0 + e)
    rf_ref[...] = jnp.where(lane == 0, gate1, jnp.where(lane == 1, gate2, 0.0))

    e1 = i1 - N_GROUPS
    e2 = i2 - N_GROUPS
    hits = (lane == e1).astype(F32) + (lane == e2).astype(F32)
    before = _dot(tri_ref[...], hits.astype(BF16)) + cnt_ref[...]
    r1 = jnp.sum(jnp.where(lane == e1, before, 0.0), axis=1, keepdims=True)
    r2 = jnp.sum(jnp.where(lane == e2, before, 0.0), axis=1, keepdims=True)
    cnt_ref[...] += jnp.sum(hits, axis=0, keepdims=True)
    packed = jnp.where(lane == 0, e1.astype(F32), jnp.where(lane == 1, e2.astype(F32),
                       jnp.where(lane == 2, r1, jnp.where(lane == 3, r2, 0.0))))
    for j in range(packed.shape[0] // LANES):
        rows_t = packed[j * LANES:(j + 1) * LANES, :].T
        for k in range(2 * TOP_K):
            ri_ref[k, j:j + 1, :] = rows_t[k:k + 1, :].astype(jnp.int32)


def _out_proj(x2, o_sb, o_gla, w_out, ln_g, w_group, b_group, w_expert, b_expert, tm):
    T, D = x2.shape
    wa = w_out[:SB_WIDTH].astype(BF16)
    wb = w_out[SB_WIDTH:].astype(BF16)
    n_r = N_GROUPS + N_EXPERTS
    wr = jnp.zeros((D, LANES), F32).at[:, :n_r].set(jnp.concatenate([w_group, w_expert], axis=1))
    br = jnp.zeros((1, LANES), F32).at[0, :n_r].set(jnp.concatenate([b_group, b_expert]))
    row = lambda n: pl.BlockSpec((tm, n), lambda i: (i, 0))
    const = lambda a: pl.BlockSpec(a.shape, lambda i: (0,) * a.ndim)
    g2 = ln_g.reshape(1, D)
    tri = jnp.asarray(np.tril(np.ones((tm, tm), np.float32), -1), BF16)
    return pl.pallas_call(
        _out_proj_kernel,
        grid=(T // tm,),
        in_specs=[row(D), row(SB_WIDTH), row(GLA_VW), const(wa), const(wb), const(g2), const(wr), const(br),
                  const(tri)],
        out_specs=[row(D), pl.BlockSpec((tm * SUBLANES, LANES), lambda i: (i, 0)),
                   pl.BlockSpec((2 * TOP_K, tm // LANES, LANES), lambda i: (0, i, 0)), row(LANES),
                   pl.BlockSpec((1, LANES), lambda i: (0, 0))],
        out_shape=[jax.ShapeDtypeStruct((T, D), F32), jax.ShapeDtypeStruct((T * SUBLANES, LANES), F32),
                   jax.ShapeDtypeStruct((2 * TOP_K, T // LANES, LANES), jnp.int32),
                   jax.ShapeDtypeStruct((T, LANES), F32), jax.ShapeDtypeStruct((1, LANES), F32)],
        compiler_params=_params("arbitrary"),
        name="out_proj",
    )(x2, o_sb, o_gla, wa, wb, g2, wr, br, tri)


def _route_tables(route, counts, n_rows, tm):
    counts = counts[0, :N_EXPERTS].astype(jnp.int32)
    ends = jnp.cumsum(counts)
    starts = ends - counts
    route = route.reshape(2 * TOP_K, -1)
    expert = route[:TOP_K, :, None] == jnp.arange(N_EXPERTS, dtype=jnp.int32)
    dest = (jnp.sum(jnp.where(expert, starts, 0), axis=-1) + route[TOP_K:]).reshape(-1)

    n_items = n_rows // tm + N_EXPERTS
    items_e = (counts + tm - 1) // tm
    item_hi = jnp.cumsum(items_e)
    item_lo = item_hi - items_e
    i = jnp.arange(n_items, dtype=jnp.int32)
    e_of = jnp.minimum(jnp.sum(item_hi[None, :] <= i[:, None], axis=1), N_EXPERTS - 1)
    owner = e_of[:, None] == jnp.arange(N_EXPERTS, dtype=jnp.int32)
    pick = lambda tab: jnp.sum(jnp.where(owner, tab, 0), axis=-1)
    item_row = jnp.where(i < item_hi[-1], pick(starts) + (i - pick(item_lo)) * tm, n_rows)
    return (dest, item_row.astype(jnp.int32), item_lo.astype(jnp.int32), item_hi.astype(jnp.int32))


ROW_ISSUE_UNROLL = 8


def _token_tile(ref, t):
    return ref.at[pl.ds(pl.multiple_of(t * SUBLANES, SUBLANES), SUBLANES), :]


def _start_token_gather(src_ref, idx_ref, idx_base, idx_stride, dst_ref, sem, n):
    def body(i, _):
        for j in range(ROW_ISSUE_UNROLL):
            r = i * ROW_ISSUE_UNROLL + j
            t = idx_ref[idx_base + r * idx_stride]
            pltpu.make_async_copy(_token_tile(src_ref, t), _token_tile(dst_ref, r), sem).start(priority=j % 2)
        return 0
    lax.fori_loop(0, n // ROW_ISSUE_UNROLL, body, 0)


def _wait_token_gather(src_ref, dst_ref, sem):
    pltpu.make_async_copy(src_ref.at[pl.ds(0, dst_ref.shape[0]), :], dst_ref, sem).wait()


def _expert_kernel(row_ref, lo_ref, hi_ref, dest_ref,
                   h_ref, wg_ref, wu_ref, wd_ref, y_ref, xbuf, ybuf, tok_ref, wg_bf, wu_bf, wd_bf,
                   gsem, wsem, *, tm):
    e = pl.program_id(0)
    n_rows = dest_ref.shape[0]
    n_tok = n_rows // TOP_K
    total = hi_ref[N_EXPERTS - 1]

    def gather_copy(i, r):
        slot = i % 2
        t = tok_ref[row_ref[i] + r]
        return pltpu.make_async_copy(_token_tile(h_ref, t), _token_tile(xbuf.at[slot], r), gsem.at[slot])

    def out_copy(i):
        slot = i % 2
        rows = pl.ds(pl.multiple_of(row_ref[i] * SUBLANES, SUBLANES), tm * SUBLANES)
        return pltpu.make_async_copy(ybuf.at[slot], y_ref.at[rows, :], wsem)

    @pl.when(e == 0)
    def _():
        def invert(t, _):
            for k in range(TOP_K):
                tok_ref[dest_ref[k * n_tok + t]] = t
            return 0
        lax.fori_loop(0, n_tok, invert, 0, unroll=ROW_ISSUE_UNROLL)

        def spare(r, _):
            tok_ref[n_rows + r] = 0
            return 0
        lax.fori_loop(0, tm, spare, 0, unroll=ROW_ISSUE_UNROLL)
        ybuf[0] = jnp.zeros(ybuf.shape[1:], ybuf.dtype)
        zero_spare = pltpu.make_async_copy(
            ybuf.at[0], y_ref.at[pl.ds(n_rows * SUBLANES, tm * SUBLANES), :], wsem)
        zero_spare.start()
        zero_spare.wait()

        @pl.when(total > 0)
        def _():
            def first(r, _):
                gather_copy(0, r).start()
                return 0
            lax.fori_loop(0, tm, first, 0, unroll=ROW_ISSUE_UNROLL)

    wg_bf[...] = wg_ref[...].astype(BF16)
    wu_bf[...] = wu_ref[...].astype(BF16)
    wd_bf[...] = wd_ref[...].astype(BF16)

    def item(i, _):
        slot = i % 2
        _wait_token_gather(h_ref, xbuf.at[slot], gsem.at[slot])
        xb = _load_token_tiles(xbuf.at[slot], tm).astype(BF16)
        g = _dot(xb, wg_bf[...])
        u = _dot(xb, wu_bf[...])
        hidden = ((g / (1.0 + jnp.exp(-g))) * u).astype(BF16)

        def down_proj():
            _store_token_tiles(ybuf.at[slot], _dot(hidden, wd_bf[...]))

        @pl.when(i + 1 < total)
        def _():
            for r in range(tm):
                gather_copy(i + 1, r).start(priority=r % 2)
            down_proj()

        @pl.when(i + 1 >= total)
        def _():
            down_proj()

        @pl.when(i > 0)
        def _():
            out_copy(i - 1).wait()
        out_copy(i).start()
        return 0

    lax.fori_loop(lo_ref[e], hi_ref[e], item, 0)

    @pl.when((e == pl.num_programs(0) - 1) & (total > 0))
    def _():
        out_copy(total - 1).wait()


def _expert_ffn(h2_tiles, dest, tables, w_gate, w_up, w_down, tm):
    N = dest.shape[0]
    _, D, F = w_gate.shape
    assert D == SUBLANES * LANES
    item_row, item_lo, item_hi = tables
    wspec = lambda shape: pl.BlockSpec((None,) + shape, lambda e, *_: (e, 0, 0))
    return pl.pallas_call(
        functools.partial(_expert_kernel, tm=tm),
        grid_spec=pltpu.PrefetchScalarGridSpec(
            num_scalar_prefetch=4,
            grid=(N_EXPERTS,),
            in_specs=[pl.BlockSpec(memory_space=pl.ANY), wspec((D, F)), wspec((D, F)), wspec((F, D))],
            out_specs=pl.BlockSpec(memory_space=pl.ANY),
            scratch_shapes=[pltpu.VMEM((2, tm * SUBLANES, LANES), F32),
                            pltpu.VMEM((2, tm * SUBLANES, LANES), F32),
                            pltpu.SMEM((N + tm,), jnp.int32),
                            pltpu.VMEM((D, F), BF16), pltpu.VMEM((D, F), BF16), pltpu.VMEM((F, D), BF16),
                            pltpu.SemaphoreType.DMA((2,)), pltpu.SemaphoreType.DMA(())],
        ),
        out_shape=jax.ShapeDtypeStruct(((N + tm) * SUBLANES, LANES), F32),
        compiler_params=_params("arbitrary"),
        name="expert_ffn",
    )(item_row, item_lo, item_hi, dest, h2_tiles, w_gate, w_up, w_down)


def _combine_kernel(dest_ref, x1_ref, rf_ref, g_ref, y_ref, o_ref, buf, sem, *, tb):
    i = pl.program_id(0)
    n_blk = pl.num_programs(0)
    slot = i % 2

    def gather(blk, s):
        for k in range(TOP_K):
            _start_token_gather(y_ref, dest_ref, k * n_blk * tb + blk * tb, 1, buf.at[s, k], sem.at[s], tb)

    @pl.when(i == 0)
    def _():
        gather(0, 0)

    @pl.when(i + 1 < n_blk)
    def _():
        gather(i + 1, 1 - slot)

    for k in range(TOP_K):
        _wait_token_gather(y_ref, buf.at[slot, k], sem.at[slot])
    rf = rf_ref[...]
    x = (x1_ref[...] + rf[:, 0:1] * _load_token_tiles(buf.at[slot, 0], tb)
         + rf[:, 1:2] * _load_token_tiles(buf.at[slot, 1], tb))
    inv = lax.rsqrt(jnp.mean(x * x, axis=-1, keepdims=True) + RMS_EPS)
    o_ref[...] = (x * inv) * g_ref[...]


def _combine(x1, route_f, y_tiles, dest, ln_g, tb):
    T, D = x1.shape
    g2 = ln_g.reshape(1, D)
    return pl.pallas_call(
        functools.partial(_combine_kernel, tb=tb),
        grid_spec=pltpu.PrefetchScalarGridSpec(
            num_scalar_prefetch=1,
            grid=(T // tb,),
            in_specs=[pl.BlockSpec((tb, D), lambda i, d: (i, 0)),
                      pl.BlockSpec((tb, LANES), lambda i, d: (i, 0)),
                      pl.BlockSpec((1, D), lambda i, d: (0, 0)),
                      pl.BlockSpec(memory_space=pl.ANY)],
            out_specs=pl.BlockSpec((tb, D), lambda i, d: (i, 0)),
            scratch_shapes=[pltpu.VMEM((2, TOP_K, tb * SUBLANES, LANES), F32),
                            pltpu.SemaphoreType.DMA((2,))],
        ),
        out_shape=jax.ShapeDtypeStruct((T, D), F32),
        compiler_params=_params("arbitrary"),
        name="combine",
    )(dest, x1, route_f, g2, y_tiles)


def _layer(x2, B, S, ln1_g, w_in, w2, gb, ng, w_out, ln2_g, w_group, b_group, w_expert, b_expert,
           w_gate, w_up, w_down, ln_f_g, *, tm, tm_out, tq, tb_gla, tb_row, tm_e):
    q, k, v, gq, gk, gv, gr, lr = _in_proj(x2, ln1_g, w_in, tm)
    o_sb = _sb_attn(q, k, v, B, S, tq)
    o_gla = _gla(gq, gk, gv, gr, lr, w2, gb, ng, B, S, tb_gla)
    x1, h2_tiles, route_i, route_f, counts = _out_proj(x2, o_sb, o_gla, w_out, ln2_g, w_group, b_group,
                                                       w_expert, b_expert, tm_out)
    T = x2.shape[0]
    dest, *tables = _route_tables(route_i, counts, TOP_K * T, tm_e)
    y_tiles = _expert_ffn(h2_tiles, dest, tables, w_gate, w_up, w_down, tm_e)
    return _combine(x1, route_f, y_tiles, dest, ln_f_g, tb_row)


def kernel(x, ln1_g, w_in, gla_gate_w2, gla_gate_b, gla_norm_g, w_out, ln2_g, w_group, b_group, w_expert, b_expert, exp_w_gate, exp_w_up, exp_w_down, ln_f_g):
    B, S, D = x.shape
    assert ln1_g.shape[0] == 1, "single-layer stack"
    out = _layer(x.reshape(B * S, D), B, S, ln1_g[0], w_in[0], gla_gate_w2[0], gla_gate_b[0],
                 gla_norm_g[0], w_out[0], ln2_g[0], w_group[0], b_group[0], w_expert[0], b_expert[0],
                 exp_w_gate[0], exp_w_up[0], exp_w_down[0], ln_f_g,
                 tm=512, tm_out=1024, tq=256, tb_gla=512, tb_row=256, tm_e=256)
    return out.reshape(B, S, D)
```

```python
import functools

import numpy as np
import jax
import jax.numpy as jnp
from jax import lax
from jax.experimental import pallas as pl
from jax.experimental.pallas import tpu as pltpu

SB_HEADS = 8
SB_HEAD_DIM = 64
SB_WIDTH = SB_HEADS * SB_HEAD_DIM
GLA_HEADS = 4
GLA_DK = 64
GLA_DV = 128
GLA_KW = GLA_HEADS * GLA_DK
GLA_VW = GLA_HEADS * GLA_DV
GLA_GATE_RANK = 16
GLA_GATE_TAU = 16.0
N_GROUPS = 4
EXPERTS_PER_GROUP = 8
N_EXPERTS = N_GROUPS * EXPERTS_PER_GROUP
TOP_K = 2
RMS_EPS = 1e-6
LOG2_E = 1.4426950408889634

LANES = 128
VMEM_LIMIT_BYTES = 56 * 1024 * 1024

F32 = jnp.float32
BF16 = jnp.bfloat16


def _dot(a, b):
    return jnp.dot(a, b, preferred_element_type=F32)


def _dot_nt(a, b):
    return lax.dot_general(a, b, (((1,), (1,)), ((), ())), preferred_element_type=F32)


def _softplus(z):
    return jnp.maximum(z, 0.0) + jnp.log(1.0 + jnp.exp(-jnp.abs(z)))


def _params(*sem, flags=None):
    return pltpu.CompilerParams(dimension_semantics=sem, vmem_limit_bytes=VMEM_LIMIT_BYTES, flags=flags)


def _in_proj_kernel(x_ref, g_ref, wq, wk, wv, wgq, wgk, wgv, wgr, wlr,
                    oq, ok, ov, ogq, ogk, ogv, ogr, olr):
    x = x_ref[...]
    inv = lax.rsqrt(jnp.mean(x * x, axis=-1, keepdims=True) + RMS_EPS)
    h = ((x * inv) * g_ref[...]).astype(BF16)
    oq[...] = (_dot(h, wq[...]) * (SB_HEAD_DIM ** -0.5 * LOG2_E)).astype(oq.dtype)
    ok[...] = _dot(h, wk[...]).astype(ok.dtype)
    ov[...] = _dot(h, wv[...]).astype(ov.dtype)
    ogq[...] = (_dot(h, wgq[...]) * (GLA_DK ** -0.5)).astype(ogq.dtype)
    ogk[...] = _dot(h, wgk[...]).astype(ogk.dtype)
    ogv[...] = _dot(h, wgv[...]).astype(ogv.dtype)
    ogr[...] = _dot(h, wgr[...]).astype(ogr.dtype)
    olr[...] = _dot(h, wlr[...]).astype(olr.dtype)


def _in_proj(x2, ln_g, w_in, tm):
    T, D = x2.shape
    sizes = (SB_WIDTH, SB_WIDTH, SB_WIDTH, GLA_KW, GLA_KW, GLA_VW, GLA_VW, GLA_GATE_RANK)
    offs = np.concatenate([[0], np.cumsum(sizes)])
    ws = [w_in[:, int(offs[i]):int(offs[i + 1])].astype(BF16) for i in range(len(sizes))]
    out_dtypes = (BF16, BF16, BF16, F32, F32, BF16, F32, F32)
    row = lambda n: pl.BlockSpec((tm, n), lambda i: (i, 0))
    full = lambda n: pl.BlockSpec((D, n), lambda i: (0, 0))
    return pl.pallas_call(
        _in_proj_kernel,
        grid=(T // tm,),
        in_specs=[row(D), pl.BlockSpec((1, D), lambda i: (0, 0))] + [full(n) for n in sizes],
        out_specs=[row(n) for n in sizes],
        out_shape=[jax.ShapeDtypeStruct((T, n), dt) for n, dt in zip(sizes, out_dtypes)],
        compiler_params=_params("parallel"),
        name="in_proj",
    )(x2, ln_g.reshape(1, D), *ws)


SB_ZERO_WEIGHT_BITS = 160.0


def _softplus2(z):
    return jnp.maximum(z, 0.0) + jnp.log2(1.0 + jnp.exp2(-jnp.abs(z)))


def _sb_attn_kernel(q_ref, k_ref, v_ref, o_ref, qh_ref, acc_ref, carry_ref, *, tq, n_pairs):
    qi = pl.program_id(1)
    r = lax.broadcasted_iota(jnp.int32, (tq, tq), 0)
    c = lax.broadcasted_iota(jnp.int32, (tq, tq), 1)
    causal = c < r
    suffix = (r >= c).astype(BF16)
    head0 = lax.broadcasted_iota(jnp.int32, (1, LANES), 1) < SB_HEAD_DIM

    for p in range(n_pairs):
        q2 = q_ref[:, p * LANES:(p + 1) * LANES]
        zero = jnp.zeros_like(q2)
        qh_ref[p, 0:tq] = jnp.where(head0, q2, zero)
        qh_ref[p, tq:2 * tq] = jnp.where(head0, zero, q2)
    acc_ref[...] = jnp.zeros_like(acc_ref)
    carry_ref[...] = jnp.zeros_like(carry_ref)
    causal2 = jnp.concatenate([causal, causal], axis=0)

    def key_block(kb, diag):
        rows = pl.ds(pl.multiple_of(kb * tq, tq), tq)
        for p in range(n_pairs):
            cols = pl.ds(p * LANES, LANES)
            k2 = k_ref[rows, cols]
            v2 = v_ref[rows, cols]
            vzero = jnp.zeros_like(v2)
            v_stack = jnp.concatenate([jnp.where(head0, v2, vzero), jnp.where(head0, vzero, v2)], axis=0)
            z = _dot_nt(qh_ref[p], k2)
            sp = _softplus2(z)
            if diag:
                sp = jnp.where(causal2, sp, 0.0)
            run = _dot(sp.astype(BF16), suffix)
            carry = carry_ref[p]
            w = jnp.exp2(z - run - jnp.concatenate([carry] * (tq // LANES), axis=1))
            if diag:
                w = jnp.where(causal2, w, 0.0)
            w = w.astype(BF16)
            acc_ref[p] += _dot(jnp.concatenate([w[0:tq], w[tq:2 * tq]], axis=1), v_stack)
            carry_ref[p] = carry + jnp.broadcast_to(run[:, 0:1], (2 * tq, LANES))

    def min_carry():
        m = carry_ref[0]
        for p in range(1, n_pairs):
            m = jnp.minimum(m, carry_ref[p])
        return jnp.min(m)

    def cond(state):
        kb, m = state
        return jnp.logical_and(kb >= 0, m < SB_ZERO_WEIGHT_BITS)

    def body(state):
        kb, _ = state
        key_block(kb, False)
        return kb - 1, min_carry()

    key_block(qi, True)
    lax.while_loop(cond, body, (qi - 1, min_carry()))
    for p in range(n_pairs):
        o_ref[:, p * LANES:(p + 1) * LANES] = acc_ref[p].astype(o_ref.dtype)


def _sb_attn(q, k, v, B, S, tq):
    W = q.shape[-1]
    n_pairs = W // LANES
    q3, k3, v3 = (a.reshape(B, S, W) for a in (q, k, v))
    blk = pl.BlockSpec((None, tq, W), lambda b, i: (b, i, 0))
    seq = pl.BlockSpec((None, S, W), lambda b, i: (b, 0, 0))
    out = pl.pallas_call(
        functools.partial(_sb_attn_kernel, tq=tq, n_pairs=n_pairs),
        grid=(B, S // tq),
        in_specs=[blk, seq, seq],
        out_specs=blk,
        out_shape=jax.ShapeDtypeStruct((B, S, W), BF16),
        scratch_shapes=[pltpu.VMEM((n_pairs, 2 * tq, LANES), BF16),
                        pltpu.VMEM((n_pairs, tq, LANES), F32),
                        pltpu.VMEM((n_pairs, 2 * tq, LANES), F32)],
        compiler_params=_params("parallel", "arbitrary"),
        name="sb_attn",
    )(q3, k3, v3)
    return out.reshape(B * S, W)


GLA_CHUNK = 64
GLA_LEVELS = (32, 16, 8, 4, 2, 1)


def _gla_tables(C=GLA_CHUNK):
    t = np.arange(C)[:, None]
    j = np.arange(C)[None, :]
    blocks = [(j <= t), (j > t)]
    q_rows, k_rows, masks = [], [], []
    for half in GLA_LEVELS:
        blk = 2 * half
        u = (t // blk) * blk + half - 1
        second = (t % blk) >= half
        q_rows.append(second & (j > u) & (j <= t))
        k_rows.append((~second) & (j > t) & (j <= u))
        s = j
        masks.append(second & ((s % blk) < half) & ((s // blk) == (t // blk)))
    blocks += q_rows + k_rows
    blocks.append(np.ones((8, C), bool))
    table = np.concatenate(blocks, axis=0).astype(np.float32)
    table = np.concatenate([table, table], axis=1)
    return table, np.concatenate(masks, axis=0).astype(np.float32)


def _split_bf16(a):
    hi = a.astype(BF16)
    lo = (a - hi.astype(F32)).astype(BF16)
    return hi, lo


def _gla_kernel(gq_ref, gk_ref, gv_ref, gr_ref, lr_ref, w2_ref, gb_ref, ng_ref, tab_ref, msk_ref,
                o_ref, state_ref, *, n_chunks):
    C = GLA_CHUNK
    n_lv = len(GLA_LEVELS)

    @pl.when(pl.program_id(1) == 0)
    def _():
        state_ref[...] = jnp.zeros_like(state_ref)

    head0 = lax.broadcasted_iota(jnp.int32, (1, LANES), 1) < GLA_DK
    eye = (lax.broadcasted_iota(jnp.int32, (C, C), 0) == lax.broadcasted_iota(jnp.int32, (C, C), 1))
    w2_hi, w2_lo = _split_bf16(w2_ref[...])
    lr_hi, lr_lo = _split_bf16(lr_ref[...])

    def stack_heads(a):
        return jnp.concatenate([jnp.where(head0, a, 0.0), jnp.where(head0, 0.0, a)], axis=0)

    for p in range(GLA_HEADS // 2):
        cols = slice(p * LANES, (p + 1) * LANES)
        pre = (_dot(lr_hi, w2_hi[:, cols]) + _dot(lr_hi, w2_lo[:, cols])
               + _dot(lr_lo, w2_hi[:, cols])) + gb_ref[:, cols]
        log_a = -_softplus(-pre) * (1.0 / GLA_GATE_TAU)
        la_hi, la_lo = _split_bf16(log_a)
        la = jnp.concatenate(
            [jnp.concatenate([la_hi[c * C:(c + 1) * C], la_lo[c * C:(c + 1) * C]], axis=0)
             for c in range(n_chunks)], axis=1)
        ex_all = jnp.exp(_dot(tab_ref[...], la))
        st = state_ref[p]
        for c in range(n_chunks):
            rows = slice(c * C, (c + 1) * C)
            ex = ex_all[:, c * LANES:(c + 1) * LANES]
            q2 = gq_ref[rows, cols]
            k2 = gk_ref[rows, cols]
            q_int = q2 * ex[0:C]
            k_dec = (k2 * ex[C:2 * C]).astype(BF16)
            dec_row = ex[(2 + 2 * n_lv) * C:(2 + 2 * n_lv) * C + 1]
            diag = jnp.sum(stack_heads(q2 * k2), axis=1, keepdims=True)
            scores = jnp.where(jnp.concatenate([eye, eye], axis=0), diag, 0.0)
            for l in range(n_lv):
                q_l = stack_heads(q2 * ex[(2 + l) * C:(3 + l) * C]).astype(BF16)
                k_l = (k2 * ex[(2 + n_lv + l) * C:(3 + n_lv + l) * C]).astype(BF16)
                m_l = msk_ref[l * C:(l + 1) * C, :] > 0.5
                scores = scores + jnp.where(jnp.concatenate([m_l, m_l], axis=0), _dot_nt(q_l, k_l), 0.0)
            scores = scores.astype(BF16)
            o_inter = _dot_nt(stack_heads(q_int).astype(BF16), st.astype(BF16))
            upd = []
            for h in range(2):
                vcols = slice((2 * p + h) * GLA_DV, (2 * p + h + 1) * GLA_DV)
                v_h = gv_ref[rows, vcols]
                o = o_inter[h * C:(h + 1) * C] + _dot(scores[h * C:(h + 1) * C], v_h)
                inv = lax.rsqrt(jnp.mean(o * o, axis=-1, keepdims=True) + RMS_EPS)
                y = (o * inv) * ng_ref[:, vcols]
                g = gr_ref[rows, vcols]
                y = y * (g / (1.0 + jnp.exp(-g)))
                o_ref[rows, vcols] = y.astype(o_ref.dtype)
                v_t = v_h.astype(F32).T.astype(BF16)
                upd.append(_dot(v_t, k_dec))
            st = st * dec_row + jnp.where(head0, upd[0], upd[1])
        state_ref[p] = st


def _gla(gq, gk, gv, gr, lr, w2, gb, ng, B, S, tb):
    table, masks = _gla_tables()
    table = jnp.asarray(table, BF16)
    masks = jnp.asarray(masks, F32)
    T = B * S
    nblk = S // tb
    row = lambda n: pl.BlockSpec((tb, n), lambda b, i: (b * nblk + i, 0))
    const = lambda a: pl.BlockSpec(a.shape, lambda b, i: (0,) * a.ndim)
    w2 = w2.astype(F32)
    gb2 = gb.reshape(1, GLA_KW)
    ng2 = ng.reshape(1, GLA_VW)
    return pl.pallas_call(
        functools.partial(_gla_kernel, n_chunks=tb // GLA_CHUNK),
        grid=(B, nblk),
        in_specs=[row(GLA_KW), row(GLA_KW), row(GLA_VW), row(GLA_VW), row(GLA_GATE_RANK),
                  const(w2), const(gb2), const(ng2), const(table), const(masks)],
        out_specs=row(GLA_VW),
        out_shape=jax.ShapeDtypeStruct((T, GLA_VW), BF16),
        scratch_shapes=[pltpu.VMEM((GLA_HEADS // 2, GLA_DV, 2 * GLA_DK), F32)],
        compiler_params=_params("parallel", "arbitrary"),
        name="gla",
    )(gq, gk, gv, gr, lr, w2, gb2, ng2, table, masks)


SUBLANES = 8


def _store_token_tiles(ref, value, accumulate=False):
    n = value.shape[0]
    for s in range(SUBLANES):
        rows = pl.ds(s, n, stride=SUBLANES)
        chunk = value[:, s * LANES:(s + 1) * LANES]
        ref[rows, :] = ref[rows, :] + chunk if accumulate else chunk


def _load_token_tiles(ref, n):
    return jnp.concatenate([ref[pl.ds(s, n, stride=SUBLANES), :] for s in range(SUBLANES)], axis=1)


def _out_proj_kernel(x_ref, oa_ref, ob_ref, wa_ref, wb_ref, g_ref, wr_ref, br_ref, tri_ref,
                     x1_ref, h2_ref, ri_ref, rf_ref, cnt_ref):
    @pl.when(pl.program_id(0) == 0)
    def _():
        cnt_ref[...] = jnp.zeros_like(cnt_ref)

    x1 = x_ref[...] + _dot(oa_ref[...], wa_ref[...]) + _dot(ob_ref[...], wb_ref[...])
    x1_ref[...] = x1
    inv = lax.rsqrt(jnp.mean(x1 * x1, axis=-1, keepdims=True) + RMS_EPS)
    h2 = (x1 * inv) * g_ref[...]
    _store_token_tiles(h2_ref, h2)

    h_hi, h_lo = _split_bf16(h2)
    w_hi, w_lo = _split_bf16(wr_ref[...])
    logits = _dot(h_hi, w_hi) + _dot(h_hi, w_lo) + _dot(h_lo, w_hi) + br_ref[...]
    lane = lax.broadcasted_iota(jnp.int32, logits.shape, 1)
    neg = -jnp.inf

    def first_max(vals):
        m = jnp.max(vals, axis=1, keepdims=True)
        idx = jnp.min(jnp.where(vals == m, lane, LANES), axis=1, keepdims=True)
        return m, idx

    is_group = lane < N_GROUPS
    g_max, g_idx = first_max(jnp.where(is_group, logits, neg))
    g_p = 1.0 / jnp.sum(jnp.where(is_group, jnp.exp(logits - g_max), 0.0), axis=1, keepdims=True)
    lo_lane = N_GROUPS + EXPERTS_PER_GROUP * g_idx
    sel = jnp.where((lane >= lo_lane) & (lane < lo_lane + EXPERTS_PER_GROUP), logits, neg)
    m1, i1 = first_max(sel)
    m2, i2 = first_max(jnp.where(lane == i1, neg, sel))
    e = jnp.exp(m2 - m1)
    gate1 = g_p / (1.0 + e)
    gate2 = g_p * e / (1.0 + e)
    rf_ref[...] = jnp.where(lane == 0, gate1, jnp.where(lane == 1, gate2, 0.0))

    e1 = i1 - N_GROUPS
    e2 = i2 - N_GROUPS
    hits = (lane == e1).astype(F32) + (lane == e2).astype(F32)
    before = _dot(tri_ref[...], hits.astype(BF16)) + cnt_ref[...]
    r1 = jnp.sum(jnp.where(lane == e1, before, 0.0), axis=1, keepdims=True)
    r2 = jnp.sum(jnp.where(lane == e2, before, 0.0), axis=1, keepdims=True)
    cnt_ref[...] += jnp.sum(hits, axis=0, keepdims=True)
    packed = jnp.where(lane == 0, e1.astype(F32), jnp.where(lane == 1, e2.astype(F32),
                       jnp.where(lane == 2, r1, jnp.where(lane == 3, r2, 0.0))))
    for j in range(packed.shape[0] // LANES):
        rows_t = packed[j * LANES:(j + 1) * LANES, :].T
        for k in range(2 * TOP_K):
            ri_ref[k, j:j + 1, :] = rows_t[k:k + 1, :].astype(jnp.int32)


def _out_proj(x2, o_sb, o_gla, w_out, ln_g, w_group, b_group, w_expert, b_expert, tm):
    T, D = x2.shape
    wa = w_out[:SB_WIDTH].astype(BF16)
    wb = w_out[SB_WIDTH:].astype(BF16)
    n_r = N_GROUPS + N_EXPERTS
    wr = jnp.zeros((D, LANES), F32).at[:, :n_r].set(jnp.concatenate([w_group, w_expert], axis=1))
    br = jnp.zeros((1, LANES), F32).at[0, :n_r].set(jnp.concatenate([b_group, b_expert]))
    row = lambda n: pl.BlockSpec((tm, n), lambda i: (i, 0))
    const = lambda a: pl.BlockSpec(a.shape, lambda i: (0,) * a.ndim)
    g2 = ln_g.reshape(1, D)
    tri = jnp.asarray(np.tril(np.ones((tm, tm), np.float32), -1), BF16)
    return pl.pallas_call(
        _out_proj_kernel,
        grid=(T // tm,),
        in_specs=[row(D), row(SB_WIDTH), row(GLA_VW), const(wa), const(wb), const(g2), const(wr), const(br),
                  const(tri)],
        out_specs=[row(D), pl.BlockSpec((tm * SUBLANES, LANES), lambda i: (i, 0)),
                   pl.BlockSpec((2 * TOP_K, tm // LANES, LANES), lambda i: (0, i, 0)), row(LANES),
                   pl.BlockSpec((1, LANES), lambda i: (0, 0))],
        out_shape=[jax.ShapeDtypeStruct((T, D), F32), jax.ShapeDtypeStruct((T * SUBLANES, LANES), F32),
                   jax.ShapeDtypeStruct((2 * TOP_K, T // LANES, LANES), jnp.int32),
                   jax.ShapeDtypeStruct((T, LANES), F32), jax.ShapeDtypeStruct((1, LANES), F32)],
        compiler_params=_params("arbitrary"),
        name="out_proj",
    )(x2, o_sb, o_gla, wa, wb, g2, wr, br, tri)


def _route_tables(route, counts, n_rows, tm):
    counts = counts[0, :N_EXPERTS].astype(jnp.int32)
    ends = jnp.cumsum(counts)
    starts = ends - counts
    route = route.reshape(2 * TOP_K, -1)
    expert = route[:TOP_K, :, None] == jnp.arange(N_EXPERTS, dtype=jnp.int32)
    dest = (jnp.sum(jnp.where(expert, starts, 0), axis=-1) + route[TOP_K:]).reshape(-1)

    n_items = n_rows // tm + N_EXPERTS
    items_e = (counts + tm - 1) // tm
    item_hi = jnp.cumsum(items_e)
    item_lo = item_hi - items_e
    i = jnp.arange(n_items, dtype=jnp.int32)
    e_of = jnp.minimum(jnp.sum(item_hi[None, :] <= i[:, None], axis=1), N_EXPERTS - 1)
    owner = e_of[:, None] == jnp.arange(N_EXPERTS, dtype=jnp.int32)
    pick = lambda tab: jnp.sum(jnp.where(owner, tab, 0), axis=-1)
    item_row = jnp.where(i < item_hi[-1], pick(starts) + (i - pick(item_lo)) * tm, n_rows)
    return (dest, item_row.astype(jnp.int32), item_lo.astype(jnp.int32), item_hi.astype(jnp.int32))


ROW_ISSUE_UNROLL = 8


def _token_tile(ref, t):
    return ref.at[pl.ds(pl.multiple_of(t * SUBLANES, SUBLANES), SUBLANES), :]


def _start_token_gather(src_ref, idx_ref, idx_base, idx_stride, dst_ref, sem, n):
    def body(i, _):
        for j in range(ROW_ISSUE_UNROLL):
            r = i * ROW_ISSUE_UNROLL + j
            t = idx_ref[idx_base + r * idx_stride]
            pltpu.make_async_copy(_token_tile(src_ref, t), _token_tile(dst_ref, r), sem).start(priority=j % 2)
        return 0
    lax.fori_loop(0, n // ROW_ISSUE_UNROLL, body, 0)


def _wait_token_gather(src_ref, dst_ref, sem):
    pltpu.make_async_copy(src_ref.at[pl.ds(0, dst_ref.shape[0]), :], dst_ref, sem).wait()


EXPERT_GATHER_DEPTH = 3


def _expert_kernel(row_ref, lo_ref, hi_ref, dest_ref,
                   h_ref, wg_ref, wu_ref, wd_ref, y_ref, xbuf, ybuf, tok_ref, wg_bf, wu_bf, wd_bf,
                   gsem, wsem, *, tm):
    e = pl.program_id(0)
    n_rows = dest_ref.shape[0]
    n_tok = n_rows // TOP_K
    total = hi_ref[N_EXPERTS - 1]

    depth = xbuf.shape[0]

    def gather_copy(i, r, slot):
        t = tok_ref[row_ref[i] + r]
        return pltpu.make_async_copy(_token_tile(h_ref, t), _token_tile(xbuf.at[slot], r), gsem.at[slot])

    def out_copy(i):
        slot = lax.rem(i, 2)
        rows = pl.ds(pl.multiple_of(row_ref[i] * SUBLANES, SUBLANES), tm * SUBLANES)
        return pltpu.make_async_copy(ybuf.at[slot], y_ref.at[rows, :], wsem)

    @pl.when(e == 0)
    def _():
        def invert(t, _):
            for k in range(TOP_K):
                tok_ref[dest_ref[k * n_tok + t]] = t
            return 0
        lax.fori_loop(0, n_tok, invert, 0, unroll=ROW_ISSUE_UNROLL)

        def spare(r, _):
            tok_ref[n_rows + r] = 0
            return 0
        lax.fori_loop(0, tm, spare, 0, unroll=ROW_ISSUE_UNROLL)
        ybuf[0] = jnp.zeros(ybuf.shape[1:], ybuf.dtype)
        zero_spare = pltpu.make_async_copy(
            ybuf.at[0], y_ref.at[pl.ds(n_rows * SUBLANES, tm * SUBLANES), :], wsem)
        zero_spare.start()
        zero_spare.wait()

        for ahead in range(depth - 1):
            @pl.when(total > ahead)
            def _():
                def first(r, _):
                    gather_copy(ahead, r, ahead).start()
                    return 0
                lax.fori_loop(0, tm, first, 0, unroll=ROW_ISSUE_UNROLL)

    wg_bf[...] = wg_ref[...].astype(BF16)
    wu_bf[...] = wu_ref[...].astype(BF16)
    wd_bf[...] = wd_ref[...].astype(BF16)

    def item(i, _):
        slot = lax.rem(i, depth)
        _wait_token_gather(h_ref, xbuf.at[slot], gsem.at[slot])
        xb = _load_token_tiles(xbuf.at[slot], tm).astype(BF16)
        g = _dot(xb, wg_bf[...])
        u = _dot(xb, wu_bf[...])
        hidden = ((g / (1.0 + jnp.exp(-g))) * u).astype(BF16)

        def down_proj():
            _store_token_tiles(ybuf.at[lax.rem(i, 2)], _dot(hidden, wd_bf[...]))

        nxt = i + depth - 1

        @pl.when(nxt < total)
        def _():
            nslot = lax.rem(nxt, depth)
            for r in range(tm):
                gather_copy(nxt, r, nslot).start(priority=r % 2)
            down_proj()

        @pl.when(nxt >= total)
        def _():
            down_proj()

        @pl.when(i > 0)
        def _():
            out_copy(i - 1).wait()
        out_copy(i).start()
        return 0

    lax.fori_loop(lo_ref[e], hi_ref[e], item, 0)

    @pl.when((e == pl.num_programs(0) - 1) & (total > 0))
    def _():
        out_copy(total - 1).wait()


def _expert_ffn(h2_tiles, dest, tables, w_gate, w_up, w_down, tm):
    N = dest.shape[0]
    _, D, F = w_gate.shape
    assert D == SUBLANES * LANES
    item_row, item_lo, item_hi = tables
    wspec = lambda shape: pl.BlockSpec((None,) + shape, lambda e, *_: (e, 0, 0))
    return pl.pallas_call(
        functools.partial(_expert_kernel, tm=tm),
        grid_spec=pltpu.PrefetchScalarGridSpec(
            num_scalar_prefetch=4,
            grid=(N_EXPERTS,),
            in_specs=[pl.BlockSpec(memory_space=pl.ANY), wspec((D, F)), wspec((D, F)), wspec((F, D))],
            out_specs=pl.BlockSpec(memory_space=pl.ANY),
            scratch_shapes=[pltpu.VMEM((EXPERT_GATHER_DEPTH, tm * SUBLANES, LANES), F32),
                            pltpu.VMEM((2, tm * SUBLANES, LANES), F32),
                            pltpu.SMEM((N + tm,), jnp.int32),
                            pltpu.VMEM((D, F), BF16), pltpu.VMEM((D, F), BF16), pltpu.VMEM((F, D), BF16),
                            pltpu.SemaphoreType.DMA((EXPERT_GATHER_DEPTH,)), pltpu.SemaphoreType.DMA(())],
        ),
        out_shape=jax.ShapeDtypeStruct(((N + tm) * SUBLANES, LANES), F32),
        compiler_params=_params("arbitrary"),
        name="expert_ffn",
    )(item_row, item_lo, item_hi, dest, h2_tiles, w_gate, w_up, w_down)


def _combine_kernel(dest_ref, x1_ref, rf_ref, g_ref, y_ref, o_ref, buf, sem, *, tb):
    i = pl.program_id(0)
    n_blk = pl.num_programs(0)
    slot = i % 2

    def gather(blk, s):
        for k in range(TOP_K):
            _start_token_gather(y_ref, dest_ref, k * n_blk * tb + blk * tb, 1, buf.at[s, k], sem.at[s], tb)

    @pl.when(i == 0)
    def _():
        gather(0, 0)

    @pl.when(i + 1 < n_blk)
    def _():
        gather(i + 1, 1 - slot)

    for k in range(TOP_K):
        _wait_token_gather(y_ref, buf.at[slot, k], sem.at[slot])
    rf = rf_ref[...]
    x = (x1_ref[...] + rf[:, 0:1] * _load_token_tiles(buf.at[slot, 0], tb)
         + rf[:, 1:2] * _load_token_tiles(buf.at[slot, 1], tb))
    inv = lax.rsqrt(jnp.mean(x * x, axis=-1, keepdims=True) + RMS_EPS)
    o_ref[...] = (x * inv) * g_ref[...]


def _combine(x1, route_f, y_tiles, dest, ln_g, tb):
    T, D = x1.shape
    g2 = ln_g.reshape(1, D)
    return pl.pallas_call(
        functools.partial(_combine_kernel, tb=tb),
        grid_spec=pltpu.PrefetchScalarGridSpec(
            num_scalar_prefetch=1,
            grid=(T // tb,),
            in_specs=[pl.BlockSpec((tb, D), lambda i, d: (i, 0)),
                      pl.BlockSpec((tb, LANES), lambda i, d: (i, 0)),
                      pl.BlockSpec((1, D), lambda i, d: (0, 0)),
                      pl.BlockSpec(memory_space=pl.ANY)],
            out_specs=pl.BlockSpec((tb, D), lambda i, d: (i, 0)),
            scratch_shapes=[pltpu.VMEM((2, TOP_K, tb * SUBLANES, LANES), F32),
                            pltpu.SemaphoreType.DMA((2,))],
        ),
        out_shape=jax.ShapeDtypeStruct((T, D), F32),
        compiler_params=_params("arbitrary"),
        name="combine",
    )(dest, x1, route_f, g2, y_tiles)


def _layer(x2, B, S, ln1_g, w_in, w2, gb, ng, w_out, ln2_g, w_group, b_group, w_expert, b_expert,
           w_gate, w_up, w_down, ln_f_g, *, tm, tm_out, tq, tb_gla, tb_row, tm_e):
    q, k, v, gq, gk, gv, gr, lr = _in_proj(x2, ln1_g, w_in, tm)
    o_sb = _sb_attn(q, k, v, B, S, tq)
    o_gla = _gla(gq, gk, gv, gr, lr, w2, gb, ng, B, S, tb_gla)
    x1, h2_tiles, route_i, route_f, counts = _out_proj(x2, o_sb, o_gla, w_out, ln2_g, w_group, b_group,
                                                       w_expert, b_expert, tm_out)
    T = x2.shape[0]
    dest, *tables = _route_tables(route_i, counts, TOP_K * T, tm_e)
    y_tiles = _expert_ffn(h2_tiles, dest, tables, w_gate, w_up, w_down, tm_e)
    return _combine(x1, route_f, y_tiles, dest, ln_f_g, tb_row)


def kernel(x, ln1_g, w_in, gla_gate_w2, gla_gate_b, gla_norm_g, w_out, ln2_g, w_group, b_group, w_expert, b_expert, exp_w_gate, exp_w_up, exp_w_down, ln_f_g):
    B, S, D = x.shape
    assert ln1_g.shape[0] == 1, "single-layer stack"
    out = _layer(x.reshape(B * S, D), B, S, ln1_g[0], w_in[0], gla_gate_w2[0], gla_gate_b[0],
                 gla_norm_g[0], w_out[0], ln2_g[0], w_group[0], b_group[0], w_expert[0], b_expert[0],
                 exp_w_gate[0], exp_w_up[0], exp_w_down[0], ln_f_g,
                 tm=512, tm_out=1024, tq=256, tb_gla=512, tb_row=256, tm_e=256)
    return out.reshape(B, S, D)
```

```python
import functools

import numpy as np
import jax
import jax.numpy as jnp
from jax import lax
from jax.experimental import pallas as pl
from jax.experimental.pallas import tpu as pltpu

SB_HEADS = 8
SB_HEAD_DIM = 64
SB_WIDTH = SB_HEADS * SB_HEAD_DIM
GLA_HEADS = 4
GLA_DK = 64
GLA_DV = 128
GLA_KW = GLA_HEADS * GLA_DK
GLA_VW = GLA_HEADS * GLA_DV
GLA_GATE_RANK = 16
GLA_GATE_TAU = 16.0
N_GROUPS = 4
EXPERTS_PER_GROUP = 8
N_EXPERTS = N_GROUPS * EXPERTS_PER_GROUP
TOP_K = 2
RMS_EPS = 1e-6
LOG2_E = 1.4426950408889634

LANES = 128
VMEM_LIMIT_BYTES = 56 * 1024 * 1024

F32 = jnp.float32
BF16 = jnp.bfloat16


def _dot(a, b):
    return jnp.dot(a, b, preferred_element_type=F32)


def _dot_nt(a, b):
    return lax.dot_general(a, b, (((1,), (1,)), ((), ())), preferred_element_type=F32)


def _softplus(z):
    return jnp.maximum(z, 0.0) + jnp.log(1.0 + jnp.exp(-jnp.abs(z)))


def _params(*sem, flags=None):
    return pltpu.CompilerParams(dimension_semantics=sem, vmem_limit_bytes=VMEM_LIMIT_BYTES, flags=flags)


def _in_proj_kernel(x_ref, g_ref, wq, wk, wv, wgq, wgk, wgv, wgr, wlr,
                    oq, ok, ov, ogq, ogk, ogv, ogr, olr):
    x = x_ref[...]
    inv = lax.rsqrt(jnp.mean(x * x, axis=-1, keepdims=True) + RMS_EPS)
    h = ((x * inv) * g_ref[...]).astype(BF16)
    oq[...] = (_dot(h, wq[...]) * (SB_HEAD_DIM ** -0.5 * LOG2_E)).astype(oq.dtype)
    ok[...] = _dot(h, wk[...]).astype(ok.dtype)
    ov[...] = _dot(h, wv[...]).astype(ov.dtype)
    ogq[...] = (_dot(h, wgq[...]) * (GLA_DK ** -0.5)).astype(ogq.dtype)
    ogk[...] = _dot(h, wgk[...]).astype(ogk.dtype)
    ogv[...] = _dot(h, wgv[...]).astype(ogv.dtype)
    ogr[...] = _dot(h, wgr[...]).astype(ogr.dtype)
    olr[...] = _dot(h, wlr[...]).astype(olr.dtype)


def _in_proj(x2, ln_g, w_in, tm):
    T, D = x2.shape
    sizes = (SB_WIDTH, SB_WIDTH, SB_WIDTH, GLA_KW, GLA_KW, GLA_VW, GLA_VW, GLA_GATE_RANK)
    offs = np.concatenate([[0], np.cumsum(sizes)])
    ws = [w_in[:, int(offs[i]):int(offs[i + 1])].astype(BF16) for i in range(len(sizes))]
    out_dtypes = (BF16, BF16, BF16, F32, F32, BF16, F32, F32)
    row = lambda n: pl.BlockSpec((tm, n), lambda i: (i, 0))
    full = lambda n: pl.BlockSpec((D, n), lambda i: (0, 0))
    return pl.pallas_call(
        _in_proj_kernel,
        grid=(T // tm,),
        in_specs=[row(D), pl.BlockSpec((1, D), lambda i: (0, 0))] + [full(n) for n in sizes],
        out_specs=[row(n) for n in sizes],
        out_shape=[jax.ShapeDtypeStruct((T, n), dt) for n, dt in zip(sizes, out_dtypes)],
        compiler_params=_params("parallel"),
        name="in_proj",
    )(x2, ln_g.reshape(1, D), *ws)


SB_ZERO_WEIGHT_BITS = 160.0


def _softplus2(z):
    return jnp.maximum(z, 0.0) + jnp.log2(1.0 + jnp.exp2(-jnp.abs(z)))


def _sb_attn_kernel(q_ref, k_ref, v_ref, o_ref, qh_ref, acc_ref, carry_ref, *, tq, n_pairs):
    qi = pl.program_id(1)
    r = lax.broadcasted_iota(jnp.int32, (tq, tq), 0)
    c = lax.broadcasted_iota(jnp.int32, (tq, tq), 1)
    causal = c < r
    suffix = (r >= c).astype(BF16)
    head0 = lax.broadcasted_iota(jnp.int32, (1, LANES), 1) < SB_HEAD_DIM

    for p in range(n_pairs):
        q2 = q_ref[:, p * LANES:(p + 1) * LANES]
        zero = jnp.zeros_like(q2)
        qh_ref[p, 0:tq] = jnp.where(head0, q2, zero)
        qh_ref[p, tq:2 * tq] = jnp.where(head0, zero, q2)
    acc_ref[...] = jnp.zeros_like(acc_ref)
    carry_ref[...] = jnp.zeros_like(carry_ref)
    causal2 = jnp.concatenate([causal, causal], axis=0)

    def key_block(kb, diag):
        rows = pl.ds(pl.multiple_of(kb * tq, tq), tq)
        for p in range(n_pairs):
            cols = pl.ds(p * LANES, LANES)
            k2 = k_ref[rows, cols]
            v2 = v_ref[rows, cols]
            vzero = jnp.zeros_like(v2)
            v_stack = jnp.concatenate([jnp.where(head0, v2, vzero), jnp.where(head0, vzero, v2)], axis=0)
            z = _dot_nt(qh_ref[p], k2)
            sp = _softplus2(z)
            if diag:
                sp = jnp.where(causal2, sp, 0.0)
            run = _dot(sp.astype(BF16), suffix)
            carry = carry_ref[p]
            w = jnp.exp2(z - run - jnp.concatenate([carry] * (tq // LANES), axis=1))
            if diag:
                w = jnp.where(causal2, w, 0.0)
            w = w.astype(BF16)
            acc_ref[p] += _dot(jnp.concatenate([w[0:tq], w[tq:2 * tq]], axis=1), v_stack)
            carry_ref[p] = carry + jnp.broadcast_to(run[:, 0:1], (2 * tq, LANES))

    def min_carry():
        m = carry_ref[0]
        for p in range(1, n_pairs):
            m = jnp.minimum(m, carry_ref[p])
        return jnp.min(m)

    def cond(state):
        kb, m = state
        return jnp.logical_and(kb >= 0, m < SB_ZERO_WEIGHT_BITS)

    def body(state):
        kb, _ = state
        key_block(kb, False)
        return kb - 1, min_carry()

    key_block(qi, True)
    lax.while_loop(cond, body, (qi - 1, min_carry()))
    for p in range(n_pairs):
        o_ref[:, p * LANES:(p + 1) * LANES] = acc_ref[p].astype(o_ref.dtype)


def _sb_attn(q, k, v, B, S, tq):
    W = q.shape[-1]
    n_pairs = W // LANES
    q3, k3, v3 = (a.reshape(B, S, W) for a in (q, k, v))
    blk = pl.BlockSpec((None, tq, W), lambda b, i: (b, i, 0))
    seq = pl.BlockSpec((None, S, W), lambda b, i: (b, 0, 0))
    out = pl.pallas_call(
        functools.partial(_sb_attn_kernel, tq=tq, n_pairs=n_pairs),
        grid=(B, S // tq),
        in_specs=[blk, seq, seq],
        out_specs=blk,
        out_shape=jax.ShapeDtypeStruct((B, S, W), BF16),
        scratch_shapes=[pltpu.VMEM((n_pairs, 2 * tq, LANES), BF16),
                        pltpu.VMEM((n_pairs, tq, LANES), F32),
                        pltpu.VMEM((n_pairs, 2 * tq, LANES), F32)],
        compiler_params=_params("parallel", "arbitrary"),
        name="sb_attn",
    )(q3, k3, v3)
    return out.reshape(B * S, W)


GLA_CHUNK = 64
GLA_LEVELS = (32, 16, 8, 4, 2, 1)


def _gla_tables(C=GLA_CHUNK):
    t = np.arange(C)[:, None]
    j = np.arange(C)[None, :]
    blocks = [(j <= t), (j > t)]
    q_rows, k_rows, masks = [], [], []
    for half in GLA_LEVELS:
        blk = 2 * half
        u = (t // blk) * blk + half - 1
        second = (t % blk) >= half
        q_rows.append(second & (j > u) & (j <= t))
        k_rows.append((~second) & (j > t) & (j <= u))
        s = j
        masks.append(second & ((s % blk) < half) & ((s // blk) == (t // blk)))
    blocks += q_rows + k_rows
    blocks.append(np.ones((8, C), bool))
    table = np.concatenate(blocks, axis=0).astype(np.float32)
    table = np.concatenate([table, table], axis=1)
    return table, np.concatenate(masks, axis=0).astype(np.float32)


def _split_bf16(a):
    hi = a.astype(BF16)
    lo = (a - hi.astype(F32)).astype(BF16)
    return hi, lo


def _gla_kernel(gq_ref, gk_ref, gv_ref, gr_ref, lr_ref, w2_ref, gb_ref, ng_ref, tab_ref, msk_ref,
                o_ref, state_ref, *, n_chunks):
    C = GLA_CHUNK
    n_lv = len(GLA_LEVELS)

    @pl.when(pl.program_id(1) == 0)
    def _():
        state_ref[...] = jnp.zeros_like(state_ref)

    head0 = lax.broadcasted_iota(jnp.int32, (1, LANES), 1) < GLA_DK
    eye = (lax.broadcasted_iota(jnp.int32, (C, C), 0) == lax.broadcasted_iota(jnp.int32, (C, C), 1))
    w2_hi, w2_lo = _split_bf16(w2_ref[...])
    lr_hi, lr_lo = _split_bf16(lr_ref[...])

    def stack_heads(a):
        return jnp.concatenate([jnp.where(head0, a, 0.0), jnp.where(head0, 0.0, a)], axis=0)

    for p in range(GLA_HEADS // 2):
        cols = slice(p * LANES, (p + 1) * LANES)
        pre = (_dot(lr_hi, w2_hi[:, cols]) + _dot(lr_hi, w2_lo[:, cols])
               + _dot(lr_lo, w2_hi[:, cols])) + gb_ref[:, cols]
        log_a = -_softplus(-pre) * (1.0 / GLA_GATE_TAU)
        la_hi, la_lo = _split_bf16(log_a)
        la = jnp.concatenate(
            [jnp.concatenate([la_hi[c * C:(c + 1) * C], la_lo[c * C:(c + 1) * C]], axis=0)
             for c in range(n_chunks)], axis=1)
        ex_all = jnp.exp(_dot(tab_ref[...], la))
        st = state_ref[p]
        for c in range(n_chunks):
            rows = slice(c * C, (c + 1) * C)
            ex = ex_all[:, c * LANES:(c + 1) * LANES]
            q2 = gq_ref[rows, cols]
            k2 = gk_ref[rows, cols]
            q_int = q2 * ex[0:C]
            k_dec = (k2 * ex[C:2 * C]).astype(BF16)
            dec_row = ex[(2 + 2 * n_lv) * C:(2 + 2 * n_lv) * C + 1]
            diag = jnp.sum(stack_heads(q2 * k2), axis=1, keepdims=True)
            scores = jnp.where(jnp.concatenate([eye, eye], axis=0), diag, 0.0)
            for l in range(n_lv):
                q_l = stack_heads(q2 * ex[(2 + l) * C:(3 + l) * C]).astype(BF16)
                k_l = (k2 * ex[(2 + n_lv + l) * C:(3 + n_lv + l) * C]).astype(BF16)
                m_l = msk_ref[l * C:(l + 1) * C, :] > 0.5
                scores = scores + jnp.where(jnp.concatenate([m_l, m_l], axis=0), _dot_nt(q_l, k_l), 0.0)
            scores = scores.astype(BF16)
            o_inter = _dot_nt(stack_heads(q_int).astype(BF16), st.astype(BF16))
            upd = []
            for h in range(2):
                vcols = slice((2 * p + h) * GLA_DV, (2 * p + h + 1) * GLA_DV)
                v_h = gv_ref[rows, vcols]
                o = o_inter[h * C:(h + 1) * C] + _dot(scores[h * C:(h + 1) * C], v_h)
                inv = lax.rsqrt(jnp.mean(o * o, axis=-1, keepdims=True) + RMS_EPS)
                y = (o * inv) * ng_ref[:, vcols]
                g = gr_ref[rows, vcols]
                y = y * (g / (1.0 + jnp.exp(-g)))
                o_ref[rows, vcols] = y.astype(o_ref.dtype)
                v_t = v_h.astype(F32).T.astype(BF16)
                upd.append(_dot(v_t, k_dec))
            st = st * dec_row + jnp.where(head0, upd[0], upd[1])
        state_ref[p] = st


def _gla(gq, gk, gv, gr, lr, w2, gb, ng, B, S, tb):
    table, masks = _gla_tables()
    table = jnp.asarray(table, BF16)
    masks = jnp.asarray(masks, F32)
    T = B * S
    nblk = S // tb
    row = lambda n: pl.BlockSpec((tb, n), lambda b, i: (b * nblk + i, 0))
    const = lambda a: pl.BlockSpec(a.shape, lambda b, i: (0,) * a.ndim)
    w2 = w2.astype(F32)
    gb2 = gb.reshape(1, GLA_KW)
    ng2 = ng.reshape(1, GLA_VW)
    return pl.pallas_call(
        functools.partial(_gla_kernel, n_chunks=tb // GLA_CHUNK),
        grid=(B, nblk),
        in_specs=[row(GLA_KW), row(GLA_KW), row(GLA_VW), row(GLA_VW), row(GLA_GATE_RANK),
                  const(w2), const(gb2), const(ng2), const(table), const(masks)],
        out_specs=row(GLA_VW),
        out_shape=jax.ShapeDtypeStruct((T, GLA_VW), BF16),
        scratch_shapes=[pltpu.VMEM((GLA_HEADS // 2, GLA_DV, 2 * GLA_DK), F32)],
        compiler_params=_params("parallel", "arbitrary"),
        name="gla",
    )(gq, gk, gv, gr, lr, w2, gb2, ng2, table, masks)


SUBLANES = 8


def _store_token_tiles(ref, value, accumulate=False):
    n = value.shape[0]
    for s in range(SUBLANES):
        rows = pl.ds(s, n, stride=SUBLANES)
        chunk = value[:, s * LANES:(s + 1) * LANES]
        ref[rows, :] = ref[rows, :] + chunk if accumulate else chunk


def _load_token_tiles(ref, n):
    return jnp.concatenate([ref[pl.ds(s, n, stride=SUBLANES), :] for s in range(SUBLANES)], axis=1)


def _out_proj_kernel(x_ref, oa_ref, ob_ref, wa_ref, wb_ref, g_ref, wr_ref, br_ref, tri_ref,
                     x1_ref, h2_ref, ri_ref, rf_ref, cnt_ref):
    @pl.when(pl.program_id(0) == 0)
    def _():
        cnt_ref[...] = jnp.zeros_like(cnt_ref)

    x1 = x_ref[...] + _dot(oa_ref[...], wa_ref[...]) + _dot(ob_ref[...], wb_ref[...])
    x1_ref[...] = x1
    inv = lax.rsqrt(jnp.mean(x1 * x1, axis=-1, keepdims=True) + RMS_EPS)
    h2 = (x1 * inv) * g_ref[...]
    _store_token_tiles(h2_ref, h2)

    h_hi, h_lo = _split_bf16(h2)
    w_hi, w_lo = _split_bf16(wr_ref[...])
    hw = _dot(h_hi, jnp.concatenate([w_hi, w_lo], axis=1))
    logits = hw[:, :LANES] + hw[:, LANES:] + _dot(h_lo, w_hi) + br_ref[...]
    lane = lax.broadcasted_iota(jnp.int32, logits.shape, 1).astype(F32)
    neg = -jnp.inf

    def first_max(vals):
        m = jnp.max(vals, axis=1, keepdims=True)
        idx = jnp.min(jnp.where(vals == m, lane, float(LANES)), axis=1, keepdims=True)
        return m, idx

    is_group = lane < N_GROUPS
    g_max, g_idx = first_max(jnp.where(is_group, logits, neg))
    g_p = 1.0 / jnp.sum(jnp.where(is_group, jnp.exp(logits - g_max), 0.0), axis=1, keepdims=True)
    lo_lane = N_GROUPS + EXPERTS_PER_GROUP * g_idx
    sel = jnp.where((lane >= lo_lane) & (lane < lo_lane + EXPERTS_PER_GROUP), logits, neg)
    m1, i1 = first_max(sel)
    m2, i2 = first_max(jnp.where(lane == i1, neg, sel))
    e = jnp.exp(m2 - m1)
    gate1 = g_p / (1.0 + e)
    gate2 = g_p * e / (1.0 + e)
    rf_ref[...] = jnp.where(lane == 0, gate1, jnp.where(lane == 1, gate2, 0.0))

    e1 = i1 - N_GROUPS
    e2 = i2 - N_GROUPS
    hits = (lane == e1).astype(F32) + (lane == e2).astype(F32)
    before = _dot(tri_ref[...], hits.astype(BF16)) + cnt_ref[...]
    r1 = jnp.sum(jnp.where(lane == e1, before, 0.0), axis=1, keepdims=True)
    r2 = jnp.sum(jnp.where(lane == e2, before, 0.0), axis=1, keepdims=True)
    cnt_ref[...] += jnp.sum(hits, axis=0, keepdims=True)
    packed = jnp.where(lane == 0, e1, jnp.where(lane == 1, e2,
                       jnp.where(lane == 2, r1, jnp.where(lane == 3, r2, 0.0))))
    for j in range(packed.shape[0] // LANES):
        rows_t = packed[j * LANES:(j + 1) * LANES, :].T
        for k in range(2 * TOP_K):
            ri_ref[k, j:j + 1, :] = rows_t[k:k + 1, :].astype(jnp.int32)


def _out_proj(x2, o_sb, o_gla, w_out, ln_g, w_group, b_group, w_expert, b_expert, tm):
    T, D = x2.shape
    wa = w_out[:SB_WIDTH].astype(BF16)
    wb = w_out[SB_WIDTH:].astype(BF16)
    n_r = N_GROUPS + N_EXPERTS
    wr = jnp.zeros((D, LANES), F32).at[:, :n_r].set(jnp.concatenate([w_group, w_expert], axis=1))
    br = jnp.zeros((1, LANES), F32).at[0, :n_r].set(jnp.concatenate([b_group, b_expert]))
    row = lambda n: pl.BlockSpec((tm, n), lambda i: (i, 0))
    const = lambda a: pl.BlockSpec(a.shape, lambda i: (0,) * a.ndim)
    g2 = ln_g.reshape(1, D)
    tri = jnp.asarray(np.tril(np.ones((tm, tm), np.float32), -1), BF16)
    return pl.pallas_call(
        _out_proj_kernel,
        grid=(T // tm,),
        in_specs=[row(D), row(SB_WIDTH), row(GLA_VW), const(wa), const(wb), const(g2), const(wr), const(br),
                  const(tri)],
        out_specs=[row(D), pl.BlockSpec((tm * SUBLANES, LANES), lambda i: (i, 0)),
                   pl.BlockSpec((2 * TOP_K, tm // LANES, LANES), lambda i: (0, i, 0)), row(LANES),
                   pl.BlockSpec((1, LANES), lambda i: (0, 0))],
        out_shape=[jax.ShapeDtypeStruct((T, D), F32), jax.ShapeDtypeStruct((T * SUBLANES, LANES), F32),
                   jax.ShapeDtypeStruct((2 * TOP_K, T // LANES, LANES), jnp.int32),
                   jax.ShapeDtypeStruct((T, LANES), F32), jax.ShapeDtypeStruct((1, LANES), F32)],
        compiler_params=_params("arbitrary"),
        name="out_proj",
    )(x2, o_sb, o_gla, wa, wb, g2, wr, br, tri)


def _route_tables(route, counts, n_rows, tm):
    counts = counts[0, :N_EXPERTS].astype(jnp.int32)
    ends = jnp.cumsum(counts)
    starts = ends - counts
    route = route.reshape(2 * TOP_K, -1)
    expert = route[:TOP_K, :, None] == jnp.arange(N_EXPERTS, dtype=jnp.int32)
    dest = (jnp.sum(jnp.where(expert, starts, 0), axis=-1) + route[TOP_K:]).reshape(-1)

    n_items = n_rows // tm + N_EXPERTS
    items_e = (counts + tm - 1) // tm
    item_hi = jnp.cumsum(items_e)
    item_lo = item_hi - items_e
    i = jnp.arange(n_items, dtype=jnp.int32)
    e_of = jnp.minimum(jnp.sum(item_hi[None, :] <= i[:, None], axis=1), N_EXPERTS - 1)
    owner = e_of[:, None] == jnp.arange(N_EXPERTS, dtype=jnp.int32)
    pick = lambda tab: jnp.sum(jnp.where(owner, tab, 0), axis=-1)
    item_row = jnp.where(i < item_hi[-1], pick(starts) + (i - pick(item_lo)) * tm, n_rows)
    return (dest, item_row.astype(jnp.int32), item_lo.astype(jnp.int32), item_hi.astype(jnp.int32))


ROW_ISSUE_UNROLL = 8


def _token_tile(ref, t):
    return ref.at[pl.ds(pl.multiple_of(t * SUBLANES, SUBLANES), SUBLANES), :]


def _start_token_gather(src_ref, idx_ref, idx_base, idx_stride, dst_ref, sem, n):
    def body(i, _):
        for j in range(ROW_ISSUE_UNROLL):
            r = i * ROW_ISSUE_UNROLL + j
            t = idx_ref[idx_base + r * idx_stride]
            pltpu.make_async_copy(_token_tile(src_ref, t), _token_tile(dst_ref, r), sem).start(priority=j % 2)
        return 0
    lax.fori_loop(0, n // ROW_ISSUE_UNROLL, body, 0)


def _wait_token_gather(src_ref, dst_ref, sem):
    pltpu.make_async_copy(src_ref.at[pl.ds(0, dst_ref.shape[0]), :], dst_ref, sem).wait()


EXPERT_GATHER_DEPTH = 4


def _expert_kernel(row_ref, lo_ref, hi_ref, dest_ref,
                   h_ref, wg_ref, wu_ref, wd_ref, y_ref, xbuf, ybuf, tok_ref, wg_bf, wu_bf, wd_bf,
                   gsem, wsem, *, tm):
    e = pl.program_id(0)
    n_rows = dest_ref.shape[0]
    n_tok = n_rows // TOP_K
    total = hi_ref[N_EXPERTS - 1]

    depth = xbuf.shape[0]

    def gather_copy(i, r, slot):
        t = tok_ref[row_ref[i] + r]
        return pltpu.make_async_copy(_token_tile(h_ref, t), _token_tile(xbuf.at[slot], r), gsem.at[slot])

    def out_copy(i):
        slot = lax.rem(i, 2)
        rows = pl.ds(pl.multiple_of(row_ref[i] * SUBLANES, SUBLANES), tm * SUBLANES)
        return pltpu.make_async_copy(ybuf.at[slot], y_ref.at[rows, :], wsem)

    @pl.when(e == 0)
    def _():
        def invert(t, _):
            for k in range(TOP_K):
                tok_ref[dest_ref[k * n_tok + t]] = t
            return 0
        lax.fori_loop(0, n_tok, invert, 0, unroll=ROW_ISSUE_UNROLL)

        def spare(r, _):
            tok_ref[n_rows + r] = 0
            return 0
        lax.fori_loop(0, tm, spare, 0, unroll=ROW_ISSUE_UNROLL)
        ybuf[0] = jnp.zeros(ybuf.shape[1:], ybuf.dtype)
        zero_spare = pltpu.make_async_copy(
            ybuf.at[0], y_ref.at[pl.ds(n_rows * SUBLANES, tm * SUBLANES), :], wsem)
        zero_spare.start()
        zero_spare.wait()

        for ahead in range(depth - 1):
            @pl.when(total > ahead)
            def _():
                def first(r, _):
                    gather_copy(ahead, r, ahead).start()
                    return 0
                lax.fori_loop(0, tm, first, 0, unroll=ROW_ISSUE_UNROLL)

    wg_bf[...] = wg_ref[...].astype(BF16)
    wu_bf[...] = wu_ref[...].astype(BF16)
    wd_bf[...] = wd_ref[...].astype(BF16)

    def item(i, _):
        slot = lax.rem(i, depth)
        _wait_token_gather(h_ref, xbuf.at[slot], gsem.at[slot])
        xb = _load_token_tiles(xbuf.at[slot], tm).astype(BF16)
        g = _dot(xb, wg_bf[...])
        u = _dot(xb, wu_bf[...])
        hidden = ((g / (1.0 + jnp.exp(-g))) * u).astype(BF16)

        def down_proj():
            _store_token_tiles(ybuf.at[lax.rem(i, 2)], _dot(hidden, wd_bf[...]))

        nxt = i + depth - 1

        @pl.when(nxt < total)
        def _():
            nslot = lax.rem(nxt, depth)
            for r in range(tm):
                gather_copy(nxt, r, nslot).start(priority=r % 2)
            down_proj()

        @pl.when(nxt >= total)
        def _():
            down_proj()

        @pl.when(i > 0)
        def _():
            out_copy(i - 1).wait()
        out_copy(i).start()
        return 0

    lax.fori_loop(lo_ref[e], hi_ref[e], item, 0)

    @pl.when((e == pl.num_programs(0) - 1) & (total > 0))
    def _():
        out_copy(total - 1).wait()


def _expert_ffn(h2_tiles, dest, tables, w_gate, w_up, w_down, tm):
    N = dest.shape[0]
    _, D, F = w_gate.shape
    assert D == SUBLANES * LANES
    item_row, item_lo, item_hi = tables
    wspec = lambda shape: pl.BlockSpec((None,) + shape, lambda e, *_: (e, 0, 0))
    return pl.pallas_call(
        functools.partial(_expert_kernel, tm=tm),
        grid_spec=pltpu.PrefetchScalarGridSpec(
            num_scalar_prefetch=4,
            grid=(N_EXPERTS,),
            in_specs=[pl.BlockSpec(memory_space=pl.ANY), wspec((D, F)), wspec((D, F)), wspec((F, D))],
            out_specs=pl.BlockSpec(memory_space=pl.ANY),
            scratch_shapes=[pltpu.VMEM((EXPERT_GATHER_DEPTH, tm * SUBLANES, LANES), F32),
                            pltpu.VMEM((2, tm * SUBLANES, LANES), F32),
                            pltpu.SMEM((N + tm,), jnp.int32),
                            pltpu.VMEM((D, F), BF16), pltpu.VMEM((D, F), BF16), pltpu.VMEM((F, D), BF16),
                            pltpu.SemaphoreType.DMA((EXPERT_GATHER_DEPTH,)), pltpu.SemaphoreType.DMA(())],
        ),
        out_shape=jax.ShapeDtypeStruct(((N + tm) * SUBLANES, LANES), F32),
        compiler_params=_params("arbitrary"),
        name="expert_ffn",
    )(item_row, item_lo, item_hi, dest, h2_tiles, w_gate, w_up, w_down)


COMBINE_RING = 4
COMBINE_AHEAD = 2


def _combine_kernel(dest_ref, x1_ref, rf_ref, g_ref, y_ref, o_ref, *scratch, tb):
    bufs, sem = scratch[:COMBINE_RING], scratch[COMBINE_RING]
    i = pl.program_id(0)
    n_steps = pl.num_programs(0)
    n_tok = n_steps * COMBINE_RING * tb

    def copies(j, slot, r):
        return [pltpu.make_async_copy(_token_tile(y_ref, dest_ref[k * n_tok + j * tb + r]),
                                      _token_tile(bufs[slot].at[k], r), sem.at[slot])
                for k in range(TOP_K)]

    @pl.when(i == 0)
    def _():
        for slot in range(COMBINE_AHEAD):
            def first(r, _):
                for c in copies(slot, slot, r):
                    c.start()
                return 0
            lax.fori_loop(0, tb, first, 0, unroll=ROW_ISSUE_UNROLL // TOP_K)

    def sub_block(s, fetch):
        j = i * COMBINE_RING + s
        for k in range(TOP_K):
            _wait_token_gather(y_ref, bufs[s].at[k], sem.at[s])
        if fetch:
            for r in range(tb):
                for k, c in enumerate(copies(j + COMBINE_AHEAD, (s + COMBINE_AHEAD) % COMBINE_RING, r)):
                    c.start(priority=k)
        rows = slice(s * tb, (s + 1) * tb)
        rf = rf_ref[rows, :]
        x = (x1_ref[rows, :] + rf[:, 0:1] * _load_token_tiles(bufs[s].at[0], tb)
             + rf[:, 1:2] * _load_token_tiles(bufs[s].at[1], tb))
        inv = lax.rsqrt(jnp.mean(x * x, axis=-1, keepdims=True) + RMS_EPS)
        o_ref[rows, :] = (x * inv) * g_ref[...]

    for s in range(COMBINE_RING):
        if s + COMBINE_AHEAD < COMBINE_RING:
            sub_block(s, True)
        else:
            @pl.when(i + 1 < n_steps)
            def _():
                sub_block(s, True)

            @pl.when(i + 1 >= n_steps)
            def _():
                sub_block(s, False)


def _combine(x1, route_f, y_tiles, dest, ln_g, tb):
    T, D = x1.shape
    g2 = ln_g.reshape(1, D)
    step = COMBINE_RING * tb
    return pl.pallas_call(
        functools.partial(_combine_kernel, tb=tb),
        grid_spec=pltpu.PrefetchScalarGridSpec(
            num_scalar_prefetch=1,
            grid=(T // step,),
            in_specs=[pl.BlockSpec((step, D), lambda i, d: (i, 0)),
                      pl.BlockSpec((step, LANES), lambda i, d: (i, 0)),
                      pl.BlockSpec((1, D), lambda i, d: (0, 0)),
                      pl.BlockSpec(memory_space=pl.ANY)],
            out_specs=pl.BlockSpec((step, D), lambda i, d: (i, 0)),
            scratch_shapes=[pltpu.VMEM((TOP_K, tb * SUBLANES, LANES), F32) for _ in range(COMBINE_RING)]
                           + [pltpu.SemaphoreType.DMA((COMBINE_RING,))],
        ),
        out_shape=jax.ShapeDtypeStruct((T, D), F32),
        compiler_params=_params("arbitrary"),
        name="combine",
    )(dest, x1, route_f, g2, y_tiles)


def _layer(x2, B, S, ln1_g, w_in, w2, gb, ng, w_out, ln2_g, w_group, b_group, w_expert, b_expert,
           w_gate, w_up, w_down, ln_f_g, *, tm, tm_out, tq, tb_gla, tb_row, tm_e):
    q, k, v, gq, gk, gv, gr, lr = _in_proj(x2, ln1_g, w_in, tm)
    o_sb = _sb_attn(q, k, v, B, S, tq)
    o_gla = _gla(gq, gk, gv, gr, lr, w2, gb, ng, B, S, tb_gla)
    x1, h2_tiles, route_i, route_f, counts = _out_proj(x2, o_sb, o_gla, w_out, ln2_g, w_group, b_group,
                                                       w_expert, b_expert, tm_out)
    T = x2.shape[0]
    dest, *tables = _route_tables(route_i, counts, TOP_K * T, tm_e)
    y_tiles = _expert_ffn(h2_tiles, dest, tables, w_gate, w_up, w_down, tm_e)
    return _combine(x1, route_f, y_tiles, dest, ln_f_g, tb_row)


def kernel(x, ln1_g, w_in, gla_gate_w2, gla_gate_b, gla_norm_g, w_out, ln2_g, w_group, b_group, w_expert, b_expert, exp_w_gate, exp_w_up, exp_w_down, ln_f_g):
    B, S, D = x.shape
    assert ln1_g.shape[0] == 1, "single-layer stack"
    out = _layer(x.reshape(B * S, D), B, S, ln1_g[0], w_in[0], gla_gate_w2[0], gla_gate_b[0],
                 gla_norm_g[0], w_out[0], ln2_g[0], w_group[0], b_group[0], w_expert[0], b_expert[0],
                 exp_w_gate[0], exp_w_up[0], exp_w_down[0], ln_f_g,
                 tm=512, tm_out=1024, tq=256, tb_gla=512, tb_row=256, tm_e=256)
    return out.reshape(B, S, D)
```

```python
import functools

import numpy as np
import jax
import jax.numpy as jnp
from jax import lax
from jax.experimental import pallas as pl
from jax.experimental.pallas import tpu as pltpu

SB_HEADS = 8
SB_HEAD_DIM = 64
SB_WIDTH = SB_HEADS * SB_HEAD_DIM
GLA_HEADS = 4
GLA_DK = 64
GLA_DV = 128
GLA_KW = GLA_HEADS * GLA_DK
GLA_VW = GLA_HEADS * GLA_DV
GLA_GATE_RANK = 16
GLA_GATE_TAU = 16.0
N_GROUPS = 4
EXPERTS_PER_GROUP = 8
N_EXPERTS = N_GROUPS * EXPERTS_PER_GROUP
TOP_K = 2
RMS_EPS = 1e-6
LOG2_E = 1.4426950408889634

LANES = 128
VMEM_LIMIT_BYTES = 56 * 1024 * 1024

F32 = jnp.float32
BF16 = jnp.bfloat16


def _dot(a, b):
    return jnp.dot(a, b, preferred_element_type=F32)


def _dot_nt(a, b):
    return lax.dot_general(a, b, (((1,), (1,)), ((), ())), preferred_element_type=F32)


def _softplus(z):
    return jnp.maximum(z, 0.0) + jnp.log(1.0 + jnp.exp(-jnp.abs(z)))


def _params(*sem, flags=None):
    return pltpu.CompilerParams(dimension_semantics=sem, vmem_limit_bytes=VMEM_LIMIT_BYTES, flags=flags)


def _in_proj_kernel(x_ref, g_ref, wq, wk, wv, wgq, wgk, wgv, wgr, wlr,
                    oq, ok, ov, ogq, ogk, ogv, ogr, olr):
    x = x_ref[...]
    inv = lax.rsqrt(jnp.mean(x * x, axis=-1, keepdims=True) + RMS_EPS)
    h = ((x * inv) * g_ref[...]).astype(BF16)
    oq[...] = (_dot(h, wq[...]) * (SB_HEAD_DIM ** -0.5 * LOG2_E)).astype(oq.dtype)
    ok[...] = _dot(h, wk[...]).astype(ok.dtype)
    ov[...] = _dot(h, wv[...]).astype(ov.dtype)
    ogq[...] = (_dot(h, wgq[...]) * (GLA_DK ** -0.5)).astype(ogq.dtype)
    ogk[...] = _dot(h, wgk[...]).astype(ogk.dtype)
    ogv[...] = _dot(h, wgv[...]).astype(ogv.dtype)
    ogr[...] = _dot(h, wgr[...]).astype(ogr.dtype)
    olr[...] = _dot(h, wlr[...]).astype(olr.dtype)


def _in_proj(x2, ln_g, w_in, tm):
    T, D = x2.shape
    sizes = (SB_WIDTH, SB_WIDTH, SB_WIDTH, GLA_KW, GLA_KW, GLA_VW, GLA_VW, GLA_GATE_RANK)
    offs = np.concatenate([[0], np.cumsum(sizes)])
    ws = [w_in[:, int(offs[i]):int(offs[i + 1])].astype(BF16) for i in range(len(sizes))]
    out_dtypes = (BF16, BF16, BF16, F32, F32, BF16, F32, F32)
    row = lambda n: pl.BlockSpec((tm, n), lambda i: (i, 0))
    full = lambda n: pl.BlockSpec((D, n), lambda i: (0, 0))
    return pl.pallas_call(
        _in_proj_kernel,
        grid=(T // tm,),
        in_specs=[row(D), pl.BlockSpec((1, D), lambda i: (0, 0))] + [full(n) for n in sizes],
        out_specs=[row(n) for n in sizes],
        out_shape=[jax.ShapeDtypeStruct((T, n), dt) for n, dt in zip(sizes, out_dtypes)],
        compiler_params=_params("parallel"),
        name="in_proj",
    )(x2, ln_g.reshape(1, D), *ws)


SB_ZERO_WEIGHT_BITS = 160.0


def _softplus2(z):
    return jnp.maximum(z, 0.0) + jnp.log2(1.0 + jnp.exp2(-jnp.abs(z)))


def _sb_query_block(q_ref, k_ref, v_ref, o_ref, qh_ref, acc_ref, carry_ref, *, row0, qi, tq, n_pairs,
                    alongside=None):
    qrows = slice(row0, row0 + tq)
    r = lax.broadcasted_iota(jnp.int32, (tq, tq), 0)
    c = lax.broadcasted_iota(jnp.int32, (tq, tq), 1)
    causal = c < r
    suffix = (r >= c).astype(BF16)
    head0 = lax.broadcasted_iota(jnp.int32, (1, LANES), 1) < SB_HEAD_DIM

    for p in range(n_pairs):
        q2 = q_ref[qrows, p * LANES:(p + 1) * LANES]
        zero = jnp.zeros_like(q2)
        qh_ref[p, 0:tq] = jnp.where(head0, q2, zero)
        qh_ref[p, tq:2 * tq] = jnp.where(head0, zero, q2)
    acc_ref[...] = jnp.zeros_like(acc_ref)
    carry_ref[...] = jnp.zeros_like(carry_ref)
    causal2 = jnp.concatenate([causal, causal], axis=0)

    def key_block(kb, diag):
        rows = pl.ds(pl.multiple_of(kb * tq, tq), tq)
        for p in range(n_pairs):
            cols = pl.ds(p * LANES, LANES)
            k2 = k_ref[rows, cols]
            v2 = v_ref[rows, cols]
            vzero = jnp.zeros_like(v2)
            v_stack = jnp.concatenate([jnp.where(head0, v2, vzero), jnp.where(head0, vzero, v2)], axis=0)
            z = _dot_nt(qh_ref[p], k2)
            sp = _softplus2(z)
            if diag:
                sp = jnp.where(causal2, sp, 0.0)
            run = _dot(sp.astype(BF16), suffix)
            carry = carry_ref[p]
            w = jnp.exp2(z - run - jnp.concatenate([carry] * (tq // LANES), axis=1))
            if diag:
                w = jnp.where(causal2, w, 0.0)
            w = w.astype(BF16)
            acc_ref[p] += _dot(jnp.concatenate([w[0:tq], w[tq:2 * tq]], axis=1), v_stack)
            carry_ref[p] = carry + jnp.broadcast_to(run[:, 0:1], (2 * tq, LANES))

    def min_carry():
        m = carry_ref[0]
        for p in range(1, n_pairs):
            m = jnp.minimum(m, carry_ref[p])
        return jnp.min(m)

    def cond(state):
        kb, m = state
        return jnp.logical_and(kb >= 0, m < SB_ZERO_WEIGHT_BITS)

    def body(state):
        kb, _ = state
        key_block(kb, False)
        return kb - 1, min_carry()

    if alongside is not None:
        alongside()
    key_block(qi, True)
    lax.while_loop(cond, body, (qi - 1, min_carry()))
    for p in range(n_pairs):
        o_ref[qrows, p * LANES:(p + 1) * LANES] = acc_ref[p].astype(o_ref.dtype)


GLA_CHUNK = 64
GLA_LEVELS = (32, 16, 8, 4, 2, 1)


def _gla_tables(C=GLA_CHUNK):
    t = np.arange(C)[:, None]
    j = np.arange(C)[None, :]
    blocks = [(j <= t), (j > t)]
    q_rows, k_rows, masks = [], [], []
    for half in GLA_LEVELS:
        blk = 2 * half
        u = (t // blk) * blk + half - 1
        second = (t % blk) >= half
        q_rows.append(second & (j > u) & (j <= t))
        k_rows.append((~second) & (j > t) & (j <= u))
        s = j
        masks.append(second & ((s % blk) < half) & ((s // blk) == (t // blk)))
    blocks += q_rows + k_rows
    blocks.append(np.ones((8, C), bool))
    table = np.concatenate(blocks, axis=0).astype(np.float32)
    table = np.concatenate([table, table], axis=1)
    return table, np.concatenate(masks, axis=0).astype(np.float32)


def _split_bf16(a):
    hi = a.astype(BF16)
    lo = (a - hi.astype(F32)).astype(BF16)
    return hi, lo


def _gla_head_pair(p, gq_ref, gk_ref, gv_ref, gr_ref, lr_ref, w2_ref, gb_ref, ng_ref, tab_ref, msk_ref,
                   o_ref, state_ref, *, n_chunks):
    C = GLA_CHUNK
    n_lv = len(GLA_LEVELS)

    head0 = lax.broadcasted_iota(jnp.int32, (1, LANES), 1) < GLA_DK
    eye = (lax.broadcasted_iota(jnp.int32, (C, C), 0) == lax.broadcasted_iota(jnp.int32, (C, C), 1))
    w2_hi, w2_lo = _split_bf16(w2_ref[...])
    lr_hi, lr_lo = _split_bf16(lr_ref[...])

    def stack_heads(a):
        return jnp.concatenate([jnp.where(head0, a, 0.0), jnp.where(head0, 0.0, a)], axis=0)

    cols = slice(p * LANES, (p + 1) * LANES)
    pre = (_dot(lr_hi, w2_hi[:, cols]) + _dot(lr_hi, w2_lo[:, cols])
           + _dot(lr_lo, w2_hi[:, cols])) + gb_ref[:, cols]
    log_a = -_softplus(-pre) * (1.0 / GLA_GATE_TAU)
    la_hi, la_lo = _split_bf16(log_a)
    la = jnp.concatenate(
        [jnp.concatenate([la_hi[c * C:(c + 1) * C], la_lo[c * C:(c + 1) * C]], axis=0)
         for c in range(n_chunks)], axis=1)
    ex_all = jnp.exp(_dot(tab_ref[...], la))
    st = state_ref[p]
    for c in range(n_chunks):
        rows = slice(c * C, (c + 1) * C)
        ex = ex_all[:, c * LANES:(c + 1) * LANES]
        q2 = gq_ref[rows, cols]
        k2 = gk_ref[rows, cols]
        q_int = q2 * ex[0:C]
        k_dec = (k2 * ex[C:2 * C]).astype(BF16)
        dec_row = ex[(2 + 2 * n_lv) * C:(2 + 2 * n_lv) * C + 1]
        diag = jnp.sum(stack_heads(q2 * k2), axis=1, keepdims=True)
        scores = jnp.where(jnp.concatenate([eye, eye], axis=0), diag, 0.0)
        for l in range(n_lv):
            q_l = stack_heads(q2 * ex[(2 + l) * C:(3 + l) * C]).astype(BF16)
            k_l = (k2 * ex[(2 + n_lv + l) * C:(3 + n_lv + l) * C]).astype(BF16)
            m_l = msk_ref[l * C:(l + 1) * C, :] > 0.5
            scores = scores + jnp.where(jnp.concatenate([m_l, m_l], axis=0), _dot_nt(q_l, k_l), 0.0)
        scores = scores.astype(BF16)
        o_inter = _dot_nt(stack_heads(q_int).astype(BF16), st.astype(BF16))
        upd = []
        for h in range(2):
            vcols = slice((2 * p + h) * GLA_DV, (2 * p + h + 1) * GLA_DV)
            v_h = gv_ref[rows, vcols]
            o = o_inter[h * C:(h + 1) * C] + _dot(scores[h * C:(h + 1) * C], v_h)
            inv = lax.rsqrt(jnp.mean(o * o, axis=-1, keepdims=True) + RMS_EPS)
            y = (o * inv) * ng_ref[:, vcols]
            g = gr_ref[rows, vcols]
            y = y * (g / (1.0 + jnp.exp(-g)))
            o_ref[rows, vcols] = y.astype(o_ref.dtype)
            v_t = v_h.astype(F32).T.astype(BF16)
            upd.append(_dot(v_t, k_dec))
        st = st * dec_row + jnp.where(head0, upd[0], upd[1])
    state_ref[p] = st


def _mixers_kernel(q_ref, k_ref, v_ref, gq_ref, gk_ref, gv_ref, gr_ref, lr_ref, w2_ref, gb_ref, ng_ref,
                   tab_ref, msk_ref, osb_ref, ogla_ref, qh_ref, acc_ref, carry_ref, state_ref,
                   *, tq, n_pairs, n_chunks):
    blk = pl.program_id(1)
    n_q = q_ref.shape[0] // tq
    assert n_q == GLA_HEADS // 2

    @pl.when(blk == 0)
    def _():
        state_ref[...] = jnp.zeros_like(state_ref)

    for half in range(n_q):
        gla = functools.partial(_gla_head_pair, half, gq_ref, gk_ref, gv_ref, gr_ref, lr_ref, w2_ref, gb_ref,
                                ng_ref, tab_ref, msk_ref, ogla_ref, state_ref, n_chunks=n_chunks)
        _sb_query_block(q_ref, k_ref, v_ref, osb_ref, qh_ref, acc_ref, carry_ref, row0=half * tq,
                        qi=blk * n_q + half, tq=tq, n_pairs=n_pairs, alongside=gla)


def _mixers(q, k, v, gq, gk, gv, gr, lr, w2, gb, ng, B, S, tb, tq):
    table, masks = _gla_tables()
    table = jnp.asarray(table, BF16)
    masks = jnp.asarray(masks, F32)
    T = B * S
    W = q.shape[-1]
    n_pairs = W // LANES
    nblk = S // tb
    q3, k3, v3 = (a.reshape(B, S, W) for a in (q, k, v))
    blk3 = pl.BlockSpec((None, tb, W), lambda b, i: (b, i, 0))
    seq3 = pl.BlockSpec((None, S, W), lambda b, i: (b, 0, 0))
    row = lambda n: pl.BlockSpec((tb, n), lambda b, i: (b * nblk + i, 0))
    const = lambda a: pl.BlockSpec(a.shape, lambda b, i: (0,) * a.ndim)
    w2 = w2.astype(F32)
    gb2 = gb.reshape(1, GLA_KW)
    ng2 = ng.reshape(1, GLA_VW)
    o_sb, o_gla = pl.pallas_call(
        functools.partial(_mixers_kernel, tq=tq, n_pairs=n_pairs, n_chunks=tb // GLA_CHUNK),
        grid=(B, nblk),
        in_specs=[blk3, seq3, seq3, row(GLA_KW), row(GLA_KW), row(GLA_VW), row(GLA_VW), row(GLA_GATE_RANK),
                  const(w2), const(gb2), const(ng2), const(table), const(masks)],
        out_specs=[blk3, row(GLA_VW)],
        out_shape=[jax.ShapeDtypeStruct((B, S, W), BF16), jax.ShapeDtypeStruct((T, GLA_VW), BF16)],
        scratch_shapes=[pltpu.VMEM((n_pairs, 2 * tq, LANES), BF16),
                        pltpu.VMEM((n_pairs, tq, LANES), F32),
                        pltpu.VMEM((n_pairs, 2 * tq, LANES), F32),
                        pltpu.VMEM((GLA_HEADS // 2, GLA_DV, 2 * GLA_DK), F32)],
        compiler_params=_params("parallel", "arbitrary"),
        name="mixers",
    )(q3, k3, v3, gq, gk, gv, gr, lr, w2, gb2, ng2, table, masks)
    return o_sb.reshape(T, W), o_gla


SUBLANES = 8


def _store_token_tiles(ref, value, accumulate=False):
    n = value.shape[0]
    for s in range(SUBLANES):
        rows = pl.ds(s, n, stride=SUBLANES)
        chunk = value[:, s * LANES:(s + 1) * LANES]
        ref[rows, :] = ref[rows, :] + chunk if accumulate else chunk


def _load_token_tiles(ref, n):
    return jnp.concatenate([ref[pl.ds(s, n, stride=SUBLANES), :] for s in range(SUBLANES)], axis=1)


def _out_proj_kernel(x_ref, oa_ref, ob_ref, wa_ref, wb_ref, g_ref, wr_ref, br_ref, tri_ref,
                     x1_ref, h2_ref, ri_ref, rf_ref, cnt_ref):
    @pl.when(pl.program_id(0) == 0)
    def _():
        cnt_ref[...] = jnp.zeros_like(cnt_ref)

    x1 = x_ref[...] + _dot(oa_ref[...], wa_ref[...]) + _dot(ob_ref[...], wb_ref[...])
    x1_ref[...] = x1
    inv = lax.rsqrt(jnp.mean(x1 * x1, axis=-1, keepdims=True) + RMS_EPS)
    h2 = (x1 * inv) * g_ref[...]
    _store_token_tiles(h2_ref, h2)

    h_hi, h_lo = _split_bf16(h2)
    w_hi, w_lo = _split_bf16(wr_ref[...])
    hw = _dot(h_hi, jnp.concatenate([w_hi, w_lo], axis=1))
    logits = hw[:, :LANES] + hw[:, LANES:] + _dot(h_lo, w_hi) + br_ref[...]
    lane = lax.broadcasted_iota(jnp.int32, logits.shape, 1).astype(F32)
    neg = -jnp.inf

    def first_max(vals):
        m = jnp.max(vals, axis=1, keepdims=True)
        idx = jnp.min(jnp.where(vals == m, lane, float(LANES)), axis=1, keepdims=True)
        return m, idx

    is_group = lane < N_GROUPS
    g_max, g_idx = first_max(jnp.where(is_group, logits, neg))
    g_p = 1.0 / jnp.sum(jnp.where(is_group, jnp.exp(logits - g_max), 0.0), axis=1, keepdims=True)
    lo_lane = N_GROUPS + EXPERTS_PER_GROUP * g_idx
    sel = jnp.where((lane >= lo_lane) & (lane < lo_lane + EXPERTS_PER_GROUP), logits, neg)
    m1, i1 = first_max(sel)
    m2, i2 = first_max(jnp.where(lane == i1, neg, sel))
    e = jnp.exp(m2 - m1)
    gate1 = g_p / (1.0 + e)
    gate2 = g_p * e / (1.0 + e)
    rf_ref[...] = jnp.where(lane == 0, gate1, jnp.where(lane == 1, gate2, 0.0))

    e1 = i1 - N_GROUPS
    e2 = i2 - N_GROUPS
    hits = (lane == e1).astype(F32) + (lane == e2).astype(F32)
    before = _dot(tri_ref[...], hits.astype(BF16)) + cnt_ref[...]
    r1 = jnp.sum(jnp.where(lane == e1, before, 0.0), axis=1, keepdims=True)
    r2 = jnp.sum(jnp.where(lane == e2, before, 0.0), axis=1, keepdims=True)
    cnt_ref[...] += jnp.sum(hits, axis=0, keepdims=True)
    packed = jnp.where(lane == 0, e1, jnp.where(lane == 1, e2,
                       jnp.where(lane == 2, r1, jnp.where(lane == 3, r2, 0.0))))
    for j in range(packed.shape[0] // LANES):
        rows_t = packed[j * LANES:(j + 1) * LANES, :].T
        for k in range(2 * TOP_K):
            ri_ref[k, j:j + 1, :] = rows_t[k:k + 1, :].astype(jnp.int32)


def _out_proj(x2, o_sb, o_gla, w_out, ln_g, w_group, b_group, w_expert, b_expert, tm):
    T, D = x2.shape
    wa = w_out[:SB_WIDTH].astype(BF16)
    wb = w_out[SB_WIDTH:].astype(BF16)
    n_r = N_GROUPS + N_EXPERTS
    wr = jnp.zeros((D, LANES), F32).at[:, :n_r].set(jnp.concatenate([w_group, w_expert], axis=1))
    br = jnp.zeros((1, LANES), F32).at[0, :n_r].set(jnp.concatenate([b_group, b_expert]))
    row = lambda n: pl.BlockSpec((tm, n), lambda i: (i, 0))
    const = lambda a: pl.BlockSpec(a.shape, lambda i: (0,) * a.ndim)
    g2 = ln_g.reshape(1, D)
    tri = jnp.asarray(np.tril(np.ones((tm, tm), np.float32), -1), BF16)
    return pl.pallas_call(
        _out_proj_kernel,
        grid=(T // tm,),
        in_specs=[row(D), row(SB_WIDTH), row(GLA_VW), const(wa), const(wb), const(g2), const(wr), const(br),
                  const(tri)],
        out_specs=[row(D), pl.BlockSpec((tm * SUBLANES, LANES), lambda i: (i, 0)),
                   pl.BlockSpec((2 * TOP_K, tm // LANES, LANES), lambda i: (0, i, 0)), row(LANES),
                   pl.BlockSpec((1, LANES), lambda i: (0, 0))],
        out_shape=[jax.ShapeDtypeStruct((T, D), F32), jax.ShapeDtypeStruct((T * SUBLANES, LANES), F32),
                   jax.ShapeDtypeStruct((2 * TOP_K, T // LANES, LANES), jnp.int32),
                   jax.ShapeDtypeStruct((T, LANES), F32), jax.ShapeDtypeStruct((1, LANES), F32)],
        compiler_params=_params("arbitrary"),
        name="out_proj",
    )(x2, o_sb, o_gla, wa, wb, g2, wr, br, tri)


def _route_tables(route, counts, n_rows, tm):
    counts = counts[0, :N_EXPERTS].astype(jnp.int32)
    ends = jnp.cumsum(counts)
    starts = ends - counts
    route = route.reshape(2 * TOP_K, -1)
    expert = route[:TOP_K, :, None] == jnp.arange(N_EXPERTS, dtype=jnp.int32)
    dest = (jnp.sum(jnp.where(expert, starts, 0), axis=-1) + route[TOP_K:]).reshape(-1)

    n_items = n_rows // tm + N_EXPERTS
    items_e = (counts + tm - 1) // tm
    item_hi = jnp.cumsum(items_e)
    item_lo = item_hi - items_e
    i = jnp.arange(n_items, dtype=jnp.int32)
    e_of = jnp.minimum(jnp.sum(item_hi[None, :] <= i[:, None], axis=1), N_EXPERTS - 1)
    owner = e_of[:, None] == jnp.arange(N_EXPERTS, dtype=jnp.int32)
    pick = lambda tab: jnp.sum(jnp.where(owner, tab, 0), axis=-1)
    item_row = jnp.where(i < item_hi[-1], pick(starts) + (i - pick(item_lo)) * tm, n_rows)
    return (dest, item_row.astype(jnp.int32), item_lo.astype(jnp.int32), item_hi.astype(jnp.int32))


ROW_ISSUE_UNROLL = 8


def _token_tile(ref, t):
    return ref.at[pl.ds(pl.multiple_of(t * SUBLANES, SUBLANES), SUBLANES), :]


def _start_token_gather(src_ref, idx_ref, idx_base, idx_stride, dst_ref, sem, n):
    def body(i, _):
        for j in range(ROW_ISSUE_UNROLL):
            r = i * ROW_ISSUE_UNROLL + j
            t = idx_ref[idx_base + r * idx_stride]
            pltpu.make_async_copy(_token_tile(src_ref, t), _token_tile(dst_ref, r), sem).start(priority=j % 2)
        return 0
    lax.fori_loop(0, n // ROW_ISSUE_UNROLL, body, 0)


def _wait_token_gather(src_ref, dst_ref, sem):
    pltpu.make_async_copy(src_ref.at[pl.ds(0, dst_ref.shape[0]), :], dst_ref, sem).wait()


EXPERT_GATHER_DEPTH = 4


def _expert_kernel(row_ref, lo_ref, hi_ref, dest_ref,
                   h_ref, wg_ref, wu_ref, wd_ref, y_ref, xbuf, ybuf, tok_ref, wg_bf, wu_bf, wd_bf,
                   gsem, wsem, *, tm):
    e = pl.program_id(0)
    n_rows = dest_ref.shape[0]
    n_tok = n_rows // TOP_K
    total = hi_ref[N_EXPERTS - 1]

    depth = xbuf.shape[0]

    def gather_copy(i, r, slot):
        t = tok_ref[row_ref[i] + r]
        return pltpu.make_async_copy(_token_tile(h_ref, t), _token_tile(xbuf.at[slot], r), gsem.at[slot])

    def out_copy(i):
        slot = lax.rem(i, 2)
        rows = pl.ds(pl.multiple_of(row_ref[i] * SUBLANES, SUBLANES), tm * SUBLANES)
        return pltpu.make_async_copy(ybuf.at[slot], y_ref.at[rows, :], wsem)

    @pl.when(e == 0)
    def _():
        def invert(t, _):
            for k in range(TOP_K):
                tok_ref[dest_ref[k * n_tok + t]] = t
            return 0
        lax.fori_loop(0, n_tok, invert, 0, unroll=ROW_ISSUE_UNROLL)

        def spare(r, _):
            tok_ref[n_rows + r] = 0
            return 0
        lax.fori_loop(0, tm, spare, 0, unroll=ROW_ISSUE_UNROLL)
        ybuf[0] = jnp.zeros(ybuf.shape[1:], ybuf.dtype)
        zero_spare = pltpu.make_async_copy(
            ybuf.at[0], y_ref.at[pl.ds(n_rows * SUBLANES, tm * SUBLANES), :], wsem)
        zero_spare.start()
        zero_spare.wait()

        for ahead in range(depth - 1):
            @pl.when(total > ahead)
            def _():
                def first(r, _):
                    gather_copy(ahead, r, ahead).start()
                    return 0
                lax.fori_loop(0, tm, first, 0, unroll=ROW_ISSUE_UNROLL)

    wg_bf[...] = wg_ref[...].astype(BF16)
    wu_bf[...] = wu_ref[...].astype(BF16)
    wd_bf[...] = wd_ref[...].astype(BF16)

    def item(i, _):
        slot = lax.rem(i, depth)
        _wait_token_gather(h_ref, xbuf.at[slot], gsem.at[slot])
        xb = _load_token_tiles(xbuf.at[slot], tm).astype(BF16)
        g = _dot(xb, wg_bf[...])
        u = _dot(xb, wu_bf[...])
        hidden = ((g / (1.0 + jnp.exp(-g))) * u).astype(BF16)

        def down_proj():
            _store_token_tiles(ybuf.at[lax.rem(i, 2)], _dot(hidden, wd_bf[...]))

        nxt = i + depth - 1

        @pl.when(nxt < total)
        def _():
            nslot = lax.rem(nxt, depth)
            for r in range(tm):
                gather_copy(nxt, r, nslot).start(priority=r % 2)
            down_proj()

        @pl.when(nxt >= total)
        def _():
            down_proj()

        @pl.when(i > 0)
        def _():
            out_copy(i - 1).wait()
        out_copy(i).start()
        return 0

    lax.fori_loop(lo_ref[e], hi_ref[e], item, 0)

    @pl.when((e == pl.num_programs(0) - 1) & (total > 0))
    def _():
        out_copy(total - 1).wait()


def _expert_ffn(h2_tiles, dest, tables, w_gate, w_up, w_down, tm):
    N = dest.shape[0]
    _, D, F = w_gate.shape
    assert D == SUBLANES * LANES
    item_row, item_lo, item_hi = tables
    wspec = lambda shape: pl.BlockSpec((None,) + shape, lambda e, *_: (e, 0, 0))
    return pl.pallas_call(
        functools.partial(_expert_kernel, tm=tm),
        grid_spec=pltpu.PrefetchScalarGridSpec(
            num_scalar_prefetch=4,
            grid=(N_EXPERTS,),
            in_specs=[pl.BlockSpec(memory_space=pl.ANY), wspec((D, F)), wspec((D, F)), wspec((F, D))],
            out_specs=pl.BlockSpec(memory_space=pl.ANY),
            scratch_shapes=[pltpu.VMEM((EXPERT_GATHER_DEPTH, tm * SUBLANES, LANES), F32),
                            pltpu.VMEM((2, tm * SUBLANES, LANES), F32),
                            pltpu.SMEM((N + tm,), jnp.int32),
                            pltpu.VMEM((D, F), BF16), pltpu.VMEM((D, F), BF16), pltpu.VMEM((F, D), BF16),
                            pltpu.SemaphoreType.DMA((EXPERT_GATHER_DEPTH,)), pltpu.SemaphoreType.DMA(())],
        ),
        out_shape=jax.ShapeDtypeStruct(((N + tm) * SUBLANES, LANES), F32),
        compiler_params=_params("arbitrary"),
        name="expert_ffn",
    )(item_row, item_lo, item_hi, dest, h2_tiles, w_gate, w_up, w_down)


COMBINE_RING = 4
COMBINE_AHEAD = 2


def _combine_kernel(dest_ref, x1_ref, rf_ref, g_ref, y_ref, o_ref, *scratch, tb):
    bufs, sem = scratch[:COMBINE_RING], scratch[COMBINE_RING]
    i = pl.program_id(0)
    n_steps = pl.num_programs(0)
    n_tok = n_steps * COMBINE_RING * tb

    def copies(j, slot, r):
        return [pltpu.make_async_copy(_token_tile(y_ref, dest_ref[k * n_tok + j * tb + r]),
                                      _token_tile(bufs[slot].at[k], r), sem.at[slot])
                for k in range(TOP_K)]

    @pl.when(i == 0)
    def _():
        for slot in range(COMBINE_AHEAD):
            def first(r, _):
                for c in copies(slot, slot, r):
                    c.start()
                return 0
            lax.fori_loop(0, tb, first, 0, unroll=ROW_ISSUE_UNROLL // TOP_K)

    def sub_block(s, fetch):
        j = i * COMBINE_RING + s
        for k in range(TOP_K):
            _wait_token_gather(y_ref, bufs[s].at[k], sem.at[s])
        if fetch:
            for r in range(tb):
                for k, c in enumerate(copies(j + COMBINE_AHEAD, (s + COMBINE_AHEAD) % COMBINE_RING, r)):
                    c.start(priority=k)
        rows = slice(s * tb, (s + 1) * tb)
        rf = rf_ref[rows, :]
        x = (x1_ref[rows, :] + rf[:, 0:1] * _load_token_tiles(bufs[s].at[0], tb)
             + rf[:, 1:2] * _load_token_tiles(bufs[s].at[1], tb))
        inv = lax.rsqrt(jnp.mean(x * x, axis=-1, keepdims=True) + RMS_EPS)
        o_ref[rows, :] = (x * inv) * g_ref[...]

    for s in range(COMBINE_RING):
        if s + COMBINE_AHEAD < COMBINE_RING:
            sub_block(s, True)
        else:
            @pl.when(i + 1 < n_steps)
            def _():
                sub_block(s, True)

            @pl.when(i + 1 >= n_steps)
            def _():
                sub_block(s, False)


def _combine(x1, route_f, y_tiles, dest, ln_g, tb):
    T, D = x1.shape
    g2 = ln_g.reshape(1, D)
    step = COMBINE_RING * tb
    return pl.pallas_call(
        functools.partial(_combine_kernel, tb=tb),
        grid_spec=pltpu.PrefetchScalarGridSpec(
            num_scalar_prefetch=1,
            grid=(T // step,),
            in_specs=[pl.BlockSpec((step, D), lambda i, d: (i, 0)),
                      pl.BlockSpec((step, LANES), lambda i, d: (i, 0)),
                      pl.BlockSpec((1, D), lambda i, d: (0, 0)),
                      pl.BlockSpec(memory_space=pl.ANY)],
            out_specs=pl.BlockSpec((step, D), lambda i, d: (i, 0)),
            scratch_shapes=[pltpu.VMEM((TOP_K, tb * SUBLANES, LANES), F32) for _ in range(COMBINE_RING)]
                           + [pltpu.SemaphoreType.DMA((COMBINE_RING,))],
        ),
        out_shape=jax.ShapeDtypeStruct((T, D), F32),
        compiler_params=_params("arbitrary"),
        name="combine",
    )(dest, x1, route_f, g2, y_tiles)


def _layer(x2, B, S, ln1_g, w_in, w2, gb, ng, w_out, ln2_g, w_group, b_group, w_expert, b_expert,
           w_gate, w_up, w_down, ln_f_g, *, tm, tm_out, tq, tb_gla, tb_row, tm_e):
    q, k, v, gq, gk, gv, gr, lr = _in_proj(x2, ln1_g, w_in, tm)
    o_sb, o_gla = _mixers(q, k, v, gq, gk, gv, gr, lr, w2, gb, ng, B, S, tb_gla, tq)
    x1, h2_tiles, route_i, route_f, counts = _out_proj(x2, o_sb, o_gla, w_out, ln2_g, w_group, b_group,
                                                       w_expert, b_expert, tm_out)
    T = x2.shape[0]
    dest, *tables = _route_tables(route_i, counts, TOP_K * T, tm_e)
    y_tiles = _expert_ffn(h2_tiles, dest, tables, w_gate, w_up, w_down, tm_e)
    return _combine(x1, route_f, y_tiles, dest, ln_f_g, tb_row)


def kernel(x, ln1_g, w_in, gla_gate_w2, gla_gate_b, gla_norm_g, w_out, ln2_g, w_group, b_group, w_expert, b_expert, exp_w_gate, exp_w_up, exp_w_down, ln_f_g):
    B, S, D = x.shape
    assert ln1_g.shape[0] == 1, "single-layer stack"
    out = _layer(x.reshape(B * S, D), B, S, ln1_g[0], w_in[0], gla_gate_w2[0], gla_gate_b[0],
                 gla_norm_g[0], w_out[0], ln2_g[0], w_group[0], b_group[0], w_expert[0], b_expert[0],
                 exp_w_gate[0], exp_w_up[0], exp_w_down[0], ln_f_g,
                 tm=512, tm_out=1024, tq=256, tb_gla=512, tb_row=256, tm_e=256)
    return out.reshape(B, S, D)
```

```python
import functools

import numpy as np
import jax
import jax.numpy as jnp
from jax import lax
from jax.experimental import pallas as pl
from jax.experimental.pallas import tpu as pltpu

SB_HEADS = 8
SB_HEAD_DIM = 64
SB_WIDTH = SB_HEADS * SB_HEAD_DIM
GLA_HEADS = 4
GLA_DK = 64
GLA_DV = 128
GLA_KW = GLA_HEADS * GLA_DK
GLA_VW = GLA_HEADS * GLA_DV
GLA_GATE_RANK = 16
GLA_GATE_TAU = 16.0
N_GROUPS = 4
EXPERTS_PER_GROUP = 8
N_EXPERTS = N_GROUPS * EXPERTS_PER_GROUP
TOP_K = 2
RMS_EPS = 1e-6
LOG2_E = 1.4426950408889634

LANES = 128
VMEM_LIMIT_BYTES = 56 * 1024 * 1024

F32 = jnp.float32
BF16 = jnp.bfloat16


def _dot(a, b):
    return jnp.dot(a, b, preferred_element_type=F32)


def _dot_nt(a, b):
    return lax.dot_general(a, b, (((1,), (1,)), ((), ())), preferred_element_type=F32)


def _softplus(z):
    return jnp.maximum(z, 0.0) + jnp.log(1.0 + jnp.exp(-jnp.abs(z)))


def _params(*sem, flags=None):
    return pltpu.CompilerParams(dimension_semantics=sem, vmem_limit_bytes=VMEM_LIMIT_BYTES, flags=flags)


def _in_proj_kernel(x_ref, g_ref, wq, wk, wv, wgq, wgk, wgv, wgr, wlr,
                    oq, ok, ov, ogq, ogk, ogv, ogr, olr):
    x = x_ref[...]
    inv = lax.rsqrt(jnp.mean(x * x, axis=-1, keepdims=True) + RMS_EPS)
    h = ((x * inv) * g_ref[...]).astype(BF16)
    oq[...] = (_dot(h, wq[...]) * (SB_HEAD_DIM ** -0.5 * LOG2_E)).astype(oq.dtype)
    ok[...] = _dot(h, wk[...]).astype(ok.dtype)
    ov[...] = _dot(h, wv[...]).astype(ov.dtype)
    ogq[...] = (_dot(h, wgq[...]) * (GLA_DK ** -0.5)).astype(ogq.dtype)
    ogk[...] = _dot(h, wgk[...]).astype(ogk.dtype)
    ogv[...] = _dot(h, wgv[...]).astype(ogv.dtype)
    ogr[...] = _dot(h, wgr[...]).astype(ogr.dtype)
    olr[...] = _dot(h, wlr[...]).astype(olr.dtype)


def _in_proj(x2, ln_g, w_in, tm):
    T, D = x2.shape
    sizes = (SB_WIDTH, SB_WIDTH, SB_WIDTH, GLA_KW, GLA_KW, GLA_VW, GLA_VW, GLA_GATE_RANK)
    offs = np.concatenate([[0], np.cumsum(sizes)])
    ws = [w_in[:, int(offs[i]):int(offs[i + 1])].astype(BF16) for i in range(len(sizes))]
    out_dtypes = (BF16, BF16, BF16, F32, F32, BF16, F32, F32)
    row = lambda n: pl.BlockSpec((tm, n), lambda i: (i, 0))
    full = lambda n: pl.BlockSpec((D, n), lambda i: (0, 0))
    return pl.pallas_call(
        _in_proj_kernel,
        grid=(T // tm,),
        in_specs=[row(D), pl.BlockSpec((1, D), lambda i: (0, 0))] + [full(n) for n in sizes],
        out_specs=[row(n) for n in sizes],
        out_shape=[jax.ShapeDtypeStruct((T, n), dt) for n, dt in zip(sizes, out_dtypes)],
        compiler_params=_params("parallel"),
        name="in_proj",
    )(x2, ln_g.reshape(1, D), *ws)


SB_ZERO_WEIGHT_BITS = 160.0


def _softplus2(z):
    return jnp.maximum(z, 0.0) + jnp.log2(1.0 + jnp.exp2(-jnp.abs(z)))


def _sb_query_block(q_ref, k_ref, v_ref, o_ref, qh_ref, acc_ref, carry_ref, *, row0, qi, tq, n_pairs,
                    alongside=None):
    qrows = slice(row0, row0 + tq)
    r = lax.broadcasted_iota(jnp.int32, (tq, tq), 0)
    c = lax.broadcasted_iota(jnp.int32, (tq, tq), 1)
    causal = c < r
    suffix = (r >= c).astype(BF16)
    head0 = lax.broadcasted_iota(jnp.int32, (1, LANES), 1) < SB_HEAD_DIM

    for p in range(n_pairs):
        q2 = q_ref[qrows, p * LANES:(p + 1) * LANES]
        zero = jnp.zeros_like(q2)
        qh_ref[p, 0:tq] = jnp.where(head0, q2, zero)
        qh_ref[p, tq:2 * tq] = jnp.where(head0, zero, q2)
    acc_ref[...] = jnp.zeros_like(acc_ref)
    carry_ref[...] = jnp.zeros_like(carry_ref)
    causal2 = jnp.concatenate([causal, causal], axis=0)

    def key_block(kb, diag):
        rows = pl.ds(pl.multiple_of(kb * tq, tq), tq)
        for p in range(n_pairs):
            cols = pl.ds(p * LANES, LANES)
            k2 = k_ref[rows, cols]
            v2 = v_ref[rows, cols]
            vzero = jnp.zeros_like(v2)
            v_stack = jnp.concatenate([jnp.where(head0, v2, vzero), jnp.where(head0, vzero, v2)], axis=0)
            z = _dot_nt(qh_ref[p], k2)
            sp = _softplus2(z)
            if diag:
                sp = jnp.where(causal2, sp, 0.0)
            run = _dot(sp.astype(BF16), suffix)
            carry = carry_ref[p]
            w = jnp.exp2(z - run - jnp.concatenate([carry] * (tq // LANES), axis=1))
            if diag:
                w = jnp.where(causal2, w, 0.0)
            w = w.astype(BF16)
            acc_ref[p] += _dot(jnp.concatenate([w[0:tq], w[tq:2 * tq]], axis=1), v_stack)
            carry_ref[p] = carry + jnp.broadcast_to(run[:, 0:1], (2 * tq, LANES))

    def min_carry():
        m = carry_ref[0]
        for p in range(1, n_pairs):
            m = jnp.minimum(m, carry_ref[p])
        return jnp.min(m)

    def cond(state):
        kb, m = state
        return jnp.logical_and(kb >= 0, m < SB_ZERO_WEIGHT_BITS)

    def body(state):
        kb, _ = state
        key_block(kb, False)
        return kb - 1, min_carry()

    if alongside is not None:
        alongside()
    key_block(qi, True)
    lax.while_loop(cond, body, (qi - 1, min_carry()))
    for p in range(n_pairs):
        o_ref[qrows, p * LANES:(p + 1) * LANES] = acc_ref[p].astype(o_ref.dtype)


GLA_CHUNK = 64
GLA_LEVELS = (32, 16, 8, 4, 2, 1)


def _gla_tables(C=GLA_CHUNK):
    t = np.arange(C)[:, None]
    j = np.arange(C)[None, :]
    blocks = [(j <= t), (j > t)]
    masks = []
    for half in GLA_LEVELS:
        blk = 2 * half
        u = (t // blk) * blk + half - 1
        second = (t % blk) >= half
        blocks.append((second & (j > u) & (j <= t)) | ((~second) & (j > t) & (j <= u)))
        s = j
        masks.append(second & ((s % blk) < half) & ((s // blk) == (t // blk)))
    blocks.append(np.ones((8, C), bool))
    table = np.concatenate(blocks, axis=0).astype(np.float32)
    table = np.concatenate([table, table], axis=1)
    return table, np.concatenate(masks, axis=0).astype(np.float32)


def _split_bf16(a):
    hi = a.astype(BF16)
    lo = (a - hi.astype(F32)).astype(BF16)
    return hi, lo


def _gla_head_pair(p, gq_ref, gk_ref, gv_ref, gr_ref, lr_ref, w2_ref, gb_ref, ng_ref, tab_ref, msk_ref,
                   o_ref, state_ref, *, n_chunks):
    C = GLA_CHUNK
    n_lv = len(GLA_LEVELS)

    head0 = lax.broadcasted_iota(jnp.int32, (1, LANES), 1) < GLA_DK
    eye = (lax.broadcasted_iota(jnp.int32, (C, C), 0) == lax.broadcasted_iota(jnp.int32, (C, C), 1))
    w2_hi, w2_lo = _split_bf16(w2_ref[...])
    lr_hi, lr_lo = _split_bf16(lr_ref[...])

    def stack_heads(a):
        return jnp.concatenate([jnp.where(head0, a, 0.0), jnp.where(head0, 0.0, a)], axis=0)

    cols = slice(p * LANES, (p + 1) * LANES)
    pre = (_dot(lr_hi, w2_hi[:, cols]) + _dot(lr_hi, w2_lo[:, cols])
           + _dot(lr_lo, w2_hi[:, cols])) + gb_ref[:, cols]
    log_a = -_softplus(-pre) * (1.0 / GLA_GATE_TAU)
    la_hi, la_lo = _split_bf16(log_a)
    la = jnp.concatenate(
        [jnp.concatenate([la_hi[c * C:(c + 1) * C], la_lo[c * C:(c + 1) * C]], axis=0)
         for c in range(n_chunks)], axis=1)
    ex_all = jnp.exp(_dot(tab_ref[...], la))
    st = state_ref[p]
    for c in range(n_chunks):
        rows = slice(c * C, (c + 1) * C)
        ex = ex_all[:, c * LANES:(c + 1) * LANES]
        q2 = gq_ref[rows, cols]
        k2 = gk_ref[rows, cols]
        q_int = q2 * ex[0:C]
        k_dec = (k2 * ex[C:2 * C]).astype(BF16)
        dec_row = ex[(2 + n_lv) * C:(2 + n_lv) * C + 1]
        diag = jnp.sum(stack_heads(q2 * k2), axis=1, keepdims=True)
        scores = jnp.where(jnp.concatenate([eye, eye], axis=0), diag, 0.0)
        for l in range(n_lv):
            ex_l = ex[(2 + l) * C:(3 + l) * C]
            q_l = stack_heads(q2 * ex_l).astype(BF16)
            k_l = (k2 * ex_l).astype(BF16)
            m_l = msk_ref[l * C:(l + 1) * C, :] > 0.5
            scores = scores + jnp.where(jnp.concatenate([m_l, m_l], axis=0), _dot_nt(q_l, k_l), 0.0)
        scores = scores.astype(BF16)
        o_inter = _dot_nt(stack_heads(q_int).astype(BF16), st.astype(BF16))
        upd = []
        for h in range(2):
            vcols = slice((2 * p + h) * GLA_DV, (2 * p + h + 1) * GLA_DV)
            v_h = gv_ref[rows, vcols]
            o = o_inter[h * C:(h + 1) * C] + _dot(scores[h * C:(h + 1) * C], v_h)
            inv = lax.rsqrt(jnp.mean(o * o, axis=-1, keepdims=True) + RMS_EPS)
            y = (o * inv) * ng_ref[:, vcols]
            g = gr_ref[rows, vcols]
            y = y * (g / (1.0 + jnp.exp(-g)))
            o_ref[rows, vcols] = y.astype(o_ref.dtype)
            v_t = v_h.astype(F32).T.astype(BF16)
            upd.append(_dot(v_t, k_dec))
        st = st * dec_row + jnp.where(head0, upd[0], upd[1])
    state_ref[p] = st


def _mixers_kernel(q_ref, k_ref, v_ref, gq_ref, gk_ref, gv_ref, gr_ref, lr_ref, w2_ref, gb_ref, ng_ref,
                   tab_ref, msk_ref, osb_ref, ogla_ref, qh_ref, acc_ref, carry_ref, state_ref,
                   *, tq, n_pairs, n_chunks):
    blk = pl.program_id(1)
    n_q = q_ref.shape[0] // tq
    assert n_q == GLA_HEADS // 2

    @pl.when(blk == 0)
    def _():
        state_ref[...] = jnp.zeros_like(state_ref)

    for half in range(n_q):
        gla = functools.partial(_gla_head_pair, half, gq_ref, gk_ref, gv_ref, gr_ref, lr_ref, w2_ref, gb_ref,
                                ng_ref, tab_ref, msk_ref, ogla_ref, state_ref, n_chunks=n_chunks)
        _sb_query_block(q_ref, k_ref, v_ref, osb_ref, qh_ref, acc_ref, carry_ref, row0=half * tq,
                        qi=blk * n_q + half, tq=tq, n_pairs=n_pairs, alongside=gla)


def _mixers(q, k, v, gq, gk, gv, gr, lr, w2, gb, ng, B, S, tb, tq):
    table, masks = _gla_tables()
    table = jnp.asarray(table, BF16)
    masks = jnp.asarray(masks, F32)
    T = B * S
    W = q.shape[-1]
    n_pairs = W // LANES
    nblk = S // tb
    q3, k3, v3 = (a.reshape(B, S, W) for a in (q, k, v))
    blk3 = pl.BlockSpec((None, tb, W), lambda b, i: (b, i, 0))
    seq3 = pl.BlockSpec((None, S, W), lambda b, i: (b, 0, 0))
    row = lambda n: pl.BlockSpec((tb, n), lambda b, i: (b * nblk + i, 0))
    const = lambda a: pl.BlockSpec(a.shape, lambda b, i: (0,) * a.ndim)
    w2 = w2.astype(F32)
    gb2 = gb.reshape(1, GLA_KW)
    ng2 = ng.reshape(1, GLA_VW)
    o_sb, o_gla = pl.pallas_call(
        functools.partial(_mixers_kernel, tq=tq, n_pairs=n_pairs, n_chunks=tb // GLA_CHUNK),
        grid=(B, nblk),
        in_specs=[blk3, seq3, seq3, row(GLA_KW), row(GLA_KW), row(GLA_VW), row(GLA_VW), row(GLA_GATE_RANK),
                  const(w2), const(gb2), const(ng2), const(table), const(masks)],
        out_specs=[blk3, row(GLA_VW)],
        out_shape=[jax.ShapeDtypeStruct((B, S, W), BF16), jax.ShapeDtypeStruct((T, GLA_VW), BF16)],
        scratch_shapes=[pltpu.VMEM((n_pairs, 2 * tq, LANES), BF16),
                        pltpu.VMEM((n_pairs, tq, LANES), F32),
                        pltpu.VMEM((n_pairs, 2 * tq, LANES), F32),
                        pltpu.VMEM((GLA_HEADS // 2, GLA_DV, 2 * GLA_DK), F32)],
        compiler_params=_params("parallel", "arbitrary"),
        name="mixers",
    )(q3, k3, v3, gq, gk, gv, gr, lr, w2, gb2, ng2, table, masks)
    return o_sb.reshape(T, W), o_gla


SUBLANES = 8


def _store_token_tiles(ref, value, accumulate=False):
    n = value.shape[0]
    for s in range(SUBLANES):
        rows = pl.ds(s, n, stride=SUBLANES)
        chunk = value[:, s * LANES:(s + 1) * LANES]
        ref[rows, :] = ref[rows, :] + chunk if accumulate else chunk


def _load_token_tiles(ref, n):
    return jnp.concatenate([ref[pl.ds(s, n, stride=SUBLANES), :] for s in range(SUBLANES)], axis=1)


def _out_proj_kernel(x_ref, oa_ref, ob_ref, wa_ref, wb_ref, g_ref, wr_ref, br_ref, tri_ref,
                     x1_ref, h2_ref, ri_ref, rf_ref, cnt_ref):
    @pl.when(pl.program_id(0) == 0)
    def _():
        cnt_ref[...] = jnp.zeros_like(cnt_ref)

    x1 = x_ref[...] + _dot(oa_ref[...], wa_ref[...]) + _dot(ob_ref[...], wb_ref[...])
    x1_ref[...] = x1
    inv = lax.rsqrt(jnp.mean(x1 * x1, axis=-1, keepdims=True) + RMS_EPS)
    h2 = (x1 * inv) * g_ref[...]
    _store_token_tiles(h2_ref, h2)

    h_hi, h_lo = _split_bf16(h2)
    w_hi, w_lo = _split_bf16(wr_ref[...])
    hw = _dot(h_hi, jnp.concatenate([w_hi, w_lo], axis=1))
    logits = hw[:, :LANES] + hw[:, LANES:] + _dot(h_lo, w_hi) + br_ref[...]
    lane = lax.broadcasted_iota(jnp.int32, logits.shape, 1).astype(F32)
    neg = -jnp.inf

    def first_max(vals):
        m = jnp.max(vals, axis=1, keepdims=True)
        idx = jnp.min(jnp.where(vals == m, lane, float(LANES)), axis=1, keepdims=True)
        return m, idx

    is_group = lane < N_GROUPS
    g_max, g_idx = first_max(jnp.where(is_group, logits, neg))
    g_p = 1.0 / jnp.sum(jnp.where(is_group, jnp.exp(logits - g_max), 0.0), axis=1, keepdims=True)
    lo_lane = N_GROUPS + EXPERTS_PER_GROUP * g_idx
    sel = jnp.where((lane >= lo_lane) & (lane < lo_lane + EXPERTS_PER_GROUP), logits, neg)
    m1, i1 = first_max(sel)
    m2, i2 = first_max(jnp.where(lane == i1, neg, sel))
    e = jnp.exp(m2 - m1)
    gate1 = g_p / (1.0 + e)
    gate2 = g_p * e / (1.0 + e)
    rf_ref[...] = jnp.where(lane == 0, gate1, jnp.where(lane == 1, gate2, 0.0))

    e1 = i1 - N_GROUPS
    e2 = i2 - N_GROUPS
    hits = (lane == e1).astype(F32) + (lane == e2).astype(F32)
    before = _dot(tri_ref[...], hits.astype(BF16)) + cnt_ref[...]
    r1 = jnp.sum(jnp.where(lane == e1, before, 0.0), axis=1, keepdims=True)
    r2 = jnp.sum(jnp.where(lane == e2, before, 0.0), axis=1, keepdims=True)
    cnt_ref[...] += jnp.sum(hits, axis=0, keepdims=True)
    packed = jnp.where(lane == 0, e1, jnp.where(lane == 1, e2,
                       jnp.where(lane == 2, r1, jnp.where(lane == 3, r2, 0.0))))
    for j in range(packed.shape[0] // LANES):
        rows_t = packed[j * LANES:(j + 1) * LANES, :].T
        for k in range(2 * TOP_K):
            ri_ref[k, j:j + 1, :] = rows_t[k:k + 1, :].astype(jnp.int32)


def _out_proj(x2, o_sb, o_gla, w_out, ln_g, w_group, b_group, w_expert, b_expert, tm):
    T, D = x2.shape
    wa = w_out[:SB_WIDTH].astype(BF16)
    wb = w_out[SB_WIDTH:].astype(BF16)
    n_r = N_GROUPS + N_EXPERTS
    wr = jnp.zeros((D, LANES), F32).at[:, :n_r].set(jnp.concatenate([w_group, w_expert], axis=1))
    br = jnp.zeros((1, LANES), F32).at[0, :n_r].set(jnp.concatenate([b_group, b_expert]))
    row = lambda n: pl.BlockSpec((tm, n), lambda i: (i, 0))
    const = lambda a: pl.BlockSpec(a.shape, lambda i: (0,) * a.ndim)
    g2 = ln_g.reshape(1, D)
    tri = jnp.asarray(np.tril(np.ones((tm, tm), np.float32), -1), BF16)
    return pl.pallas_call(
        _out_proj_kernel,
        grid=(T // tm,),
        in_specs=[row(D), row(SB_WIDTH), row(GLA_VW), const(wa), const(wb), const(g2), const(wr), const(br),
                  const(tri)],
        out_specs=[row(D), pl.BlockSpec((tm * SUBLANES, LANES), lambda i: (i, 0)),
                   pl.BlockSpec((2 * TOP_K, tm // LANES, LANES), lambda i: (0, i, 0)), row(LANES),
                   pl.BlockSpec((1, LANES), lambda i: (0, 0))],
        out_shape=[jax.ShapeDtypeStruct((T, D), F32), jax.ShapeDtypeStruct((T * SUBLANES, LANES), F32),
                   jax.ShapeDtypeStruct((2 * TOP_K, T // LANES, LANES), jnp.int32),
                   jax.ShapeDtypeStruct((T, LANES), F32), jax.ShapeDtypeStruct((1, LANES), F32)],
        compiler_params=_params("arbitrary"),
        name="out_proj",
    )(x2, o_sb, o_gla, wa, wb, g2, wr, br, tri)


def _route_tables(route, counts, n_rows, tm):
    counts = counts[0, :N_EXPERTS].astype(jnp.int32)
    ends = jnp.cumsum(counts)
    starts = ends - counts
    route = route.reshape(2 * TOP_K, -1)
    expert = route[:TOP_K, :, None] == jnp.arange(N_EXPERTS, dtype=jnp.int32)
    dest = (jnp.sum(jnp.where(expert, starts, 0), axis=-1) + route[TOP_K:]).reshape(-1)

    n_items = n_rows // tm + N_EXPERTS + EXPERT_GATHER_DEPTH
    items_e = (counts + tm - 1) // tm
    item_hi = jnp.cumsum(items_e)
    item_lo = item_hi - items_e
    i = jnp.arange(n_items, dtype=jnp.int32)
    e_of = jnp.minimum(jnp.sum(item_hi[None, :] <= i[:, None], axis=1), N_EXPERTS - 1)
    owner = e_of[:, None] == jnp.arange(N_EXPERTS, dtype=jnp.int32)
    pick = lambda tab: jnp.sum(jnp.where(owner, tab, 0), axis=-1)
    item_row = jnp.where(i < item_hi[-1], pick(starts) + (i - pick(item_lo)) * tm, n_rows)
    return (dest, item_row.astype(jnp.int32), item_lo.astype(jnp.int32), item_hi.astype(jnp.int32))


ROW_ISSUE_UNROLL = 8
EXPERT_GATHER_DEPTH = 4


def _token_tile(ref, t):
    return ref.at[pl.ds(pl.multiple_of(t * SUBLANES, SUBLANES), SUBLANES), :]


def _start_token_gather(src_ref, idx_ref, idx_base, idx_stride, dst_ref, sem, n):
    def body(i, _):
        for j in range(ROW_ISSUE_UNROLL):
            r = i * ROW_ISSUE_UNROLL + j
            t = idx_ref[idx_base + r * idx_stride]
            pltpu.make_async_copy(_token_tile(src_ref, t), _token_tile(dst_ref, r), sem).start(priority=j % 2)
        return 0
    lax.fori_loop(0, n // ROW_ISSUE_UNROLL, body, 0)


def _wait_token_gather(src_ref, dst_ref, sem):
    pltpu.make_async_copy(src_ref.at[pl.ds(0, dst_ref.shape[0]), :], dst_ref, sem).wait()


def _expert_kernel(row_ref, lo_ref, hi_ref, dest_ref,
                   h_ref, wg_ref, wu_ref, wd_ref, y_ref, *scratch, tm):
    depth = EXPERT_GATHER_DEPTH
    xbufs = scratch[:depth]
    ybuf, tok_ref, wg_bf, wu_bf, wd_bf, gsem, wsem = scratch[depth:]
    e = pl.program_id(0)
    n_rows = dest_ref.shape[0]
    n_tok = n_rows // TOP_K
    total = hi_ref[N_EXPERTS - 1]

    def gather_copy(i, r, slot):
        t = tok_ref[row_ref[i] + r]
        return pltpu.make_async_copy(_token_tile(h_ref, t), _token_tile(xbufs[slot], r), gsem.at[slot])

    def out_copy(row, slot):
        rows = pl.ds(pl.multiple_of(row * SUBLANES, SUBLANES), tm * SUBLANES)
        return pltpu.make_async_copy(ybuf.at[slot], y_ref.at[rows, :], wsem)

    @pl.when(e == 0)
    def _():
        def invert(t, _):
            for k in range(TOP_K):
                tok_ref[dest_ref[k * n_tok + t]] = t
            return 0
        lax.fori_loop(0, n_tok, invert, 0, unroll=ROW_ISSUE_UNROLL)

        def spare(r, _):
            tok_ref[n_rows + r] = 0
            return 0
        lax.fori_loop(0, tm, spare, 0, unroll=ROW_ISSUE_UNROLL)
        ybuf[1] = jnp.zeros(ybuf.shape[1:], ybuf.dtype)
        out_copy(n_rows, 1).start()
        for ahead in range(depth - 1):
            def first(r, _):
                gather_copy(ahead, r, ahead).start()
                return 0
            lax.fori_loop(0, tm, first, 0, unroll=ROW_ISSUE_UNROLL)

    wg_bf[...] = wg_ref[...].astype(BF16)
    wu_bf[...] = wu_ref[...].astype(BF16)
    wd_bf[...] = wd_ref[...].astype(BF16)

    def item_in_slot(i, slot):
        _wait_token_gather(h_ref, xbufs[slot], gsem.at[slot])
        nslot = (slot + depth - 1) % depth
        for r in range(tm):
            gather_copy(i + depth - 1, r, nslot).start(priority=r % 2)
        xb = _load_token_tiles(xbufs[slot], tm).astype(BF16)
        g = _dot(xb, wg_bf[...])
        u = _dot(xb, wu_bf[...])
        hidden = ((g / (1.0 + jnp.exp(-g))) * u).astype(BF16)
        _store_token_tiles(ybuf.at[slot % 2], _dot(hidden, wd_bf[...]))
        out_copy(0, slot % 2).wait()
        out_copy(row_ref[i], slot % 2).start()

    def item(i, _):
        slot = lax.rem(i, depth)
        for s in range(depth):
            @pl.when(slot == s)
            def _():
                item_in_slot(i, s)
        return 0

    lax.fori_loop(lo_ref[e], hi_ref[e], item, 0)

    @pl.when(e == pl.num_programs(0) - 1)
    def _():
        out_copy(0, 0).wait()
        for ahead in range(depth - 1):
            for s in range(depth):
                @pl.when(lax.rem(total + ahead, depth) == s)
                def _():
                    _wait_token_gather(h_ref, xbufs[s], gsem.at[s])


def _expert_ffn(h2_tiles, dest, tables, w_gate, w_up, w_down, tm):
    N = dest.shape[0]
    _, D, F = w_gate.shape
    assert D == SUBLANES * LANES
    item_row, item_lo, item_hi = tables
    wspec = lambda shape: pl.BlockSpec((None,) + shape, lambda e, *_: (e, 0, 0))
    return pl.pallas_call(
        functools.partial(_expert_kernel, tm=tm),
        grid_spec=pltpu.PrefetchScalarGridSpec(
            num_scalar_prefetch=4,
            grid=(N_EXPERTS,),
            in_specs=[pl.BlockSpec(memory_space=pl.ANY), wspec((D, F)), wspec((D, F)), wspec((F, D))],
            out_specs=pl.BlockSpec(memory_space=pl.ANY),
            scratch_shapes=[pltpu.VMEM((tm * SUBLANES, LANES), F32) for _ in range(EXPERT_GATHER_DEPTH)]
                           + [pltpu.VMEM((2, tm * SUBLANES, LANES), F32),
                            pltpu.SMEM((N + tm,), jnp.int32),
                            pltpu.VMEM((D, F), BF16), pltpu.VMEM((D, F), BF16), pltpu.VMEM((F, D), BF16),
                            pltpu.SemaphoreType.DMA((EXPERT_GATHER_DEPTH,)), pltpu.SemaphoreType.DMA(())],
        ),
        out_shape=jax.ShapeDtypeStruct(((N + tm) * SUBLANES, LANES), F32),
        compiler_params=_params("arbitrary"),
        name="expert_ffn",
    )(item_row, item_lo, item_hi, dest, h2_tiles, w_gate, w_up, w_down)


COMBINE_RING = 4
COMBINE_AHEAD = 2


def _combine_kernel(dest_ref, x1_ref, rf_ref, g_ref, y_ref, o_ref, *scratch, tb):
    bufs, sem = scratch[:COMBINE_RING], scratch[COMBINE_RING]
    i = pl.program_id(0)
    n_steps = pl.num_programs(0)
    n_tok = n_steps * COMBINE_RING * tb

    def copies(j, slot, r):
        return [pltpu.make_async_copy(_token_tile(y_ref, dest_ref[k * n_tok + j * tb + r]),
                                      _token_tile(bufs[slot].at[k], r), sem.at[slot])
                for k in range(TOP_K)]

    @pl.when(i == 0)
    def _():
        for slot in range(COMBINE_AHEAD):
            def first(r, _):
                for c in copies(slot, slot, r):
                    c.start()
                return 0
            lax.fori_loop(0, tb, first, 0, unroll=ROW_ISSUE_UNROLL // TOP_K)

    def sub_block(s, fetch):
        j = i * COMBINE_RING + s
        for k in range(TOP_K):
            _wait_token_gather(y_ref, bufs[s].at[k], sem.at[s])
        if fetch:
            for r in range(tb):
                for k, c in enumerate(copies(j + COMBINE_AHEAD, (s + COMBINE_AHEAD) % COMBINE_RING, r)):
                    c.start(priority=k)
        rows = slice(s * tb, (s + 1) * tb)
        rf = rf_ref[rows, :]
        x = (x1_ref[rows, :] + rf[:, 0:1] * _load_token_tiles(bufs[s].at[0], tb)
             + rf[:, 1:2] * _load_token_tiles(bufs[s].at[1], tb))
        inv = lax.rsqrt(jnp.mean(x * x, axis=-1, keepdims=True) + RMS_EPS)
        o_ref[rows, :] = (x * inv) * g_ref[...]

    for s in range(COMBINE_RING):
        if s + COMBINE_AHEAD < COMBINE_RING:
            sub_block(s, True)
        else:
            @pl.when(i + 1 < n_steps)
            def _():
                sub_block(s, True)

            @pl.when(i + 1 >= n_steps)
            def _():
                sub_block(s, False)


def _combine(x1, route_f, y_tiles, dest, ln_g, tb):
    T, D = x1.shape
    g2 = ln_g.reshape(1, D)
    step = COMBINE_RING * tb
    return pl.pallas_call(
        functools.partial(_combine_kernel, tb=tb),
        grid_spec=pltpu.PrefetchScalarGridSpec(
            num_scalar_prefetch=1,
            grid=(T // step,),
            in_specs=[pl.BlockSpec((step, D), lambda i, d: (i, 0)),
                      pl.BlockSpec((step, LANES), lambda i, d: (i, 0)),
                      pl.BlockSpec((1, D), lambda i, d: (0, 0)),
                      pl.BlockSpec(memory_space=pl.ANY)],
            out_specs=pl.BlockSpec((step, D), lambda i, d: (i, 0)),
            scratch_shapes=[pltpu.VMEM((TOP_K, tb * SUBLANES, LANES), F32) for _ in range(COMBINE_RING)]
                           + [pltpu.SemaphoreType.DMA((COMBINE_RING,))],
        ),
        out_shape=jax.ShapeDtypeStruct((T, D), F32),
        compiler_params=_params("arbitrary"),
        name="combine",
    )(dest, x1, route_f, g2, y_tiles)


def _layer(x2, B, S, ln1_g, w_in, w2, gb, ng, w_out, ln2_g, w_group, b_group, w_expert, b_expert,
           w_gate, w_up, w_down, ln_f_g, *, tm, tm_out, tq, tb_gla, tb_row, tm_e):
    q, k, v, gq, gk, gv, gr, lr = _in_proj(x2, ln1_g, w_in, tm)
    o_sb, o_gla = _mixers(q, k, v, gq, gk, gv, gr, lr, w2, gb, ng, B, S, tb_gla, tq)
    x1, h2_tiles, route_i, route_f, counts = _out_proj(x2, o_sb, o_gla, w_out, ln2_g, w_group, b_group,
                                                       w_expert, b_expert, tm_out)
    T = x2.shape[0]
    dest, *tables = _route_tables(route_i, counts, TOP_K * T, tm_e)
    y_tiles = _expert_ffn(h2_tiles, dest, tables, w_gate, w_up, w_down, tm_e)
    return _combine(x1, route_f, y_tiles, dest, ln_f_g, tb_row)


def kernel(x, ln1_g, w_in, gla_gate_w2, gla_gate_b, gla_norm_g, w_out, ln2_g, w_group, b_group, w_expert, b_expert, exp_w_gate, exp_w_up, exp_w_down, ln_f_g):
    B, S, D = x.shape
    assert ln1_g.shape[0] == 1, "single-layer stack"
    out = _layer(x.reshape(B * S, D), B, S, ln1_g[0], w_in[0], gla_gate_w2[0], gla_gate_b[0],
                 gla_norm_g[0], w_out[0], ln2_g[0], w_group[0], b_group[0], w_expert[0], b_expert[0],
                 exp_w_gate[0], exp_w_up[0], exp_w_down[0], ln_f_g,
                 tm=512, tm_out=1024, tq=256, tb_gla=512, tb_row=256, tm_e=256)
    return out.reshape(B, S, D)
```

```python
import functools
from typing import NamedTuple

import numpy as np
import jax
import jax.numpy as jnp
from jax import lax
from jax.experimental import pallas as pl
from jax.experimental.pallas import tpu as pltpu

SB_HEADS = 8
SB_HEAD_DIM = 64
SB_WIDTH = SB_HEADS * SB_HEAD_DIM
GLA_HEADS = 4
GLA_DK = 64
GLA_DV = 128
GLA_KW = GLA_HEADS * GLA_DK
GLA_VW = GLA_HEADS * GLA_DV
GLA_GATE_RANK = 16
GLA_GATE_TAU = 16.0
N_GROUPS = 4
EXPERTS_PER_GROUP = 8
N_EXPERTS = N_GROUPS * EXPERTS_PER_GROUP
TOP_K = 2
RMS_EPS = 1e-6
LOG2_E = 1.4426950408889634

LANES = 128
VMEM_LIMIT_BYTES = 56 * 1024 * 1024

F32 = jnp.float32
BF16 = jnp.bfloat16


def _dot(a, b):
    return jnp.dot(a, b, preferred_element_type=F32)


def _dot_nt(a, b):
    return lax.dot_general(a, b, (((1,), (1,)), ((), ())), preferred_element_type=F32)


def _softplus(z):
    return jnp.maximum(z, 0.0) + jnp.log(1.0 + jnp.exp(-jnp.abs(z)))


def _params(*sem):
    return pltpu.CompilerParams(dimension_semantics=sem, vmem_limit_bytes=VMEM_LIMIT_BYTES)


def _in_proj_kernel(x_ref, g_ref, wq, wk, wv, wgq, wgk, wgv, wgr, wlr,
                    oq, ok, ov, ogq, ogk, ogv, ogr, olr):
    x = x_ref[...]
    inv = lax.rsqrt(jnp.mean(x * x, axis=-1, keepdims=True) + RMS_EPS)
    h = ((x * inv) * g_ref[...]).astype(BF16)
    oq[...] = (_dot(h, wq[...]) * (SB_HEAD_DIM ** -0.5 * LOG2_E)).astype(oq.dtype)
    ok[...] = _dot(h, wk[...]).astype(ok.dtype)
    ov[...] = _dot(h, wv[...]).astype(ov.dtype)
    ogq[...] = (_dot(h, wgq[...]) * (GLA_DK ** -0.5)).astype(ogq.dtype)
    ogk[...] = _dot(h, wgk[...]).astype(ogk.dtype)
    ogv[...] = _dot(h, wgv[...]).astype(ogv.dtype)
    ogr[...] = _dot(h, wgr[...]).astype(ogr.dtype)
    olr[...] = _dot(h, wlr[...]).astype(olr.dtype)


def _in_proj(x2, ln_g, w_in, tm):
    T, D = x2.shape
    sizes = (SB_WIDTH, SB_WIDTH, SB_WIDTH, GLA_KW, GLA_KW, GLA_VW, GLA_VW, GLA_GATE_RANK)
    offs = np.concatenate([[0], np.cumsum(sizes)])
    ws = [w_in[:, int(offs[i]):int(offs[i + 1])].astype(BF16) for i in range(len(sizes))]
    out_dtypes = (BF16, BF16, BF16, F32, F32, BF16, F32, F32)
    row = lambda n: pl.BlockSpec((tm, n), lambda i: (i, 0))
    full = lambda n: pl.BlockSpec((D, n), lambda i: (0, 0))
    return pl.pallas_call(
        _in_proj_kernel,
        grid=(T // tm,),
        in_specs=[row(D), pl.BlockSpec((1, D), lambda i: (0, 0))] + [full(n) for n in sizes],
        out_specs=[row(n) for n in sizes],
        out_shape=[jax.ShapeDtypeStruct((T, n), dt) for n, dt in zip(sizes, out_dtypes)],
        compiler_params=_params("parallel"),
        name="in_proj",
    )(x2, ln_g.reshape(1, D), *ws)


SB_ZERO_WEIGHT_BITS = 160.0


def _softplus2(z):
    return jnp.maximum(z, 0.0) + jnp.log2(1.0 + jnp.exp2(-jnp.abs(z)))


def _sb_query_block(q_ref, k_ref, v_ref, o_ref, qh_ref, acc_ref, carry_ref, *, row0, qi, tq, n_pairs,
                    alongside=None):
    qrows = slice(row0, row0 + tq)
    r = lax.broadcasted_iota(jnp.int32, (tq, tq), 0)
    c = lax.broadcasted_iota(jnp.int32, (tq, tq), 1)
    causal = c < r
    later = (r > c).astype(BF16)
    head0 = lax.broadcasted_iota(jnp.int32, (1, LANES), 1) < SB_HEAD_DIM

    for p in range(n_pairs):
        q2 = q_ref[qrows, p * LANES:(p + 1) * LANES]
        zero = jnp.zeros_like(q2)
        qh_ref[p, 0:tq] = jnp.where(head0, q2, zero)
        qh_ref[p, tq:2 * tq] = jnp.where(head0, zero, q2)
    acc_ref[...] = jnp.zeros_like(acc_ref)
    carry_ref[...] = jnp.zeros_like(carry_ref)
    causal2 = jnp.concatenate([causal, causal], axis=0)

    def key_block(kb, diag):
        rows = pl.ds(pl.multiple_of(kb * tq, tq), tq)
        for p in range(n_pairs):
            cols = pl.ds(p * LANES, LANES)
            k2 = k_ref[rows, cols]
            v2 = v_ref[rows, cols]
            vzero = jnp.zeros_like(v2)
            v_stack = jnp.concatenate([jnp.where(head0, v2, vzero), jnp.where(head0, vzero, v2)], axis=0)
            z = _dot_nt(qh_ref[p], k2)
            sp = _softplus2(z)
            if diag:
                sp = jnp.where(causal2, sp, 0.0)
            run = _dot(sp.astype(BF16), later)
            carry = carry_ref[p]
            w = jnp.exp2((z - sp) - run - jnp.concatenate([carry] * (tq // LANES), axis=1))
            if diag:
                w = jnp.where(causal2, w, 0.0)
            w = w.astype(BF16)
            acc_ref[p] += _dot(jnp.concatenate([w[0:tq], w[tq:2 * tq]], axis=1), v_stack)
            carry_ref[p] = carry + jnp.broadcast_to(run[:, 0:1] + sp[:, 0:1], (2 * tq, LANES))

    def min_carry():
        m = carry_ref[0]
        for p in range(1, n_pairs):
            m = jnp.minimum(m, carry_ref[p])
        return jnp.min(m)

    def cond(state):
        kb, m = state
        return jnp.logical_and(kb >= 0, m < SB_ZERO_WEIGHT_BITS)

    def body(state):
        kb, _ = state
        key_block(kb, False)
        return kb - 1, min_carry()

    if alongside is not None:
        alongside()
    key_block(qi, True)
    lax.while_loop(cond, body, (qi - 1, min_carry()))
    for p in range(n_pairs):
        o_ref[qrows, p * LANES:(p + 1) * LANES] = acc_ref[p].astype(o_ref.dtype)


GLA_CHUNK = 64
GLA_LEVELS = (32, 16, 8, 4, 2, 1)


def _gla_tables(C=GLA_CHUNK):
    t = np.arange(C)[:, None]
    j = np.arange(C)[None, :]
    blocks = [(j <= t), (j > t)]
    masks = []
    for half in GLA_LEVELS:
        blk = 2 * half
        u = (t // blk) * blk + half - 1
        second = (t % blk) >= half
        blocks.append((second & (j > u) & (j <= t)) | ((~second) & (j > t) & (j <= u)))
        s = j
        masks.append(second & ((s % blk) < half) & ((s // blk) == (t // blk)))
    blocks.append(np.ones((8, C), bool))
    table = np.concatenate(blocks, axis=0).astype(np.float32)
    table = np.concatenate([table, table], axis=1)
    return table, np.concatenate(masks, axis=0).astype(np.float32)


def _split_bf16(a):
    hi = a.astype(BF16)
    lo = (a - hi.astype(F32)).astype(BF16)
    return hi, lo


def _gla_head_pair(p, gq_ref, gk_ref, gv_ref, gr_ref, lr_ref, w2_ref, gb_ref, ng_ref, tab_ref, msk_ref,
                   o_ref, state_ref, *, n_chunks):
    C = GLA_CHUNK
    n_lv = len(GLA_LEVELS)

    head0 = lax.broadcasted_iota(jnp.int32, (1, LANES), 1) < GLA_DK
    eye = (lax.broadcasted_iota(jnp.int32, (C, C), 0) == lax.broadcasted_iota(jnp.int32, (C, C), 1))
    w2_hi, w2_lo = _split_bf16(w2_ref[...])
    lr_hi, lr_lo = _split_bf16(lr_ref[...])

    def stack_heads(a):
        return jnp.concatenate([jnp.where(head0, a, 0.0), jnp.where(head0, 0.0, a)], axis=0)

    cols = slice(p * LANES, (p + 1) * LANES)
    pre = (_dot(lr_hi, w2_hi[:, cols]) + _dot(lr_hi, w2_lo[:, cols])
           + _dot(lr_lo, w2_hi[:, cols])) + gb_ref[:, cols]
    log_a = -_softplus(-pre) * (1.0 / GLA_GATE_TAU)
    la_hi, la_lo = _split_bf16(log_a)
    la = jnp.concatenate(
        [jnp.concatenate([la_hi[c * C:(c + 1) * C], la_lo[c * C:(c + 1) * C]], axis=0)
         for c in range(n_chunks)], axis=1)
    ex_all = jnp.exp(_dot(tab_ref[...], la))
    st = state_ref[p]
    for c in range(n_chunks):
        rows = slice(c * C, (c + 1) * C)
        ex = ex_all[:, c * LANES:(c + 1) * LANES]
        q2 = gq_ref[rows, cols]
        k2 = gk_ref[rows, cols]
        q_int = q2 * ex[0:C]
        k_dec = (k2 * ex[C:2 * C]).astype(BF16)
        dec_row = ex[(2 + n_lv) * C:(2 + n_lv) * C + 1]
        diag = jnp.sum(stack_heads(q2 * k2), axis=1, keepdims=True)
        scores = jnp.where(jnp.concatenate([eye, eye], axis=0), diag, 0.0)
        for l in range(n_lv):
            ex_l = ex[(2 + l) * C:(3 + l) * C]
            q_l = stack_heads(q2 * ex_l).astype(BF16)
            k_l = (k2 * ex_l).astype(BF16)
            m_l = msk_ref[l * C:(l + 1) * C, :] > 0.5
            scores = scores + jnp.where(jnp.concatenate([m_l, m_l], axis=0), _dot_nt(q_l, k_l), 0.0)
        scores = scores.astype(BF16)
        o_inter = _dot_nt(stack_heads(q_int).astype(BF16), st.astype(BF16))
        upd = []
        for h in range(2):
            vcols = slice((2 * p + h) * GLA_DV, (2 * p + h + 1) * GLA_DV)
            v_h = gv_ref[rows, vcols]
            o = o_inter[h * C:(h + 1) * C] + _dot(scores[h * C:(h + 1) * C], v_h)
            inv = lax.rsqrt(jnp.mean(o * o, axis=-1, keepdims=True) + RMS_EPS)
            y = (o * inv) * ng_ref[:, vcols]
            g = gr_ref[rows, vcols]
            y = y * (g / (1.0 + jnp.exp(-g)))
            o_ref[rows, vcols] = y.astype(o_ref.dtype)
            v_t = v_h.astype(F32).T.astype(BF16)
            upd.append(_dot(v_t, k_dec))
        st = st * dec_row + jnp.where(head0, upd[0], upd[1])
    state_ref[p] = st


def _mixers_kernel(q_ref, k_ref, v_ref, gq_ref, gk_ref, gv_ref, gr_ref, lr_ref, w2_ref, gb_ref, ng_ref,
                   tab_ref, msk_ref, osb_ref, ogla_ref, qh_ref, acc_ref, carry_ref, state_ref,
                   *, tq, n_pairs, n_chunks):
    blk = pl.program_id(1)
    n_q = q_ref.shape[0] // tq
    assert n_q == GLA_HEADS // 2

    @pl.when(blk == 0)
    def _():
        state_ref[...] = jnp.zeros_like(state_ref)

    for half in range(n_q):
        gla = functools.partial(_gla_head_pair, half, gq_ref, gk_ref, gv_ref, gr_ref, lr_ref, w2_ref, gb_ref,
                                ng_ref, tab_ref, msk_ref, ogla_ref, state_ref, n_chunks=n_chunks)
        _sb_query_block(q_ref, k_ref, v_ref, osb_ref, qh_ref, acc_ref, carry_ref, row0=half * tq,
                        qi=blk * n_q + half, tq=tq, n_pairs=n_pairs, alongside=gla)


def _mixers(q, k, v, gq, gk, gv, gr, lr, w2, gb, ng, B, S, tb, tq):
    table, masks = _gla_tables()
    table = jnp.asarray(table, BF16)
    masks = jnp.asarray(masks, F32)
    T = B * S
    W = q.shape[-1]
    n_pairs = W // LANES
    nblk = S // tb
    q3, k3, v3 = (a.reshape(B, S, W) for a in (q, k, v))
    blk3 = pl.BlockSpec((None, tb, W), lambda b, i: (b, i, 0))
    seq3 = pl.BlockSpec((None, S, W), lambda b, i: (b, 0, 0))
    row = lambda n: pl.BlockSpec((tb, n), lambda b, i: (b * nblk + i, 0))
    const = lambda a: pl.BlockSpec(a.shape, lambda b, i: (0,) * a.ndim)
    w2 = w2.astype(F32)
    gb2 = gb.reshape(1, GLA_KW)
    ng2 = ng.reshape(1, GLA_VW)
    o_sb, o_gla = pl.pallas_call(
        functools.partial(_mixers_kernel, tq=tq, n_pairs=n_pairs, n_chunks=tb // GLA_CHUNK),
        grid=(B, nblk),
        in_specs=[blk3, seq3, seq3, row(GLA_KW), row(GLA_KW), row(GLA_VW), row(GLA_VW), row(GLA_GATE_RANK),
                  const(w2), const(gb2), const(ng2), const(table), const(masks)],
        out_specs=[blk3, row(GLA_VW)],
        out_shape=[jax.ShapeDtypeStruct((B, S, W), BF16), jax.ShapeDtypeStruct((T, GLA_VW), BF16)],
        scratch_shapes=[pltpu.VMEM((n_pairs, 2 * tq, LANES), BF16),
                        pltpu.VMEM((n_pairs, tq, LANES), F32),
                        pltpu.VMEM((n_pairs, 2 * tq, LANES), F32),
                        pltpu.VMEM((GLA_HEADS // 2, GLA_DV, 2 * GLA_DK), F32)],
        compiler_params=_params("parallel", "arbitrary"),
        name="mixers",
    )(q3, k3, v3, gq, gk, gv, gr, lr, w2, gb2, ng2, table, masks)
    return o_sb.reshape(T, W), o_gla


SUBLANES = 8


def _store_token_tiles(ref, value, accumulate=False):
    n = value.shape[0]
    for s in range(SUBLANES):
        rows = pl.ds(s, n, stride=SUBLANES)
        chunk = value[:, s * LANES:(s + 1) * LANES]
        ref[rows, :] = ref[rows, :] + chunk if accumulate else chunk


def _load_token_tiles(ref, n):
    return jnp.concatenate([ref[pl.ds(s, n, stride=SUBLANES), :] for s in range(SUBLANES)], axis=1)


def _out_proj_kernel(x_ref, oa_ref, ob_ref, wa_ref, wb_ref, g_ref, wr_ref, br_ref, tri_ref,
                     x1_ref, h2_ref, ri_ref, rf_ref, cnt_ref):
    @pl.when(pl.program_id(0) == 0)
    def _():
        cnt_ref[...] = jnp.zeros_like(cnt_ref)

    x1 = x_ref[...] + _dot(oa_ref[...], wa_ref[...]) + _dot(ob_ref[...], wb_ref[...])
    x1_ref[...] = x1
    inv = lax.rsqrt(jnp.mean(x1 * x1, axis=-1, keepdims=True) + RMS_EPS)
    h2 = (x1 * inv) * g_ref[...]
    _store_token_tiles(h2_ref, h2)

    h_hi, h_lo = _split_bf16(h2)
    w_hi, w_lo = _split_bf16(wr_ref[...])
    hw = _dot(h_hi, jnp.concatenate([w_hi, w_lo], axis=1))
    logits = hw[:, :LANES] + hw[:, LANES:] + _dot(h_lo, w_hi) + br_ref[...]
    lane = lax.broadcasted_iota(jnp.int32, logits.shape, 1).astype(F32)
    neg = -jnp.inf

    def first_max(vals):
        m = jnp.max(vals, axis=1, keepdims=True)
        idx = jnp.min(jnp.where(vals == m, lane, float(LANES)), axis=1, keepdims=True)
        return m, idx

    is_group = lane < N_GROUPS
    g_max, g_idx = first_max(jnp.where(is_group, logits, neg))
    g_p = 1.0 / jnp.sum(jnp.where(is_group, jnp.exp(logits - g_max), 0.0), axis=1, keepdims=True)
    lo_lane = N_GROUPS + EXPERTS_PER_GROUP * g_idx
    sel = jnp.where((lane >= lo_lane) & (lane < lo_lane + EXPERTS_PER_GROUP), logits, neg)
    m1, i1 = first_max(sel)
    m2, i2 = first_max(jnp.where(lane == i1, neg, sel))
    e = jnp.exp(m2 - m1)
    gate1 = g_p / (1.0 + e)
    gate2 = g_p * e / (1.0 + e)
    rf_ref[...] = jnp.where(lane == 0, gate1, jnp.where(lane == 1, gate2, 0.0))

    e1 = i1 - N_GROUPS
    e2 = i2 - N_GROUPS
    hits = (lane == e1).astype(F32) + (lane == e2).astype(F32)
    before = _dot(tri_ref[...], hits.astype(BF16)) + cnt_ref[...]
    r1 = jnp.sum(jnp.where(lane == e1, before, 0.0), axis=1, keepdims=True)
    r2 = jnp.sum(jnp.where(lane == e2, before, 0.0), axis=1, keepdims=True)
    cnt_ref[...] += jnp.sum(hits, axis=0, keepdims=True)
    packed = jnp.where(lane == 0, e1, jnp.where(lane == 1, e2,
                       jnp.where(lane == 2, r1, jnp.where(lane == 3, r2, 0.0))))
    for j in range(packed.shape[0] // LANES):
        rows_t = packed[j * LANES:(j + 1) * LANES, :].T
        for k in range(2 * TOP_K):
            ri_ref[k, j:j + 1, :] = rows_t[k:k + 1, :].astype(jnp.int32)


def _out_proj(x2, o_sb, o_gla, w_out, ln_g, w_group, b_group, w_expert, b_expert, tm):
    T, D = x2.shape
    wa = w_out[:SB_WIDTH].astype(BF16)
    wb = w_out[SB_WIDTH:].astype(BF16)
    n_r = N_GROUPS + N_EXPERTS
    wr = jnp.zeros((D, LANES), F32).at[:, :n_r].set(jnp.concatenate([w_group, w_expert], axis=1))
    br = jnp.zeros((1, LANES), F32).at[0, :n_r].set(jnp.concatenate([b_group, b_expert]))
    row = lambda n: pl.BlockSpec((tm, n), lambda i: (i, 0))
    const = lambda a: pl.BlockSpec(a.shape, lambda i: (0,) * a.ndim)
    g2 = ln_g.reshape(1, D)
    tri = jnp.asarray(np.tril(np.ones((tm, tm), np.float32), -1), BF16)
    return pl.pallas_call(
        _out_proj_kernel,
        grid=(T // tm,),
        in_specs=[row(D), row(SB_WIDTH), row(GLA_VW), const(wa), const(wb), const(g2), const(wr), const(br),
                  const(tri)],
        out_specs=[row(D), pl.BlockSpec((tm * SUBLANES, LANES), lambda i: (i, 0)),
                   pl.BlockSpec((2 * TOP_K, tm // LANES, LANES), lambda i: (0, i, 0)), row(LANES),
                   pl.BlockSpec((1, LANES), lambda i: (0, 0))],
        out_shape=[jax.ShapeDtypeStruct((T, D), F32), jax.ShapeDtypeStruct((T * SUBLANES, LANES), F32),
                   jax.ShapeDtypeStruct((2 * TOP_K, T // LANES, LANES), jnp.int32),
                   jax.ShapeDtypeStruct((T, LANES), F32), jax.ShapeDtypeStruct((1, LANES), F32)],
        compiler_params=_params("arbitrary"),
        name="out_proj",
    )(x2, o_sb, o_gla, wa, wb, g2, wr, br, tri)


def _route_tables(route, counts, n_rows, tm):
    counts = counts[0, :N_EXPERTS].astype(jnp.int32)
    ends = jnp.cumsum(counts)
    starts = ends - counts
    route = route.reshape(2 * TOP_K, -1)
    expert = route[:TOP_K, :, None] == jnp.arange(N_EXPERTS, dtype=jnp.int32)
    dest = (jnp.sum(jnp.where(expert, starts, 0), axis=-1) + route[TOP_K:]).reshape(-1)

    n_items = n_rows // tm + N_EXPERTS + EXPERT_GATHER_DEPTH
    items_e = (counts + tm - 1) // tm
    item_hi = jnp.cumsum(items_e)
    item_lo = item_hi - items_e
    i = jnp.arange(n_items, dtype=jnp.int32)
    e_of = jnp.minimum(jnp.sum(item_hi[None, :] <= i[:, None], axis=1), N_EXPERTS - 1)
    owner = e_of[:, None] == jnp.arange(N_EXPERTS, dtype=jnp.int32)
    pick = lambda tab: jnp.sum(jnp.where(owner, tab, 0), axis=-1)
    item_row = jnp.where(i < item_hi[-1], pick(starts) + (i - pick(item_lo)) * tm, n_rows)
    return (dest, item_row.astype(jnp.int32), item_lo.astype(jnp.int32), item_hi.astype(jnp.int32))


ROW_ISSUE_UNROLL = 8
EXPERT_GATHER_DEPTH = 4


def _token_tile(ref, t):
    return ref.at[pl.ds(pl.multiple_of(t * SUBLANES, SUBLANES), SUBLANES), :]


def _wait_token_gather(src_ref, dst_ref, sem):
    pltpu.make_async_copy(src_ref.at[pl.ds(0, dst_ref.shape[0]), :], dst_ref, sem).wait()


def _expert_kernel(row_ref, lo_ref, hi_ref, dest_ref,
                   h_ref, wg_ref, wu_ref, wd_ref, y_ref, *scratch, tm):
    depth = EXPERT_GATHER_DEPTH
    xbufs = scratch[:depth]
    ybuf, tok_ref, wg_bf, wu_bf, wd_bf, gsem, wsem = scratch[depth:]
    e = pl.program_id(0)
    n_rows = dest_ref.shape[0]
    n_tok = n_rows // TOP_K
    total = hi_ref[N_EXPERTS - 1]

    def gather_copy(i, r, slot):
        t = tok_ref[row_ref[i] + r]
        return pltpu.make_async_copy(_token_tile(h_ref, t), _token_tile(xbufs[slot], r), gsem.at[slot])

    def out_copy(row, slot):
        rows = pl.ds(pl.multiple_of(row * SUBLANES, SUBLANES), tm * SUBLANES)
        return pltpu.make_async_copy(ybuf.at[slot], y_ref.at[rows, :], wsem)

    @pl.when(e == 0)
    def _():
        def invert(t, _):
            for k in range(TOP_K):
                tok_ref[dest_ref[k * n_tok + t]] = t
            return 0
        lax.fori_loop(0, n_tok, invert, 0, unroll=ROW_ISSUE_UNROLL)

        def spare(r, _):
            tok_ref[n_rows + r] = 0
            return 0
        lax.fori_loop(0, tm, spare, 0, unroll=ROW_ISSUE_UNROLL)
        ybuf[1] = jnp.zeros(ybuf.shape[1:], ybuf.dtype)
        out_copy(n_rows, 1).start()
        for ahead in range(depth - 1):
            def first(r, _):
                gather_copy(ahead, r, ahead).start()
                return 0
            lax.fori_loop(0, tm, first, 0, unroll=ROW_ISSUE_UNROLL)

    wg_bf[...] = wg_ref[...].astype(BF16)
    wu_bf[...] = wu_ref[...].astype(BF16)
    wd_bf[...] = wd_ref[...].astype(BF16)

    def item_in_slot(i, slot):
        _wait_token_gather(h_ref, xbufs[slot], gsem.at[slot])
        nslot = (slot + depth - 1) % depth
        for r in range(tm):
            gather_copy(i + depth - 1, r, nslot).start(priority=r % 2)
        xb = _load_token_tiles(xbufs[slot], tm).astype(BF16)
        g = _dot(xb, wg_bf[...])
        u = _dot(xb, wu_bf[...])
        hidden = ((g / (1.0 + jnp.exp(-g))) * u).astype(BF16)
        _store_token_tiles(ybuf.at[slot % 2], _dot(hidden, wd_bf[...]))
        out_copy(0, slot % 2).wait()
        out_copy(row_ref[i], slot % 2).start()

    def item(i, _):
        slot = lax.rem(i, depth)
        for s in range(depth):
            @pl.when(slot == s)
            def _():
                item_in_slot(i, s)
        return 0

    lax.fori_loop(lo_ref[e], hi_ref[e], item, 0)

    @pl.when(e == pl.num_programs(0) - 1)
    def _():
        out_copy(0, 0).wait()
        for ahead in range(depth - 1):
            for s in range(depth):
                @pl.when(lax.rem(total + ahead, depth) == s)
                def _():
                    _wait_token_gather(h_ref, xbufs[s], gsem.at[s])


def _expert_ffn(h2_tiles, dest, tables, w_gate, w_up, w_down, tm):
    N = dest.shape[0]
    _, D, F = w_gate.shape
    assert D == SUBLANES * LANES
    item_row, item_lo, item_hi = tables
    wspec = lambda shape: pl.BlockSpec((None,) + shape, lambda e, *_: (e, 0, 0))
    return pl.pallas_call(
        functools.partial(_expert_kernel, tm=tm),
        grid_spec=pltpu.PrefetchScalarGridSpec(
            num_scalar_prefetch=4,
            grid=(N_EXPERTS,),
            in_specs=[pl.BlockSpec(memory_space=pl.ANY), wspec((D, F)), wspec((D, F)), wspec((F, D))],
            out_specs=pl.BlockSpec(memory_space=pl.ANY),
            scratch_shapes=[pltpu.VMEM((tm * SUBLANES, LANES), F32) for _ in range(EXPERT_GATHER_DEPTH)]
                           + [pltpu.VMEM((2, tm * SUBLANES, LANES), F32),
                            pltpu.SMEM((N + tm,), jnp.int32),
                            pltpu.VMEM((D, F), BF16), pltpu.VMEM((D, F), BF16), pltpu.VMEM((F, D), BF16),
                            pltpu.SemaphoreType.DMA((EXPERT_GATHER_DEPTH,)), pltpu.SemaphoreType.DMA(())],
        ),
        out_shape=jax.ShapeDtypeStruct(((N + tm) * SUBLANES, LANES), F32),
        compiler_params=_params("arbitrary"),
        name="expert_ffn",
    )(item_row, item_lo, item_hi, dest, h2_tiles, w_gate, w_up, w_down)


COMBINE_RING = 4
COMBINE_AHEAD = 2


def _combine_kernel(dest_ref, x1_ref, rf_ref, g_ref, y_ref, o_ref, *scratch, tb):
    bufs, sem = scratch[:COMBINE_RING], scratch[COMBINE_RING]
    i = pl.program_id(0)
    n_steps = pl.num_programs(0)
    n_tok = n_steps * COMBINE_RING * tb

    def copies(j, slot, r):
        return [pltpu.make_async_copy(_token_tile(y_ref, dest_ref[k * n_tok + j * tb + r]),
                                      _token_tile(bufs[slot].at[k], r), sem.at[slot])
                for k in range(TOP_K)]

    @pl.when(i == 0)
    def _():
        for slot in range(COMBINE_AHEAD):
            def first(r, _):
                for c in copies(slot, slot, r):
                    c.start()
                return 0
            lax.fori_loop(0, tb, first, 0, unroll=ROW_ISSUE_UNROLL // TOP_K)

    def sub_block(s, fetch):
        j = i * COMBINE_RING + s
        for k in range(TOP_K):
            _wait_token_gather(y_ref, bufs[s].at[k], sem.at[s])
        if fetch:
            for r in range(tb):
                for k, c in enumerate(copies(j + COMBINE_AHEAD, (s + COMBINE_AHEAD) % COMBINE_RING, r)):
                    c.start(priority=k)
        rows = slice(s * tb, (s + 1) * tb)
        rf = rf_ref[rows, :]
        x = (x1_ref[rows, :] + rf[:, 0:1] * _load_token_tiles(bufs[s].at[0], tb)
             + rf[:, 1:2] * _load_token_tiles(bufs[s].at[1], tb))
        inv = lax.rsqrt(jnp.mean(x * x, axis=-1, keepdims=True) + RMS_EPS)
        o_ref[rows, :] = (x * inv) * g_ref[...]

    for s in range(COMBINE_RING):
        if s + COMBINE_AHEAD < COMBINE_RING:
            sub_block(s, True)
        else:
            @pl.when(i + 1 < n_steps)
            def _():
                sub_block(s, True)

            @pl.when(i + 1 >= n_steps)
            def _():
                sub_block(s, False)


def _combine(x1, route_f, y_tiles, dest, ln_g, tb):
    T, D = x1.shape
    g2 = ln_g.reshape(1, D)
    step = COMBINE_RING * tb
    return pl.pallas_call(
        functools.partial(_combine_kernel, tb=tb),
        grid_spec=pltpu.PrefetchScalarGridSpec(
            num_scalar_prefetch=1,
            grid=(T // step,),
            in_specs=[pl.BlockSpec((step, D), lambda i, d: (i, 0)),
                      pl.BlockSpec((step, LANES), lambda i, d: (i, 0)),
                      pl.BlockSpec((1, D), lambda i, d: (0, 0)),
                      pl.BlockSpec(memory_space=pl.ANY)],
            out_specs=pl.BlockSpec((step, D), lambda i, d: (i, 0)),
            scratch_shapes=[pltpu.VMEM((TOP_K, tb * SUBLANES, LANES), F32) for _ in range(COMBINE_RING)]
                           + [pltpu.SemaphoreType.DMA((COMBINE_RING,))],
        ),
        out_shape=jax.ShapeDtypeStruct((T, D), F32),
        compiler_params=_params("arbitrary"),
        name="combine",
    )(dest, x1, route_f, g2, y_tiles)


MXU_WIDTH = 256


class _Tiles(NamedTuple):
    in_proj: int
    query: int
    mixers: int
    out_proj: int
    expert: int
    combine: int


def _tiles(n_tokens, seq_len):
    query = MXU_WIDTH
    tiles = _Tiles(in_proj=4 * MXU_WIDTH, query=query, mixers=query * (GLA_HEADS // 2),
                   out_proj=SUBLANES * LANES, expert=MXU_WIDTH, combine=MXU_WIDTH)
    assert seq_len % tiles.mixers == 0 and tiles.mixers % GLA_CHUNK == 0
    assert n_tokens % tiles.in_proj == 0 and n_tokens % tiles.out_proj == 0
    assert n_tokens % (COMBINE_RING * tiles.combine) == 0 and (TOP_K * n_tokens) % tiles.expert == 0
    return tiles


def _layer(x2, B, S, ln1_g, w_in, w2, gb, ng, w_out, ln2_g, w_group, b_group, w_expert, b_expert,
           w_gate, w_up, w_down, ln_f_g, tiles):
    q, k, v, gq, gk, gv, gr, lr = _in_proj(x2, ln1_g, w_in, tiles.in_proj)
    o_sb, o_gla = _mixers(q, k, v, gq, gk, gv, gr, lr, w2, gb, ng, B, S, tiles.mixers, tiles.query)
    x1, h2_tiles, route_i, route_f, counts = _out_proj(x2, o_sb, o_gla, w_out, ln2_g, w_group, b_group,
                                                       w_expert, b_expert, tiles.out_proj)
    T = x2.shape[0]
    dest, *tables = _route_tables(route_i, counts, TOP_K * T, tiles.expert)
    y_tiles = _expert_ffn(h2_tiles, dest, tables, w_gate, w_up, w_down, tiles.expert)
    return _combine(x1, route_f, y_tiles, dest, ln_f_g, tiles.combine)


def kernel(x, ln1_g, w_in, gla_gate_w2, gla_gate_b, gla_norm_g, w_out, ln2_g, w_group, b_group, w_expert, b_expert, exp_w_gate, exp_w_up, exp_w_down, ln_f_g):
    B, S, D = x.shape
    assert ln1_g.shape[0] == 1, "single-layer stack"
    assert D == SUBLANES * LANES, "one (SUBLANES, LANES) tile per token"
    out = _layer(x.reshape(B * S, D), B, S, ln1_g[0], w_in[0], gla_gate_w2[0], gla_gate_b[0],
                 gla_norm_g[0], w_out[0], ln2_g[0], w_group[0], b_group[0], w_expert[0], b_expert[0],
                 exp_w_gate[0], exp_w_up[0], exp_w_down[0], ln_f_g, _tiles(B * S, S))
    return out.reshape(B, S, D)
```

```python
import functools
from typing import NamedTuple

import numpy as np
import jax
import jax.numpy as jnp
from jax import lax
from jax.experimental import pallas as pl
from jax.experimental.pallas import tpu as pltpu

SB_HEADS = 8
SB_HEAD_DIM = 64
SB_WIDTH = SB_HEADS * SB_HEAD_DIM
GLA_HEADS = 4
GLA_DK = 64
GLA_DV = 128
GLA_KW = GLA_HEADS * GLA_DK
GLA_VW = GLA_HEADS * GLA_DV
GLA_GATE_RANK = 16
GLA_GATE_TAU = 16.0
N_GROUPS = 4
EXPERTS_PER_GROUP = 8
N_EXPERTS = N_GROUPS * EXPERTS_PER_GROUP
TOP_K = 2
RMS_EPS = 1e-6
LOG2_E = 1.4426950408889634

LANES = 128
VMEM_LIMIT_BYTES = 56 * 1024 * 1024

F32 = jnp.float32
BF16 = jnp.bfloat16


def _dot(a, b):
    return jnp.dot(a, b, preferred_element_type=F32)


def _dot_nt(a, b):
    return lax.dot_general(a, b, (((1,), (1,)), ((), ())), preferred_element_type=F32)


def _softplus(z):
    return jnp.maximum(z, 0.0) + jnp.log(1.0 + jnp.exp(-jnp.abs(z)))


def _params(*sem):
    return pltpu.CompilerParams(dimension_semantics=sem, vmem_limit_bytes=VMEM_LIMIT_BYTES)


def _in_proj_kernel(x_ref, g_ref, wq, wk, wv, wgq, wgk, wgv, wgr, wlr,
                    oq, ok, ov, ogq, ogk, ogv, ogr, olr):
    x = x_ref[...]
    inv = lax.rsqrt(jnp.mean(x * x, axis=-1, keepdims=True) + RMS_EPS)
    h = ((x * inv) * g_ref[...]).astype(BF16)
    oq[...] = (_dot(h, wq[...]) * (SB_HEAD_DIM ** -0.5 * LOG2_E)).astype(oq.dtype)
    ok[...] = _dot(h, wk[...]).astype(ok.dtype)
    ov[...] = _dot(h, wv[...]).astype(ov.dtype)
    ogq[...] = (_dot(h, wgq[...]) * (GLA_DK ** -0.5)).astype(ogq.dtype)
    ogk[...] = _dot(h, wgk[...]).astype(ogk.dtype)
    ogv[...] = _dot(h, wgv[...]).astype(ogv.dtype)
    ogr[...] = _dot(h, wgr[...]).astype(ogr.dtype)
    olr[...] = _dot(h, wlr[...]).astype(olr.dtype)


def _in_proj(x2, ln_g, w_in, tm):
    T, D = x2.shape
    sizes = (SB_WIDTH, SB_WIDTH, SB_WIDTH, GLA_KW, GLA_KW, GLA_VW, GLA_VW, GLA_GATE_RANK)
    offs = np.concatenate([[0], np.cumsum(sizes)])
    ws = [w_in[:, int(offs[i]):int(offs[i + 1])].astype(BF16) for i in range(len(sizes))]
    out_dtypes = (BF16, BF16, BF16, F32, F32, BF16, F32, F32)
    row = lambda n: pl.BlockSpec((tm, n), lambda i: (i, 0))
    full = lambda n: pl.BlockSpec((D, n), lambda i: (0, 0))
    return pl.pallas_call(
        _in_proj_kernel,
        grid=(T // tm,),
        in_specs=[row(D), pl.BlockSpec((1, D), lambda i: (0, 0))] + [full(n) for n in sizes],
        out_specs=[row(n) for n in sizes],
        out_shape=[jax.ShapeDtypeStruct((T, n), dt) for n, dt in zip(sizes, out_dtypes)],
        compiler_params=_params("parallel"),
        name="in_proj",
    )(x2, ln_g.reshape(1, D), *ws)


SB_ZERO_WEIGHT_BITS = 160.0


def _softplus2(z):
    return jnp.maximum(z, 0.0) + jnp.log2(1.0 + jnp.exp2(-jnp.abs(z)))


def _sb_query_block(q_ref, k_ref, v_ref, o_ref, qh_ref, acc_ref, carry_ref, *, row0, qi, tq, n_pairs,
                    alongside=None):
    qrows = slice(row0, row0 + tq)
    r = lax.broadcasted_iota(jnp.int32, (tq, tq), 0)
    c = lax.broadcasted_iota(jnp.int32, (tq, tq), 1)
    causal = c < r
    later = (r > c).astype(BF16)
    head0 = lax.broadcasted_iota(jnp.int32, (1, LANES), 1) < SB_HEAD_DIM

    for p in range(n_pairs):
        q2 = q_ref[qrows, p * LANES:(p + 1) * LANES]
        zero = jnp.zeros_like(q2)
        qh_ref[p, 0:tq] = jnp.where(head0, q2, zero)
        qh_ref[p, tq:2 * tq] = jnp.where(head0, zero, q2)
    acc_ref[...] = jnp.zeros_like(acc_ref)
    carry_ref[...] = jnp.zeros_like(carry_ref)
    causal2 = jnp.concatenate([causal, causal], axis=0)

    def key_block(kb, diag):
        rows = pl.ds(pl.multiple_of(kb * tq, tq), tq)
        pairs = range(n_pairs)
        zs = [_dot_nt(qh_ref[p], k_ref[rows, pl.ds(p * LANES, LANES)]) for p in pairs]

        def suffix_stage(p):
            sp = _softplus2(zs[p])
            if diag:
                sp = jnp.where(causal2, sp, 0.0)
            return sp, _dot(sp.astype(BF16), later)

        def weight_stage(p, sp, run):
            z = zs[p]
            v2 = v_ref[rows, pl.ds(p * LANES, LANES)]
            vzero = jnp.zeros_like(v2)
            v_stack = jnp.concatenate([jnp.where(head0, v2, vzero), jnp.where(head0, vzero, v2)], axis=0)
            carry = carry_ref[p]
            w = jnp.exp2((z - sp) - run - jnp.concatenate([carry] * (tq // LANES), axis=1))
            if diag:
                w = jnp.where(causal2, w, 0.0)
            w = w.astype(BF16)
            acc_ref[p] += _dot(jnp.concatenate([w[0:tq], w[tq:2 * tq]], axis=1), v_stack)
            carry_ref[p] = carry + jnp.broadcast_to(run[:, 0:1] + sp[:, 0:1], (2 * tq, LANES))

        pending = None
        for p in pairs:
            staged = suffix_stage(p)
            if pending is not None:
                weight_stage(*pending)
            pending = (p,) + staged
        weight_stage(*pending)

    def min_carry():
        m = carry_ref[0]
        for p in range(1, n_pairs):
            m = jnp.minimum(m, carry_ref[p])
        return jnp.min(m)

    def cond(state):
        kb, m = state
        return jnp.logical_and(kb >= 0, m < SB_ZERO_WEIGHT_BITS)

    def body(state):
        kb, _ = state
        key_block(kb, False)
        return kb - 1, min_carry()

    if alongside is not None:
        alongside()
    key_block(qi, True)
    lax.while_loop(cond, body, (qi - 1, min_carry()))
    for p in range(n_pairs):
        o_ref[qrows, p * LANES:(p + 1) * LANES] = acc_ref[p].astype(o_ref.dtype)


GLA_CHUNK = 64
GLA_LEVELS = (32, 16, 8, 4, 2, 1)


def _gla_tables(C=GLA_CHUNK):
    t = np.arange(C)[:, None]
    j = np.arange(C)[None, :]
    blocks = [(j <= t), (j > t)]
    masks = []
    for half in GLA_LEVELS:
        blk = 2 * half
        u = (t // blk) * blk + half - 1
        second = (t % blk) >= half
        blocks.append((second & (j > u) & (j <= t)) | ((~second) & (j > t) & (j <= u)))
        s = j
        masks.append(second & ((s % blk) < half) & ((s // blk) == (t // blk)))
    blocks.append(np.ones((8, C), bool))
    table = np.concatenate(blocks, axis=0).astype(np.float32)
    table = np.concatenate([table, table], axis=1)
    return table, np.concatenate(masks, axis=0).astype(np.float32)


def _split_bf16(a):
    hi = a.astype(BF16)
    lo = (a - hi.astype(F32)).astype(BF16)
    return hi, lo


def _gla_head_pair(p, gq_ref, gk_ref, gv_ref, gr_ref, lr_ref, w2_ref, gb_ref, ng_ref, tab_ref, msk_ref,
                   o_ref, state_ref, *, n_chunks):
    C = GLA_CHUNK
    n_lv = len(GLA_LEVELS)

    head0 = lax.broadcasted_iota(jnp.int32, (1, LANES), 1) < GLA_DK
    eye = (lax.broadcasted_iota(jnp.int32, (C, C), 0) == lax.broadcasted_iota(jnp.int32, (C, C), 1))
    w2_hi, w2_lo = _split_bf16(w2_ref[...])
    lr_hi, lr_lo = _split_bf16(lr_ref[...])

    def stack_heads(a):
        return jnp.concatenate([jnp.where(head0, a, 0.0), jnp.where(head0, 0.0, a)], axis=0)

    cols = slice(p * LANES, (p + 1) * LANES)
    pre = (_dot(lr_hi, w2_hi[:, cols]) + _dot(lr_hi, w2_lo[:, cols])
           + _dot(lr_lo, w2_hi[:, cols])) + gb_ref[:, cols]
    log_a = -_softplus(-pre) * (1.0 / GLA_GATE_TAU)
    la_hi, la_lo = _split_bf16(log_a)
    la = jnp.concatenate(
        [jnp.concatenate([la_hi[c * C:(c + 1) * C], la_lo[c * C:(c + 1) * C]], axis=0)
         for c in range(n_chunks)], axis=1)
    ex_all = jnp.exp(_dot(tab_ref[...], la))
    st = state_ref[p]
    for c in range(n_chunks):
        rows = slice(c * C, (c + 1) * C)
        ex = ex_all[:, c * LANES:(c + 1) * LANES]
        q2 = gq_ref[rows, cols]
        k2 = gk_ref[rows, cols]
        q_int = q2 * ex[0:C]
        k_dec = (k2 * ex[C:2 * C]).astype(BF16)
        dec_row = ex[(2 + n_lv) * C:(2 + n_lv) * C + 1]
        diag = jnp.sum(stack_heads(q2 * k2), axis=1, keepdims=True)
        scores = jnp.where(jnp.concatenate([eye, eye], axis=0), diag, 0.0)
        for l in range(n_lv):
            ex_l = ex[(2 + l) * C:(3 + l) * C]
            q_l = stack_heads(q2 * ex_l).astype(BF16)
            k_l = (k2 * ex_l).astype(BF16)
            m_l = msk_ref[l * C:(l + 1) * C, :] > 0.5
            scores = scores + jnp.where(jnp.concatenate([m_l, m_l], axis=0), _dot_nt(q_l, k_l), 0.0)
        scores = scores.astype(BF16)
        o_inter = _dot_nt(stack_heads(q_int).astype(BF16), st.astype(BF16))
        upd = []
        for h in range(2):
            vcols = slice((2 * p + h) * GLA_DV, (2 * p + h + 1) * GLA_DV)
            v_h = gv_ref[rows, vcols]
            o = o_inter[h * C:(h + 1) * C] + _dot(scores[h * C:(h + 1) * C], v_h)
            inv = lax.rsqrt(jnp.mean(o * o, axis=-1, keepdims=True) + RMS_EPS)
            y = (o * inv) * ng_ref[:, vcols]
            g = gr_ref[rows, vcols]
            y = y * (g / (1.0 + jnp.exp(-g)))
            o_ref[rows, vcols] = y.astype(o_ref.dtype)
            v_t = v_h.astype(F32).T.astype(BF16)
            upd.append(_dot(v_t, k_dec))
        st = st * dec_row + jnp.where(head0, upd[0], upd[1])
    state_ref[p] = st


def _mixers_kernel(q_ref, k_ref, v_ref, gq_ref, gk_ref, gv_ref, gr_ref, lr_ref, w2_ref, gb_ref, ng_ref,
                   tab_ref, msk_ref, osb_ref, ogla_ref, qh_ref, acc_ref, carry_ref, state_ref,
                   *, tq, n_pairs, n_chunks):
    blk = pl.program_id(1)
    n_q = q_ref.shape[0] // tq
    assert n_q == GLA_HEADS // 2

    @pl.when(blk == 0)
    def _():
        state_ref[...] = jnp.zeros_like(state_ref)

    for half in range(n_q):
        gla = functools.partial(_gla_head_pair, half, gq_ref, gk_ref, gv_ref, gr_ref, lr_ref, w2_ref, gb_ref,
                                ng_ref, tab_ref, msk_ref, ogla_ref, state_ref, n_chunks=n_chunks)
        _sb_query_block(q_ref, k_ref, v_ref, osb_ref, qh_ref, acc_ref, carry_ref, row0=half * tq,
                        qi=blk * n_q + half, tq=tq, n_pairs=n_pairs, alongside=gla)


def _mixers(q, k, v, gq, gk, gv, gr, lr, w2, gb, ng, B, S, tb, tq):
    table, masks = _gla_tables()
    table = jnp.asarray(table, BF16)
    masks = jnp.asarray(masks, F32)
    T = B * S
    W = q.shape[-1]
    n_pairs = W // LANES
    nblk = S // tb
    q3, k3, v3 = (a.reshape(B, S, W) for a in (q, k, v))
    blk3 = pl.BlockSpec((None, tb, W), lambda b, i: (b, i, 0))
    seq3 = pl.BlockSpec((None, S, W), lambda b, i: (b, 0, 0))
    row = lambda n: pl.BlockSpec((tb, n), lambda b, i: (b * nblk + i, 0))
    const = lambda a: pl.BlockSpec(a.shape, lambda b, i: (0,) * a.ndim)
    w2 = w2.astype(F32)
    gb2 = gb.reshape(1, GLA_KW)
    ng2 = ng.reshape(1, GLA_VW)
    o_sb, o_gla = pl.pallas_call(
        functools.partial(_mixers_kernel, tq=tq, n_pairs=n_pairs, n_chunks=tb // GLA_CHUNK),
        grid=(B, nblk),
        in_specs=[blk3, seq3, seq3, row(GLA_KW), row(GLA_KW), row(GLA_VW), row(GLA_VW), row(GLA_GATE_RANK),
                  const(w2), const(gb2), const(ng2), const(table), const(masks)],
        out_specs=[blk3, row(GLA_VW)],
        out_shape=[jax.ShapeDtypeStruct((B, S, W), BF16), jax.ShapeDtypeStruct((T, GLA_VW), BF16)],
        scratch_shapes=[pltpu.VMEM((n_pairs, 2 * tq, LANES), BF16),
                        pltpu.VMEM((n_pairs, tq, LANES), F32),
                        pltpu.VMEM((n_pairs, 2 * tq, LANES), F32),
                        pltpu.VMEM((GLA_HEADS // 2, GLA_DV, 2 * GLA_DK), F32)],
        compiler_params=_params("parallel", "arbitrary"),
        name="mixers",
    )(q3, k3, v3, gq, gk, gv, gr, lr, w2, gb2, ng2, table, masks)
    return o_sb.reshape(T, W), o_gla


SUBLANES = 8


def _store_token_tiles(ref, value, accumulate=False):
    n = value.shape[0]
    for s in range(SUBLANES):
        rows = pl.ds(s, n, stride=SUBLANES)
        chunk = value[:, s * LANES:(s + 1) * LANES]
        ref[rows, :] = ref[rows, :] + chunk if accumulate else chunk


def _load_token_tiles(ref, n):
    return jnp.concatenate([ref[pl.ds(s, n, stride=SUBLANES), :] for s in range(SUBLANES)], axis=1)


def _out_proj_kernel(x_ref, oa_ref, ob_ref, wa_ref, wb_ref, g_ref, wr_ref, br_ref, tri_ref,
                     x1_ref, h2_ref, ri_ref, rf_ref, cnt_ref):
    @pl.when(pl.program_id(0) == 0)
    def _():
        cnt_ref[...] = jnp.zeros_like(cnt_ref)

    x1 = x_ref[...] + _dot(oa_ref[...], wa_ref[...]) + _dot(ob_ref[...], wb_ref[...])
    x1_ref[...] = x1
    inv = lax.rsqrt(jnp.mean(x1 * x1, axis=-1, keepdims=True) + RMS_EPS)
    h2 = (x1 * inv) * g_ref[...]
    _store_token_tiles(h2_ref, h2)

    h_hi, h_lo = _split_bf16(h2)
    w_hi, w_lo = _split_bf16(wr_ref[...])
    hw = _dot(h_hi, jnp.concatenate([w_hi, w_lo], axis=1))
    logits = hw[:, :LANES] + hw[:, LANES:] + _dot(h_lo, w_hi) + br_ref[...]
    lane = lax.broadcasted_iota(jnp.int32, logits.shape, 1).astype(F32)
    neg = -jnp.inf

    def first_max(vals):
        m = jnp.max(vals, axis=1, keepdims=True)
        idx = jnp.min(jnp.where(vals == m, lane, float(LANES)), axis=1, keepdims=True)
        return m, idx

    is_group = lane < N_GROUPS
    g_max, g_idx = first_max(jnp.where(is_group, logits, neg))
    g_p = 1.0 / jnp.sum(jnp.where(is_group, jnp.exp(logits - g_max), 0.0), axis=1, keepdims=True)
    lo_lane = N_GROUPS + EXPERTS_PER_GROUP * g_idx
    sel = jnp.where((lane >= lo_lane) & (lane < lo_lane + EXPERTS_PER_GROUP), logits, neg)
    m1, i1 = first_max(sel)
    m2, i2 = first_max(jnp.where(lane == i1, neg, sel))
    e = jnp.exp(m2 - m1)
    gate1 = g_p / (1.0 + e)
    gate2 = g_p * e / (1.0 + e)
    rf_ref[...] = jnp.where(lane == 0, gate1, jnp.where(lane == 1, gate2, 0.0))

    e1 = i1 - N_GROUPS
    e2 = i2 - N_GROUPS
    hits = (lane == e1).astype(F32) + (lane == e2).astype(F32)
    before = _dot(tri_ref[...], hits.astype(BF16)) + cnt_ref[...]
    r1 = jnp.sum(jnp.where(lane == e1, before, 0.0), axis=1, keepdims=True)
    r2 = jnp.sum(jnp.where(lane == e2, before, 0.0), axis=1, keepdims=True)
    cnt_ref[...] += jnp.sum(hits, axis=0, keepdims=True)
    packed = jnp.where(lane == 0, e1, jnp.where(lane == 1, e2,
                       jnp.where(lane == 2, r1, jnp.where(lane == 3, r2, 0.0))))
    for j in range(packed.shape[0] // LANES):
        rows_t = packed[j * LANES:(j + 1) * LANES, :].T
        for k in range(2 * TOP_K):
            ri_ref[k, j:j + 1, :] = rows_t[k:k + 1, :].astype(jnp.int32)


def _out_proj(x2, o_sb, o_gla, w_out, ln_g, w_group, b_group, w_expert, b_expert, tm):
    T, D = x2.shape
    wa = w_out[:SB_WIDTH].astype(BF16)
    wb = w_out[SB_WIDTH:].astype(BF16)
    n_r = N_GROUPS + N_EXPERTS
    wr = jnp.zeros((D, LANES), F32).at[:, :n_r].set(jnp.concatenate([w_group, w_expert], axis=1))
    br = jnp.zeros((1, LANES), F32).at[0, :n_r].set(jnp.concatenate([b_group, b_expert]))
    row = lambda n: pl.BlockSpec((tm, n), lambda i: (i, 0))
    const = lambda a: pl.BlockSpec(a.shape, lambda i: (0,) * a.ndim)
    g2 = ln_g.reshape(1, D)
    tri = jnp.asarray(np.tril(np.ones((tm, tm), np.float32), -1), BF16)
    return pl.pallas_call(
        _out_proj_kernel,
        grid=(T // tm,),
        in_specs=[row(D), row(SB_WIDTH), row(GLA_VW), const(wa), const(wb), const(g2), const(wr), const(br),
                  const(tri)],
        out_specs=[row(D), pl.BlockSpec((tm * SUBLANES, LANES), lambda i: (i, 0)),
                   pl.BlockSpec((2 * TOP_K, tm // LANES, LANES), lambda i: (0, i, 0)), row(LANES),
                   pl.BlockSpec((1, LANES), lambda i: (0, 0))],
        out_shape=[jax.ShapeDtypeStruct((T, D), F32), jax.ShapeDtypeStruct((T * SUBLANES, LANES), F32),
                   jax.ShapeDtypeStruct((2 * TOP_K, T // LANES, LANES), jnp.int32),
                   jax.ShapeDtypeStruct((T, LANES), F32), jax.ShapeDtypeStruct((1, LANES), F32)],
        compiler_params=_params("arbitrary"),
        name="out_proj",
    )(x2, o_sb, o_gla, wa, wb, g2, wr, br, tri)


def _route_tables(route, counts, n_rows, tm):
    counts = counts[0, :N_EXPERTS].astype(jnp.int32)
    ends = jnp.cumsum(counts)
    starts = ends - counts
    route = route.reshape(2 * TOP_K, -1)
    expert = route[:TOP_K, :, None] == jnp.arange(N_EXPERTS, dtype=jnp.int32)
    dest = (jnp.sum(jnp.where(expert, starts, 0), axis=-1) + route[TOP_K:]).reshape(-1)

    n_items = n_rows // tm + N_EXPERTS + EXPERT_GATHER_DEPTH
    items_e = (counts + tm - 1) // tm
    item_hi = jnp.cumsum(items_e)
    item_lo = item_hi - items_e
    i = jnp.arange(n_items, dtype=jnp.int32)
    e_of = jnp.minimum(jnp.sum(item_hi[None, :] <= i[:, None], axis=1), N_EXPERTS - 1)
    owner = e_of[:, None] == jnp.arange(N_EXPERTS, dtype=jnp.int32)
    pick = lambda tab: jnp.sum(jnp.where(owner, tab, 0), axis=-1)
    item_row = jnp.where(i < item_hi[-1], pick(starts) + (i - pick(item_lo)) * tm, n_rows)
    return (dest, item_row.astype(jnp.int32), item_lo.astype(jnp.int32), item_hi.astype(jnp.int32))


ROW_ISSUE_UNROLL = 8
EXPERT_GATHER_DEPTH = 4


def _token_tile(ref, t):
    return ref.at[pl.ds(pl.multiple_of(t * SUBLANES, SUBLANES), SUBLANES), :]


def _wait_token_gather(src_ref, dst_ref, sem):
    pltpu.make_async_copy(src_ref.at[pl.ds(0, dst_ref.shape[0]), :], dst_ref, sem).wait()


def _expert_kernel(row_ref, lo_ref, hi_ref, dest_ref,
                   h_ref, wg_ref, wu_ref, wd_ref, y_ref, *scratch, tm):
    depth = EXPERT_GATHER_DEPTH
    xbufs = scratch[:depth]
    ybuf, tok_ref, wg_bf, wu_bf, wd_bf, gsem, wsem = scratch[depth:]
    e = pl.program_id(0)
    n_rows = dest_ref.shape[0]
    n_tok = n_rows // TOP_K
    total = hi_ref[N_EXPERTS - 1]

    def gather_copy(i, r, slot):
        t = tok_ref[row_ref[i] + r]
        return pltpu.make_async_copy(_token_tile(h_ref, t), _token_tile(xbufs[slot], r), gsem.at[slot])

    def out_copy(row, slot):
        rows = pl.ds(pl.multiple_of(row * SUBLANES, SUBLANES), tm * SUBLANES)
        return pltpu.make_async_copy(ybuf.at[slot], y_ref.at[rows, :], wsem)

    @pl.when(e == 0)
    def _():
        def invert(t, _):
            for k in range(TOP_K):
                tok_ref[dest_ref[k * n_tok + t]] = t
            return 0
        lax.fori_loop(0, n_tok, invert, 0, unroll=ROW_ISSUE_UNROLL)

        def spare(r, _):
            tok_ref[n_rows + r] = 0
            return 0
        lax.fori_loop(0, tm, spare, 0, unroll=ROW_ISSUE_UNROLL)
        ybuf[1] = jnp.zeros(ybuf.shape[1:], ybuf.dtype)
        out_copy(n_rows, 1).start()
        for ahead in range(depth - 1):
            def first(r, _):
                gather_copy(ahead, r, ahead).start()
                return 0
            lax.fori_loop(0, tm, first, 0, unroll=ROW_ISSUE_UNROLL)

    wg_bf[...] = wg_ref[...].astype(BF16)
    wu_bf[...] = wu_ref[...].astype(BF16)
    wd_bf[...] = wd_ref[...].astype(BF16)

    def item_in_slot(i, slot):
        _wait_token_gather(h_ref, xbufs[slot], gsem.at[slot])
        nslot = (slot + depth - 1) % depth
        for r in range(tm):
            gather_copy(i + depth - 1, r, nslot).start(priority=1)
        xb = _load_token_tiles(xbufs[slot], tm).astype(BF16)
        g = _dot(xb, wg_bf[...])
        u = _dot(xb, wu_bf[...])
        hidden = ((g / (1.0 + jnp.exp(-g))) * u).astype(BF16)
        _store_token_tiles(ybuf.at[slot % 2], _dot(hidden, wd_bf[...]))
        out_copy(0, slot % 2).wait()
        out_copy(row_ref[i], slot % 2).start()

    def item(i, _):
        slot = lax.rem(i, depth)
        for s in range(depth):
            @pl.when(slot == s)
            def _():
                item_in_slot(i, s)
        return 0

    lax.fori_loop(lo_ref[e], hi_ref[e], item, 0)

    @pl.when(e == pl.num_programs(0) - 1)
    def _():
        out_copy(0, 0).wait()
        for ahead in range(depth - 1):
            for s in range(depth):
                @pl.when(lax.rem(total + ahead, depth) == s)
                def _():
                    _wait_token_gather(h_ref, xbufs[s], gsem.at[s])


def _expert_ffn(h2_tiles, dest, tables, w_gate, w_up, w_down, tm):
    N = dest.shape[0]
    _, D, F = w_gate.shape
    assert D == SUBLANES * LANES
    item_row, item_lo, item_hi = tables
    wspec = lambda shape: pl.BlockSpec((None,) + shape, lambda e, *_: (e, 0, 0))
    return pl.pallas_call(
        functools.partial(_expert_kernel, tm=tm),
        grid_spec=pltpu.PrefetchScalarGridSpec(
            num_scalar_prefetch=4,
            grid=(N_EXPERTS,),
            in_specs=[pl.BlockSpec(memory_space=pl.ANY), wspec((D, F)), wspec((D, F)), wspec((F, D))],
            out_specs=pl.BlockSpec(memory_space=pl.ANY),
            scratch_shapes=[pltpu.VMEM((tm * SUBLANES, LANES), F32) for _ in range(EXPERT_GATHER_DEPTH)]
                           + [pltpu.VMEM((2, tm * SUBLANES, LANES), F32),
                            pltpu.SMEM((N + tm,), jnp.int32),
                            pltpu.VMEM((D, F), BF16), pltpu.VMEM((D, F), BF16), pltpu.VMEM((F, D), BF16),
                            pltpu.SemaphoreType.DMA((EXPERT_GATHER_DEPTH,)), pltpu.SemaphoreType.DMA(())],
        ),
        out_shape=jax.ShapeDtypeStruct(((N + tm) * SUBLANES, LANES), F32),
        compiler_params=_params("arbitrary"),
        name="expert_ffn",
    )(item_row, item_lo, item_hi, dest, h2_tiles, w_gate, w_up, w_down)


COMBINE_RING = 4
COMBINE_AHEAD = 2


def _combine_kernel(dest_ref, x1_ref, rf_ref, g_ref, y_ref, o_ref, *scratch, tb):
    bufs, sem = scratch[:COMBINE_RING], scratch[COMBINE_RING]
    i = pl.program_id(0)
    n_steps = pl.num_programs(0)
    n_tok = n_steps * COMBINE_RING * tb

    def copies(j, slot, r):
        return [pltpu.make_async_copy(_token_tile(y_ref, dest_ref[k * n_tok + j * tb + r]),
                                      _token_tile(bufs[slot].at[k], r), sem.at[slot])
                for k in range(TOP_K)]

    @pl.when(i == 0)
    def _():
        for slot in range(COMBINE_AHEAD):
            def first(r, _):
                for c in copies(slot, slot, r):
                    c.start()
                return 0
            lax.fori_loop(0, tb, first, 0, unroll=ROW_ISSUE_UNROLL // TOP_K)

    def sub_block(s, fetch):
        j = i * COMBINE_RING + s
        for k in range(TOP_K):
            _wait_token_gather(y_ref, bufs[s].at[k], sem.at[s])
        if fetch:
            for r in range(tb):
                for k, c in enumerate(copies(j + COMBINE_AHEAD, (s + COMBINE_AHEAD) % COMBINE_RING, r)):
                    c.start(priority=k)
        rows = slice(s * tb, (s + 1) * tb)
        rf = rf_ref[rows, :]
        x = (x1_ref[rows, :] + rf[:, 0:1] * _load_token_tiles(bufs[s].at[0], tb)
             + rf[:, 1:2] * _load_token_tiles(bufs[s].at[1], tb))
        inv = lax.rsqrt(jnp.mean(x * x, axis=-1, keepdims=True) + RMS_EPS)
        o_ref[rows, :] = (x * inv) * g_ref[...]

    for s in range(COMBINE_RING):
        if s + COMBINE_AHEAD < COMBINE_RING:
            sub_block(s, True)
        else:
            @pl.when(i + 1 < n_steps)
            def _():
                sub_block(s, True)

            @pl.when(i + 1 >= n_steps)
            def _():
                sub_block(s, False)


def _combine(x1, route_f, y_tiles, dest, ln_g, tb):
    T, D = x1.shape
    g2 = ln_g.reshape(1, D)
    step = COMBINE_RING * tb
    return pl.pallas_call(
        functools.partial(_combine_kernel, tb=tb),
        grid_spec=pltpu.PrefetchScalarGridSpec(
            num_scalar_prefetch=1,
            grid=(T // step,),
            in_specs=[pl.BlockSpec((step, D), lambda i, d: (i, 0)),
                      pl.BlockSpec((step, LANES), lambda i, d: (i, 0)),
                      pl.BlockSpec((1, D), lambda i, d: (0, 0)),
                      pl.BlockSpec(memory_space=pl.ANY)],
            out_specs=pl.BlockSpec((step, D), lambda i, d: (i, 0)),
            scratch_shapes=[pltpu.VMEM((TOP_K, tb * SUBLANES, LANES), F32) for _ in range(COMBINE_RING)]
                           + [pltpu.SemaphoreType.DMA((COMBINE_RING,))],
        ),
        out_shape=jax.ShapeDtypeStruct((T, D), F32),
        compiler_params=_params("arbitrary"),
        name="combine",
    )(dest, x1, route_f, g2, y_tiles)


MXU_WIDTH = 256


class _Tiles(NamedTuple):
    in_proj: int
    query: int
    mixers: int
    out_proj: int
    expert: int
    combine: int


def _tiles(n_tokens, seq_len):
    query = MXU_WIDTH
    tiles = _Tiles(in_proj=4 * MXU_WIDTH, query=query, mixers=query * (GLA_HEADS // 2),
                   out_proj=SUBLANES * LANES, expert=MXU_WIDTH, combine=MXU_WIDTH)
    assert seq_len % tiles.mixers == 0 and tiles.mixers % GLA_CHUNK == 0
    assert n_tokens % tiles.in_proj == 0 and n_tokens % tiles.out_proj == 0
    assert n_tokens % (COMBINE_RING * tiles.combine) == 0 and (TOP_K * n_tokens) % tiles.expert == 0
    return tiles


def _layer(x2, B, S, ln1_g, w_in, w2, gb, ng, w_out, ln2_g, w_group, b_group, w_expert, b_expert,
           w_gate, w_up, w_down, ln_f_g, tiles):
    q, k, v, gq, gk, gv, gr, lr = _in_proj(x2, ln1_g, w_in, tiles.in_proj)
    o_sb, o_gla = _mixers(q, k, v, gq, gk, gv, gr, lr, w2, gb, ng, B, S, tiles.mixers, tiles.query)
    x1, h2_tiles, route_i, route_f, counts = _out_proj(x2, o_sb, o_gla, w_out, ln2_g, w_group, b_group,
                                                       w_expert, b_expert, tiles.out_proj)
    T = x2.shape[0]
    dest, *tables = _route_tables(route_i, counts, TOP_K * T, tiles.expert)
    y_tiles = _expert_ffn(h2_tiles, dest, tables, w_gate, w_up, w_down, tiles.expert)
    return _combine(x1, route_f, y_tiles, dest, ln_f_g, tiles.combine)


def kernel(x, ln1_g, w_in, gla_gate_w2, gla_gate_b, gla_norm_g, w_out, ln2_g, w_group, b_group, w_expert, b_expert, exp_w_gate, exp_w_up, exp_w_down, ln_f_g):
    B, S, D = x.shape
    assert ln1_g.shape[0] == 1, "single-layer stack"
    assert D == SUBLANES * LANES, "one (SUBLANES, LANES) tile per token"
    out = _layer(x.reshape(B * S, D), B, S, ln1_g[0], w_in[0], gla_gate_w2[0], gla_gate_b[0],
                 gla_norm_g[0], w_out[0], ln2_g[0], w_group[0], b_group[0], w_expert[0], b_expert[0],
                 exp_w_gate[0], exp_w_up[0], exp_w_down[0], ln_f_g, _tiles(B * S, S))
    return out.reshape(B, S, D)
```

```python
import functools
from typing import NamedTuple

import numpy as np
import jax
import jax.numpy as jnp
from jax import lax
from jax.experimental import pallas as pl
from jax.experimental.pallas import tpu as pltpu

SB_HEADS = 8
SB_HEAD_DIM = 64
SB_WIDTH = SB_HEADS * SB_HEAD_DIM
GLA_HEADS = 4
GLA_DK = 64
GLA_DV = 128
GLA_KW = GLA_HEADS * GLA_DK
GLA_VW = GLA_HEADS * GLA_DV
GLA_GATE_RANK = 16
GLA_GATE_TAU = 16.0
N_GROUPS = 4
EXPERTS_PER_GROUP = 8
N_EXPERTS = N_GROUPS * EXPERTS_PER_GROUP
TOP_K = 2
RMS_EPS = 1e-6
LOG2_E = 1.4426950408889634

LANES = 128
VMEM_LIMIT_BYTES = 56 * 1024 * 1024

F32 = jnp.float32
BF16 = jnp.bfloat16


def _dot(a, b):
    return jnp.dot(a, b, preferred_element_type=F32)


def _dot_nt(a, b):
    return lax.dot_general(a, b, (((1,), (1,)), ((), ())), preferred_element_type=F32)


def _softplus(z):
    return jnp.maximum(z, 0.0) + jnp.log(1.0 + jnp.exp(-jnp.abs(z)))


def _params(*sem):
    return pltpu.CompilerParams(dimension_semantics=sem, vmem_limit_bytes=VMEM_LIMIT_BYTES)


def _in_proj_kernel(x_ref, g_ref, wq, wk, wv, wgq, wgk, wgv, wgr, wlr,
                    oq, ok, ov, ogq, ogk, ogv, ogr, olr):
    x = x_ref[...]
    inv = lax.rsqrt(jnp.mean(x * x, axis=-1, keepdims=True) + RMS_EPS)
    h = ((x * inv) * g_ref[...]).astype(BF16)
    oq[...] = (_dot(h, wq[...]) * (SB_HEAD_DIM ** -0.5 * LOG2_E)).astype(oq.dtype)
    ok[...] = _dot(h, wk[...]).astype(ok.dtype)
    ov[...] = _dot(h, wv[...]).astype(ov.dtype)
    ogq[...] = (_dot(h, wgq[...]) * (GLA_DK ** -0.5)).astype(ogq.dtype)
    ogk[...] = _dot(h, wgk[...]).astype(ogk.dtype)
    ogv[...] = _dot(h, wgv[...]).astype(ogv.dtype)
    ogr[...] = _dot(h, wgr[...]).astype(ogr.dtype)
    olr[...] = _dot(h, wlr[...]).astype(olr.dtype)


def _in_proj(x2, ln_g, w_in, tm):
    T, D = x2.shape
    sizes = (SB_WIDTH, SB_WIDTH, SB_WIDTH, GLA_KW, GLA_KW, GLA_VW, GLA_VW, GLA_GATE_RANK)
    offs = np.concatenate([[0], np.cumsum(sizes)])
    ws = [w_in[:, int(offs[i]):int(offs[i + 1])].astype(BF16) for i in range(len(sizes))]
    out_dtypes = (BF16, BF16, BF16, F32, F32, BF16, F32, F32)
    row = lambda n: pl.BlockSpec((tm, n), lambda i: (i, 0))
    full = lambda n: pl.BlockSpec((D, n), lambda i: (0, 0))
    return pl.pallas_call(
        _in_proj_kernel,
        grid=(T // tm,),
        in_specs=[row(D), pl.BlockSpec((1, D), lambda i: (0, 0))] + [full(n) for n in sizes],
        out_specs=[row(n) for n in sizes],
        out_shape=[jax.ShapeDtypeStruct((T, n), dt) for n, dt in zip(sizes, out_dtypes)],
        compiler_params=_params("parallel"),
        name="in_proj",
    )(x2, ln_g.reshape(1, D), *ws)


SB_ZERO_WEIGHT_BITS = 160.0


def _softplus2(z):
    return jnp.maximum(z, 0.0) + jnp.log2(1.0 + jnp.exp2(-jnp.abs(z)))


def _sb_query_block(q_ref, k_ref, v_ref, o_ref, qh_ref, acc_ref, carry_ref, *, row0, qi, tq, n_pairs,
                    alongside=None):
    qrows = slice(row0, row0 + tq)
    r = lax.broadcasted_iota(jnp.int32, (tq, tq), 0)
    c = lax.broadcasted_iota(jnp.int32, (tq, tq), 1)
    causal = c < r
    later = (r > c).astype(BF16)
    head0 = lax.broadcasted_iota(jnp.int32, (1, LANES), 1) < SB_HEAD_DIM

    for p in range(n_pairs):
        q2 = q_ref[qrows, p * LANES:(p + 1) * LANES]
        zero = jnp.zeros_like(q2)
        qh_ref[p, 0:tq] = jnp.where(head0, q2, zero)
        qh_ref[p, tq:2 * tq] = jnp.where(head0, zero, q2)
    acc_ref[...] = jnp.zeros_like(acc_ref)
    carry_ref[...] = jnp.zeros_like(carry_ref)
    causal2 = jnp.concatenate([causal, causal], axis=0)

    def key_block(kb, diag):
        rows = pl.ds(pl.multiple_of(kb * tq, tq), tq)
        pairs = range(n_pairs)
        zs = {}

        def scores(p):
            zs[p] = _dot_nt(qh_ref[p], k_ref[rows, pl.ds(p * LANES, LANES)])
        scores(0)
        scores(1)

        def suffix_stage(p):
            sp = _softplus2(zs[p])
            if diag:
                sp = jnp.where(causal2, sp, 0.0)
            return sp, _dot(sp.astype(BF16), later)

        def weight_stage(p, sp, run):
            z = zs[p]
            v2 = v_ref[rows, pl.ds(p * LANES, LANES)]
            vzero = jnp.zeros_like(v2)
            v_stack = jnp.concatenate([jnp.where(head0, v2, vzero), jnp.where(head0, vzero, v2)], axis=0)
            carry = carry_ref[p]
            w = jnp.exp2((z - sp) - run - jnp.concatenate([carry] * (tq // LANES), axis=1))
            if diag:
                w = jnp.where(causal2, w, 0.0)
            w = w.astype(BF16)
            acc_ref[p] += _dot(jnp.concatenate([w[0:tq], w[tq:2 * tq]], axis=1), v_stack)
            carry_ref[p] = carry + jnp.broadcast_to(run[:, 0:1] + sp[:, 0:1], (2 * tq, LANES))

        pending = None
        for p in pairs:
            if p + 2 < n_pairs:
                scores(p + 2)
            staged = suffix_stage(p)
            if pending is not None:
                weight_stage(*pending)
            pending = (p,) + staged
        weight_stage(*pending)

    def min_carry():
        m = carry_ref[0]
        for p in range(1, n_pairs):
            m = jnp.minimum(m, carry_ref[p])
        return jnp.min(m)

    def cond(state):
        kb, m = state
        return jnp.logical_and(kb >= 0, m < SB_ZERO_WEIGHT_BITS)

    def body(state):
        kb, _ = state
        key_block(kb, False)
        return kb - 1, min_carry()

    if alongside is not None:
        alongside()
    key_block(qi, True)
    lax.while_loop(cond, body, (qi - 1, min_carry()))
    for p in range(n_pairs):
        o_ref[qrows, p * LANES:(p + 1) * LANES] = acc_ref[p].astype(o_ref.dtype)


GLA_CHUNK = 64
GLA_LEVELS = (32, 16, 8, 4, 2, 1)


def _gla_tables(C=GLA_CHUNK):
    t = np.arange(C)[:, None]
    j = np.arange(C)[None, :]
    blocks = [(j <= t), (j > t)]
    masks = []
    for half in GLA_LEVELS:
        blk = 2 * half
        u = (t // blk) * blk + half - 1
        second = (t % blk) >= half
        blocks.append((second & (j > u) & (j <= t)) | ((~second) & (j > t) & (j <= u)))
        s = j
        masks.append(second & ((s % blk) < half) & ((s // blk) == (t // blk)))
    blocks.append(np.ones((8, C), bool))
    table = np.concatenate(blocks, axis=0).astype(np.float32)
    table = np.concatenate([table, table], axis=1)
    return table, np.concatenate(masks, axis=0).astype(np.float32)


def _split_bf16(a):
    hi = a.astype(BF16)
    lo = (a - hi.astype(F32)).astype(BF16)
    return hi, lo


def _gla_head_pair(p, gq_ref, gk_ref, gv_ref, gr_ref, lr_ref, w2_ref, gb_ref, ng_ref, tab_ref, msk_ref,
                   o_ref, state_ref, *, n_chunks):
    C = GLA_CHUNK
    n_lv = len(GLA_LEVELS)

    head0 = lax.broadcasted_iota(jnp.int32, (1, LANES), 1) < GLA_DK
    eye = (lax.broadcasted_iota(jnp.int32, (C, C), 0) == lax.broadcasted_iota(jnp.int32, (C, C), 1))
    w2_hi, w2_lo = _split_bf16(w2_ref[...])
    lr_hi, lr_lo = _split_bf16(lr_ref[...])

    def stack_heads(a):
        return jnp.concatenate([jnp.where(head0, a, 0.0), jnp.where(head0, 0.0, a)], axis=0)

    cols = slice(p * LANES, (p + 1) * LANES)
    pre = (_dot(lr_hi, w2_hi[:, cols]) + _dot(lr_hi, w2_lo[:, cols])
           + _dot(lr_lo, w2_hi[:, cols])) + gb_ref[:, cols]
    log_a = -_softplus(-pre) * (1.0 / GLA_GATE_TAU)
    la_hi, la_lo = _split_bf16(log_a)
    la = jnp.concatenate(
        [jnp.concatenate([la_hi[c * C:(c + 1) * C], la_lo[c * C:(c + 1) * C]], axis=0)
         for c in range(n_chunks)], axis=1)
    ex_all = jnp.exp(_dot(tab_ref[...], la))
    st = state_ref[p]
    def score_stage(c):
        rows = slice(c * C, (c + 1) * C)
        ex = ex_all[:, c * LANES:(c + 1) * LANES]
        q2 = gq_ref[rows, cols]
        k2 = gk_ref[rows, cols]
        q_int = q2 * ex[0:C]
        k_dec = (k2 * ex[C:2 * C]).astype(BF16)
        dec_row = ex[(2 + n_lv) * C:(2 + n_lv) * C + 1]
        diag = jnp.sum(stack_heads(q2 * k2), axis=1, keepdims=True)
        scores = jnp.where(jnp.concatenate([eye, eye], axis=0), diag, 0.0)
        for l in range(n_lv):
            ex_l = ex[(2 + l) * C:(3 + l) * C]
            q_l = stack_heads(q2 * ex_l).astype(BF16)
            k_l = (k2 * ex_l).astype(BF16)
            m_l = msk_ref[l * C:(l + 1) * C, :] > 0.5
            scores = scores + jnp.where(jnp.concatenate([m_l, m_l], axis=0), _dot_nt(q_l, k_l), 0.0)
        return c, q_int, k_dec, dec_row, scores.astype(BF16)

    def state_stage(st, c, q_int, k_dec, dec_row, scores):
        rows = slice(c * C, (c + 1) * C)
        o_inter = _dot_nt(stack_heads(q_int).astype(BF16), st.astype(BF16))
        upd = []
        for h in range(2):
            vcols = slice((2 * p + h) * GLA_DV, (2 * p + h + 1) * GLA_DV)
            v_h = gv_ref[rows, vcols]
            o = o_inter[h * C:(h + 1) * C] + _dot(scores[h * C:(h + 1) * C], v_h)
            inv = lax.rsqrt(jnp.mean(o * o, axis=-1, keepdims=True) + RMS_EPS)
            y = (o * inv) * ng_ref[:, vcols]
            g = gr_ref[rows, vcols]
            y = y * (g / (1.0 + jnp.exp(-g)))
            o_ref[rows, vcols] = y.astype(o_ref.dtype)
            v_t = v_h.astype(F32).T.astype(BF16)
            upd.append(_dot(v_t, k_dec))
        return st * dec_row + jnp.where(head0, upd[0], upd[1])

    pending = score_stage(0)
    for c in range(1, n_chunks):
        ahead = score_stage(c)
        st = state_stage(st, *pending)
        pending = ahead
    st = state_stage(st, *pending)
    state_ref[p] = st


def _mixers_kernel(q_ref, k_ref, v_ref, gq_ref, gk_ref, gv_ref, gr_ref, lr_ref, w2_ref, gb_ref, ng_ref,
                   tab_ref, msk_ref, osb_ref, ogla_ref, qh_ref, acc_ref, carry_ref, state_ref,
                   *, tq, n_pairs, n_chunks):
    blk = pl.program_id(1)
    n_q = q_ref.shape[0] // tq
    assert n_q == GLA_HEADS // 2

    @pl.when(blk == 0)
    def _():
        state_ref[...] = jnp.zeros_like(state_ref)

    for half in range(n_q):
        gla = functools.partial(_gla_head_pair, half, gq_ref, gk_ref, gv_ref, gr_ref, lr_ref, w2_ref, gb_ref,
                                ng_ref, tab_ref, msk_ref, ogla_ref, state_ref, n_chunks=n_chunks)
        _sb_query_block(q_ref, k_ref, v_ref, osb_ref, qh_ref, acc_ref, carry_ref, row0=half * tq,
                        qi=blk * n_q + half, tq=tq, n_pairs=n_pairs, alongside=gla)


def _mixers(q, k, v, gq, gk, gv, gr, lr, w2, gb, ng, B, S, tb, tq):
    table, masks = _gla_tables()
    table = jnp.asarray(table, BF16)
    masks = jnp.asarray(masks, F32)
    T = B * S
    W = q.shape[-1]
    n_pairs = W // LANES
    nblk = S // tb
    q3, k3, v3 = (a.reshape(B, S, W) for a in (q, k, v))
    blk3 = pl.BlockSpec((None, tb, W), lambda b, i: (b, i, 0))
    seq3 = pl.BlockSpec((None, S, W), lambda b, i: (b, 0, 0))
    row = lambda n: pl.BlockSpec((tb, n), lambda b, i: (b * nblk + i, 0))
    const = lambda a: pl.BlockSpec(a.shape, lambda b, i: (0,) * a.ndim)
    w2 = w2.astype(F32)
    gb2 = gb.reshape(1, GLA_KW)
    ng2 = ng.reshape(1, GLA_VW)
    o_sb, o_gla = pl.pallas_call(
        functools.partial(_mixers_kernel, tq=tq, n_pairs=n_pairs, n_chunks=tb // GLA_CHUNK),
        grid=(B, nblk),
        in_specs=[blk3, seq3, seq3, row(GLA_KW), row(GLA_KW), row(GLA_VW), row(GLA_VW), row(GLA_GATE_RANK),
                  const(w2), const(gb2), const(ng2), const(table), const(masks)],
        out_specs=[blk3, row(GLA_VW)],
        out_shape=[jax.ShapeDtypeStruct((B, S, W), BF16), jax.ShapeDtypeStruct((T, GLA_VW), BF16)],
        scratch_shapes=[pltpu.VMEM((n_pairs, 2 * tq, LANES), BF16),
                        pltpu.VMEM((n_pairs, tq, LANES), F32),
                        pltpu.VMEM((n_pairs, 2 * tq, LANES), F32),
                        pltpu.VMEM((GLA_HEADS // 2, GLA_DV, 2 * GLA_DK), F32)],
        compiler_params=_params("parallel", "arbitrary"),
        name="mixers",
    )(q3, k3, v3, gq, gk, gv, gr, lr, w2, gb2, ng2, table, masks)
    return o_sb.reshape(T, W), o_gla


SUBLANES = 8


def _store_token_tiles(ref, value, accumulate=False):
    n = value.shape[0]
    for s in range(SUBLANES):
        rows = pl.ds(s, n, stride=SUBLANES)
        chunk = value[:, s * LANES:(s + 1) * LANES]
        ref[rows, :] = ref[rows, :] + chunk if accumulate else chunk


def _load_token_tiles(ref, n):
    return jnp.concatenate([ref[pl.ds(s, n, stride=SUBLANES), :] for s in range(SUBLANES)], axis=1)


def _out_proj_kernel(x_ref, oa_ref, ob_ref, wa_ref, wb_ref, g_ref, wr_ref, br_ref, tri_ref,
                     x1_ref, h2_ref, ri_ref, rf_ref, cnt_ref):
    @pl.when(pl.program_id(0) == 0)
    def _():
        cnt_ref[...] = jnp.zeros_like(cnt_ref)

    x1 = x_ref[...] + _dot(oa_ref[...], wa_ref[...]) + _dot(ob_ref[...], wb_ref[...])
    x1_ref[...] = x1
    inv = lax.rsqrt(jnp.mean(x1 * x1, axis=-1, keepdims=True) + RMS_EPS)
    h2 = (x1 * inv) * g_ref[...]
    _store_token_tiles(h2_ref, h2)

    h_hi, h_lo = _split_bf16(h2)
    w_hi, w_lo = _split_bf16(wr_ref[...])
    hw = _dot(h_hi, jnp.concatenate([w_hi, w_lo], axis=1))
    logits = hw[:, :LANES] + hw[:, LANES:] + _dot(h_lo, w_hi) + br_ref[...]
    lane = lax.broadcasted_iota(jnp.int32, logits.shape, 1).astype(F32)
    neg = -jnp.inf

    def first_max(vals):
        m = jnp.max(vals, axis=1, keepdims=True)
        idx = jnp.min(jnp.where(vals == m, lane, float(LANES)), axis=1, keepdims=True)
        return m, idx

    is_group = lane < N_GROUPS
    g_max, g_idx = first_max(jnp.where(is_group, logits, neg))
    g_p = 1.0 / jnp.sum(jnp.where(is_group, jnp.exp(logits - g_max), 0.0), axis=1, keepdims=True)
    lo_lane = N_GROUPS + EXPERTS_PER_GROUP * g_idx
    sel = jnp.where((lane >= lo_lane) & (lane < lo_lane + EXPERTS_PER_GROUP), logits, neg)
    m1, i1 = first_max(sel)
    m2, i2 = first_max(jnp.where(lane == i1, neg, sel))
    e = jnp.exp(m2 - m1)
    gate1 = g_p / (1.0 + e)
    gate2 = g_p * e / (1.0 + e)
    rf_ref[...] = jnp.where(lane == 0, gate1, jnp.where(lane == 1, gate2, 0.0))

    e1 = i1 - N_GROUPS
    e2 = i2 - N_GROUPS
    hits = (lane == e1).astype(F32) + (lane == e2).astype(F32)
    before = _dot(tri_ref[...], hits.astype(BF16)) + cnt_ref[...]
    r1 = jnp.sum(jnp.where(lane == e1, before, 0.0), axis=1, keepdims=True)
    r2 = jnp.sum(jnp.where(lane == e2, before, 0.0), axis=1, keepdims=True)
    cnt_ref[...] += jnp.sum(hits, axis=0, keepdims=True)
    packed = jnp.where(lane == 0, e1, jnp.where(lane == 1, e2,
                       jnp.where(lane == 2, r1, jnp.where(lane == 3, r2, 0.0))))
    for j in range(packed.shape[0] // LANES):
        rows_t = packed[j * LANES:(j + 1) * LANES, :].T
        for k in range(2 * TOP_K):
            ri_ref[k, j:j + 1, :] = rows_t[k:k + 1, :].astype(jnp.int32)


def _out_proj(x2, o_sb, o_gla, w_out, ln_g, w_group, b_group, w_expert, b_expert, tm):
    T, D = x2.shape
    wa = w_out[:SB_WIDTH].astype(BF16)
    wb = w_out[SB_WIDTH:].astype(BF16)
    n_r = N_GROUPS + N_EXPERTS
    wr = jnp.zeros((D, LANES), F32).at[:, :n_r].set(jnp.concatenate([w_group, w_expert], axis=1))
    br = jnp.zeros((1, LANES), F32).at[0, :n_r].set(jnp.concatenate([b_group, b_expert]))
    row = lambda n: pl.BlockSpec((tm, n), lambda i: (i, 0))
    const = lambda a: pl.BlockSpec(a.shape, lambda i: (0,) * a.ndim)
    g2 = ln_g.reshape(1, D)
    tri = jnp.asarray(np.tril(np.ones((tm, tm), np.float32), -1), BF16)
    return pl.pallas_call(
        _out_proj_kernel,
        grid=(T // tm,),
        in_specs=[row(D), row(SB_WIDTH), row(GLA_VW), const(wa), const(wb), const(g2), const(wr), const(br),
                  const(tri)],
        out_specs=[row(D), pl.BlockSpec((tm * SUBLANES, LANES), lambda i: (i, 0)),
                   pl.BlockSpec((2 * TOP_K, tm // LANES, LANES), lambda i: (0, i, 0)), row(LANES),
                   pl.BlockSpec((1, LANES), lambda i: (0, 0))],
        out_shape=[jax.ShapeDtypeStruct((T, D), F32), jax.ShapeDtypeStruct((T * SUBLANES, LANES), F32),
                   jax.ShapeDtypeStruct((2 * TOP_K, T // LANES, LANES), jnp.int32),
                   jax.ShapeDtypeStruct((T, LANES), F32), jax.ShapeDtypeStruct((1, LANES), F32)],
        compiler_params=_params("arbitrary"),
        name="out_proj",
    )(x2, o_sb, o_gla, wa, wb, g2, wr, br, tri)


def _route_tables(route, counts, n_rows, tm):
    counts = counts[0, :N_EXPERTS].astype(jnp.int32)
    ends = jnp.cumsum(counts)
    starts = ends - counts
    route = route.reshape(2 * TOP_K, -1)
    expert = route[:TOP_K, :, None] == jnp.arange(N_EXPERTS, dtype=jnp.int32)
    dest = (jnp.sum(jnp.where(expert, starts, 0), axis=-1) + route[TOP_K:]).reshape(-1)

    n_items = n_rows // tm + N_EXPERTS + EXPERT_GATHER_DEPTH
    items_e = (counts + tm - 1) // tm
    item_hi = jnp.cumsum(items_e)
    item_lo = item_hi - items_e
    i = jnp.arange(n_items, dtype=jnp.int32)
    e_of = jnp.minimum(jnp.sum(item_hi[None, :] <= i[:, None], axis=1), N_EXPERTS - 1)
    owner = e_of[:, None] == jnp.arange(N_EXPERTS, dtype=jnp.int32)
    pick = lambda tab: jnp.sum(jnp.where(owner, tab, 0), axis=-1)
    item_row = jnp.where(i < item_hi[-1], pick(starts) + (i - pick(item_lo)) * tm, n_rows)
    return (dest, item_row.astype(jnp.int32), item_lo.astype(jnp.int32), item_hi.astype(jnp.int32))


ROW_ISSUE_UNROLL = 8
EXPERT_GATHER_DEPTH = 4


def _token_tile(ref, t):
    return ref.at[pl.ds(pl.multiple_of(t * SUBLANES, SUBLANES), SUBLANES), :]


def _wait_token_gather(src_ref, dst_ref, sem):
    pltpu.make_async_copy(src_ref.at[pl.ds(0, dst_ref.shape[0]), :], dst_ref, sem).wait()


def _expert_kernel(row_ref, lo_ref, hi_ref, dest_ref,
                   h_ref, wg_ref, wu_ref, wd_ref, y_ref, *scratch, tm):
    depth = EXPERT_GATHER_DEPTH
    xbufs = scratch[:depth]
    ybuf, tok_ref, wg_bf, wu_bf, wd_bf, gsem, wsem = scratch[depth:]
    e = pl.program_id(0)
    n_rows = dest_ref.shape[0]
    n_tok = n_rows // TOP_K
    total = hi_ref[N_EXPERTS - 1]

    def gather_copy(i, r, slot):
        t = tok_ref[row_ref[i] + r]
        return pltpu.make_async_copy(_token_tile(h_ref, t), _token_tile(xbufs[slot], r), gsem.at[slot])

    def out_copy(row, slot):
        rows = pl.ds(pl.multiple_of(row * SUBLANES, SUBLANES), tm * SUBLANES)
        return pltpu.make_async_copy(ybuf.at[slot], y_ref.at[rows, :], wsem)

    @pl.when(e == 0)
    def _():
        def invert(t, _):
            for k in range(TOP_K):
                tok_ref[dest_ref[k * n_tok + t]] = t
            return 0
        lax.fori_loop(0, n_tok, invert, 0, unroll=ROW_ISSUE_UNROLL)

        def spare(r, _):
            tok_ref[n_rows + r] = 0
            return 0
        lax.fori_loop(0, tm, spare, 0, unroll=ROW_ISSUE_UNROLL)
        ybuf[1] = jnp.zeros(ybuf.shape[1:], ybuf.dtype)
        out_copy(n_rows, 1).start()
        for ahead in range(depth - 1):
            def first(r, _):
                gather_copy(ahead, r, ahead).start()
                return 0
            lax.fori_loop(0, tm, first, 0, unroll=ROW_ISSUE_UNROLL)

    wg_bf[...] = wg_ref[...].astype(BF16)
    wu_bf[...] = wu_ref[...].astype(BF16)
    wd_bf[...] = wd_ref[...].astype(BF16)

    def item_in_slot(i, slot):
        _wait_token_gather(h_ref, xbufs[slot], gsem.at[slot])
        nslot = (slot + depth - 1) % depth
        for r in range(tm):
            gather_copy(i + depth - 1, r, nslot).start(priority=1)
        xb = _load_token_tiles(xbufs[slot], tm).astype(BF16)
        g = _dot(xb, wg_bf[...])
        u = _dot(xb, wu_bf[...])
        hidden = ((g / (1.0 + jnp.exp(-g))) * u).astype(BF16)
        _store_token_tiles(ybuf.at[slot % 2], _dot(hidden, wd_bf[...]))
        out_copy(0, slot % 2).wait()
        out_copy(row_ref[i], slot % 2).start()

    def item(i, _):
        slot = lax.rem(i, depth)
        for s in range(depth):
            @pl.when(slot == s)
            def _():
                item_in_slot(i, s)
        return 0

    lax.fori_loop(lo_ref[e], hi_ref[e], item, 0)

    @pl.when(e == pl.num_programs(0) - 1)
    def _():
        out_copy(0, 0).wait()
        for ahead in range(depth - 1):
            for s in range(depth):
                @pl.when(lax.rem(total + ahead, depth) == s)
                def _():
                    _wait_token_gather(h_ref, xbufs[s], gsem.at[s])


def _expert_ffn(h2_tiles, dest, tables, w_gate, w_up, w_down, tm):
    N = dest.shape[0]
    _, D, F = w_gate.shape
    assert D == SUBLANES * LANES
    item_row, item_lo, item_hi = tables
    wspec = lambda shape: pl.BlockSpec((None,) + shape, lambda e, *_: (e, 0, 0))
    return pl.pallas_call(
        functools.partial(_expert_kernel, tm=tm),
        grid_spec=pltpu.PrefetchScalarGridSpec(
            num_scalar_prefetch=4,
            grid=(N_EXPERTS,),
            in_specs=[pl.BlockSpec(memory_space=pl.ANY), wspec((D, F)), wspec((D, F)), wspec((F, D))],
            out_specs=pl.BlockSpec(memory_space=pl.ANY),
            scratch_shapes=[pltpu.VMEM((tm * SUBLANES, LANES), F32) for _ in range(EXPERT_GATHER_DEPTH)]
                           + [pltpu.VMEM((2, tm * SUBLANES, LANES), F32),
                            pltpu.SMEM((N + tm,), jnp.int32),
                            pltpu.VMEM((D, F), BF16), pltpu.VMEM((D, F), BF16), pltpu.VMEM((F, D), BF16),
                            pltpu.SemaphoreType.DMA((EXPERT_GATHER_DEPTH,)), pltpu.SemaphoreType.DMA(())],
        ),
        out_shape=jax.ShapeDtypeStruct(((N + tm) * SUBLANES, LANES), F32),
        compiler_params=_params("arbitrary"),
        name="expert_ffn",
    )(item_row, item_lo, item_hi, dest, h2_tiles, w_gate, w_up, w_down)


COMBINE_RING = 4
COMBINE_AHEAD = 2


def _combine_kernel(dest_ref, x1_ref, rf_ref, g_ref, y_ref, o_ref, *scratch, tb):
    bufs, sem = scratch[:COMBINE_RING], scratch[COMBINE_RING]
    i = pl.program_id(0)
    n_steps = pl.num_programs(0)
    n_tok = n_steps * COMBINE_RING * tb

    def copies(j, slot, r):
        return [pltpu.make_async_copy(_token_tile(y_ref, dest_ref[k * n_tok + j * tb + r]),
                                      _token_tile(bufs[slot].at[k], r), sem.at[slot])
                for k in range(TOP_K)]

    @pl.when(i == 0)
    def _():
        for slot in range(COMBINE_AHEAD):
            def first(r, _):
                for c in copies(slot, slot, r):
                    c.start()
                return 0
            lax.fori_loop(0, tb, first, 0, unroll=ROW_ISSUE_UNROLL // TOP_K)

    def sub_block(s, fetch):
        j = i * COMBINE_RING + s
        for k in range(TOP_K):
            _wait_token_gather(y_ref, bufs[s].at[k], sem.at[s])
        if fetch:
            for r in range(tb):
                for k, c in enumerate(copies(j + COMBINE_AHEAD, (s + COMBINE_AHEAD) % COMBINE_RING, r)):
                    c.start(priority=k)
        rows = slice(s * tb, (s + 1) * tb)
        rf = rf_ref[rows, :]
        x = (x1_ref[rows, :] + rf[:, 0:1] * _load_token_tiles(bufs[s].at[0], tb)
             + rf[:, 1:2] * _load_token_tiles(bufs[s].at[1], tb))
        inv = lax.rsqrt(jnp.mean(x * x, axis=-1, keepdims=True) + RMS_EPS)
        o_ref[rows, :] = (x * inv) * g_ref[...]

    for s in range(COMBINE_RING):
        if s + COMBINE_AHEAD < COMBINE_RING:
            sub_block(s, True)
        else:
            @pl.when(i + 1 < n_steps)
            def _():
                sub_block(s, True)

            @pl.when(i + 1 >= n_steps)
            def _():
                sub_block(s, False)


def _combine(x1, route_f, y_tiles, dest, ln_g, tb):
    T, D = x1.shape
    g2 = ln_g.reshape(1, D)
    step = COMBINE_RING * tb
    return pl.pallas_call(
        functools.partial(_combine_kernel, tb=tb),
        grid_spec=pltpu.PrefetchScalarGridSpec(
            num_scalar_prefetch=1,
            grid=(T // step,),
            in_specs=[pl.BlockSpec((step, D), lambda i, d: (i, 0)),
                      pl.BlockSpec((step, LANES), lambda i, d: (i, 0)),
                      pl.BlockSpec((1, D), lambda i, d: (0, 0)),
                      pl.BlockSpec(memory_space=pl.ANY)],
            out_specs=pl.BlockSpec((step, D), lambda i, d: (i, 0)),
            scratch_shapes=[pltpu.VMEM((TOP_K, tb * SUBLANES, LANES), F32) for _ in range(COMBINE_RING)]
                           + [pltpu.SemaphoreType.DMA((COMBINE_RING,))],
        ),
        out_shape=jax.ShapeDtypeStruct((T, D), F32),
        compiler_params=_params("arbitrary"),
        name="combine",
    )(dest, x1, route_f, g2, y_tiles)


MXU_WIDTH = 256


class _Tiles(NamedTuple):
    in_proj: int
    query: int
    mixers: int
    out_proj: int
    expert: int
    combine: int


def _tiles(n_tokens, seq_len):
    query = MXU_WIDTH
    tiles = _Tiles(in_proj=4 * MXU_WIDTH, query=query, mixers=query * (GLA_HEADS // 2),
                   out_proj=SUBLANES * LANES, expert=MXU_WIDTH, combine=MXU_WIDTH)
    assert seq_len % tiles.mixers == 0 and tiles.mixers % GLA_CHUNK == 0
    assert n_tokens % tiles.in_proj == 0 and n_tokens % tiles.out_proj == 0
    assert n_tokens % (COMBINE_RING * tiles.combine) == 0 and (TOP_K * n_tokens) % tiles.expert == 0
    return tiles


def _layer(x2, B, S, ln1_g, w_in, w2, gb, ng, w_out, ln2_g, w_group, b_group, w_expert, b_expert,
           w_gate, w_up, w_down, ln_f_g, tiles):
    q, k, v, gq, gk, gv, gr, lr = _in_proj(x2, ln1_g, w_in, tiles.in_proj)
    o_sb, o_gla = _mixers(q, k, v, gq, gk, gv, gr, lr, w2, gb, ng, B, S, tiles.mixers, tiles.query)
    x1, h2_tiles, route_i, route_f, counts = _out_proj(x2, o_sb, o_gla, w_out, ln2_g, w_group, b_group,
                                                       w_expert, b_expert, tiles.out_proj)
    T = x2.shape[0]
    dest, *tables = _route_tables(route_i, counts, TOP_K * T, tiles.expert)
    y_tiles = _expert_ffn(h2_tiles, dest, tables, w_gate, w_up, w_down, tiles.expert)
    return _combine(x1, route_f, y_tiles, dest, ln_f_g, tiles.combine)


def kernel(x, ln1_g, w_in, gla_gate_w2, gla_gate_b, gla_norm_g, w_out, ln2_g, w_group, b_group, w_expert, b_expert, exp_w_gate, exp_w_up, exp_w_down, ln_f_g):
    B, S, D = x.shape
    assert ln1_g.shape[0] == 1, "single-layer stack"
    assert D == SUBLANES * LANES, "one (SUBLANES, LANES) tile per token"
    out = _layer(x.reshape(B * S, D), B, S, ln1_g[0], w_in[0], gla_gate_w2[0], gla_gate_b[0],
                 gla_norm_g[0], w_out[0], ln2_g[0], w_group[0], b_group[0], w_expert[0], b_expert[0],
                 exp_w_gate[0], exp_w_up[0], exp_w_down[0], ln_f_g, _tiles(B * S, S))
    return out.reshape(B, S, D)
```

```python
import functools
from typing import NamedTuple

import numpy as np
import jax
import jax.numpy as jnp
from jax import lax
from jax.experimental import pallas as pl
from jax.experimental.pallas import tpu as pltpu

SB_HEADS = 8
SB_HEAD_DIM = 64
SB_WIDTH = SB_HEADS * SB_HEAD_DIM
GLA_HEADS = 4
GLA_DK = 64
GLA_DV = 128
GLA_KW = GLA_HEADS * GLA_DK
GLA_VW = GLA_HEADS * GLA_DV
GLA_GATE_RANK = 16
GLA_GATE_TAU = 16.0
N_GROUPS = 4
EXPERTS_PER_GROUP = 8
N_EXPERTS = N_GROUPS * EXPERTS_PER_GROUP
TOP_K = 2
RMS_EPS = 1e-6
LOG2_E = 1.4426950408889634

LANES = 128
VMEM_LIMIT_BYTES = 56 * 1024 * 1024

F32 = jnp.float32
BF16 = jnp.bfloat16


def _dot(a, b):
    return jnp.dot(a, b, preferred_element_type=F32)


def _dot_nt(a, b):
    return lax.dot_general(a, b, (((1,), (1,)), ((), ())), preferred_element_type=F32)


def _softplus(z):
    return jnp.maximum(z, 0.0) + jnp.log(1.0 + jnp.exp(-jnp.abs(z)))


def _params(*sem):
    return pltpu.CompilerParams(dimension_semantics=sem, vmem_limit_bytes=VMEM_LIMIT_BYTES)


def _in_proj_kernel(x_ref, g_ref, wq, wk, wv, wgq, wgk, wgv, wgr, wlr,
                    oq, ok, ov, ogq, ogk, ogv, ogr, olr):
    x = x_ref[...]
    inv = lax.rsqrt(jnp.mean(x * x, axis=-1, keepdims=True) + RMS_EPS)
    h = ((x * inv) * g_ref[...]).astype(BF16)
    oq[...] = (_dot(h, wq[...]) * (SB_HEAD_DIM ** -0.5 * LOG2_E)).astype(oq.dtype)
    ok[...] = _dot(h, wk[...]).astype(ok.dtype)
    ov[...] = _dot(h, wv[...]).astype(ov.dtype)
    ogq[...] = (_dot(h, wgq[...]) * (GLA_DK ** -0.5)).astype(ogq.dtype)
    ogk[...] = _dot(h, wgk[...]).astype(ogk.dtype)
    ogv[...] = _dot(h, wgv[...]).astype(ogv.dtype)
    ogr[...] = _dot(h, wgr[...]).astype(ogr.dtype)
    olr[...] = _dot(h, wlr[...]).astype(olr.dtype)


def _in_proj(x2, ln_g, w_in, tm):
    T, D = x2.shape
    sizes = (SB_WIDTH, SB_WIDTH, SB_WIDTH, GLA_KW, GLA_KW, GLA_VW, GLA_VW, GLA_GATE_RANK)
    offs = np.concatenate([[0], np.cumsum(sizes)])
    ws = [w_in[:, int(offs[i]):int(offs[i + 1])].astype(BF16) for i in range(len(sizes))]
    out_dtypes = (BF16, BF16, BF16, F32, F32, BF16, F32, F32)
    row = lambda n: pl.BlockSpec((tm, n), lambda i: (i, 0))
    full = lambda n: pl.BlockSpec((D, n), lambda i: (0, 0))
    return pl.pallas_call(
        _in_proj_kernel,
        grid=(T // tm,),
        in_specs=[row(D), pl.BlockSpec((1, D), lambda i: (0, 0))] + [full(n) for n in sizes],
        out_specs=[row(n) for n in sizes],
        out_shape=[jax.ShapeDtypeStruct((T, n), dt) for n, dt in zip(sizes, out_dtypes)],
        compiler_params=_params("parallel"),
        name="in_proj",
    )(x2, ln_g.reshape(1, D), *ws)


SB_ZERO_WEIGHT_BITS = 160.0


def _softplus2(z):
    return jnp.maximum(z, 0.0) + jnp.log2(1.0 + jnp.exp2(-jnp.abs(z)))


def _sb_query_block(q_ref, k_ref, v_ref, o_ref, qh_ref, acc_ref, carry_ref, *, row0, qi, tq, n_pairs,
                    alongside=None):
    qrows = slice(row0, row0 + tq)
    r = lax.broadcasted_iota(jnp.int32, (tq, tq), 0)
    c = lax.broadcasted_iota(jnp.int32, (tq, tq), 1)
    causal = c < r
    later = (r > c).astype(BF16)
    head0 = lax.broadcasted_iota(jnp.int32, (1, LANES), 1) < SB_HEAD_DIM

    for p in range(n_pairs):
        q2 = q_ref[qrows, p * LANES:(p + 1) * LANES]
        zero = jnp.zeros_like(q2)
        qh_ref[p, 0:tq] = jnp.where(head0, q2, zero)
        qh_ref[p, tq:2 * tq] = jnp.where(head0, zero, q2)
    acc_ref[...] = jnp.zeros_like(acc_ref)
    carry_ref[...] = jnp.zeros_like(carry_ref)
    causal2 = jnp.concatenate([causal, causal], axis=0)

    def key_block(kb, diag):
        rows = pl.ds(pl.multiple_of(kb * tq, tq), tq)
        pairs = range(n_pairs)
        zs = {}

        def scores(p):
            zs[p] = _dot_nt(qh_ref[p], k_ref[rows, pl.ds(p * LANES, LANES)])
        scores(0)
        scores(1)

        def suffix_stage(p):
            sp = _softplus2(zs[p])
            if diag:
                sp = jnp.where(causal2, sp, 0.0)
            return sp, _dot(sp.astype(BF16), later)

        def weight_stage(p, sp, run):
            z = zs[p]
            v2 = v_ref[rows, pl.ds(p * LANES, LANES)]
            vzero = jnp.zeros_like(v2)
            v_stack = jnp.concatenate([jnp.where(head0, v2, vzero), jnp.where(head0, vzero, v2)], axis=0)
            carry = carry_ref[p]
            w = jnp.exp2((z - sp) - run - jnp.concatenate([carry] * (tq // LANES), axis=1))
            if diag:
                w = jnp.where(causal2, w, 0.0)
            w = w.astype(BF16)
            acc_ref[p] += _dot(jnp.concatenate([w[0:tq], w[tq:2 * tq]], axis=1), v_stack)
            carry_ref[p] = carry + jnp.broadcast_to(run[:, 0:1] + sp[:, 0:1], (2 * tq, LANES))

        pending = None
        for p in pairs:
            if p + 2 < n_pairs:
                scores(p + 2)
            staged = suffix_stage(p)
            if pending is not None:
                weight_stage(*pending)
            pending = (p,) + staged
        weight_stage(*pending)

    def min_carry():
        m = carry_ref[0]
        for p in range(1, n_pairs):
            m = jnp.minimum(m, carry_ref[p])
        return jnp.min(m)

    def cond(state):
        kb, m = state
        return jnp.logical_and(kb >= 0, m < SB_ZERO_WEIGHT_BITS)

    def body(state):
        kb, _ = state
        key_block(kb, False)
        return kb - 1, min_carry()

    if alongside is not None:
        alongside()
    key_block(qi, True)
    lax.while_loop(cond, body, (qi - 1, min_carry()))
    for p in range(n_pairs):
        o_ref[qrows, p * LANES:(p + 1) * LANES] = acc_ref[p].astype(o_ref.dtype)


GLA_CHUNK = 64
GLA_LEVELS = (32, 16, 8, 4, 2, 1)


def _gla_tables(C=GLA_CHUNK):
    t = np.arange(C)[:, None]
    j = np.arange(C)[None, :]
    blocks = [(j <= t), (j > t)]
    masks = []
    for half in GLA_LEVELS:
        blk = 2 * half
        u = (t // blk) * blk + half - 1
        second = (t % blk) >= half
        blocks.append((second & (j > u) & (j <= t)) | ((~second) & (j > t) & (j <= u)))
        s = j
        masks.append(second & ((s % blk) < half) & ((s // blk) == (t // blk)))
    blocks.append(np.ones((8, C), bool))
    table = np.concatenate(blocks, axis=0).astype(np.float32)
    table = np.concatenate([table, table], axis=1)
    return table, np.concatenate(masks, axis=0).astype(np.float32)


def _split_bf16(a):
    hi = a.astype(BF16)
    lo = (a - hi.astype(F32)).astype(BF16)
    return hi, lo


def _gla_head_pair(p, gq_ref, gk_ref, gv_ref, gr_ref, lr_ref, w2_ref, gb_ref, ng_ref, tab_ref, msk_ref,
                   o_ref, state_ref, *, n_chunks):
    C = GLA_CHUNK
    n_lv = len(GLA_LEVELS)

    head0 = lax.broadcasted_iota(jnp.int32, (1, LANES), 1) < GLA_DK
    eye = (lax.broadcasted_iota(jnp.int32, (C, C), 0) == lax.broadcasted_iota(jnp.int32, (C, C), 1))
    w2_hi, w2_lo = _split_bf16(w2_ref[...])
    lr_hi, lr_lo = _split_bf16(lr_ref[...])

    def stack_heads(a):
        return jnp.concatenate([jnp.where(head0, a, 0.0), jnp.where(head0, 0.0, a)], axis=0)

    cols = slice(p * LANES, (p + 1) * LANES)
    pre = (_dot(lr_hi, w2_hi[:, cols]) + _dot(lr_hi, w2_lo[:, cols])
           + _dot(lr_lo, w2_hi[:, cols])) + gb_ref[:, cols]
    log_a = -_softplus(-pre) * (1.0 / GLA_GATE_TAU)
    la_hi, la_lo = _split_bf16(log_a)
    la = jnp.concatenate(
        [jnp.concatenate([la_hi[c * C:(c + 1) * C], la_lo[c * C:(c + 1) * C]], axis=0)
         for c in range(n_chunks)], axis=1)
    ex_all = jnp.exp(_dot(tab_ref[...], la))
    st = state_ref[p]
    def score_stage(c):
        rows = slice(c * C, (c + 1) * C)
        ex = ex_all[:, c * LANES:(c + 1) * LANES]
        q2 = gq_ref[rows, cols]
        k2 = gk_ref[rows, cols]
        q_int = q2 * ex[0:C]
        k_dec = (k2 * ex[C:2 * C]).astype(BF16)
        dec_row = ex[(2 + n_lv) * C:(2 + n_lv) * C + 1]
        diag = jnp.sum(stack_heads(q2 * k2), axis=1, keepdims=True)
        scores = jnp.where(jnp.concatenate([eye, eye], axis=0), diag, 0.0)
        for l in range(n_lv):
            ex_l = ex[(2 + l) * C:(3 + l) * C]
            q_l = stack_heads(q2 * ex_l).astype(BF16)
            k_l = (k2 * ex_l).astype(BF16)
            m_l = msk_ref[l * C:(l + 1) * C, :] > 0.5
            scores = scores + jnp.where(jnp.concatenate([m_l, m_l], axis=0), _dot_nt(q_l, k_l), 0.0)
        return c, q_int, k_dec, dec_row, scores.astype(BF16)

    def state_stage(st, c, q_int, k_dec, dec_row, scores):
        rows = slice(c * C, (c + 1) * C)
        o_inter = _dot_nt(stack_heads(q_int).astype(BF16), st.astype(BF16))
        upd = []
        for h in range(2):
            vcols = slice((2 * p + h) * GLA_DV, (2 * p + h + 1) * GLA_DV)
            v_h = gv_ref[rows, vcols]
            o = o_inter[h * C:(h + 1) * C] + _dot(scores[h * C:(h + 1) * C], v_h)
            inv = lax.rsqrt(jnp.mean(o * o, axis=-1, keepdims=True) + RMS_EPS)
            y = (o * inv) * ng_ref[:, vcols]
            g = gr_ref[rows, vcols]
            y = y * (g / (1.0 + jnp.exp(-g)))
            o_ref[rows, vcols] = y.astype(o_ref.dtype)
            v_t = v_h.astype(F32).T.astype(BF16)
            upd.append(_dot(v_t, k_dec))
        return st * dec_row + jnp.where(head0, upd[0], upd[1])

    pending = score_stage(0)
    for c in range(1, n_chunks):
        ahead = score_stage(c)
        st = state_stage(st, *pending)
        pending = ahead
    st = state_stage(st, *pending)
    state_ref[p] = st


def _mixers_kernel(q_ref, k_ref, v_ref, gq_ref, gk_ref, gv_ref, gr_ref, lr_ref, w2_ref, gb_ref, ng_ref,
                   tab_ref, msk_ref, osb_ref, ogla_ref, qh_ref, acc_ref, carry_ref, state_ref,
                   *, tq, n_pairs, n_chunks):
    blk = pl.program_id(1)
    n_q = q_ref.shape[0] // tq
    assert n_q == GLA_HEADS // 2

    @pl.when(blk == 0)
    def _():
        state_ref[...] = jnp.zeros_like(state_ref)

    for half in range(n_q):
        gla = functools.partial(_gla_head_pair, half, gq_ref, gk_ref, gv_ref, gr_ref, lr_ref, w2_ref, gb_ref,
                                ng_ref, tab_ref, msk_ref, ogla_ref, state_ref, n_chunks=n_chunks)
        _sb_query_block(q_ref, k_ref, v_ref, osb_ref, qh_ref, acc_ref, carry_ref, row0=half * tq,
                        qi=blk * n_q + half, tq=tq, n_pairs=n_pairs, alongside=gla)


def _mixers(q, k, v, gq, gk, gv, gr, lr, w2, gb, ng, B, S, tb, tq):
    table, masks = _gla_tables()
    table = jnp.asarray(table, BF16)
    masks = jnp.asarray(masks, F32)
    T = B * S
    W = q.shape[-1]
    n_pairs = W // LANES
    nblk = S // tb
    q3, k3, v3 = (a.reshape(B, S, W) for a in (q, k, v))
    blk3 = pl.BlockSpec((None, tb, W), lambda b, i: (b, i, 0))
    seq3 = pl.BlockSpec((None, S, W), lambda b, i: (b, 0, 0))
    row = lambda n: pl.BlockSpec((tb, n), lambda b, i: (b * nblk + i, 0))
    const = lambda a: pl.BlockSpec(a.shape, lambda b, i: (0,) * a.ndim)
    w2 = w2.astype(F32)
    gb2 = gb.reshape(1, GLA_KW)
    ng2 = ng.reshape(1, GLA_VW)
    o_sb, o_gla = pl.pallas_call(
        functools.partial(_mixers_kernel, tq=tq, n_pairs=n_pairs, n_chunks=tb // GLA_CHUNK),
        grid=(B, nblk),
        in_specs=[blk3, seq3, seq3, row(GLA_KW), row(GLA_KW), row(GLA_VW), row(GLA_VW), row(GLA_GATE_RANK),
                  const(w2), const(gb2), const(ng2), const(table), const(masks)],
        out_specs=[blk3, row(GLA_VW)],
        out_shape=[jax.ShapeDtypeStruct((B, S, W), BF16), jax.ShapeDtypeStruct((T, GLA_VW), BF16)],
        scratch_shapes=[pltpu.VMEM((n_pairs, 2 * tq, LANES), BF16),
                        pltpu.VMEM((n_pairs, tq, LANES), F32),
                        pltpu.VMEM((n_pairs, 2 * tq, LANES), F32),
                        pltpu.VMEM((GLA_HEADS // 2, GLA_DV, 2 * GLA_DK), F32)],
        compiler_params=_params("parallel", "arbitrary"),
        name="mixers",
    )(q3, k3, v3, gq, gk, gv, gr, lr, w2, gb2, ng2, table, masks)
    return o_sb.reshape(T, W), o_gla


SUBLANES = 8


def _store_token_tiles(ref, value, accumulate=False):
    n = value.shape[0]
    for s in range(SUBLANES):
        rows = pl.ds(s, n, stride=SUBLANES)
        chunk = value[:, s * LANES:(s + 1) * LANES]
        ref[rows, :] = ref[rows, :] + chunk if accumulate else chunk


def _load_token_tiles(ref, n):
    return jnp.concatenate([ref[pl.ds(s, n, stride=SUBLANES), :] for s in range(SUBLANES)], axis=1)


def _out_proj_kernel(x_ref, oa_ref, ob_ref, wa_ref, wb_ref, g_ref, wr_ref, br_ref, tri_ref,
                     x1_ref, h2_ref, ri_ref, rf_ref, cnt_ref):
    @pl.when(pl.program_id(0) == 0)
    def _():
        cnt_ref[...] = jnp.zeros_like(cnt_ref)

    half = tri_ref.shape[0]
    halves = [slice(i * half, (i + 1) * half) for i in range(x_ref.shape[0] // half)]
    w_hi, w_lo = _split_bf16(wr_ref[...])
    w_cat = jnp.concatenate([w_hi, w_lo], axis=1)
    lane = lax.broadcasted_iota(jnp.int32, (half, LANES), 1).astype(F32)
    neg = -jnp.inf

    def residual_stage(rows):
        x1 = x_ref[rows, :] + _dot(oa_ref[rows, :], wa_ref[...]) + _dot(ob_ref[rows, :], wb_ref[...])
        x1_ref[rows, :] = x1
        inv = lax.rsqrt(jnp.mean(x1 * x1, axis=-1, keepdims=True) + RMS_EPS)
        h2 = (x1 * inv) * g_ref[...]
        _store_token_tiles(h2_ref.at[pl.ds(rows.start * SUBLANES, half * SUBLANES), :], h2)
        return _split_bf16(h2)

    def logits_stage(h_hi, h_lo):
        hw = _dot(h_hi, w_cat)
        return hw[:, :LANES] + hw[:, LANES:] + _dot(h_lo, w_hi) + br_ref[...]

    def first_max(vals):
        m = jnp.max(vals, axis=1, keepdims=True)
        idx = jnp.min(jnp.where(vals == m, lane, float(LANES)), axis=1, keepdims=True)
        return m, idx

    def select_stage(rows, logits):
        is_group = lane < N_GROUPS
        g_max, g_idx = first_max(jnp.where(is_group, logits, neg))
        g_p = 1.0 / jnp.sum(jnp.where(is_group, jnp.exp(logits - g_max), 0.0), axis=1, keepdims=True)
        lo_lane = N_GROUPS + EXPERTS_PER_GROUP * g_idx
        sel = jnp.where((lane >= lo_lane) & (lane < lo_lane + EXPERTS_PER_GROUP), logits, neg)
        m1, i1 = first_max(sel)
        m2, i2 = first_max(jnp.where(lane == i1, neg, sel))
        e = jnp.exp(m2 - m1)
        gate1 = g_p / (1.0 + e)
        gate2 = g_p * e / (1.0 + e)
        rf_ref[rows, :] = jnp.where(lane == 0, gate1, jnp.where(lane == 1, gate2, 0.0))
        e1 = i1 - N_GROUPS
        e2 = i2 - N_GROUPS
        hits = (lane == e1).astype(F32) + (lane == e2).astype(F32)
        return e1, e2, hits

    pieces = [residual_stage(rows) for rows in halves]
    logits = [logits_stage(*hl) for hl in pieces]
    chosen = [select_stage(rows, lg) for rows, lg in zip(halves, logits)]

    running = cnt_ref[...]
    for i, (e1, e2, hits) in enumerate(chosen):
        before = _dot(tri_ref[...], hits.astype(BF16)) + running
        r1 = jnp.sum(jnp.where(lane == e1, before, 0.0), axis=1, keepdims=True)
        r2 = jnp.sum(jnp.where(lane == e2, before, 0.0), axis=1, keepdims=True)
        running = running + jnp.sum(hits, axis=0, keepdims=True)
        packed = jnp.where(lane == 0, e1, jnp.where(lane == 1, e2,
                           jnp.where(lane == 2, r1, jnp.where(lane == 3, r2, 0.0))))
        for j in range(half // LANES):
            rows_t = packed[j * LANES:(j + 1) * LANES, :].T
            jj = i * (half // LANES) + j
            for k in range(2 * TOP_K):
                ri_ref[k, jj:jj + 1, :] = rows_t[k:k + 1, :].astype(jnp.int32)
    cnt_ref[...] = running


def _out_proj(x2, o_sb, o_gla, w_out, ln_g, w_group, b_group, w_expert, b_expert, tm):
    T, D = x2.shape
    wa = w_out[:SB_WIDTH].astype(BF16)
    wb = w_out[SB_WIDTH:].astype(BF16)
    n_r = N_GROUPS + N_EXPERTS
    wr = jnp.zeros((D, LANES), F32).at[:, :n_r].set(jnp.concatenate([w_group, w_expert], axis=1))
    br = jnp.zeros((1, LANES), F32).at[0, :n_r].set(jnp.concatenate([b_group, b_expert]))
    row = lambda n: pl.BlockSpec((tm, n), lambda i: (i, 0))
    const = lambda a: pl.BlockSpec(a.shape, lambda i: (0,) * a.ndim)
    g2 = ln_g.reshape(1, D)
    half = tm // 2
    tri = jnp.asarray(np.tril(np.ones((half, half), np.float32), -1), BF16)
    return pl.pallas_call(
        _out_proj_kernel,
        grid=(T // tm,),
        in_specs=[row(D), row(SB_WIDTH), row(GLA_VW), const(wa), const(wb), const(g2), const(wr), const(br),
                  const(tri)],
        out_specs=[row(D), pl.BlockSpec((tm * SUBLANES, LANES), lambda i: (i, 0)),
                   pl.BlockSpec((2 * TOP_K, tm // LANES, LANES), lambda i: (0, i, 0)), row(LANES),
                   pl.BlockSpec((1, LANES), lambda i: (0, 0))],
        out_shape=[jax.ShapeDtypeStruct((T, D), F32), jax.ShapeDtypeStruct((T * SUBLANES, LANES), F32),
                   jax.ShapeDtypeStruct((2 * TOP_K, T // LANES, LANES), jnp.int32),
                   jax.ShapeDtypeStruct((T, LANES), F32), jax.ShapeDtypeStruct((1, LANES), F32)],
        compiler_params=_params("arbitrary"),
        name="out_proj",
    )(x2, o_sb, o_gla, wa, wb, g2, wr, br, tri)


def _route_tables(route, counts, n_rows, tm):
    counts = counts[0, :N_EXPERTS].astype(jnp.int32)
    ends = jnp.cumsum(counts)
    starts = ends - counts
    route = route.reshape(2 * TOP_K, -1)
    expert = route[:TOP_K, :, None] == jnp.arange(N_EXPERTS, dtype=jnp.int32)
    dest = (jnp.sum(jnp.where(expert, starts, 0), axis=-1) + route[TOP_K:]).reshape(-1)

    n_items = n_rows // tm + N_EXPERTS + EXPERT_GATHER_DEPTH
    items_e = (counts + tm - 1) // tm
    item_hi = jnp.cumsum(items_e)
    item_lo = item_hi - items_e
    i = jnp.arange(n_items, dtype=jnp.int32)
    e_of = jnp.minimum(jnp.sum(item_hi[None, :] <= i[:, None], axis=1), N_EXPERTS - 1)
    owner = e_of[:, None] == jnp.arange(N_EXPERTS, dtype=jnp.int32)
    pick = lambda tab: jnp.sum(jnp.where(owner, tab, 0), axis=-1)
    item_row = jnp.where(i < item_hi[-1], pick(starts) + (i - pick(item_lo)) * tm, n_rows)
    return (dest, item_row.astype(jnp.int32), item_lo.astype(jnp.int32), item_hi.astype(jnp.int32))


ROW_ISSUE_UNROLL = 8
EXPERT_GATHER_DEPTH = 4


def _token_tile(ref, t):
    return ref.at[pl.ds(pl.multiple_of(t * SUBLANES, SUBLANES), SUBLANES), :]


def _wait_token_gather(src_ref, dst_ref, sem):
    pltpu.make_async_copy(src_ref.at[pl.ds(0, dst_ref.shape[0]), :], dst_ref, sem).wait()


def _expert_kernel(row_ref, lo_ref, hi_ref, dest_ref,
                   h_ref, wg_ref, wu_ref, wd_ref, y_ref, *scratch, tm):
    depth = EXPERT_GATHER_DEPTH
    xbufs = scratch[:depth]
    ybuf, tok_ref, wg_bf, wu_bf, wd_bf, gsem, wsem = scratch[depth:]
    e = pl.program_id(0)
    n_rows = dest_ref.shape[0]
    n_tok = n_rows // TOP_K
    total = hi_ref[N_EXPERTS - 1]

    def gather_copy(i, r, slot):
        t = tok_ref[row_ref[i] + r]
        return pltpu.make_async_copy(_token_tile(h_ref, t), _token_tile(xbufs[slot], r), gsem.at[slot])

    def out_copy(row, slot):
        rows = pl.ds(pl.multiple_of(row * SUBLANES, SUBLANES), tm * SUBLANES)
        return pltpu.make_async_copy(ybuf.at[slot], y_ref.at[rows, :], wsem)

    @pl.when(e == 0)
    def _():
        def invert(t, _):
            for k in range(TOP_K):
                tok_ref[dest_ref[k * n_tok + t]] = t
            return 0
        lax.fori_loop(0, n_tok, invert, 0, unroll=ROW_ISSUE_UNROLL)

        def spare(r, _):
            tok_ref[n_rows + r] = 0
            return 0
        lax.fori_loop(0, tm, spare, 0, unroll=ROW_ISSUE_UNROLL)
        ybuf[1] = jnp.zeros(ybuf.shape[1:], ybuf.dtype)
        out_copy(n_rows, 1).start()
        for ahead in range(depth - 1):
            def first(r, _):
                gather_copy(ahead, r, ahead).start()
                return 0
            lax.fori_loop(0, tm, first, 0, unroll=ROW_ISSUE_UNROLL)

    wg_bf[...] = wg_ref[...].astype(BF16)
    wu_bf[...] = wu_ref[...].astype(BF16)
    wd_bf[...] = wd_ref[...].astype(BF16)

    def item_in_slot(i, slot):
        _wait_token_gather(h_ref, xbufs[slot], gsem.at[slot])
        nslot = (slot + depth - 1) % depth
        for r in range(tm):
            gather_copy(i + depth - 1, r, nslot).start(priority=1)
        xb = _load_token_tiles(xbufs[slot], tm).astype(BF16)
        g = _dot(xb, wg_bf[...])
        u = _dot(xb, wu_bf[...])
        hidden = ((g / (1.0 + jnp.exp(-g))) * u).astype(BF16)
        _store_token_tiles(ybuf.at[slot % 2], _dot(hidden, wd_bf[...]))
        out_copy(0, slot % 2).wait()
        out_copy(row_ref[i], slot % 2).start()

    def item(i, _):
        slot = lax.rem(i, depth)
        for s in range(depth):
            @pl.when(slot == s)
            def _():
                item_in_slot(i, s)
        return 0

    lax.fori_loop(lo_ref[e], hi_ref[e], item, 0)

    @pl.when(e == pl.num_programs(0) - 1)
    def _():
        out_copy(0, 0).wait()
        for ahead in range(depth - 1):
            for s in range(depth):
                @pl.when(lax.rem(total + ahead, depth) == s)
                def _():
                    _wait_token_gather(h_ref, xbufs[s], gsem.at[s])


def _expert_ffn(h2_tiles, dest, tables, w_gate, w_up, w_down, tm):
    N = dest.shape[0]
    _, D, F = w_gate.shape
    assert D == SUBLANES * LANES
    item_row, item_lo, item_hi = tables
    wspec = lambda shape: pl.BlockSpec((None,) + shape, lambda e, *_: (e, 0, 0))
    return pl.pallas_call(
        functools.partial(_expert_kernel, tm=tm),
        grid_spec=pltpu.PrefetchScalarGridSpec(
            num_scalar_prefetch=4,
            grid=(N_EXPERTS,),
            in_specs=[pl.BlockSpec(memory_space=pl.ANY), wspec((D, F)), wspec((D, F)), wspec((F, D))],
            out_specs=pl.BlockSpec(memory_space=pl.ANY),
            scratch_shapes=[pltpu.VMEM((tm * SUBLANES, LANES), F32) for _ in range(EXPERT_GATHER_DEPTH)]
                           + [pltpu.VMEM((2, tm * SUBLANES, LANES), F32),
                            pltpu.SMEM((N + tm,), jnp.int32),
                            pltpu.VMEM((D, F), BF16), pltpu.VMEM((D, F), BF16), pltpu.VMEM((F, D), BF16),
                            pltpu.SemaphoreType.DMA((EXPERT_GATHER_DEPTH,)), pltpu.SemaphoreType.DMA(())],
        ),
        out_shape=jax.ShapeDtypeStruct(((N + tm) * SUBLANES, LANES), F32),
        compiler_params=_params("arbitrary"),
        name="expert_ffn",
    )(item_row, item_lo, item_hi, dest, h2_tiles, w_gate, w_up, w_down)


COMBINE_RING = 4
COMBINE_AHEAD = 2


def _combine_kernel(dest_ref, x1_ref, rf_ref, g_ref, y_ref, o_ref, *scratch, tb):
    bufs, sem = scratch[:COMBINE_RING], scratch[COMBINE_RING]
    i = pl.program_id(0)
    n_steps = pl.num_programs(0)
    n_tok = n_steps * COMBINE_RING * tb

    def copies(j, slot, r):
        return [pltpu.make_async_copy(_token_tile(y_ref, dest_ref[k * n_tok + j * tb + r]),
                                      _token_tile(bufs[slot].at[k], r), sem.at[slot])
                for k in range(TOP_K)]

    @pl.when(i == 0)
    def _():
        for slot in range(COMBINE_AHEAD):
            def first(r, _):
                for c in copies(slot, slot, r):
                    c.start()
                return 0
            lax.fori_loop(0, tb, first, 0, unroll=ROW_ISSUE_UNROLL // TOP_K)

    def sub_block(s, fetch):
        j = i * COMBINE_RING + s
        for k in range(TOP_K):
            _wait_token_gather(y_ref, bufs[s].at[k], sem.at[s])
        if fetch:
            for r in range(tb):
                for k, c in enumerate(copies(j + COMBINE_AHEAD, (s + COMBINE_AHEAD) % COMBINE_RING, r)):
                    c.start(priority=k)
        rows = slice(s * tb, (s + 1) * tb)
        rf = rf_ref[rows, :]
        x = (x1_ref[rows, :] + rf[:, 0:1] * _load_token_tiles(bufs[s].at[0], tb)
             + rf[:, 1:2] * _load_token_tiles(bufs[s].at[1], tb))
        inv = lax.rsqrt(jnp.mean(x * x, axis=-1, keepdims=True) + RMS_EPS)
        o_ref[rows, :] = (x * inv) * g_ref[...]

    for s in range(COMBINE_RING):
        if s + COMBINE_AHEAD < COMBINE_RING:
            sub_block(s, True)
        else:
            @pl.when(i + 1 < n_steps)
            def _():
                sub_block(s, True)

            @pl.when(i + 1 >= n_steps)
            def _():
                sub_block(s, False)


def _combine(x1, route_f, y_tiles, dest, ln_g, tb):
    T, D = x1.shape
    g2 = ln_g.reshape(1, D)
    step = COMBINE_RING * tb
    return pl.pallas_call(
        functools.partial(_combine_kernel, tb=tb),
        grid_spec=pltpu.PrefetchScalarGridSpec(
            num_scalar_prefetch=1,
            grid=(T // step,),
            in_specs=[pl.BlockSpec((step, D), lambda i, d: (i, 0)),
                      pl.BlockSpec((step, LANES), lambda i, d: (i, 0)),
                      pl.BlockSpec((1, D), lambda i, d: (0, 0)),
                      pl.BlockSpec(memory_space=pl.ANY)],
            out_specs=pl.BlockSpec((step, D), lambda i, d: (i, 0)),
            scratch_shapes=[pltpu.VMEM((TOP_K, tb * SUBLANES, LANES), F32) for _ in range(COMBINE_RING)]
                           + [pltpu.SemaphoreType.DMA((COMBINE_RING,))],
        ),
        out_shape=jax.ShapeDtypeStruct((T, D), F32),
        compiler_params=_params("arbitrary"),
        name="combine",
    )(dest, x1, route_f, g2, y_tiles)


MXU_WIDTH = 256


class _Tiles(NamedTuple):
    in_proj: int
    query: int
    mixers: int
    out_proj: int
    expert: int
    combine: int


def _tiles(n_tokens, seq_len):
    query = MXU_WIDTH
    tiles = _Tiles(in_proj=4 * MXU_WIDTH, query=query, mixers=query * (GLA_HEADS // 2),
                   out_proj=SUBLANES * LANES, expert=MXU_WIDTH, combine=MXU_WIDTH)
    assert seq_len % tiles.mixers == 0 and tiles.mixers % GLA_CHUNK == 0
    assert n_tokens % tiles.in_proj == 0 and n_tokens % tiles.out_proj == 0
    assert n_tokens % (COMBINE_RING * tiles.combine) == 0 and (TOP_K * n_tokens) % tiles.expert == 0
    return tiles


def _layer(x2, B, S, ln1_g, w_in, w2, gb, ng, w_out, ln2_g, w_group, b_group, w_expert, b_expert,
           w_gate, w_up, w_down, ln_f_g, tiles):
    q, k, v, gq, gk, gv, gr, lr = _in_proj(x2, ln1_g, w_in, tiles.in_proj)
    o_sb, o_gla = _mixers(q, k, v, gq, gk, gv, gr, lr, w2, gb, ng, B, S, tiles.mixers, tiles.query)
    x1, h2_tiles, route_i, route_f, counts = _out_proj(x2, o_sb, o_gla, w_out, ln2_g, w_group, b_group,
                                                       w_expert, b_expert, tiles.out_proj)
    T = x2.shape[0]
    dest, *tables = _route_tables(route_i, counts, TOP_K * T, tiles.expert)
    y_tiles = _expert_ffn(h2_tiles, dest, tables, w_gate, w_up, w_down, tiles.expert)
    return _combine(x1, route_f, y_tiles, dest, ln_f_g, tiles.combine)


def kernel(x, ln1_g, w_in, gla_gate_w2, gla_gate_b, gla_norm_g, w_out, ln2_g, w_group, b_group, w_expert, b_expert, exp_w_gate, exp_w_up, exp_w_down, ln_f_g):
    B, S, D = x.shape
    assert ln1_g.shape[0] == 1, "single-layer stack"
    assert D == SUBLANES * LANES, "one (SUBLANES, LANES) tile per token"
    out = _layer(x.reshape(B * S, D), B, S, ln1_g[0], w_in[0], gla_gate_w2[0], gla_gate_b[0],
                 gla_norm_g[0], w_out[0], ln2_g[0], w_group[0], b_group[0], w_expert[0], b_expert[0],
                 exp_w_gate[0], exp_w_up[0], exp_w_down[0], ln_f_g, _tiles(B * S, S))
    return out.reshape(B, S, D)
```

```python
import functools
from typing import NamedTuple

import numpy as np
import jax
import jax.numpy as jnp
from jax import lax
from jax.experimental import pallas as pl
from jax.experimental.pallas import tpu as pltpu

SB_HEADS = 8
SB_HEAD_DIM = 64
SB_WIDTH = SB_HEADS * SB_HEAD_DIM
GLA_HEADS = 4
GLA_DK = 64
GLA_DV = 128
GLA_KW = GLA_HEADS * GLA_DK
GLA_VW = GLA_HEADS * GLA_DV
GLA_GATE_RANK = 16
GLA_GATE_TAU = 16.0
N_GROUPS = 4
EXPERTS_PER_GROUP = 8
N_EXPERTS = N_GROUPS * EXPERTS_PER_GROUP
TOP_K = 2
RMS_EPS = 1e-6
LOG2_E = 1.4426950408889634

LANES = 128
VMEM_LIMIT_BYTES = 56 * 1024 * 1024

F32 = jnp.float32
BF16 = jnp.bfloat16


def _dot(a, b):
    return jnp.dot(a, b, preferred_element_type=F32)


def _dot_nt(a, b):
    return lax.dot_general(a, b, (((1,), (1,)), ((), ())), preferred_element_type=F32)


def _softplus(z):
    return jnp.maximum(z, 0.0) + jnp.log(1.0 + jnp.exp(-jnp.abs(z)))


def _params(*sem):
    return pltpu.CompilerParams(dimension_semantics=sem, vmem_limit_bytes=VMEM_LIMIT_BYTES)


def _in_proj_kernel(x_ref, g_ref, wq, wk, wv, wgq, wgk, wgv, wgr, wlr,
                    oq, ok, ov, ogq, ogk, ogv, ogr, olr):
    x = x_ref[...]
    inv = lax.rsqrt(jnp.mean(x * x, axis=-1, keepdims=True) + RMS_EPS)
    h = ((x * inv) * g_ref[...]).astype(BF16)
    oq[...] = (_dot(h, wq[...]) * (SB_HEAD_DIM ** -0.5 * LOG2_E)).astype(oq.dtype)
    ok[...] = _dot(h, wk[...]).astype(ok.dtype)
    ov[...] = _dot(h, wv[...]).astype(ov.dtype)
    ogq[...] = (_dot(h, wgq[...]) * (GLA_DK ** -0.5)).astype(ogq.dtype)
    ogk[...] = _dot(h, wgk[...]).astype(ogk.dtype)
    ogv[...] = _dot(h, wgv[...]).astype(ogv.dtype)
    ogr[...] = _dot(h, wgr[...]).astype(ogr.dtype)
    olr[...] = _dot(h, wlr[...]).astype(olr.dtype)


def _in_proj(x2, ln_g, w_in, tm):
    T, D = x2.shape
    sizes = (SB_WIDTH, SB_WIDTH, SB_WIDTH, GLA_KW, GLA_KW, GLA_VW, GLA_VW, GLA_GATE_RANK)
    offs = np.concatenate([[0], np.cumsum(sizes)])
    ws = [w_in[:, int(offs[i]):int(offs[i + 1])].astype(BF16) for i in range(len(sizes))]
    out_dtypes = (BF16, BF16, BF16, F32, F32, BF16, F32, F32)
    row = lambda n: pl.BlockSpec((tm, n), lambda i: (i, 0))
    full = lambda n: pl.BlockSpec((D, n), lambda i: (0, 0))
    return pl.pallas_call(
        _in_proj_kernel,
        grid=(T // tm,),
        in_specs=[row(D), pl.BlockSpec((1, D), lambda i: (0, 0))] + [full(n) for n in sizes],
        out_specs=[row(n) for n in sizes],
        out_shape=[jax.ShapeDtypeStruct((T, n), dt) for n, dt in zip(sizes, out_dtypes)],
        compiler_params=_params("parallel"),
        name="in_proj",
    )(x2, ln_g.reshape(1, D), *ws)


SB_ZERO_WEIGHT_BITS = 160.0


def _softplus2(z):
    return jnp.maximum(z, 0.0) + jnp.log2(1.0 + jnp.exp2(-jnp.abs(z)))


def _sb_query_block(q_ref, k_ref, v_ref, o_ref, qh_ref, acc_ref, carry_ref, *, row0, qi, tq, n_pairs,
                    alongside=None):
    qrows = slice(row0, row0 + tq)
    r = lax.broadcasted_iota(jnp.int32, (tq, tq), 0)
    c = lax.broadcasted_iota(jnp.int32, (tq, tq), 1)
    causal = c < r
    later = (r > c).astype(BF16)
    head0 = lax.broadcasted_iota(jnp.int32, (1, LANES), 1) < SB_HEAD_DIM

    for p in range(n_pairs):
        q2 = q_ref[qrows, p * LANES:(p + 1) * LANES]
        zero = jnp.zeros_like(q2)
        qh_ref[p, 0:tq] = jnp.where(head0, q2, zero)
        qh_ref[p, tq:2 * tq] = jnp.where(head0, zero, q2)
    acc_ref[...] = jnp.zeros_like(acc_ref)
    carry_ref[...] = jnp.zeros_like(carry_ref)
    causal2 = jnp.concatenate([causal, causal], axis=0)

    def key_block(kb, diag):
        rows = pl.ds(pl.multiple_of(kb * tq, tq), tq)
        pairs = range(n_pairs)
        zs = {}

        def scores(p):
            zs[p] = _dot_nt(qh_ref[p], k_ref[rows, pl.ds(p * LANES, LANES)])
        scores(0)
        scores(1)

        def suffix_stage(p):
            sp = _softplus2(zs[p])
            if diag:
                sp = jnp.where(causal2, sp, 0.0)
            return sp, _dot(sp.astype(BF16), later)

        def weight_stage(p, sp, run):
            z = zs[p]
            v2 = v_ref[rows, pl.ds(p * LANES, LANES)]
            vzero = jnp.zeros_like(v2)
            v_stack = jnp.concatenate([jnp.where(head0, v2, vzero), jnp.where(head0, vzero, v2)], axis=0)
            carry = carry_ref[p]
            w = jnp.exp2((z - sp) - run - jnp.concatenate([carry] * (tq // LANES), axis=1))
            if diag:
                w = jnp.where(causal2, w, 0.0)
            w = w.astype(BF16)
            acc_ref[p] += _dot(jnp.concatenate([w[0:tq], w[tq:2 * tq]], axis=1), v_stack)
            carry_ref[p] = carry + jnp.broadcast_to(run[:, 0:1] + sp[:, 0:1], (2 * tq, LANES))

        pending = None
        for p in pairs:
            if p + 2 < n_pairs:
                scores(p + 2)
            staged = suffix_stage(p)
            if pending is not None:
                weight_stage(*pending)
            pending = (p,) + staged
        weight_stage(*pending)

    def min_carry():
        m = carry_ref[0]
        for p in range(1, n_pairs):
            m = jnp.minimum(m, carry_ref[p])
        return jnp.min(m)

    def cond(state):
        kb, m = state
        return jnp.logical_and(kb >= 0, m < SB_ZERO_WEIGHT_BITS)

    def body(state):
        kb, _ = state
        key_block(kb, False)
        return kb - 1, min_carry()

    if alongside is not None:
        alongside()
    key_block(qi, True)
    lax.while_loop(cond, body, (qi - 1, min_carry()))
    for p in range(n_pairs):
        o_ref[qrows, p * LANES:(p + 1) * LANES] = acc_ref[p].astype(o_ref.dtype)


GLA_CHUNK = 64
GLA_LEVELS = (32, 16, 8, 4, 2, 1)


def _gla_tables(C=GLA_CHUNK):
    t = np.arange(C)[:, None]
    j = np.arange(C)[None, :]
    blocks = [(j <= t), (j > t)]
    masks = []
    for half in GLA_LEVELS:
        blk = 2 * half
        u = (t // blk) * blk + half - 1
        second = (t % blk) >= half
        blocks.append((second & (j > u) & (j <= t)) | ((~second) & (j > t) & (j <= u)))
        s = j
        masks.append(second & ((s % blk) < half) & ((s // blk) == (t // blk)))
    blocks.append(np.ones((8, C), bool))
    table = np.concatenate(blocks, axis=0).astype(np.float32)
    table = np.concatenate([table, table], axis=1)
    return table, np.concatenate(masks, axis=0).astype(np.float32)


def _split_bf16(a):
    hi = a.astype(BF16)
    lo = (a - hi.astype(F32)).astype(BF16)
    return hi, lo


def _gla_head_pair(p, gq_ref, gk_ref, gv_ref, gr_ref, lr_ref, w2_ref, gb_ref, ng_ref, tab_ref, msk_ref,
                   o_ref, state_ref, *, n_chunks):
    C = GLA_CHUNK
    n_lv = len(GLA_LEVELS)

    head0 = lax.broadcasted_iota(jnp.int32, (1, LANES), 1) < GLA_DK
    eye = (lax.broadcasted_iota(jnp.int32, (C, C), 0) == lax.broadcasted_iota(jnp.int32, (C, C), 1))
    w2_hi, w2_lo = _split_bf16(w2_ref[...])
    lr_hi, lr_lo = _split_bf16(lr_ref[...])

    def stack_heads(a):
        return jnp.concatenate([jnp.where(head0, a, 0.0), jnp.where(head0, 0.0, a)], axis=0)

    cols = slice(p * LANES, (p + 1) * LANES)
    pre = (_dot(lr_hi, w2_hi[:, cols]) + _dot(lr_hi, w2_lo[:, cols])
           + _dot(lr_lo, w2_hi[:, cols])) + gb_ref[:, cols]
    log_a = -_softplus(-pre) * (1.0 / GLA_GATE_TAU)
    la_hi, la_lo = _split_bf16(log_a)
    la = jnp.concatenate(
        [jnp.concatenate([la_hi[c * C:(c + 1) * C], la_lo[c * C:(c + 1) * C]], axis=0)
         for c in range(n_chunks)], axis=1)
    ex_all = jnp.exp(_dot(tab_ref[...], la))
    st = state_ref[p]
    def score_stage(c):
        rows = slice(c * C, (c + 1) * C)
        ex = ex_all[:, c * LANES:(c + 1) * LANES]
        q2 = gq_ref[rows, cols]
        k2 = gk_ref[rows, cols]
        q_int = q2 * ex[0:C]
        k_dec = (k2 * ex[C:2 * C]).astype(BF16)
        dec_row = ex[(2 + n_lv) * C:(2 + n_lv) * C + 1]
        diag = jnp.sum(stack_heads(q2 * k2), axis=1, keepdims=True)
        scores = jnp.where(jnp.concatenate([eye, eye], axis=0), diag, 0.0)
        for l in range(n_lv):
            ex_l = ex[(2 + l) * C:(3 + l) * C]
            q_l = stack_heads(q2 * ex_l).astype(BF16)
            k_l = (k2 * ex_l).astype(BF16)
            m_l = msk_ref[l * C:(l + 1) * C, :] > 0.5
            scores = scores + jnp.where(jnp.concatenate([m_l, m_l], axis=0), _dot_nt(q_l, k_l), 0.0)
        return c, q_int, k_dec, dec_row, scores.astype(BF16)

    def state_stage(st, c, q_int, k_dec, dec_row, scores):
        rows = slice(c * C, (c + 1) * C)
        o_inter = _dot_nt(stack_heads(q_int).astype(BF16), st.astype(BF16))
        upd = []
        for h in range(2):
            vcols = slice((2 * p + h) * GLA_DV, (2 * p + h + 1) * GLA_DV)
            v_h = gv_ref[rows, vcols]
            o = o_inter[h * C:(h + 1) * C] + _dot(scores[h * C:(h + 1) * C], v_h)
            inv = lax.rsqrt(jnp.mean(o * o, axis=-1, keepdims=True) + RMS_EPS)
            y = (o * inv) * ng_ref[:, vcols]
            g = gr_ref[rows, vcols]
            y = y * (g / (1.0 + jnp.exp(-g)))
            o_ref[rows, vcols] = y.astype(o_ref.dtype)
            v_t = v_h.astype(F32).T.astype(BF16)
            upd.append(_dot(v_t, k_dec))
        return st * dec_row + jnp.where(head0, upd[0], upd[1])

    pending = score_stage(0)
    for c in range(1, n_chunks):
        ahead = score_stage(c)
        st = state_stage(st, *pending)
        pending = ahead
    st = state_stage(st, *pending)
    state_ref[p] = st


def _mixers_kernel(q_ref, k_ref, v_ref, gq_ref, gk_ref, gv_ref, gr_ref, lr_ref, w2_ref, gb_ref, ng_ref,
                   tab_ref, msk_ref, osb_ref, ogla_ref, qh_ref, acc_ref, carry_ref, state_ref,
                   *, tq, n_pairs, n_chunks):
    blk = pl.program_id(1)
    n_q = q_ref.shape[0] // tq
    assert n_q == GLA_HEADS // 2

    @pl.when(blk == 0)
    def _():
        state_ref[...] = jnp.zeros_like(state_ref)

    for half in range(n_q):
        gla = functools.partial(_gla_head_pair, half, gq_ref, gk_ref, gv_ref, gr_ref, lr_ref, w2_ref, gb_ref,
                                ng_ref, tab_ref, msk_ref, ogla_ref, state_ref, n_chunks=n_chunks)
        _sb_query_block(q_ref, k_ref, v_ref, osb_ref, qh_ref, acc_ref, carry_ref, row0=half * tq,
                        qi=blk * n_q + half, tq=tq, n_pairs=n_pairs, alongside=gla)


def _mixers(q, k, v, gq, gk, gv, gr, lr, w2, gb, ng, B, S, tb, tq):
    table, masks = _gla_tables()
    table = jnp.asarray(table, BF16)
    masks = jnp.asarray(masks, F32)
    T = B * S
    W = q.shape[-1]
    n_pairs = W // LANES
    nblk = S // tb
    q3, k3, v3 = (a.reshape(B, S, W) for a in (q, k, v))
    blk3 = pl.BlockSpec((None, tb, W), lambda b, i: (b, i, 0))
    seq3 = pl.BlockSpec((None, S, W), lambda b, i: (b, 0, 0))
    row = lambda n: pl.BlockSpec((tb, n), lambda b, i: (b * nblk + i, 0))
    const = lambda a: pl.BlockSpec(a.shape, lambda b, i: (0,) * a.ndim)
    w2 = w2.astype(F32)
    gb2 = gb.reshape(1, GLA_KW)
    ng2 = ng.reshape(1, GLA_VW)
    o_sb, o_gla = pl.pallas_call(
        functools.partial(_mixers_kernel, tq=tq, n_pairs=n_pairs, n_chunks=tb // GLA_CHUNK),
        grid=(B, nblk),
        in_specs=[blk3, seq3, seq3, row(GLA_KW), row(GLA_KW), row(GLA_VW), row(GLA_VW), row(GLA_GATE_RANK),
                  const(w2), const(gb2), const(ng2), const(table), const(masks)],
        out_specs=[blk3, row(GLA_VW)],
        out_shape=[jax.ShapeDtypeStruct((B, S, W), BF16), jax.ShapeDtypeStruct((T, GLA_VW), BF16)],
        scratch_shapes=[pltpu.VMEM((n_pairs, 2 * tq, LANES), BF16),
                        pltpu.VMEM((n_pairs, tq, LANES), F32),
                        pltpu.VMEM((n_pairs, 2 * tq, LANES), F32),
                        pltpu.VMEM((GLA_HEADS // 2, GLA_DV, 2 * GLA_DK), F32)],
        compiler_params=_params("parallel", "arbitrary"),
        name="mixers",
    )(q3, k3, v3, gq, gk, gv, gr, lr, w2, gb2, ng2, table, masks)
    return o_sb.reshape(T, W), o_gla


SUBLANES = 8


def _store_token_tiles(ref, value, accumulate=False):
    n = value.shape[0]
    for s in range(SUBLANES):
        rows = pl.ds(s, n, stride=SUBLANES)
        chunk = value[:, s * LANES:(s + 1) * LANES]
        ref[rows, :] = ref[rows, :] + chunk if accumulate else chunk


def _load_token_tiles(ref, n):
    return jnp.concatenate([ref[pl.ds(s, n, stride=SUBLANES), :] for s in range(SUBLANES)], axis=1)


def _out_proj_kernel(x_ref, oa_ref, ob_ref, wa_ref, wb_ref, g_ref, wr_ref, br_ref, tri_ref,
                     x1_ref, h2_ref, ri_ref, rf_ref, cnt_ref):
    @pl.when(pl.program_id(0) == 0)
    def _():
        cnt_ref[...] = jnp.zeros_like(cnt_ref)

    half = tri_ref.shape[0]
    halves = [slice(i * half, (i + 1) * half) for i in range(x_ref.shape[0] // half)]
    w_hi, w_lo = _split_bf16(wr_ref[...])
    w_cat = jnp.concatenate([w_hi, w_lo], axis=1)
    lane = lax.broadcasted_iota(jnp.int32, (half, LANES), 1).astype(F32)
    neg = -jnp.inf

    def residual_stage(rows):
        x1 = x_ref[rows, :] + _dot(oa_ref[rows, :], wa_ref[...]) + _dot(ob_ref[rows, :], wb_ref[...])
        x1_ref[rows, :] = x1
        inv = lax.rsqrt(jnp.mean(x1 * x1, axis=-1, keepdims=True) + RMS_EPS)
        h2 = (x1 * inv) * g_ref[...]
        _store_token_tiles(h2_ref.at[pl.ds(rows.start * SUBLANES, half * SUBLANES), :], h2)
        return _split_bf16(h2)

    def logits_stage(h_hi, h_lo):
        hw = _dot(h_hi, w_cat)
        return hw[:, :LANES] + hw[:, LANES:] + _dot(h_lo, w_hi) + br_ref[...]

    def first_max(vals):
        m = jnp.max(vals, axis=1, keepdims=True)
        idx = jnp.min(jnp.where(vals == m, lane, float(LANES)), axis=1, keepdims=True)
        return m, idx

    def select_stage(rows, logits):
        is_group = lane < N_GROUPS
        g_max, g_idx = first_max(jnp.where(is_group, logits, neg))
        g_p = 1.0 / jnp.sum(jnp.where(is_group, jnp.exp(logits - g_max), 0.0), axis=1, keepdims=True)
        lo_lane = N_GROUPS + EXPERTS_PER_GROUP * g_idx
        sel = jnp.where((lane >= lo_lane) & (lane < lo_lane + EXPERTS_PER_GROUP), logits, neg)
        m1, i1 = first_max(sel)
        m2, i2 = first_max(jnp.where(lane == i1, neg, sel))
        e = jnp.exp(m2 - m1)
        gate1 = g_p / (1.0 + e)
        gate2 = g_p * e / (1.0 + e)
        rf_ref[rows, :] = jnp.where(lane == 0, gate1, jnp.where(lane == 1, gate2, 0.0))
        e1 = i1 - N_GROUPS
        e2 = i2 - N_GROUPS
        hits = (lane == e1).astype(F32) + (lane == e2).astype(F32)
        return e1, e2, hits

    pieces = [residual_stage(rows) for rows in halves]
    logits = [logits_stage(*hl) for hl in pieces]
    chosen = [select_stage(rows, lg) for rows, lg in zip(halves, logits)]

    running = cnt_ref[...]
    for i, (e1, e2, hits) in enumerate(chosen):
        before = _dot(tri_ref[...], hits.astype(BF16)) + running
        r1 = jnp.sum(jnp.where(lane == e1, before, 0.0), axis=1, keepdims=True)
        r2 = jnp.sum(jnp.where(lane == e2, before, 0.0), axis=1, keepdims=True)
        running = running + jnp.sum(hits, axis=0, keepdims=True)
        packed = jnp.where(lane == 0, e1, jnp.where(lane == 1, e2,
                           jnp.where(lane == 2, r1, jnp.where(lane == 3, r2, 0.0))))
        for j in range(half // LANES):
            rows_t = packed[j * LANES:(j + 1) * LANES, :].T
            jj = i * (half // LANES) + j
            for k in range(2 * TOP_K):
                ri_ref[k, jj:jj + 1, :] = rows_t[k:k + 1, :].astype(jnp.int32)
    cnt_ref[...] = running


def _out_proj(x2, o_sb, o_gla, w_out, ln_g, w_group, b_group, w_expert, b_expert, tm):
    T, D = x2.shape
    wa = w_out[:SB_WIDTH].astype(BF16)
    wb = w_out[SB_WIDTH:].astype(BF16)
    n_r = N_GROUPS + N_EXPERTS
    wr = jnp.zeros((D, LANES), F32).at[:, :n_r].set(jnp.concatenate([w_group, w_expert], axis=1))
    br = jnp.zeros((1, LANES), F32).at[0, :n_r].set(jnp.concatenate([b_group, b_expert]))
    row = lambda n: pl.BlockSpec((tm, n), lambda i: (i, 0))
    const = lambda a: pl.BlockSpec(a.shape, lambda i: (0,) * a.ndim)
    g2 = ln_g.reshape(1, D)
    half = tm // 2
    tri = jnp.asarray(np.tril(np.ones((half, half), np.float32), -1), BF16)
    return pl.pallas_call(
        _out_proj_kernel,
        grid=(T // tm,),
        in_specs=[row(D), row(SB_WIDTH), row(GLA_VW), const(wa), const(wb), const(g2), const(wr), const(br),
                  const(tri)],
        out_specs=[row(D), pl.BlockSpec((tm * SUBLANES, LANES), lambda i: (i, 0)),
                   pl.BlockSpec((2 * TOP_K, tm // LANES, LANES), lambda i: (0, i, 0)), row(LANES),
                   pl.BlockSpec((1, LANES), lambda i: (0, 0))],
        out_shape=[jax.ShapeDtypeStruct((T, D), F32), jax.ShapeDtypeStruct((T * SUBLANES, LANES), F32),
                   jax.ShapeDtypeStruct((2 * TOP_K, T // LANES, LANES), jnp.int32),
                   jax.ShapeDtypeStruct((T, LANES), F32), jax.ShapeDtypeStruct((1, LANES), F32)],
        compiler_params=_params("arbitrary"),
        name="out_proj",
    )(x2, o_sb, o_gla, wa, wb, g2, wr, br, tri)


def _route_tables(route, counts, n_rows, tm):
    counts = counts[0, :N_EXPERTS].astype(jnp.int32)
    ends = jnp.cumsum(counts)
    starts = ends - counts
    route = route.reshape(2 * TOP_K, -1)
    expert = route[:TOP_K, :, None] == jnp.arange(N_EXPERTS, dtype=jnp.int32)
    dest = (jnp.sum(jnp.where(expert, starts, 0), axis=-1) + route[TOP_K:]).reshape(-1)

    n_items = n_rows // tm + N_EXPERTS + EXPERT_GATHER_DEPTH
    items_e = (counts + tm - 1) // tm
    item_hi = jnp.cumsum(items_e)
    item_lo = item_hi - items_e
    i = jnp.arange(n_items, dtype=jnp.int32)
    e_of = jnp.minimum(jnp.sum(item_hi[None, :] <= i[:, None], axis=1), N_EXPERTS - 1)
    owner = e_of[:, None] == jnp.arange(N_EXPERTS, dtype=jnp.int32)
    pick = lambda tab: jnp.sum(jnp.where(owner, tab, 0), axis=-1)
    item_row = jnp.where(i < item_hi[-1], pick(starts) + (i - pick(item_lo)) * tm, n_rows)
    return (dest, item_row.astype(jnp.int32), item_lo.astype(jnp.int32), item_hi.astype(jnp.int32))


ROW_ISSUE_UNROLL = 8
EXPERT_GATHER_DEPTH = 4


def _token_tile(ref, t):
    return ref.at[pl.ds(pl.multiple_of(t * SUBLANES, SUBLANES), SUBLANES), :]


def _wait_token_gather(src_ref, dst_ref, sem):
    pltpu.make_async_copy(src_ref.at[pl.ds(0, dst_ref.shape[0]), :], dst_ref, sem).wait()


def _expert_kernel(row_ref, lo_ref, hi_ref, dest_ref,
                   h_ref, wg_ref, wu_ref, wd_ref, y_ref, *scratch, tm):
    depth = EXPERT_GATHER_DEPTH
    xbufs = scratch[:depth]
    ybuf, tok_ref, wg_bf, wu_bf, wd_bf, gsem, wsem = scratch[depth:]
    e = pl.program_id(0)
    n_rows = dest_ref.shape[0]
    n_tok = n_rows // TOP_K
    total = hi_ref[N_EXPERTS - 1]

    def gather_copy(i, r, slot):
        t = tok_ref[row_ref[i] + r]
        return pltpu.make_async_copy(_token_tile(h_ref, t), _token_tile(xbufs[slot], r), gsem.at[slot])

    def out_copy(row, slot):
        rows = pl.ds(pl.multiple_of(row * SUBLANES, SUBLANES), tm * SUBLANES)
        return pltpu.make_async_copy(ybuf.at[slot], y_ref.at[rows, :], wsem)

    @pl.when(e == 0)
    def _():
        def invert(t, _):
            for k in range(TOP_K):
                tok_ref[dest_ref[k * n_tok + t]] = t
            return 0
        lax.fori_loop(0, n_tok, invert, 0, unroll=ROW_ISSUE_UNROLL)

        def spare(r, _):
            tok_ref[n_rows + r] = 0
            return 0
        lax.fori_loop(0, tm, spare, 0, unroll=ROW_ISSUE_UNROLL)
        ybuf[1] = jnp.zeros(ybuf.shape[1:], ybuf.dtype)
        out_copy(n_rows, 1).start()
        for ahead in range(depth - 1):
            def first(r, _):
                gather_copy(ahead, r, ahead).start()
                return 0
            lax.fori_loop(0, tm, first, 0, unroll=ROW_ISSUE_UNROLL)

    wg_bf[...] = wg_ref[...].astype(BF16)
    wu_bf[...] = wu_ref[...].astype(BF16)
    wd_bf[...] = wd_ref[...].astype(BF16)

    def item_in_slot(i, slot):
        _wait_token_gather(h_ref, xbufs[slot], gsem.at[slot])
        nslot = (slot + depth - 1) % depth
        for r in range(tm):
            gather_copy(i + depth - 1, r, nslot).start(priority=1)
        xb = _load_token_tiles(xbufs[slot], tm).astype(BF16)
        g = _dot(xb, wg_bf[...])
        u = _dot(xb, wu_bf[...])
        hidden = ((g / (1.0 + jnp.exp(-g))) * u).astype(BF16)
        _store_token_tiles(ybuf.at[slot % 2], _dot(hidden, wd_bf[...]))
        out_copy(0, slot % 2).wait()
        out_copy(row_ref[i], slot % 2).start()

    def item(i, _):
        slot = lax.rem(i, depth)
        for s in range(depth):
            @pl.when(slot == s)
            def _():
                item_in_slot(i, s)
        return 0

    lax.fori_loop(lo_ref[e], hi_ref[e], item, 0)

    @pl.when(e == pl.num_programs(0) - 1)
    def _():
        out_copy(0, 0).wait()
        for ahead in range(depth - 1):
            for s in range(depth):
                @pl.when(lax.rem(total + ahead, depth) == s)
                def _():
                    _wait_token_gather(h_ref, xbufs[s], gsem.at[s])


def _expert_ffn(h2_tiles, dest, tables, w_gate, w_up, w_down, tm):
    N = dest.shape[0]
    _, D, F = w_gate.shape
    assert D == SUBLANES * LANES
    item_row, item_lo, item_hi = tables
    wspec = lambda shape: pl.BlockSpec((None,) + shape, lambda e, *_: (e, 0, 0))
    return pl.pallas_call(
        functools.partial(_expert_kernel, tm=tm),
        grid_spec=pltpu.PrefetchScalarGridSpec(
            num_scalar_prefetch=4,
            grid=(N_EXPERTS,),
            in_specs=[pl.BlockSpec(memory_space=pl.ANY), wspec((D, F)), wspec((D, F)), wspec((F, D))],
            out_specs=pl.BlockSpec(memory_space=pl.ANY),
            scratch_shapes=[pltpu.VMEM((tm * SUBLANES, LANES), F32) for _ in range(EXPERT_GATHER_DEPTH)]
                           + [pltpu.VMEM((2, tm * SUBLANES, LANES), F32),
                            pltpu.SMEM((N + tm,), jnp.int32),
                            pltpu.VMEM((D, F), BF16), pltpu.VMEM((D, F), BF16), pltpu.VMEM((F, D), BF16),
                            pltpu.SemaphoreType.DMA((EXPERT_GATHER_DEPTH,)), pltpu.SemaphoreType.DMA(())],
        ),
        out_shape=jax.ShapeDtypeStruct(((N + tm) * SUBLANES, LANES), F32),
        compiler_params=_params("arbitrary"),
        name="expert_ffn",
    )(item_row, item_lo, item_hi, dest, h2_tiles, w_gate, w_up, w_down)


COMBINE_RING = 4
COMBINE_AHEAD = 3


def _combine_kernel(dest_ref, x1_ref, rf_ref, g_ref, y_ref, o_ref, *scratch, tb):
    bufs, sem = scratch[:COMBINE_RING], scratch[COMBINE_RING]
    i = pl.program_id(0)
    n_steps = pl.num_programs(0)
    n_tok = n_steps * COMBINE_RING * tb

    def copies(j, slot, r):
        return [pltpu.make_async_copy(_token_tile(y_ref, dest_ref[k * n_tok + j * tb + r]),
                                      _token_tile(bufs[slot].at[k], r), sem.at[slot])
                for k in range(TOP_K)]

    @pl.when(i == 0)
    def _():
        for slot in range(COMBINE_AHEAD):
            def first(r, _):
                for c in copies(slot, slot, r):
                    c.start()
                return 0
            lax.fori_loop(0, tb, first, 0, unroll=ROW_ISSUE_UNROLL // TOP_K)

    def sub_block(s, fetch):
        j = i * COMBINE_RING + s
        for k in range(TOP_K):
            _wait_token_gather(y_ref, bufs[s].at[k], sem.at[s])
        if fetch:
            for r in range(tb):
                for k, c in enumerate(copies(j + COMBINE_AHEAD, (s + COMBINE_AHEAD) % COMBINE_RING, r)):
                    c.start(priority=k)
        rows = slice(s * tb, (s + 1) * tb)
        rf = rf_ref[rows, :]
        x = (x1_ref[rows, :] + rf[:, 0:1] * _load_token_tiles(bufs[s].at[0], tb)
             + rf[:, 1:2] * _load_token_tiles(bufs[s].at[1], tb))
        inv = lax.rsqrt(jnp.mean(x * x, axis=-1, keepdims=True) + RMS_EPS)
        o_ref[rows, :] = (x * inv) * g_ref[...]

    for s in range(COMBINE_RING):
        if s + COMBINE_AHEAD < COMBINE_RING:
            sub_block(s, True)
        else:
            @pl.when(i + 1 < n_steps)
            def _():
                sub_block(s, True)

            @pl.when(i + 1 >= n_steps)
            def _():
                sub_block(s, False)


def _combine(x1, route_f, y_tiles, dest, ln_g, tb):
    T, D = x1.shape
    g2 = ln_g.reshape(1, D)
    step = COMBINE_RING * tb
    return pl.pallas_call(
        functools.partial(_combine_kernel, tb=tb),
        grid_spec=pltpu.PrefetchScalarGridSpec(
            num_scalar_prefetch=1,
            grid=(T // step,),
            in_specs=[pl.BlockSpec((step, D), lambda i, d: (i, 0)),
                      pl.BlockSpec((step, LANES), lambda i, d: (i, 0)),
                      pl.BlockSpec((1, D), lambda i, d: (0, 0)),
                      pl.BlockSpec(memory_space=pl.ANY)],
            out_specs=pl.BlockSpec((step, D), lambda i, d: (i, 0)),
            scratch_shapes=[pltpu.VMEM((TOP_K, tb * SUBLANES, LANES), F32) for _ in range(COMBINE_RING)]
                           + [pltpu.SemaphoreType.DMA((COMBINE_RING,))],
        ),
        out_shape=jax.ShapeDtypeStruct((T, D), F32),
        compiler_params=_params("arbitrary"),
        name="combine",
    )(dest, x1, route_f, g2, y_tiles)


MXU_WIDTH = 256


class _Tiles(NamedTuple):
    in_proj: int
    query: int
    mixers: int
    out_proj: int
    expert: int
    combine: int


def _tiles(n_tokens, seq_len):
    query = MXU_WIDTH
    tiles = _Tiles(in_proj=4 * MXU_WIDTH, query=query, mixers=query * (GLA_HEADS // 2),
                   out_proj=SUBLANES * LANES, expert=MXU_WIDTH, combine=MXU_WIDTH)
    assert seq_len % tiles.mixers == 0 and tiles.mixers % GLA_CHUNK == 0
    assert n_tokens % tiles.in_proj == 0 and n_tokens % tiles.out_proj == 0
    assert n_tokens % (COMBINE_RING * tiles.combine) == 0 and (TOP_K * n_tokens) % tiles.expert == 0
    return tiles


def _layer(x2, B, S, ln1_g, w_in, w2, gb, ng, w_out, ln2_g, w_group, b_group, w_expert, b_expert,
           w_gate, w_up, w_down, ln_f_g, tiles):
    q, k, v, gq, gk, gv, gr, lr = _in_proj(x2, ln1_g, w_in, tiles.in_proj)
    o_sb, o_gla = _mixers(q, k, v, gq, gk, gv, gr, lr, w2, gb, ng, B, S, tiles.mixers, tiles.query)
    x1, h2_tiles, route_i, route_f, counts = _out_proj(x2, o_sb, o_gla, w_out, ln2_g, w_group, b_group,
                                                       w_expert, b_expert, tiles.out_proj)
    T = x2.shape[0]
    dest, *tables = _route_tables(route_i, counts, TOP_K * T, tiles.expert)
    y_tiles = _expert_ffn(h2_tiles, dest, tables, w_gate, w_up, w_down, tiles.expert)
    return _combine(x1, route_f, y_tiles, dest, ln_f_g, tiles.combine)


def kernel(x, ln1_g, w_in, gla_gate_w2, gla_gate_b, gla_norm_g, w_out, ln2_g, w_group, b_group, w_expert, b_expert, exp_w_gate, exp_w_up, exp_w_down, ln_f_g):
    B, S, D = x.shape
    assert ln1_g.shape[0] == 1, "single-layer stack"
    assert D == SUBLANES * LANES, "one (SUBLANES, LANES) tile per token"
    out = _layer(x.reshape(B * S, D), B, S, ln1_g[0], w_in[0], gla_gate_w2[0], gla_gate_b[0],
                 gla_norm_g[0], w_out[0], ln2_g[0], w_group[0], b_group[0], w_expert[0], b_expert[0],
                 exp_w_gate[0], exp_w_up[0], exp_w_down[0], ln_f_g, _tiles(B * S, S))
    return out.reshape(B, S, D)
```

```python
import functools
from typing import NamedTuple

import numpy as np
import jax
import jax.numpy as jnp
from jax import lax
from jax.experimental import pallas as pl
from jax.experimental.pallas import tpu as pltpu

SB_HEADS = 8
SB_HEAD_DIM = 64
SB_WIDTH = SB_HEADS * SB_HEAD_DIM
GLA_HEADS = 4
GLA_DK = 64
GLA_DV = 128
GLA_KW = GLA_HEADS * GLA_DK
GLA_VW = GLA_HEADS * GLA_DV
GLA_GATE_RANK = 16
GLA_GATE_TAU = 16.0
N_GROUPS = 4
EXPERTS_PER_GROUP = 8
N_EXPERTS = N_GROUPS * EXPERTS_PER_GROUP
TOP_K = 2
RMS_EPS = 1e-6
LOG2_E = 1.4426950408889634

LANES = 128
VMEM_LIMIT_BYTES = 56 * 1024 * 1024

F32 = jnp.float32
BF16 = jnp.bfloat16


def _dot(a, b):
    return jnp.dot(a, b, preferred_element_type=F32)


def _dot_nt(a, b):
    return lax.dot_general(a, b, (((1,), (1,)), ((), ())), preferred_element_type=F32)


def _softplus(z):
    return jnp.maximum(z, 0.0) + jnp.log(1.0 + jnp.exp(-jnp.abs(z)))


def _params(*sem):
    return pltpu.CompilerParams(dimension_semantics=sem, vmem_limit_bytes=VMEM_LIMIT_BYTES)


def _in_proj_kernel(x_ref, g_ref, wq, wk, wv, wgq, wgk, wgv, wgr, wlr,
                    oq, ok, ov, ogq, ogk, ogv, ogr, olr):
    x = x_ref[...]
    inv = lax.rsqrt(jnp.mean(x * x, axis=-1, keepdims=True) + RMS_EPS)
    h = ((x * inv) * g_ref[...]).astype(BF16)
    oq[...] = (_dot(h, wq[...]) * (SB_HEAD_DIM ** -0.5 * LOG2_E)).astype(oq.dtype)
    ok[...] = _dot(h, wk[...]).astype(ok.dtype)
    ov[...] = _dot(h, wv[...]).astype(ov.dtype)
    ogq[...] = (_dot(h, wgq[...]) * (GLA_DK ** -0.5)).astype(ogq.dtype)
    ogk[...] = _dot(h, wgk[...]).astype(ogk.dtype)
    ogv[...] = _dot(h, wgv[...]).astype(ogv.dtype)
    ogr[...] = _dot(h, wgr[...]).astype(ogr.dtype)
    olr[...] = _dot(h, wlr[...]).astype(olr.dtype)


def _in_proj(x2, ln_g, w_in, tm):
    T, D = x2.shape
    sizes = (SB_WIDTH, SB_WIDTH, SB_WIDTH, GLA_KW, GLA_KW, GLA_VW, GLA_VW, GLA_GATE_RANK)
    offs = np.concatenate([[0], np.cumsum(sizes)])
    ws = [w_in[:, int(offs[i]):int(offs[i + 1])].astype(BF16) for i in range(len(sizes))]
    out_dtypes = (BF16, BF16, BF16, F32, F32, BF16, F32, F32)
    row = lambda n: pl.BlockSpec((tm, n), lambda i: (i, 0))
    full = lambda n: pl.BlockSpec((D, n), lambda i: (0, 0))
    return pl.pallas_call(
        _in_proj_kernel,
        grid=(T // tm,),
        in_specs=[row(D), pl.BlockSpec((1, D), lambda i: (0, 0))] + [full(n) for n in sizes],
        out_specs=[row(n) for n in sizes],
        out_shape=[jax.ShapeDtypeStruct((T, n), dt) for n, dt in zip(sizes, out_dtypes)],
        compiler_params=_params("parallel"),
        name="in_proj",
    )(x2, ln_g.reshape(1, D), *ws)


SB_ZERO_WEIGHT_BITS = 160.0


SOFTPLUS2_CLAMP = 126.0


def _softplus2(z):
    return jnp.maximum(z, jnp.log2(1.0 + jnp.exp2(jnp.minimum(z, SOFTPLUS2_CLAMP))))


def _sb_query_block(q_ref, k_ref, v_ref, o_ref, qh_ref, acc_ref, carry_ref, *, row0, qi, tq, n_pairs,
                    alongside=None):
    qrows = slice(row0, row0 + tq)
    r = lax.broadcasted_iota(jnp.int32, (tq, tq), 0)
    c = lax.broadcasted_iota(jnp.int32, (tq, tq), 1)
    causal = c < r
    later = (r > c).astype(BF16)
    head0 = lax.broadcasted_iota(jnp.int32, (1, LANES), 1) < SB_HEAD_DIM

    for p in range(n_pairs):
        q2 = q_ref[qrows, p * LANES:(p + 1) * LANES]
        zero = jnp.zeros_like(q2)
        qh_ref[p, 0:tq] = jnp.where(head0, q2, zero)
        qh_ref[p, tq:2 * tq] = jnp.where(head0, zero, q2)
    acc_ref[...] = jnp.zeros_like(acc_ref)
    carry_ref[...] = jnp.zeros_like(carry_ref)
    causal2 = jnp.concatenate([causal, causal], axis=0)

    def key_block(kb, diag):
        rows = pl.ds(pl.multiple_of(kb * tq, tq), tq)
        pairs = range(n_pairs)
        zs = {}

        def scores(p):
            zs[p] = _dot_nt(qh_ref[p], k_ref[rows, pl.ds(p * LANES, LANES)])
        scores(0)
        scores(1)

        def suffix_stage(p):
            sp = _softplus2(zs[p])
            if diag:
                sp = jnp.where(causal2, sp, 0.0)
            return sp, _dot(sp.astype(BF16), later)

        def weight_stage(p, sp, run):
            z = zs[p]
            v2 = v_ref[rows, pl.ds(p * LANES, LANES)]
            vzero = jnp.zeros_like(v2)
            v_stack = jnp.concatenate([jnp.where(head0, v2, vzero), jnp.where(head0, vzero, v2)], axis=0)
            carry = carry_ref[p]
            w = jnp.exp2((z - sp) - run - jnp.concatenate([carry] * (tq // LANES), axis=1))
            if diag:
                w = jnp.where(causal2, w, 0.0)
            w = w.astype(BF16)
            acc_ref[p] += _dot(jnp.concatenate([w[0:tq], w[tq:2 * tq]], axis=1), v_stack)
            carry_ref[p] = carry + jnp.broadcast_to(run[:, 0:1] + sp[:, 0:1], (2 * tq, LANES))

        pending = None
        for p in pairs:
            if p + 2 < n_pairs:
                scores(p + 2)
            staged = suffix_stage(p)
            if pending is not None:
                weight_stage(*pending)
            pending = (p,) + staged
        weight_stage(*pending)

    def min_carry():
        m = carry_ref[0]
        for p in range(1, n_pairs):
            m = jnp.minimum(m, carry_ref[p])
        return jnp.min(m)

    def cond(state):
        kb, m = state
        return jnp.logical_and(kb >= 0, m < SB_ZERO_WEIGHT_BITS)

    def body(state):
        kb, _ = state
        key_block(kb, False)
        return kb - 1, min_carry()

    if alongside is not None:
        alongside()
    key_block(qi, True)
    lax.while_loop(cond, body, (qi - 1, min_carry()))
    for p in range(n_pairs):
        o_ref[qrows, p * LANES:(p + 1) * LANES] = acc_ref[p].astype(o_ref.dtype)


GLA_CHUNK = 64
GLA_LEVELS = (32, 16, 8, 4, 2, 1)


def _gla_tables(C=GLA_CHUNK):
    t = np.arange(C)[:, None]
    j = np.arange(C)[None, :]
    blocks = [(j <= t), (j > t)]
    masks = []
    for half in GLA_LEVELS:
        blk = 2 * half
        u = (t // blk) * blk + half - 1
        second = (t % blk) >= half
        blocks.append((second & (j > u) & (j <= t)) | ((~second) & (j > t) & (j <= u)))
        s = j
        masks.append(second & ((s % blk) < half) & ((s // blk) == (t // blk)))
    blocks.append(np.ones((8, C), bool))
    table = np.concatenate(blocks, axis=0).astype(np.float32)
    table = np.concatenate([table, table], axis=1)
    return table, np.concatenate(masks, axis=0).astype(np.float32)


def _split_bf16(a):
    hi = a.astype(BF16)
    lo = (a - hi.astype(F32)).astype(BF16)
    return hi, lo


def _gla_head_pair(p, gq_ref, gk_ref, gv_ref, gr_ref, lr_ref, w2_ref, gb_ref, ng_ref, tab_ref, msk_ref,
                   o_ref, state_ref, *, n_chunks):
    C = GLA_CHUNK
    n_lv = len(GLA_LEVELS)

    head0 = lax.broadcasted_iota(jnp.int32, (1, LANES), 1) < GLA_DK
    eye = (lax.broadcasted_iota(jnp.int32, (C, C), 0) == lax.broadcasted_iota(jnp.int32, (C, C), 1))
    w2_hi, w2_lo = _split_bf16(w2_ref[...])
    lr_hi, lr_lo = _split_bf16(lr_ref[...])

    def stack_heads(a):
        return jnp.concatenate([jnp.where(head0, a, 0.0), jnp.where(head0, 0.0, a)], axis=0)

    cols = slice(p * LANES, (p + 1) * LANES)
    pre = (_dot(lr_hi, w2_hi[:, cols]) + _dot(lr_hi, w2_lo[:, cols])
           + _dot(lr_lo, w2_hi[:, cols])) + gb_ref[:, cols]
    log_a = -_softplus(-pre) * (1.0 / GLA_GATE_TAU)
    la_hi, la_lo = _split_bf16(log_a)
    la = jnp.concatenate(
        [jnp.concatenate([la_hi[c * C:(c + 1) * C], la_lo[c * C:(c + 1) * C]], axis=0)
         for c in range(n_chunks)], axis=1)
    ex_all = jnp.exp(_dot(tab_ref[...], la))
    st = state_ref[p]
    def score_stage(c):
        rows = slice(c * C, (c + 1) * C)
        ex = ex_all[:, c * LANES:(c + 1) * LANES]
        q2 = gq_ref[rows, cols]
        k2 = gk_ref[rows, cols]
        q_int = q2 * ex[0:C]
        k_dec = (k2 * ex[C:2 * C]).astype(BF16)
        dec_row = ex[(2 + n_lv) * C:(2 + n_lv) * C + 1]
        diag = jnp.sum(stack_heads(q2 * k2), axis=1, keepdims=True)
        scores = jnp.where(jnp.concatenate([eye, eye], axis=0), diag, 0.0)
        for l in range(n_lv):
            ex_l = ex[(2 + l) * C:(3 + l) * C]
            q_l = stack_heads(q2 * ex_l).astype(BF16)
            k_l = (k2 * ex_l).astype(BF16)
            m_l = msk_ref[l * C:(l + 1) * C, :] > 0.5
            scores = scores + jnp.where(jnp.concatenate([m_l, m_l], axis=0), _dot_nt(q_l, k_l), 0.0)
        return c, q_int, k_dec, dec_row, scores.astype(BF16)

    def state_stage(st, c, q_int, k_dec, dec_row, scores):
        rows = slice(c * C, (c + 1) * C)
        o_inter = _dot_nt(stack_heads(q_int).astype(BF16), st.astype(BF16))
        upd = []
        for h in range(2):
            vcols = slice((2 * p + h) * GLA_DV, (2 * p + h + 1) * GLA_DV)
            v_h = gv_ref[rows, vcols]
            o = o_inter[h * C:(h + 1) * C] + _dot(scores[h * C:(h + 1) * C], v_h)
            inv = lax.rsqrt(jnp.mean(o * o, axis=-1, keepdims=True) + RMS_EPS)
            y = (o * inv) * ng_ref[:, vcols]
            g = gr_ref[rows, vcols]
            y = y * (g / (1.0 + jnp.exp(-g)))
            o_ref[rows, vcols] = y.astype(o_ref.dtype)
            v_t = v_h.astype(F32).T.astype(BF16)
            upd.append(_dot(v_t, k_dec))
        return st * dec_row + jnp.where(head0, upd[0], upd[1])

    pending = score_stage(0)
    for c in range(1, n_chunks):
        ahead = score_stage(c)
        st = state_stage(st, *pending)
        pending = ahead
    st = state_stage(st, *pending)
    state_ref[p] = st


def _mixers_kernel(q_ref, k_ref, v_ref, gq_ref, gk_ref, gv_ref, gr_ref, lr_ref, w2_ref, gb_ref, ng_ref,
                   tab_ref, msk_ref, osb_ref, ogla_ref, qh_ref, acc_ref, carry_ref, state_ref,
                   *, tq, n_pairs, n_chunks):
    blk = pl.program_id(1)
    n_q = q_ref.shape[0] // tq
    assert n_q == GLA_HEADS // 2

    @pl.when(blk == 0)
    def _():
        state_ref[...] = jnp.zeros_like(state_ref)

    for half in range(n_q):
        gla = functools.partial(_gla_head_pair, half, gq_ref, gk_ref, gv_ref, gr_ref, lr_ref, w2_ref, gb_ref,
                                ng_ref, tab_ref, msk_ref, ogla_ref, state_ref, n_chunks=n_chunks)
        _sb_query_block(q_ref, k_ref, v_ref, osb_ref, qh_ref, acc_ref, carry_ref, row0=half * tq,
                        qi=blk * n_q + half, tq=tq, n_pairs=n_pairs, alongside=gla)


def _mixers(q, k, v, gq, gk, gv, gr, lr, w2, gb, ng, B, S, tb, tq):
    table, masks = _gla_tables()
    table = jnp.asarray(table, BF16)
    masks = jnp.asarray(masks, F32)
    T = B * S
    W = q.shape[-1]
    n_pairs = W // LANES
    nblk = S // tb
    q3, k3, v3 = (a.reshape(B, S, W) for a in (q, k, v))
    blk3 = pl.BlockSpec((None, tb, W), lambda b, i: (b, i, 0))
    seq3 = pl.BlockSpec((None, S, W), lambda b, i: (b, 0, 0))
    row = lambda n: pl.BlockSpec((tb, n), lambda b, i: (b * nblk + i, 0))
    const = lambda a: pl.BlockSpec(a.shape, lambda b, i: (0,) * a.ndim)
    w2 = w2.astype(F32)
    gb2 = gb.reshape(1, GLA_KW)
    ng2 = ng.reshape(1, GLA_VW)
    o_sb, o_gla = pl.pallas_call(
        functools.partial(_mixers_kernel, tq=tq, n_pairs=n_pairs, n_chunks=tb // GLA_CHUNK),
        grid=(B, nblk),
        in_specs=[blk3, seq3, seq3, row(GLA_KW), row(GLA_KW), row(GLA_VW), row(GLA_VW), row(GLA_GATE_RANK),
                  const(w2), const(gb2), const(ng2), const(table), const(masks)],
        out_specs=[blk3, row(GLA_VW)],
        out_shape=[jax.ShapeDtypeStruct((B, S, W), BF16), jax.ShapeDtypeStruct((T, GLA_VW), BF16)],
        scratch_shapes=[pltpu.VMEM((n_pairs, 2 * tq, LANES), BF16),
                        pltpu.VMEM((n_pairs, tq, LANES), F32),
                        pltpu.VMEM((n_pairs, 2 * tq, LANES), F32),
                        pltpu.VMEM((GLA_HEADS // 2, GLA_DV, 2 * GLA_DK), F32)],
        compiler_params=_params("parallel", "arbitrary"),
        name="mixers",
    )(q3, k3, v3, gq, gk, gv, gr, lr, w2, gb2, ng2, table, masks)
    return o_sb.reshape(T, W), o_gla


SUBLANES = 8


def _store_token_tiles(ref, value, accumulate=False):
    n = value.shape[0]
    for s in range(SUBLANES):
        rows = pl.ds(s, n, stride=SUBLANES)
        chunk = value[:, s * LANES:(s + 1) * LANES]
        ref[rows, :] = ref[rows, :] + chunk if accumulate else chunk


def _load_token_tiles(ref, n):
    return jnp.concatenate([ref[pl.ds(s, n, stride=SUBLANES), :] for s in range(SUBLANES)], axis=1)


def _out_proj_kernel(x_ref, oa_ref, ob_ref, wa_ref, wb_ref, g_ref, wr_ref, br_ref, tri_ref,
                     x1_ref, h2_ref, ri_ref, rf_ref, cnt_ref):
    @pl.when(pl.program_id(0) == 0)
    def _():
        cnt_ref[...] = jnp.zeros_like(cnt_ref)

    half = tri_ref.shape[0]
    halves = [slice(i * half, (i + 1) * half) for i in range(x_ref.shape[0] // half)]
    w_hi, w_lo = _split_bf16(wr_ref[...])
    w_cat = jnp.concatenate([w_hi, w_lo], axis=1)
    lane = lax.broadcasted_iota(jnp.int32, (half, LANES), 1).astype(F32)
    neg = -jnp.inf

    def residual_stage(rows):
        x1 = x_ref[rows, :] + _dot(oa_ref[rows, :], wa_ref[...]) + _dot(ob_ref[rows, :], wb_ref[...])
        x1_ref[rows, :] = x1
        inv = lax.rsqrt(jnp.mean(x1 * x1, axis=-1, keepdims=True) + RMS_EPS)
        h2 = (x1 * inv) * g_ref[...]
        _store_token_tiles(h2_ref.at[pl.ds(rows.start * SUBLANES, half * SUBLANES), :], h2)
        return _split_bf16(h2)

    def logits_stage(h_hi, h_lo):
        hw = _dot(h_hi, w_cat)
        return hw[:, :LANES] + hw[:, LANES:] + _dot(h_lo, w_hi) + br_ref[...]

    def first_max(vals):
        m = jnp.max(vals, axis=1, keepdims=True)
        idx = jnp.min(jnp.where(vals == m, lane, float(LANES)), axis=1, keepdims=True)
        return m, idx

    def select_stage(rows, logits):
        is_group = lane < N_GROUPS
        g_max, g_idx = first_max(jnp.where(is_group, logits, neg))
        g_p = 1.0 / jnp.sum(jnp.where(is_group, jnp.exp(logits - g_max), 0.0), axis=1, keepdims=True)
        lo_lane = N_GROUPS + EXPERTS_PER_GROUP * g_idx
        sel = jnp.where((lane >= lo_lane) & (lane < lo_lane + EXPERTS_PER_GROUP), logits, neg)
        m1, i1 = first_max(sel)
        m2, i2 = first_max(jnp.where(lane == i1, neg, sel))
        e = jnp.exp(m2 - m1)
        gate1 = g_p / (1.0 + e)
        gate2 = g_p * e / (1.0 + e)
        rf_ref[rows, :] = jnp.where(lane == 0, gate1, jnp.where(lane == 1, gate2, 0.0))
        e1 = i1 - N_GROUPS
        e2 = i2 - N_GROUPS
        hits = (lane == e1).astype(F32) + (lane == e2).astype(F32)
        return e1, e2, hits

    pieces = [residual_stage(rows) for rows in halves]
    logits = [logits_stage(*hl) for hl in pieces]
    chosen = [select_stage(rows, lg) for rows, lg in zip(halves, logits)]

    running = cnt_ref[...]
    for i, (e1, e2, hits) in enumerate(chosen):
        before = _dot(tri_ref[...], hits.astype(BF16)) + running
        r1 = jnp.sum(jnp.where(lane == e1, before, 0.0), axis=1, keepdims=True)
        r2 = jnp.sum(jnp.where(lane == e2, before, 0.0), axis=1, keepdims=True)
        running = running + jnp.sum(hits, axis=0, keepdims=True)
        packed = jnp.where(lane == 0, e1, jnp.where(lane == 1, e2,
                           jnp.where(lane == 2, r1, jnp.where(lane == 3, r2, 0.0))))
        for j in range(half // LANES):
            rows_t = packed[j * LANES:(j + 1) * LANES, :].T
            jj = i * (half // LANES) + j
            for k in range(2 * TOP_K):
                ri_ref[k, jj:jj + 1, :] = rows_t[k:k + 1, :].astype(jnp.int32)
    cnt_ref[...] = running


def _out_proj(x2, o_sb, o_gla, w_out, ln_g, w_group, b_group, w_expert, b_expert, tm):
    T, D = x2.shape
    wa = w_out[:SB_WIDTH].astype(BF16)
    wb = w_out[SB_WIDTH:].astype(BF16)
    n_r = N_GROUPS + N_EXPERTS
    wr = jnp.zeros((D, LANES), F32).at[:, :n_r].set(jnp.concatenate([w_group, w_expert], axis=1))
    br = jnp.zeros((1, LANES), F32).at[0, :n_r].set(jnp.concatenate([b_group, b_expert]))
    row = lambda n: pl.BlockSpec((tm, n), lambda i: (i, 0))
    const = lambda a: pl.BlockSpec(a.shape, lambda i: (0,) * a.ndim)
    g2 = ln_g.reshape(1, D)
    half = tm // 2
    tri = jnp.asarray(np.tril(np.ones((half, half), np.float32), -1), BF16)
    return pl.pallas_call(
        _out_proj_kernel,
        grid=(T // tm,),
        in_specs=[row(D), row(SB_WIDTH), row(GLA_VW), const(wa), const(wb), const(g2), const(wr), const(br),
                  const(tri)],
        out_specs=[row(D), pl.BlockSpec((tm * SUBLANES, LANES), lambda i: (i, 0)),
                   pl.BlockSpec((2 * TOP_K, tm // LANES, LANES), lambda i: (0, i, 0)), row(LANES),
                   pl.BlockSpec((1, LANES), lambda i: (0, 0))],
        out_shape=[jax.ShapeDtypeStruct((T, D), F32), jax.ShapeDtypeStruct((T * SUBLANES, LANES), F32),
                   jax.ShapeDtypeStruct((2 * TOP_K, T // LANES, LANES), jnp.int32),
                   jax.ShapeDtypeStruct((T, LANES), F32), jax.ShapeDtypeStruct((1, LANES), F32)],
        compiler_params=_params("arbitrary"),
        name="out_proj",
    )(x2, o_sb, o_gla, wa, wb, g2, wr, br, tri)


def _route_tables(route, counts, n_rows, tm):
    counts = counts[0, :N_EXPERTS].astype(jnp.int32)
    ends = jnp.cumsum(counts)
    starts = ends - counts
    route = route.reshape(2 * TOP_K, -1)
    expert = route[:TOP_K, :, None] == jnp.arange(N_EXPERTS, dtype=jnp.int32)
    dest = (jnp.sum(jnp.where(expert, starts, 0), axis=-1) + route[TOP_K:]).reshape(-1)

    n_items = n_rows // tm + N_EXPERTS + EXPERT_GATHER_DEPTH
    items_e = (counts + tm - 1) // tm
    item_hi = jnp.cumsum(items_e)
    item_lo = item_hi - items_e
    i = jnp.arange(n_items, dtype=jnp.int32)
    e_of = jnp.minimum(jnp.sum(item_hi[None, :] <= i[:, None], axis=1), N_EXPERTS - 1)
    owner = e_of[:, None] == jnp.arange(N_EXPERTS, dtype=jnp.int32)
    pick = lambda tab: jnp.sum(jnp.where(owner, tab, 0), axis=-1)
    item_row = jnp.where(i < item_hi[-1], pick(starts) + (i - pick(item_lo)) * tm, n_rows)
    return (dest, item_row.astype(jnp.int32), item_lo.astype(jnp.int32), item_hi.astype(jnp.int32))


ROW_ISSUE_UNROLL = 8
EXPERT_GATHER_DEPTH = 4


def _token_tile(ref, t):
    return ref.at[pl.ds(pl.multiple_of(t * SUBLANES, SUBLANES), SUBLANES), :]


def _wait_token_gather(src_ref, dst_ref, sem):
    pltpu.make_async_copy(src_ref.at[pl.ds(0, dst_ref.shape[0]), :], dst_ref, sem).wait()


def _expert_kernel(row_ref, lo_ref, hi_ref, dest_ref,
                   h_ref, wg_ref, wu_ref, wd_ref, y_ref, *scratch, tm):
    depth = EXPERT_GATHER_DEPTH
    xbufs = scratch[:depth]
    ybuf, tok_ref, wg_bf, wu_bf, wd_bf, gsem, wsem = scratch[depth:]
    e = pl.program_id(0)
    n_rows = dest_ref.shape[0]
    n_tok = n_rows // TOP_K
    total = hi_ref[N_EXPERTS - 1]

    def gather_copy(i, r, slot):
        t = tok_ref[row_ref[i] + r]
        return pltpu.make_async_copy(_token_tile(h_ref, t), _token_tile(xbufs[slot], r), gsem.at[slot])

    def out_copy(row, slot):
        rows = pl.ds(pl.multiple_of(row * SUBLANES, SUBLANES), tm * SUBLANES)
        return pltpu.make_async_copy(ybuf.at[slot], y_ref.at[rows, :], wsem)

    @pl.when(e == 0)
    def _():
        def invert(t, _):
            for k in range(TOP_K):
                tok_ref[dest_ref[k * n_tok + t]] = t
            return 0
        lax.fori_loop(0, n_tok, invert, 0, unroll=ROW_ISSUE_UNROLL)

        def spare(r, _):
            tok_ref[n_rows + r] = 0
            return 0
        lax.fori_loop(0, tm, spare, 0, unroll=ROW_ISSUE_UNROLL)
        ybuf[1] = jnp.zeros(ybuf.shape[1:], ybuf.dtype)
        out_copy(n_rows, 1).start()
        for ahead in range(depth - 1):
            def first(r, _):
                gather_copy(ahead, r, ahead).start()
                return 0
            lax.fori_loop(0, tm, first, 0, unroll=ROW_ISSUE_UNROLL)

    wg_bf[...] = wg_ref[...].astype(BF16)
    wu_bf[...] = wu_ref[...].astype(BF16)
    wd_bf[...] = wd_ref[...].astype(BF16)

    def item_in_slot(i, slot):
        _wait_token_gather(h_ref, xbufs[slot], gsem.at[slot])
        nslot = (slot + depth - 1) % depth
        for r in range(tm):
            gather_copy(i + depth - 1, r, nslot).start(priority=1)
        xb = _load_token_tiles(xbufs[slot], tm).astype(BF16)
        g = _dot(xb, wg_bf[...])
        u = _dot(xb, wu_bf[...])
        hidden = ((g / (1.0 + jnp.exp(-g))) * u).astype(BF16)
        _store_token_tiles(ybuf.at[slot % 2], _dot(hidden, wd_bf[...]))
        out_copy(0, slot % 2).wait()
        out_copy(row_ref[i], slot % 2).start()

    def item(i, _):
        slot = lax.rem(i, depth)
        for s in range(depth):
            @pl.when(slot == s)
            def _():
                item_in_slot(i, s)
        return 0

    lax.fori_loop(lo_ref[e], hi_ref[e], item, 0)

    @pl.when(e == pl.num_programs(0) - 1)
    def _():
        out_copy(0, 0).wait()
        for ahead in range(depth - 1):
            for s in range(depth):
                @pl.when(lax.rem(total + ahead, depth) == s)
                def _():
                    _wait_token_gather(h_ref, xbufs[s], gsem.at[s])


def _expert_ffn(h2_tiles, dest, tables, w_gate, w_up, w_down, tm):
    N = dest.shape[0]
    _, D, F = w_gate.shape
    assert D == SUBLANES * LANES
    item_row, item_lo, item_hi = tables
    wspec = lambda shape: pl.BlockSpec((None,) + shape, lambda e, *_: (e, 0, 0))
    return pl.pallas_call(
        functools.partial(_expert_kernel, tm=tm),
        grid_spec=pltpu.PrefetchScalarGridSpec(
            num_scalar_prefetch=4,
            grid=(N_EXPERTS,),
            in_specs=[pl.BlockSpec(memory_space=pl.ANY), wspec((D, F)), wspec((D, F)), wspec((F, D))],
            out_specs=pl.BlockSpec(memory_space=pl.ANY),
            scratch_shapes=[pltpu.VMEM((tm * SUBLANES, LANES), F32) for _ in range(EXPERT_GATHER_DEPTH)]
                           + [pltpu.VMEM((2, tm * SUBLANES, LANES), F32),
                            pltpu.SMEM((N + tm,), jnp.int32),
                            pltpu.VMEM((D, F), BF16), pltpu.VMEM((D, F), BF16), pltpu.VMEM((F, D), BF16),
                            pltpu.SemaphoreType.DMA((EXPERT_GATHER_DEPTH,)), pltpu.SemaphoreType.DMA(())],
        ),
        out_shape=jax.ShapeDtypeStruct(((N + tm) * SUBLANES, LANES), F32),
        compiler_params=_params("arbitrary"),
        name="expert_ffn",
    )(item_row, item_lo, item_hi, dest, h2_tiles, w_gate, w_up, w_down)


COMBINE_RING = 4
COMBINE_AHEAD = 3


def _combine_kernel(dest_ref, x1_ref, rf_ref, g_ref, y_ref, o_ref, *scratch, tb):
    bufs, sem = scratch[:COMBINE_RING], scratch[COMBINE_RING]
    i = pl.program_id(0)
    n_steps = pl.num_programs(0)
    n_tok = n_steps * COMBINE_RING * tb

    def copies(j, slot, r):
        return [pltpu.make_async_copy(_token_tile(y_ref, dest_ref[k * n_tok + j * tb + r]),
                                      _token_tile(bufs[slot].at[k], r), sem.at[slot])
                for k in range(TOP_K)]

    @pl.when(i == 0)
    def _():
        for slot in range(COMBINE_AHEAD):
            def first(r, _):
                for c in copies(slot, slot, r):
                    c.start()
                return 0
            lax.fori_loop(0, tb, first, 0, unroll=ROW_ISSUE_UNROLL // TOP_K)

    def sub_block(s, fetch):
        j = i * COMBINE_RING + s
        for k in range(TOP_K):
            _wait_token_gather(y_ref, bufs[s].at[k], sem.at[s])
        if fetch:
            for r in range(tb):
                for k, c in enumerate(copies(j + COMBINE_AHEAD, (s + COMBINE_AHEAD) % COMBINE_RING, r)):
                    c.start(priority=k)
        rows = slice(s * tb, (s + 1) * tb)
        rf = rf_ref[rows, :]
        x = (x1_ref[rows, :] + rf[:, 0:1] * _load_token_tiles(bufs[s].at[0], tb)
             + rf[:, 1:2] * _load_token_tiles(bufs[s].at[1], tb))
        inv = lax.rsqrt(jnp.mean(x * x, axis=-1, keepdims=True) + RMS_EPS)
        o_ref[rows, :] = (x * inv) * g_ref[...]

    for s in range(COMBINE_RING):
        if s + COMBINE_AHEAD < COMBINE_RING:
            sub_block(s, True)
        else:
            @pl.when(i + 1 < n_steps)
            def _():
                sub_block(s, True)

            @pl.when(i + 1 >= n_steps)
            def _():
                sub_block(s, False)


def _combine(x1, route_f, y_tiles, dest, ln_g, tb):
    T, D = x1.shape
    g2 = ln_g.reshape(1, D)
    step = COMBINE_RING * tb
    return pl.pallas_call(
        functools.partial(_combine_kernel, tb=tb),
        grid_spec=pltpu.PrefetchScalarGridSpec(
            num_scalar_prefetch=1,
            grid=(T // step,),
            in_specs=[pl.BlockSpec((step, D), lambda i, d: (i, 0)),
                      pl.BlockSpec((step, LANES), lambda i, d: (i, 0)),
                      pl.BlockSpec((1, D), lambda i, d: (0, 0)),
                      pl.BlockSpec(memory_space=pl.ANY)],
            out_specs=pl.BlockSpec((step, D), lambda i, d: (i, 0)),
            scratch_shapes=[pltpu.VMEM((TOP_K, tb * SUBLANES, LANES), F32) for _ in range(COMBINE_RING)]
                           + [pltpu.SemaphoreType.DMA((COMBINE_RING,))],
        ),
        out_shape=jax.ShapeDtypeStruct((T, D), F32),
        compiler_params=_params("arbitrary"),
        name="combine",
    )(dest, x1, route_f, g2, y_tiles)


MXU_WIDTH = 256


class _Tiles(NamedTuple):
    in_proj: int
    query: int
    mixers: int
    out_proj: int
    expert: int
    combine: int


def _tiles(n_tokens, seq_len):
    query = MXU_WIDTH
    tiles = _Tiles(in_proj=4 * MXU_WIDTH, query=query, mixers=query * (GLA_HEADS // 2),
                   out_proj=SUBLANES * LANES, expert=MXU_WIDTH, combine=MXU_WIDTH)
    assert seq_len % tiles.mixers == 0 and tiles.mixers % GLA_CHUNK == 0
    assert n_tokens % tiles.in_proj == 0 and n_tokens % tiles.out_proj == 0
    assert n_tokens % (COMBINE_RING * tiles.combine) == 0 and (TOP_K * n_tokens) % tiles.expert == 0
    return tiles


def _layer(x2, B, S, ln1_g, w_in, w2, gb, ng, w_out, ln2_g, w_group, b_group, w_expert, b_expert,
           w_gate, w_up, w_down, ln_f_g, tiles):
    q, k, v, gq, gk, gv, gr, lr = _in_proj(x2, ln1_g, w_in, tiles.in_proj)
    o_sb, o_gla = _mixers(q, k, v, gq, gk, gv, gr, lr, w2, gb, ng, B, S, tiles.mixers, tiles.query)
    x1, h2_tiles, route_i, route_f, counts = _out_proj(x2, o_sb, o_gla, w_out, ln2_g, w_group, b_group,
                                                       w_expert, b_expert, tiles.out_proj)
    T = x2.shape[0]
    dest, *tables = _route_tables(route_i, counts, TOP_K * T, tiles.expert)
    y_tiles = _expert_ffn(h2_tiles, dest, tables, w_gate, w_up, w_down, tiles.expert)
    return _combine(x1, route_f, y_tiles, dest, ln_f_g, tiles.combine)


def kernel(x, ln1_g, w_in, gla_gate_w2, gla_gate_b, gla_norm_g, w_out, ln2_g, w_group, b_group, w_expert, b_expert, exp_w_gate, exp_w_up, exp_w_down, ln_f_g):
    B, S, D = x.shape
    assert ln1_g.shape[0] == 1, "single-layer stack"
    assert D == SUBLANES * LANES, "one (SUBLANES, LANES) tile per token"
    out = _layer(x.reshape(B * S, D), B, S, ln1_g[0], w_in[0], gla_gate_w2[0], gla_gate_b[0],
                 gla_norm_g[0], w_out[0], ln2_g[0], w_group[0], b_group[0], w_expert[0], b_expert[0],
                 exp_w_gate[0], exp_w_up[0], exp_w_down[0], ln_f_g, _tiles(B * S, S))
    return out.reshape(B, S, D)
```

```python
import functools
from typing import NamedTuple

import numpy as np
import jax
import jax.numpy as jnp
from jax import lax
from jax.experimental import pallas as pl
from jax.experimental.pallas import tpu as pltpu

SB_HEADS = 8
SB_HEAD_DIM = 64
SB_WIDTH = SB_HEADS * SB_HEAD_DIM
GLA_HEADS = 4
GLA_DK = 64
GLA_DV = 128
GLA_KW = GLA_HEADS * GLA_DK
GLA_VW = GLA_HEADS * GLA_DV
GLA_GATE_RANK = 16
GLA_GATE_TAU = 16.0
N_GROUPS = 4
EXPERTS_PER_GROUP = 8
N_EXPERTS = N_GROUPS * EXPERTS_PER_GROUP
TOP_K = 2
RMS_EPS = 1e-6
LOG2_E = 1.4426950408889634

LANES = 128
VMEM_LIMIT_BYTES = 56 * 1024 * 1024

F32 = jnp.float32
BF16 = jnp.bfloat16


def _dot(a, b):
    return jnp.dot(a, b, preferred_element_type=F32)


def _dot_nt(a, b):
    return lax.dot_general(a, b, (((1,), (1,)), ((), ())), preferred_element_type=F32)


def _softplus(z):
    return jnp.maximum(z, 0.0) + jnp.log(1.0 + jnp.exp(-jnp.abs(z)))


def _params(*sem):
    return pltpu.CompilerParams(dimension_semantics=sem, vmem_limit_bytes=VMEM_LIMIT_BYTES)


def _in_proj_kernel(x_ref, g_ref, wq, wk, wv, wgq, wgk, wgv, wgr, wlr,
                    oq, ok, ov, ogq, ogk, ogv, ogr, olr):
    x = x_ref[...]
    inv = lax.rsqrt(jnp.mean(x * x, axis=-1, keepdims=True) + RMS_EPS)
    h = ((x * inv) * g_ref[...]).astype(BF16)
    oq[...] = (_dot(h, wq[...]) * (SB_HEAD_DIM ** -0.5 * LOG2_E)).astype(oq.dtype)
    ok[...] = _dot(h, wk[...]).astype(ok.dtype)
    ov[...] = _dot(h, wv[...]).astype(ov.dtype)
    ogq[...] = (_dot(h, wgq[...]) * (GLA_DK ** -0.5)).astype(ogq.dtype)
    ogk[...] = _dot(h, wgk[...]).astype(ogk.dtype)
    ogv[...] = _dot(h, wgv[...]).astype(ogv.dtype)
    ogr[...] = _dot(h, wgr[...]).astype(ogr.dtype)
    olr[...] = _dot(h, wlr[...]).astype(olr.dtype)


def _in_proj(x2, ln_g, w_in, tm):
    T, D = x2.shape
    sizes = (SB_WIDTH, SB_WIDTH, SB_WIDTH, GLA_KW, GLA_KW, GLA_VW, GLA_VW, GLA_GATE_RANK)
    offs = np.concatenate([[0], np.cumsum(sizes)])
    ws = [w_in[:, int(offs[i]):int(offs[i + 1])].astype(BF16) for i in range(len(sizes))]
    out_dtypes = (BF16, BF16, BF16, F32, F32, BF16, F32, F32)
    row = lambda n: pl.BlockSpec((tm, n), lambda i: (i, 0))
    full = lambda n: pl.BlockSpec((D, n), lambda i: (0, 0))
    return pl.pallas_call(
        _in_proj_kernel,
        grid=(T // tm,),
        in_specs=[row(D), pl.BlockSpec((1, D), lambda i: (0, 0))] + [full(n) for n in sizes],
        out_specs=[row(n) for n in sizes],
        out_shape=[jax.ShapeDtypeStruct((T, n), dt) for n, dt in zip(sizes, out_dtypes)],
        compiler_params=_params("parallel"),
        name="in_proj",
    )(x2, ln_g.reshape(1, D), *ws)


SB_ZERO_WEIGHT_BITS = 160.0


SOFTPLUS2_CLAMP = 126.0


def _softplus2(z):
    return jnp.maximum(z, jnp.log2(1.0 + jnp.exp2(jnp.minimum(z, SOFTPLUS2_CLAMP))))


def _sb_query_block(q_ref, k_ref, v_ref, o_ref, qh_ref, acc_ref, carry_ref, *, row0, qi, tq, n_pairs,
                    alongside=None):
    qrows = slice(row0, row0 + tq)
    r = lax.broadcasted_iota(jnp.int32, (tq, tq), 0)
    c = lax.broadcasted_iota(jnp.int32, (tq, tq), 1)
    causal = c < r
    later = (r > c).astype(BF16)
    head0 = lax.broadcasted_iota(jnp.int32, (1, LANES), 1) < SB_HEAD_DIM

    for p in range(n_pairs):
        q2 = q_ref[qrows, p * LANES:(p + 1) * LANES]
        zero = jnp.zeros_like(q2)
        qh_ref[p, 0:tq] = jnp.where(head0, q2, zero)
        qh_ref[p, tq:2 * tq] = jnp.where(head0, zero, q2)
    acc_ref[...] = jnp.zeros_like(acc_ref)
    carry_ref[...] = jnp.zeros_like(carry_ref)
    causal2 = jnp.concatenate([causal, causal], axis=0)

    def key_block(kb, diag):
        rows = pl.ds(pl.multiple_of(kb * tq, tq), tq)
        pairs = range(n_pairs)
        zs = {}

        def scores(p):
            zs[p] = _dot_nt(qh_ref[p], k_ref[rows, pl.ds(p * LANES, LANES)])
        scores(0)
        scores(1)

        def suffix_stage(p):
            sp = _softplus2(zs[p])
            if diag:
                sp = jnp.where(causal2, sp, 0.0)
            own = zs.pop(p) - sp
            return own, sp[:, 0:1], _dot(sp.astype(BF16), later)

        def weight_stage(p, own, sp0, run):
            v2 = v_ref[rows, pl.ds(p * LANES, LANES)]
            vzero = jnp.zeros_like(v2)
            v_stack = jnp.concatenate([jnp.where(head0, v2, vzero), jnp.where(head0, vzero, v2)], axis=0)
            carry = carry_ref[p]
            w = jnp.exp2(own - run - jnp.concatenate([carry] * (tq // LANES), axis=1))
            if diag:
                w = jnp.where(causal2, w, 0.0)
            w = w.astype(BF16)
            acc_ref[p] += _dot(jnp.concatenate([w[0:tq], w[tq:2 * tq]], axis=1), v_stack)
            carry_ref[p] = carry + jnp.broadcast_to(run[:, 0:1] + sp0, (2 * tq, LANES))

        pending = None
        for p in pairs:
            if p + 2 < n_pairs:
                scores(p + 2)
            staged = suffix_stage(p)
            if pending is not None:
                weight_stage(*pending)
            pending = (p,) + staged
        weight_stage(*pending)

    def min_carry():
        m = carry_ref[0]
        for p in range(1, n_pairs):
            m = jnp.minimum(m, carry_ref[p])
        return jnp.min(m)

    def cond(state):
        kb, m = state
        return jnp.logical_and(kb >= 0, m < SB_ZERO_WEIGHT_BITS)

    def body(state):
        kb, _ = state
        key_block(kb, False)
        return kb - 1, min_carry()

    if alongside is not None:
        alongside()
    key_block(qi, True)
    lax.while_loop(cond, body, (qi - 1, min_carry()))
    for p in range(n_pairs):
        o_ref[qrows, p * LANES:(p + 1) * LANES] = acc_ref[p].astype(o_ref.dtype)


GLA_CHUNK = 64
GLA_LEVELS = (32, 16, 8, 4, 2, 1)


def _gla_tables(C=GLA_CHUNK):
    t = np.arange(C)[:, None]
    j = np.arange(C)[None, :]
    blocks = [(j <= t), (j > t)]
    masks = []
    for half in GLA_LEVELS:
        blk = 2 * half
        u = (t // blk) * blk + half - 1
        second = (t % blk) >= half
        blocks.append((second & (j > u) & (j <= t)) | ((~second) & (j > t) & (j <= u)))
        s = j
        masks.append(second & ((s % blk) < half) & ((s // blk) == (t // blk)))
    blocks.append(np.ones((8, C), bool))
    table = np.concatenate(blocks, axis=0).astype(np.float32)
    table = np.concatenate([table, table], axis=1)
    return table, np.concatenate(masks, axis=0).astype(np.float32)


def _split_bf16(a):
    hi = a.astype(BF16)
    lo = (a - hi.astype(F32)).astype(BF16)
    return hi, lo


def _gla_head_pair(p, gq_ref, gk_ref, gv_ref, gr_ref, lr_ref, w2_ref, gb_ref, ng_ref, tab_ref, msk_ref,
                   o_ref, state_ref, *, n_chunks):
    C = GLA_CHUNK
    n_lv = len(GLA_LEVELS)

    head0 = lax.broadcasted_iota(jnp.int32, (1, LANES), 1) < GLA_DK
    eye = (lax.broadcasted_iota(jnp.int32, (C, C), 0) == lax.broadcasted_iota(jnp.int32, (C, C), 1))
    w2_hi, w2_lo = _split_bf16(w2_ref[...])
    lr_hi, lr_lo = _split_bf16(lr_ref[...])

    def stack_heads(a):
        return jnp.concatenate([jnp.where(head0, a, 0.0), jnp.where(head0, 0.0, a)], axis=0)

    cols = slice(p * LANES, (p + 1) * LANES)
    pre = (_dot(lr_hi, w2_hi[:, cols]) + _dot(lr_hi, w2_lo[:, cols])
           + _dot(lr_lo, w2_hi[:, cols])) + gb_ref[:, cols]
    log2_a = -_softplus(-pre) * (LOG2_E / GLA_GATE_TAU)
    la_hi, la_lo = _split_bf16(log2_a)
    la = jnp.concatenate(
        [jnp.concatenate([la_hi[c * C:(c + 1) * C], la_lo[c * C:(c + 1) * C]], axis=0)
         for c in range(n_chunks)], axis=1)
    ex_all = jnp.exp2(_dot(tab_ref[...], la))
    st = state_ref[p]
    def score_stage(c):
        rows = slice(c * C, (c + 1) * C)
        ex = ex_all[:, c * LANES:(c + 1) * LANES]
        q2 = gq_ref[rows, cols]
        k2 = gk_ref[rows, cols]
        q_int = q2 * ex[0:C]
        k_dec = (k2 * ex[C:2 * C]).astype(BF16)
        dec_row = ex[(2 + n_lv) * C:(2 + n_lv) * C + 1]
        diag = jnp.sum(stack_heads(q2 * k2), axis=1, keepdims=True)
        scores = jnp.where(jnp.concatenate([eye, eye], axis=0), diag, 0.0)
        for l in range(n_lv):
            ex_l = ex[(2 + l) * C:(3 + l) * C]
            q_l = stack_heads(q2 * ex_l).astype(BF16)
            k_l = (k2 * ex_l).astype(BF16)
            m_l = msk_ref[l * C:(l + 1) * C, :] > 0.5
            scores = jnp.where(jnp.concatenate([m_l, m_l], axis=0), _dot_nt(q_l, k_l), scores)
        return c, q_int, k_dec, dec_row, scores.astype(BF16)

    def state_stage(st, c, q_int, k_dec, dec_row, scores):
        rows = slice(c * C, (c + 1) * C)
        o_inter = _dot_nt(stack_heads(q_int).astype(BF16), st.astype(BF16))
        upd = []
        for h in range(2):
            vcols = slice((2 * p + h) * GLA_DV, (2 * p + h + 1) * GLA_DV)
            v_h = gv_ref[rows, vcols]
            o = o_inter[h * C:(h + 1) * C] + _dot(scores[h * C:(h + 1) * C], v_h)
            inv = lax.rsqrt(jnp.mean(o * o, axis=-1, keepdims=True) + RMS_EPS)
            y = (o * inv) * ng_ref[:, vcols]
            g = gr_ref[rows, vcols]
            y = y * (g / (1.0 + jnp.exp(-g)))
            o_ref[rows, vcols] = y.astype(o_ref.dtype)
            v_t = v_h.astype(F32).T.astype(BF16)
            upd.append(_dot(v_t, k_dec))
        return st * dec_row + jnp.where(head0, upd[0], upd[1])

    pending = score_stage(0)
    for c in range(1, n_chunks):
        ahead = score_stage(c)
        st = state_stage(st, *pending)
        pending = ahead
    st = state_stage(st, *pending)
    state_ref[p] = st


def _mixers_kernel(q_ref, k_ref, v_ref, gq_ref, gk_ref, gv_ref, gr_ref, lr_ref, w2_ref, gb_ref, ng_ref,
                   tab_ref, msk_ref, osb_ref, ogla_ref, qh_ref, acc_ref, carry_ref, state_ref,
                   *, tq, n_pairs, n_chunks):
    blk = pl.program_id(1)
    n_q = q_ref.shape[0] // tq
    assert n_q == GLA_HEADS // 2

    @pl.when(blk == 0)
    def _():
        state_ref[...] = jnp.zeros_like(state_ref)

    for half in range(n_q):
        gla = functools.partial(_gla_head_pair, half, gq_ref, gk_ref, gv_ref, gr_ref, lr_ref, w2_ref, gb_ref,
                                ng_ref, tab_ref, msk_ref, ogla_ref, state_ref, n_chunks=n_chunks)
        _sb_query_block(q_ref, k_ref, v_ref, osb_ref, qh_ref, acc_ref, carry_ref, row0=half * tq,
                        qi=blk * n_q + half, tq=tq, n_pairs=n_pairs, alongside=gla)


def _mixers(q, k, v, gq, gk, gv, gr, lr, w2, gb, ng, B, S, tb, tq):
    table, masks = _gla_tables()
    table = jnp.asarray(table, BF16)
    masks = jnp.asarray(masks, F32)
    T = B * S
    W = q.shape[-1]
    n_pairs = W // LANES
    nblk = S // tb
    q3, k3, v3 = (a.reshape(B, S, W) for a in (q, k, v))
    blk3 = pl.BlockSpec((None, tb, W), lambda b, i: (b, i, 0))
    seq3 = pl.BlockSpec((None, S, W), lambda b, i: (b, 0, 0))
    row = lambda n: pl.BlockSpec((tb, n), lambda b, i: (b * nblk + i, 0))
    const = lambda a: pl.BlockSpec(a.shape, lambda b, i: (0,) * a.ndim)
    w2 = w2.astype(F32)
    gb2 = gb.reshape(1, GLA_KW)
    ng2 = ng.reshape(1, GLA_VW)
    o_sb, o_gla = pl.pallas_call(
        functools.partial(_mixers_kernel, tq=tq, n_pairs=n_pairs, n_chunks=tb // GLA_CHUNK),
        grid=(B, nblk),
        in_specs=[blk3, seq3, seq3, row(GLA_KW), row(GLA_KW), row(GLA_VW), row(GLA_VW), row(GLA_GATE_RANK),
                  const(w2), const(gb2), const(ng2), const(table), const(masks)],
        out_specs=[blk3, row(GLA_VW)],
        out_shape=[jax.ShapeDtypeStruct((B, S, W), BF16), jax.ShapeDtypeStruct((T, GLA_VW), BF16)],
        scratch_shapes=[pltpu.VMEM((n_pairs, 2 * tq, LANES), BF16),
                        pltpu.VMEM((n_pairs, tq, LANES), F32),
                        pltpu.VMEM((n_pairs, 2 * tq, LANES), F32),
                        pltpu.VMEM((GLA_HEADS // 2, GLA_DV, 2 * GLA_DK), F32)],
        compiler_params=_params("parallel", "arbitrary"),
        name="mixers",
    )(q3, k3, v3, gq, gk, gv, gr, lr, w2, gb2, ng2, table, masks)
    return o_sb.reshape(T, W), o_gla


SUBLANES = 8


def _store_token_tiles(ref, value, accumulate=False):
    n = value.shape[0]
    for s in range(SUBLANES):
        rows = pl.ds(s, n, stride=SUBLANES)
        chunk = value[:, s * LANES:(s + 1) * LANES]
        ref[rows, :] = ref[rows, :] + chunk if accumulate else chunk


def _load_token_tiles(ref, n):
    return jnp.concatenate([ref[pl.ds(s, n, stride=SUBLANES), :] for s in range(SUBLANES)], axis=1)


def _out_proj_kernel(x_ref, oa_ref, ob_ref, wa_ref, wb_ref, g_ref, wr_ref, br_ref, tri_ref,
                     x1_ref, h2_ref, ri_ref, rf_ref, cnt_ref):
    @pl.when(pl.program_id(0) == 0)
    def _():
        cnt_ref[...] = jnp.zeros_like(cnt_ref)

    half = tri_ref.shape[0]
    halves = [slice(i * half, (i + 1) * half) for i in range(x_ref.shape[0] // half)]
    w_hi, w_lo = _split_bf16(wr_ref[...])
    w_cat = jnp.concatenate([w_hi, w_lo], axis=1)
    lane = lax.broadcasted_iota(jnp.int32, (half, LANES), 1).astype(F32)
    neg = -jnp.inf

    def residual_stage(rows):
        x1 = x_ref[rows, :] + _dot(oa_ref[rows, :], wa_ref[...]) + _dot(ob_ref[rows, :], wb_ref[...])
        x1_ref[rows, :] = x1
        inv = lax.rsqrt(jnp.mean(x1 * x1, axis=-1, keepdims=True) + RMS_EPS)
        h2 = (x1 * inv) * g_ref[...]
        _store_token_tiles(h2_ref.at[pl.ds(rows.start * SUBLANES, half * SUBLANES), :], h2)
        return _split_bf16(h2)

    def logits_stage(h_hi, h_lo):
        hw = _dot(h_hi, w_cat)
        return hw[:, :LANES] + hw[:, LANES:] + _dot(h_lo, w_hi) + br_ref[...]

    def first_max(vals):
        m = jnp.max(vals, axis=1, keepdims=True)
        idx = jnp.min(jnp.where(vals == m, lane, float(LANES)), axis=1, keepdims=True)
        return m, idx

    def select_stage(rows, logits):
        is_group = lane < N_GROUPS
        g_max, g_idx = first_max(jnp.where(is_group, logits, neg))
        g_p = 1.0 / jnp.sum(jnp.where(is_group, jnp.exp(logits - g_max), 0.0), axis=1, keepdims=True)
        lo_lane = N_GROUPS + EXPERTS_PER_GROUP * g_idx
        sel = jnp.where((lane >= lo_lane) & (lane < lo_lane + EXPERTS_PER_GROUP), logits, neg)
        m1, i1 = first_max(sel)
        m2, i2 = first_max(jnp.where(lane == i1, neg, sel))
        e = jnp.exp(m2 - m1)
        gate1 = g_p / (1.0 + e)
        gate2 = g_p * e / (1.0 + e)
        rf_ref[rows, :] = jnp.where(lane == 0, gate1, jnp.where(lane == 1, gate2, 0.0))
        e1 = i1 - N_GROUPS
        e2 = i2 - N_GROUPS
        hits = (lane == e1).astype(F32) + (lane == e2).astype(F32)
        return e1, e2, hits

    pieces = [residual_stage(rows) for rows in halves]
    logits = [logits_stage(*hl) for hl in pieces]
    chosen = [select_stage(rows, lg) for rows, lg in zip(halves, logits)]

    running = cnt_ref[...]
    for i, (e1, e2, hits) in enumerate(chosen):
        before = _dot(tri_ref[...], hits.astype(BF16)) + running
        r1 = jnp.sum(jnp.where(lane == e1, before, 0.0), axis=1, keepdims=True)
        r2 = jnp.sum(jnp.where(lane == e2, before, 0.0), axis=1, keepdims=True)
        running = running + jnp.sum(hits, axis=0, keepdims=True)
        packed = jnp.where(lane == 0, e1, jnp.where(lane == 1, e2,
                           jnp.where(lane == 2, r1, jnp.where(lane == 3, r2, 0.0))))
        for j in range(half // LANES):
            rows_t = packed[j * LANES:(j + 1) * LANES, :].T
            jj = i * (half // LANES) + j
            for k in range(2 * TOP_K):
                ri_ref[k, jj:jj + 1, :] = rows_t[k:k + 1, :].astype(jnp.int32)
    cnt_ref[...] = running


def _out_proj(x2, o_sb, o_gla, w_out, ln_g, w_group, b_group, w_expert, b_expert, tm):
    T, D = x2.shape
    wa = w_out[:SB_WIDTH].astype(BF16)
    wb = w_out[SB_WIDTH:].astype(BF16)
    n_r = N_GROUPS + N_EXPERTS
    wr = jnp.zeros((D, LANES), F32).at[:, :n_r].set(jnp.concatenate([w_group, w_expert], axis=1))
    br = jnp.zeros((1, LANES), F32).at[0, :n_r].set(jnp.concatenate([b_group, b_expert]))
    row = lambda n: pl.BlockSpec((tm, n), lambda i: (i, 0))
    const = lambda a: pl.BlockSpec(a.shape, lambda i: (0,) * a.ndim)
    g2 = ln_g.reshape(1, D)
    half = tm // 2
    tri = jnp.asarray(np.tril(np.ones((half, half), np.float32), -1), BF16)
    return pl.pallas_call(
        _out_proj_kernel,
        grid=(T // tm,),
        in_specs=[row(D), row(SB_WIDTH), row(GLA_VW), const(wa), const(wb), const(g2), const(wr), const(br),
                  const(tri)],
        out_specs=[row(D), pl.BlockSpec((tm * SUBLANES, LANES), lambda i: (i, 0)),
                   pl.BlockSpec((2 * TOP_K, tm // LANES, LANES), lambda i: (0, i, 0)), row(LANES),
                   pl.BlockSpec((1, LANES), lambda i: (0, 0))],
        out_shape=[jax.ShapeDtypeStruct((T, D), F32), jax.ShapeDtypeStruct((T * SUBLANES, LANES), F32),
                   jax.ShapeDtypeStruct((2 * TOP_K, T // LANES, LANES), jnp.int32),
                   jax.ShapeDtypeStruct((T, LANES), F32), jax.ShapeDtypeStruct((1, LANES), F32)],
        compiler_params=_params("arbitrary"),
        name="out_proj",
    )(x2, o_sb, o_gla, wa, wb, g2, wr, br, tri)


def _route_tables(route, counts, n_rows, tm):
    counts = counts[0, :N_EXPERTS].astype(jnp.int32)
    ends = jnp.cumsum(counts)
    starts = ends - counts
    route = route.reshape(2 * TOP_K, -1)
    expert = route[:TOP_K, :, None] == jnp.arange(N_EXPERTS, dtype=jnp.int32)
    dest = (jnp.sum(jnp.where(expert, starts, 0), axis=-1) + route[TOP_K:]).reshape(-1)

    n_items = n_rows // tm + N_EXPERTS + EXPERT_GATHER_DEPTH
    items_e = (counts + tm - 1) // tm
    item_hi = jnp.cumsum(items_e)
    item_lo = item_hi - items_e
    i = jnp.arange(n_items, dtype=jnp.int32)
    e_of = jnp.minimum(jnp.sum(item_hi[None, :] <= i[:, None], axis=1), N_EXPERTS - 1)
    owner = e_of[:, None] == jnp.arange(N_EXPERTS, dtype=jnp.int32)
    pick = lambda tab: jnp.sum(jnp.where(owner, tab, 0), axis=-1)
    item_row = jnp.where(i < item_hi[-1], pick(starts) + (i - pick(item_lo)) * tm, n_rows)
    return (dest, item_row.astype(jnp.int32), item_lo.astype(jnp.int32), item_hi.astype(jnp.int32))


ROW_ISSUE_UNROLL = 8
EXPERT_GATHER_DEPTH = 4


def _token_tile(ref, t):
    return ref.at[pl.ds(pl.multiple_of(t * SUBLANES, SUBLANES), SUBLANES), :]


def _wait_token_gather(src_ref, dst_ref, sem):
    pltpu.make_async_copy(src_ref.at[pl.ds(0, dst_ref.shape[0]), :], dst_ref, sem).wait()


def _expert_kernel(row_ref, lo_ref, hi_ref, dest_ref,
                   h_ref, wg_ref, wu_ref, wd_ref, y_ref, *scratch, tm):
    depth = EXPERT_GATHER_DEPTH
    xbufs = scratch[:depth]
    ybuf, tok_ref, wg_bf, wu_bf, wd_bf, gsem, wsem = scratch[depth:]
    e = pl.program_id(0)
    n_rows = dest_ref.shape[0]
    n_tok = n_rows // TOP_K
    total = hi_ref[N_EXPERTS - 1]

    def gather_copy(i, r, slot):
        t = tok_ref[row_ref[i] + r]
        return pltpu.make_async_copy(_token_tile(h_ref, t), _token_tile(xbufs[slot], r), gsem.at[slot])

    def out_copy(row, slot):
        rows = pl.ds(pl.multiple_of(row * SUBLANES, SUBLANES), tm * SUBLANES)
        return pltpu.make_async_copy(ybuf.at[slot], y_ref.at[rows, :], wsem)

    @pl.when(e == 0)
    def _():
        def invert(t, _):
            for k in range(TOP_K):
                tok_ref[dest_ref[k * n_tok + t]] = t
            return 0
        lax.fori_loop(0, n_tok, invert, 0, unroll=ROW_ISSUE_UNROLL)

        def spare(r, _):
            tok_ref[n_rows + r] = 0
            return 0
        lax.fori_loop(0, tm, spare, 0, unroll=ROW_ISSUE_UNROLL)
        ybuf[1] = jnp.zeros(ybuf.shape[1:], ybuf.dtype)
        out_copy(n_rows, 1).start()
        for ahead in range(depth - 1):
            def first(r, _):
                gather_copy(ahead, r, ahead).start()
                return 0
            lax.fori_loop(0, tm, first, 0, unroll=ROW_ISSUE_UNROLL)

    wg_bf[...] = wg_ref[...].astype(BF16)
    wu_bf[...] = wu_ref[...].astype(BF16)
    wd_bf[...] = wd_ref[...].astype(BF16)

    def item_in_slot(i, slot):
        _wait_token_gather(h_ref, xbufs[slot], gsem.at[slot])
        nslot = (slot + depth - 1) % depth
        for r in range(tm):
            gather_copy(i + depth - 1, r, nslot).start(priority=1)
        xb = _load_token_tiles(xbufs[slot], tm).astype(BF16)
        g = _dot(xb, wg_bf[...])
        u = _dot(xb, wu_bf[...])
        hidden = ((g / (1.0 + jnp.exp(-g))) * u).astype(BF16)
        _store_token_tiles(ybuf.at[slot % 2], _dot(hidden, wd_bf[...]))
        out_copy(0, slot % 2).wait()
        out_copy(row_ref[i], slot % 2).start()

    def item(i, _):
        slot = lax.rem(i, depth)
        for s in range(depth):
            @pl.when(slot == s)
            def _():
                item_in_slot(i, s)
        return 0

    lax.fori_loop(lo_ref[e], hi_ref[e], item, 0)

    @pl.when(e == pl.num_programs(0) - 1)
    def _():
        out_copy(0, 0).wait()
        for ahead in range(depth - 1):
            for s in range(depth):
                @pl.when(lax.rem(total + ahead, depth) == s)
                def _():
                    _wait_token_gather(h_ref, xbufs[s], gsem.at[s])


def _expert_ffn(h2_tiles, dest, tables, w_gate, w_up, w_down, tm):
    N = dest.shape[0]
    _, D, F = w_gate.shape
    assert D == SUBLANES * LANES
    item_row, item_lo, item_hi = tables
    wspec = lambda shape: pl.BlockSpec((None,) + shape, lambda e, *_: (e, 0, 0))
    return pl.pallas_call(
        functools.partial(_expert_kernel, tm=tm),
        grid_spec=pltpu.PrefetchScalarGridSpec(
            num_scalar_prefetch=4,
            grid=(N_EXPERTS,),
            in_specs=[pl.BlockSpec(memory_space=pl.ANY), wspec((D, F)), wspec((D, F)), wspec((F, D))],
            out_specs=pl.BlockSpec(memory_space=pl.ANY),
            scratch_shapes=[pltpu.VMEM((tm * SUBLANES, LANES), F32) for _ in range(EXPERT_GATHER_DEPTH)]
                           + [pltpu.VMEM((2, tm * SUBLANES, LANES), F32),
                            pltpu.SMEM((N + tm,), jnp.int32),
                            pltpu.VMEM((D, F), BF16), pltpu.VMEM((D, F), BF16), pltpu.VMEM((F, D), BF16),
                            pltpu.SemaphoreType.DMA((EXPERT_GATHER_DEPTH,)), pltpu.SemaphoreType.DMA(())],
        ),
        out_shape=jax.ShapeDtypeStruct(((N + tm) * SUBLANES, LANES), F32),
        compiler_params=_params("arbitrary"),
        name="expert_ffn",
    )(item_row, item_lo, item_hi, dest, h2_tiles, w_gate, w_up, w_down)


COMBINE_RING = 4
COMBINE_AHEAD = 3


def _combine_kernel(dest_ref, x1_ref, rf_ref, g_ref, y_ref, o_ref, *scratch, tb):
    bufs, sem = scratch[:COMBINE_RING], scratch[COMBINE_RING]
    i = pl.program_id(0)
    n_steps = pl.num_programs(0)
    n_tok = n_steps * COMBINE_RING * tb

    def copies(j, slot, r):
        return [pltpu.make_async_copy(_token_tile(y_ref, dest_ref[k * n_tok + j * tb + r]),
                                      _token_tile(bufs[slot].at[k], r), sem.at[slot])
                for k in range(TOP_K)]

    @pl.when(i == 0)
    def _():
        for slot in range(COMBINE_AHEAD):
            def first(r, _):
                for c in copies(slot, slot, r):
                    c.start()
                return 0
            lax.fori_loop(0, tb, first, 0, unroll=ROW_ISSUE_UNROLL // TOP_K)

    def sub_block(s, fetch):
        j = i * COMBINE_RING + s
        for k in range(TOP_K):
            _wait_token_gather(y_ref, bufs[s].at[k], sem.at[s])
        if fetch:
            for r in range(tb):
                for k, c in enumerate(copies(j + COMBINE_AHEAD, (s + COMBINE_AHEAD) % COMBINE_RING, r)):
                    c.start(priority=k)
        rows = slice(s * tb, (s + 1) * tb)
        rf = rf_ref[rows, :]
        x = (x1_ref[rows, :] + rf[:, 0:1] * _load_token_tiles(bufs[s].at[0], tb)
             + rf[:, 1:2] * _load_token_tiles(bufs[s].at[1], tb))
        inv = lax.rsqrt(jnp.mean(x * x, axis=-1, keepdims=True) + RMS_EPS)
        o_ref[rows, :] = (x * inv) * g_ref[...]

    for s in range(COMBINE_RING):
        if s + COMBINE_AHEAD < COMBINE_RING:
            sub_block(s, True)
        else:
            @pl.when(i + 1 < n_steps)
            def _():
                sub_block(s, True)

            @pl.when(i + 1 >= n_steps)
            def _():
                sub_block(s, False)


def _combine(x1, route_f, y_tiles, dest, ln_g, tb):
    T, D = x1.shape
    g2 = ln_g.reshape(1, D)
    step = COMBINE_RING * tb
    return pl.pallas_call(
        functools.partial(_combine_kernel, tb=tb),
        grid_spec=pltpu.PrefetchScalarGridSpec(
            num_scalar_prefetch=1,
            grid=(T // step,),
            in_specs=[pl.BlockSpec((step, D), lambda i, d: (i, 0)),
                      pl.BlockSpec((step, LANES), lambda i, d: (i, 0)),
                      pl.BlockSpec((1, D), lambda i, d: (0, 0)),
                      pl.BlockSpec(memory_space=pl.ANY)],
            out_specs=pl.BlockSpec((step, D), lambda i, d: (i, 0)),
            scratch_shapes=[pltpu.VMEM((TOP_K, tb * SUBLANES, LANES), F32) for _ in range(COMBINE_RING)]
                           + [pltpu.SemaphoreType.DMA((COMBINE_RING,))],
        ),
        out_shape=jax.ShapeDtypeStruct((T, D), F32),
        compiler_params=_params("arbitrary"),
        name="combine",
    )(dest, x1, route_f, g2, y_tiles)


MXU_WIDTH = 256


class _Tiles(NamedTuple):
    in_proj: int
    query: int
    mixers: int
    out_proj: int
    expert: int
    combine: int


def _tiles(n_tokens, seq_len):
    query = MXU_WIDTH
    tiles = _Tiles(in_proj=4 * MXU_WIDTH, query=query, mixers=query * (GLA_HEADS // 2),
                   out_proj=SUBLANES * LANES, expert=MXU_WIDTH, combine=MXU_WIDTH)
    assert seq_len % tiles.mixers == 0 and tiles.mixers % GLA_CHUNK == 0
    assert n_tokens % tiles.in_proj == 0 and n_tokens % tiles.out_proj == 0
    assert n_tokens % (COMBINE_RING * tiles.combine) == 0 and (TOP_K * n_tokens) % tiles.expert == 0
    return tiles


def _layer(x2, B, S, ln1_g, w_in, w2, gb, ng, w_out, ln2_g, w_group, b_group, w_expert, b_expert,
           w_gate, w_up, w_down, ln_f_g, tiles):
    q, k, v, gq, gk, gv, gr, lr = _in_proj(x2, ln1_g, w_in, tiles.in_proj)
    o_sb, o_gla = _mixers(q, k, v, gq, gk, gv, gr, lr, w2, gb, ng, B, S, tiles.mixers, tiles.query)
    x1, h2_tiles, route_i, route_f, counts = _out_proj(x2, o_sb, o_gla, w_out, ln2_g, w_group, b_group,
                                                       w_expert, b_expert, tiles.out_proj)
    T = x2.shape[0]
    dest, *tables = _route_tables(route_i, counts, TOP_K * T, tiles.expert)
    y_tiles = _expert_ffn(h2_tiles, dest, tables, w_gate, w_up, w_down, tiles.expert)
    return _combine(x1, route_f, y_tiles, dest, ln_f_g, tiles.combine)


def kernel(x, ln1_g, w_in, gla_gate_w2, gla_gate_b, gla_norm_g, w_out, ln2_g, w_group, b_group, w_expert, b_expert, exp_w_gate, exp_w_up, exp_w_down, ln_f_g):
    B, S, D = x.shape
    assert ln1_g.shape[0] == 1, "single-layer stack"
    assert D == SUBLANES * LANES, "one (SUBLANES, LANES) tile per token"
    out = _layer(x.reshape(B * S, D), B, S, ln1_g[0], w_in[0], gla_gate_w2[0], gla_gate_b[0],
                 gla_norm_g[0], w_out[0], ln2_g[0], w_group[0], b_group[0], w_expert[0], b_expert[0],
                 exp_w_gate[0], exp_w_up[0], exp_w_down[0], ln_f_g, _tiles(B * S, S))
    return out.reshape(B, S, D)
```

```python
import functools
from typing import NamedTuple

import numpy as np
import jax
import jax.numpy as jnp
from jax import lax
from jax.experimental import pallas as pl
from jax.experimental.pallas import tpu as pltpu

SB_HEADS = 8
SB_HEAD_DIM = 64
SB_WIDTH = SB_HEADS * SB_HEAD_DIM
GLA_HEADS = 4
GLA_DK = 64
GLA_DV = 128
GLA_KW = GLA_HEADS * GLA_DK
GLA_VW = GLA_HEADS * GLA_DV
GLA_GATE_RANK = 16
GLA_GATE_TAU = 16.0
N_GROUPS = 4
EXPERTS_PER_GROUP = 8
N_EXPERTS = N_GROUPS * EXPERTS_PER_GROUP
TOP_K = 2
RMS_EPS = 1e-6
LOG2_E = 1.4426950408889634

LANES = 128
VMEM_LIMIT_BYTES = 56 * 1024 * 1024

F32 = jnp.float32
BF16 = jnp.bfloat16


def _dot(a, b):
    return jnp.dot(a, b, preferred_element_type=F32)


def _dot_nt(a, b):
    return lax.dot_general(a, b, (((1,), (1,)), ((), ())), preferred_element_type=F32)


def _softplus(z):
    return jnp.maximum(z, 0.0) + jnp.log(1.0 + jnp.exp(-jnp.abs(z)))


def _params(*sem):
    return pltpu.CompilerParams(dimension_semantics=sem, vmem_limit_bytes=VMEM_LIMIT_BYTES)


def _in_proj_kernel(x_ref, g_ref, wq, wk, wv, wgq, wgk, wgv, wgr, wlr,
                    oq, ok, ov, ogq, ogk, ogv, ogr, olr):
    x = x_ref[...]
    inv = lax.rsqrt(jnp.mean(x * x, axis=-1, keepdims=True) + RMS_EPS)
    h = ((x * inv) * g_ref[...]).astype(BF16)
    oq[...] = (_dot(h, wq[...]) * (SB_HEAD_DIM ** -0.5 * LOG2_E)).astype(oq.dtype)
    ok[...] = _dot(h, wk[...]).astype(ok.dtype)
    ov[...] = _dot(h, wv[...]).astype(ov.dtype)
    ogq[...] = (_dot(h, wgq[...]) * (GLA_DK ** -0.5)).astype(ogq.dtype)
    ogk[...] = _dot(h, wgk[...]).astype(ogk.dtype)
    ogv[...] = _dot(h, wgv[...]).astype(ogv.dtype)
    ogr[...] = _dot(h, wgr[...]).astype(ogr.dtype)
    olr[...] = _dot(h, wlr[...]).astype(olr.dtype)


def _in_proj(x2, ln_g, w_in, tm):
    T, D = x2.shape
    sizes = (SB_WIDTH, SB_WIDTH, SB_WIDTH, GLA_KW, GLA_KW, GLA_VW, GLA_VW, GLA_GATE_RANK)
    offs = np.concatenate([[0], np.cumsum(sizes)])
    ws = [w_in[:, int(offs[i]):int(offs[i + 1])].astype(BF16) for i in range(len(sizes))]
    out_dtypes = (BF16, BF16, BF16, F32, F32, BF16, F32, F32)
    row = lambda n: pl.BlockSpec((tm, n), lambda i: (i, 0))
    full = lambda n: pl.BlockSpec((D, n), lambda i: (0, 0))
    return pl.pallas_call(
        _in_proj_kernel,
        grid=(T // tm,),
        in_specs=[row(D), pl.BlockSpec((1, D), lambda i: (0, 0))] + [full(n) for n in sizes],
        out_specs=[row(n) for n in sizes],
        out_shape=[jax.ShapeDtypeStruct((T, n), dt) for n, dt in zip(sizes, out_dtypes)],
        compiler_params=_params("parallel"),
        name="in_proj",
    )(x2, ln_g.reshape(1, D), *ws)


SB_ZERO_WEIGHT_BITS = 160.0


SOFTPLUS2_CLAMP = 126.0


def _softplus2(z):
    return jnp.maximum(z, jnp.log2(1.0 + jnp.exp2(jnp.minimum(z, SOFTPLUS2_CLAMP))))


def _sb_query_block(q_ref, k_ref, v_ref, o_ref, qh_ref, acc_ref, carry_ref, *, row0, qi, tq, n_pairs,
                    alongside=None):
    qrows = slice(row0, row0 + tq)
    r = lax.broadcasted_iota(jnp.int32, (tq, tq), 0)
    c = lax.broadcasted_iota(jnp.int32, (tq, tq), 1)
    causal = c < r
    later = (r > c).astype(BF16)
    head0 = lax.broadcasted_iota(jnp.int32, (1, LANES), 1) < SB_HEAD_DIM

    for p in range(n_pairs):
        q2 = q_ref[qrows, p * LANES:(p + 1) * LANES]
        zero = jnp.zeros_like(q2)
        qh_ref[p, 0:tq] = jnp.where(head0, q2, zero)
        qh_ref[p, tq:2 * tq] = jnp.where(head0, zero, q2)
    acc_ref[...] = jnp.zeros_like(acc_ref)
    carry_ref[...] = jnp.zeros_like(carry_ref)
    causal2 = jnp.concatenate([causal, causal], axis=0)

    def key_block(kb, diag):
        rows = pl.ds(pl.multiple_of(kb * tq, tq), tq)
        pairs = range(n_pairs)
        zs = {}

        def scores(p):
            zs[p] = _dot_nt(qh_ref[p], k_ref[rows, pl.ds(p * LANES, LANES)])
        scores(0)

        def suffix_stage(p):
            sp = _softplus2(zs[p])
            if diag:
                sp = jnp.where(causal2, sp, 0.0)
            own = zs.pop(p) - sp
            return own, sp[:, 0:1], _dot(sp.astype(BF16), later)

        def weight_stage(p, own, sp0, run):
            v2 = v_ref[rows, pl.ds(p * LANES, LANES)]
            vzero = jnp.zeros_like(v2)
            v_stack = jnp.concatenate([jnp.where(head0, v2, vzero), jnp.where(head0, vzero, v2)], axis=0)
            carry = carry_ref[p]
            w = jnp.exp2(own - run - jnp.concatenate([carry] * (tq // LANES), axis=1))
            if diag:
                w = jnp.where(causal2, w, 0.0)
            w = w.astype(BF16)
            acc_ref[p] += _dot(jnp.concatenate([w[0:tq], w[tq:2 * tq]], axis=1), v_stack)
            carry_ref[p] = carry + jnp.broadcast_to(run[:, 0:1] + sp0, (2 * tq, LANES))

        pending = None
        for p in pairs:
            if p + 1 < n_pairs:
                scores(p + 1)
            staged = suffix_stage(p)
            if pending is not None:
                weight_stage(*pending)
            pending = (p,) + staged
        weight_stage(*pending)

    def min_carry():
        m = carry_ref[0]
        for p in range(1, n_pairs):
            m = jnp.minimum(m, carry_ref[p])
        return jnp.min(m)

    def cond(state):
        kb, m = state
        return jnp.logical_and(kb >= 0, m < SB_ZERO_WEIGHT_BITS)

    def body(state):
        kb, _ = state
        key_block(kb, False)
        return kb - 1, min_carry()

    if alongside is not None:
        alongside()
    key_block(qi, True)
    lax.while_loop(cond, body, (qi - 1, min_carry()))
    for p in range(n_pairs):
        o_ref[qrows, p * LANES:(p + 1) * LANES] = acc_ref[p].astype(o_ref.dtype)


GLA_CHUNK = 64
GLA_LEVELS = (32, 16, 8, 4, 2, 1)


def _gla_tables(C=GLA_CHUNK):
    t = np.arange(C)[:, None]
    j = np.arange(C)[None, :]
    blocks = [(j <= t), (j > t)]
    masks = []
    for half in GLA_LEVELS:
        blk = 2 * half
        u = (t // blk) * blk + half - 1
        second = (t % blk) >= half
        blocks.append((second & (j > u) & (j <= t)) | ((~second) & (j > t) & (j <= u)))
        s = j
        masks.append(second & ((s % blk) < half) & ((s // blk) == (t // blk)))
    blocks.append(np.ones((8, C), bool))
    table = np.concatenate(blocks, axis=0).astype(np.float32)
    table = np.concatenate([table, table], axis=1)
    return table, np.concatenate(masks, axis=0).astype(np.float32)


def _split_bf16(a):
    hi = a.astype(BF16)
    lo = (a - hi.astype(F32)).astype(BF16)
    return hi, lo


def _gla_head_pair(p, gq_ref, gk_ref, gv_ref, gr_ref, lr_ref, w2_ref, gb_ref, ng_ref, tab_ref, msk_ref,
                   o_ref, state_ref, *, n_chunks):
    C = GLA_CHUNK
    n_lv = len(GLA_LEVELS)

    head0 = lax.broadcasted_iota(jnp.int32, (1, LANES), 1) < GLA_DK
    eye = (lax.broadcasted_iota(jnp.int32, (C, C), 0) == lax.broadcasted_iota(jnp.int32, (C, C), 1))
    w2_hi, w2_lo = _split_bf16(w2_ref[...])
    lr_hi, lr_lo = _split_bf16(lr_ref[...])

    def stack_heads(a):
        return jnp.concatenate([jnp.where(head0, a, 0.0), jnp.where(head0, 0.0, a)], axis=0)

    cols = slice(p * LANES, (p + 1) * LANES)
    pre = (_dot(lr_hi, w2_hi[:, cols]) + _dot(lr_hi, w2_lo[:, cols])
           + _dot(lr_lo, w2_hi[:, cols])) + gb_ref[:, cols]
    log2_a = -_softplus(-pre) * (LOG2_E / GLA_GATE_TAU)
    la_hi, la_lo = _split_bf16(log2_a)
    la = jnp.concatenate(
        [jnp.concatenate([la_hi[c * C:(c + 1) * C], la_lo[c * C:(c + 1) * C]], axis=0)
         for c in range(n_chunks)], axis=1)
    ex_all = jnp.exp2(_dot(tab_ref[...], la))
    st = state_ref[p]
    def score_stage(c):
        rows = slice(c * C, (c + 1) * C)
        ex = ex_all[:, c * LANES:(c + 1) * LANES]
        q2 = gq_ref[rows, cols]
        k2 = gk_ref[rows, cols]
        q_int = q2 * ex[0:C]
        k_dec = (k2 * ex[C:2 * C]).astype(BF16)
        dec_row = ex[(2 + n_lv) * C:(2 + n_lv) * C + 1]
        diag = jnp.sum(stack_heads(q2 * k2), axis=1, keepdims=True)
        scores = jnp.where(jnp.concatenate([eye, eye], axis=0), diag, 0.0)
        for l in range(n_lv):
            ex_l = ex[(2 + l) * C:(3 + l) * C]
            q_l = stack_heads(q2 * ex_l).astype(BF16)
            k_l = (k2 * ex_l).astype(BF16)
            m_l = msk_ref[l * C:(l + 1) * C, :] > 0.5
            scores = jnp.where(jnp.concatenate([m_l, m_l], axis=0), _dot_nt(q_l, k_l), scores)
        return c, q_int, k_dec, dec_row, scores.astype(BF16)

    def state_stage(st, c, q_int, k_dec, dec_row, scores):
        rows = slice(c * C, (c + 1) * C)
        o_inter = _dot_nt(stack_heads(q_int).astype(BF16), st.astype(BF16))
        upd = []
        for h in range(2):
            vcols = slice((2 * p + h) * GLA_DV, (2 * p + h + 1) * GLA_DV)
            v_h = gv_ref[rows, vcols]
            o = o_inter[h * C:(h + 1) * C] + _dot(scores[h * C:(h + 1) * C], v_h)
            inv = lax.rsqrt(jnp.mean(o * o, axis=-1, keepdims=True) + RMS_EPS)
            y = (o * inv) * ng_ref[:, vcols]
            g = gr_ref[rows, vcols]
            y = y * (g / (1.0 + jnp.exp(-g)))
            o_ref[rows, vcols] = y.astype(o_ref.dtype)
            v_t = v_h.astype(F32).T.astype(BF16)
            upd.append(_dot(v_t, k_dec))
        return st * dec_row + jnp.where(head0, upd[0], upd[1])

    pending = score_stage(0)
    for c in range(1, n_chunks):
        ahead = score_stage(c)
        st = state_stage(st, *pending)
        pending = ahead
    st = state_stage(st, *pending)
    state_ref[p] = st


def _mixers_kernel(q_ref, k_ref, v_ref, gq_ref, gk_ref, gv_ref, gr_ref, lr_ref, w2_ref, gb_ref, ng_ref,
                   tab_ref, msk_ref, osb_ref, ogla_ref, qh_ref, acc_ref, carry_ref, state_ref,
                   *, tq, n_pairs, n_chunks):
    blk = pl.program_id(1)
    n_q = q_ref.shape[0] // tq
    assert n_q == GLA_HEADS // 2

    @pl.when(blk == 0)
    def _():
        state_ref[...] = jnp.zeros_like(state_ref)

    for half in range(n_q):
        gla = functools.partial(_gla_head_pair, half, gq_ref, gk_ref, gv_ref, gr_ref, lr_ref, w2_ref, gb_ref,
                                ng_ref, tab_ref, msk_ref, ogla_ref, state_ref, n_chunks=n_chunks)
        _sb_query_block(q_ref, k_ref, v_ref, osb_ref, qh_ref, acc_ref, carry_ref, row0=half * tq,
                        qi=blk * n_q + half, tq=tq, n_pairs=n_pairs, alongside=gla)


def _mixers(q, k, v, gq, gk, gv, gr, lr, w2, gb, ng, B, S, tb, tq):
    table, masks = _gla_tables()
    table = jnp.asarray(table, BF16)
    masks = jnp.asarray(masks, F32)
    T = B * S
    W = q.shape[-1]
    n_pairs = W // LANES
    nblk = S // tb
    q3, k3, v3 = (a.reshape(B, S, W) for a in (q, k, v))
    blk3 = pl.BlockSpec((None, tb, W), lambda b, i: (b, i, 0))
    seq3 = pl.BlockSpec((None, S, W), lambda b, i: (b, 0, 0))
    row = lambda n: pl.BlockSpec((tb, n), lambda b, i: (b * nblk + i, 0))
    const = lambda a: pl.BlockSpec(a.shape, lambda b, i: (0,) * a.ndim)
    w2 = w2.astype(F32)
    gb2 = gb.reshape(1, GLA_KW)
    ng2 = ng.reshape(1, GLA_VW)
    o_sb, o_gla = pl.pallas_call(
        functools.partial(_mixers_kernel, tq=tq, n_pairs=n_pairs, n_chunks=tb // GLA_CHUNK),
        grid=(B, nblk),
        in_specs=[blk3, seq3, seq3, row(GLA_KW), row(GLA_KW), row(GLA_VW), row(GLA_VW), row(GLA_GATE_RANK),
                  const(w2), const(gb2), const(ng2), const(table), const(masks)],
        out_specs=[blk3, row(GLA_VW)],
        out_shape=[jax.ShapeDtypeStruct((B, S, W), BF16), jax.ShapeDtypeStruct((T, GLA_VW), BF16)],
        scratch_shapes=[pltpu.VMEM((n_pairs, 2 * tq, LANES), BF16),
                        pltpu.VMEM((n_pairs, tq, LANES), F32),
                        pltpu.VMEM((n_pairs, 2 * tq, LANES), F32),
                        pltpu.VMEM((GLA_HEADS // 2, GLA_DV, 2 * GLA_DK), F32)],
        compiler_params=_params("parallel", "arbitrary"),
        name="mixers",
    )(q3, k3, v3, gq, gk, gv, gr, lr, w2, gb2, ng2, table, masks)
    return o_sb.reshape(T, W), o_gla


SUBLANES = 8


def _store_token_tiles(ref, value, accumulate=False):
    n = value.shape[0]
    for s in range(SUBLANES):
        rows = pl.ds(s, n, stride=SUBLANES)
        chunk = value[:, s * LANES:(s + 1) * LANES]
        ref[rows, :] = ref[rows, :] + chunk if accumulate else chunk


def _load_token_tiles(ref, n):
    return jnp.concatenate([ref[pl.ds(s, n, stride=SUBLANES), :] for s in range(SUBLANES)], axis=1)


def _out_proj_kernel(x_ref, oa_ref, ob_ref, wa_ref, wb_ref, g_ref, wr_ref, br_ref, tri_ref,
                     x1_ref, h2_ref, ri_ref, rf_ref, cnt_ref):
    @pl.when(pl.program_id(0) == 0)
    def _():
        cnt_ref[...] = jnp.zeros_like(cnt_ref)

    half = tri_ref.shape[0]
    halves = [slice(i * half, (i + 1) * half) for i in range(x_ref.shape[0] // half)]
    w_hi, w_lo = _split_bf16(wr_ref[...])
    w_cat = jnp.concatenate([w_hi, w_lo], axis=1)
    lane = lax.broadcasted_iota(jnp.int32, (half, LANES), 1).astype(F32)
    neg = -jnp.inf

    def residual_stage(rows):
        x1 = x_ref[rows, :] + _dot(oa_ref[rows, :], wa_ref[...]) + _dot(ob_ref[rows, :], wb_ref[...])
        x1_ref[rows, :] = x1
        inv = lax.rsqrt(jnp.mean(x1 * x1, axis=-1, keepdims=True) + RMS_EPS)
        h2 = (x1 * inv) * g_ref[...]
        _store_token_tiles(h2_ref.at[pl.ds(rows.start * SUBLANES, half * SUBLANES), :], h2)
        return _split_bf16(h2)

    def logits_stage(h_hi, h_lo):
        hw = _dot(h_hi, w_cat)
        return hw[:, :LANES] + hw[:, LANES:] + _dot(h_lo, w_hi) + br_ref[...]

    def first_max(vals):
        m = jnp.max(vals, axis=1, keepdims=True)
        idx = jnp.min(jnp.where(vals == m, lane, float(LANES)), axis=1, keepdims=True)
        return m, idx

    def select_stage(rows, logits):
        is_group = lane < N_GROUPS
        g_max, g_idx = first_max(jnp.where(is_group, logits, neg))
        g_p = 1.0 / jnp.sum(jnp.where(is_group, jnp.exp(logits - g_max), 0.0), axis=1, keepdims=True)
        lo_lane = N_GROUPS + EXPERTS_PER_GROUP * g_idx
        sel = jnp.where((lane >= lo_lane) & (lane < lo_lane + EXPERTS_PER_GROUP), logits, neg)
        m1, i1 = first_max(sel)
        m2, i2 = first_max(jnp.where(lane == i1, neg, sel))
        e = jnp.exp(m2 - m1)
        gate1 = g_p / (1.0 + e)
        gate2 = g_p * e / (1.0 + e)
        rf_ref[rows, :] = jnp.where(lane == 0, gate1, jnp.where(lane == 1, gate2, 0.0))
        e1 = i1 - N_GROUPS
        e2 = i2 - N_GROUPS
        hits = (lane == e1).astype(F32) + (lane == e2).astype(F32)
        return e1, e2, hits

    pieces = [residual_stage(rows) for rows in halves]
    logits = [logits_stage(*hl) for hl in pieces]
    chosen = [select_stage(rows, lg) for rows, lg in zip(halves, logits)]

    running = cnt_ref[...]
    for i, (e1, e2, hits) in enumerate(chosen):
        before = _dot(tri_ref[...], hits.astype(BF16)) + running
        r1 = jnp.sum(jnp.where(lane == e1, before, 0.0), axis=1, keepdims=True)
        r2 = jnp.sum(jnp.where(lane == e2, before, 0.0), axis=1, keepdims=True)
        running = running + jnp.sum(hits, axis=0, keepdims=True)
        packed = jnp.where(lane == 0, e1, jnp.where(lane == 1, e2,
                           jnp.where(lane == 2, r1, jnp.where(lane == 3, r2, 0.0))))
        for j in range(half // LANES):
            rows_t = packed[j * LANES:(j + 1) * LANES, :].T
            jj = i * (half // LANES) + j
            for k in range(2 * TOP_K):
                ri_ref[k, jj:jj + 1, :] = rows_t[k:k + 1, :].astype(jnp.int32)
    cnt_ref[...] = running


def _out_proj(x2, o_sb, o_gla, w_out, ln_g, w_group, b_group, w_expert, b_expert, tm):
    T, D = x2.shape
    wa = w_out[:SB_WIDTH].astype(BF16)
    wb = w_out[SB_WIDTH:].astype(BF16)
    n_r = N_GROUPS + N_EXPERTS
    wr = jnp.zeros((D, LANES), F32).at[:, :n_r].set(jnp.concatenate([w_group, w_expert], axis=1))
    br = jnp.zeros((1, LANES), F32).at[0, :n_r].set(jnp.concatenate([b_group, b_expert]))
    row = lambda n: pl.BlockSpec((tm, n), lambda i: (i, 0))
    const = lambda a: pl.BlockSpec(a.shape, lambda i: (0,) * a.ndim)
    g2 = ln_g.reshape(1, D)
    half = tm // 2
    tri = jnp.asarray(np.tril(np.ones((half, half), np.float32), -1), BF16)
    return pl.pallas_call(
        _out_proj_kernel,
        grid=(T // tm,),
        in_specs=[row(D), row(SB_WIDTH), row(GLA_VW), const(wa), const(wb), const(g2), const(wr), const(br),
                  const(tri)],
        out_specs=[row(D), pl.BlockSpec((tm * SUBLANES, LANES), lambda i: (i, 0)),
                   pl.BlockSpec((2 * TOP_K, tm // LANES, LANES), lambda i: (0, i, 0)), row(LANES),
                   pl.BlockSpec((1, LANES), lambda i: (0, 0))],
        out_shape=[jax.ShapeDtypeStruct((T, D), F32), jax.ShapeDtypeStruct((T * SUBLANES, LANES), F32),
                   jax.ShapeDtypeStruct((2 * TOP_K, T // LANES, LANES), jnp.int32),
                   jax.ShapeDtypeStruct((T, LANES), F32), jax.ShapeDtypeStruct((1, LANES), F32)],
        compiler_params=_params("arbitrary"),
        name="out_proj",
    )(x2, o_sb, o_gla, wa, wb, g2, wr, br, tri)


def _route_tables(route, counts, n_rows, tm):
    counts = counts[0, :N_EXPERTS].astype(jnp.int32)
    ends = jnp.cumsum(counts)
    starts = ends - counts
    route = route.reshape(2 * TOP_K, -1)
    expert = route[:TOP_K, :, None] == jnp.arange(N_EXPERTS, dtype=jnp.int32)
    dest = (jnp.sum(jnp.where(expert, starts, 0), axis=-1) + route[TOP_K:]).reshape(-1)

    n_items = n_rows // tm + N_EXPERTS + EXPERT_GATHER_DEPTH
    items_e = (counts + tm - 1) // tm
    item_hi = jnp.cumsum(items_e)
    item_lo = item_hi - items_e
    i = jnp.arange(n_items, dtype=jnp.int32)
    e_of = jnp.minimum(jnp.sum(item_hi[None, :] <= i[:, None], axis=1), N_EXPERTS - 1)
    owner = e_of[:, None] == jnp.arange(N_EXPERTS, dtype=jnp.int32)
    pick = lambda tab: jnp.sum(jnp.where(owner, tab, 0), axis=-1)
    item_row = jnp.where(i < item_hi[-1], pick(starts) + (i - pick(item_lo)) * tm, n_rows)
    return (dest, item_row.astype(jnp.int32), item_lo.astype(jnp.int32), item_hi.astype(jnp.int32))


ROW_ISSUE_UNROLL = 8
EXPERT_GATHER_DEPTH = 4


def _token_tile(ref, t):
    return ref.at[pl.ds(pl.multiple_of(t * SUBLANES, SUBLANES), SUBLANES), :]


def _wait_token_gather(src_ref, dst_ref, sem):
    pltpu.make_async_copy(src_ref.at[pl.ds(0, dst_ref.shape[0]), :], dst_ref, sem).wait()


def _expert_kernel(row_ref, lo_ref, hi_ref, dest_ref,
                   h_ref, wg_ref, wu_ref, wd_ref, y_ref, *scratch, tm):
    depth = EXPERT_GATHER_DEPTH
    xbufs = scratch[:depth]
    ybuf, tok_ref, wg_bf, wu_bf, wd_bf, gsem, wsem = scratch[depth:]
    e = pl.program_id(0)
    n_rows = dest_ref.shape[0]
    n_tok = n_rows // TOP_K
    total = hi_ref[N_EXPERTS - 1]

    def gather_copy(i, r, slot):
        t = tok_ref[row_ref[i] + r]
        return pltpu.make_async_copy(_token_tile(h_ref, t), _token_tile(xbufs[slot], r), gsem.at[slot])

    def out_copy(row, slot):
        rows = pl.ds(pl.multiple_of(row * SUBLANES, SUBLANES), tm * SUBLANES)
        return pltpu.make_async_copy(ybuf.at[slot], y_ref.at[rows, :], wsem)

    @pl.when(e == 0)
    def _():
        def invert(t, _):
            for k in range(TOP_K):
                tok_ref[dest_ref[k * n_tok + t]] = t
            return 0
        lax.fori_loop(0, n_tok, invert, 0, unroll=ROW_ISSUE_UNROLL)

        def spare(r, _):
            tok_ref[n_rows + r] = 0
            return 0
        lax.fori_loop(0, tm, spare, 0, unroll=ROW_ISSUE_UNROLL)
        ybuf[1] = jnp.zeros(ybuf.shape[1:], ybuf.dtype)
        out_copy(n_rows, 1).start()
        for ahead in range(depth - 1):
            def first(r, _):
                gather_copy(ahead, r, ahead).start()
                return 0
            lax.fori_loop(0, tm, first, 0, unroll=ROW_ISSUE_UNROLL)

    wg_bf[...] = wg_ref[...].astype(BF16)
    wu_bf[...] = wu_ref[...].astype(BF16)
    wd_bf[...] = wd_ref[...].astype(BF16)

    def item_in_slot(i, slot):
        _wait_token_gather(h_ref, xbufs[slot], gsem.at[slot])
        nslot = (slot + depth - 1) % depth
        for r in range(tm):
            gather_copy(i + depth - 1, r, nslot).start(priority=1)
        xb = _load_token_tiles(xbufs[slot], tm).astype(BF16)
        g = _dot(xb, wg_bf[...])
        u = _dot(xb, wu_bf[...])
        hidden = ((g / (1.0 + jnp.exp(-g))) * u).astype(BF16)
        _store_token_tiles(ybuf.at[slot % 2], _dot(hidden, wd_bf[...]))
        out_copy(0, slot % 2).wait()
        out_copy(row_ref[i], slot % 2).start()

    def item(i, _):
        slot = lax.rem(i, depth)
        for s in range(depth):
            @pl.when(slot == s)
            def _():
                item_in_slot(i, s)
        return 0

    lax.fori_loop(lo_ref[e], hi_ref[e], item, 0)

    @pl.when(e == pl.num_programs(0) - 1)
    def _():
        out_copy(0, 0).wait()
        for ahead in range(depth - 1):
            for s in range(depth):
                @pl.when(lax.rem(total + ahead, depth) == s)
                def _():
                    _wait_token_gather(h_ref, xbufs[s], gsem.at[s])


def _expert_ffn(h2_tiles, dest, tables, w_gate, w_up, w_down, tm):
    N = dest.shape[0]
    _, D, F = w_gate.shape
    assert D == SUBLANES * LANES
    item_row, item_lo, item_hi = tables
    wspec = lambda shape: pl.BlockSpec((None,) + shape, lambda e, *_: (e, 0, 0))
    return pl.pallas_call(
        functools.partial(_expert_kernel, tm=tm),
        grid_spec=pltpu.PrefetchScalarGridSpec(
            num_scalar_prefetch=4,
            grid=(N_EXPERTS,),
            in_specs=[pl.BlockSpec(memory_space=pl.ANY), wspec((D, F)), wspec((D, F)), wspec((F, D))],
            out_specs=pl.BlockSpec(memory_space=pl.ANY),
            scratch_shapes=[pltpu.VMEM((tm * SUBLANES, LANES), F32) for _ in range(EXPERT_GATHER_DEPTH)]
                           + [pltpu.VMEM((2, tm * SUBLANES, LANES), F32),
                            pltpu.SMEM((N + tm,), jnp.int32),
                            pltpu.VMEM((D, F), BF16), pltpu.VMEM((D, F), BF16), pltpu.VMEM((F, D), BF16),
                            pltpu.SemaphoreType.DMA((EXPERT_GATHER_DEPTH,)), pltpu.SemaphoreType.DMA(())],
        ),
        out_shape=jax.ShapeDtypeStruct(((N + tm) * SUBLANES, LANES), F32),
        compiler_params=_params("arbitrary"),
        name="expert_ffn",
    )(item_row, item_lo, item_hi, dest, h2_tiles, w_gate, w_up, w_down)


COMBINE_RING = 4
COMBINE_AHEAD = 3


def _combine_kernel(dest_ref, x1_ref, rf_ref, g_ref, y_ref, o_ref, *scratch, tb):
    bufs, sem = scratch[:COMBINE_RING], scratch[COMBINE_RING]
    i = pl.program_id(0)
    n_steps = pl.num_programs(0)
    n_tok = n_steps * COMBINE_RING * tb

    def copies(j, slot, r):
        return [pltpu.make_async_copy(_token_tile(y_ref, dest_ref[k * n_tok + j * tb + r]),
                                      _token_tile(bufs[slot].at[k], r), sem.at[slot])
                for k in range(TOP_K)]

    @pl.when(i == 0)
    def _():
        for slot in range(COMBINE_AHEAD):
            def first(r, _):
                for c in copies(slot, slot, r):
                    c.start()
                return 0
            lax.fori_loop(0, tb, first, 0, unroll=ROW_ISSUE_UNROLL // TOP_K)

    def sub_block(s, fetch):
        j = i * COMBINE_RING + s
        for k in range(TOP_K):
            _wait_token_gather(y_ref, bufs[s].at[k], sem.at[s])
        if fetch:
            for r in range(tb):
                for k, c in enumerate(copies(j + COMBINE_AHEAD, (s + COMBINE_AHEAD) % COMBINE_RING, r)):
                    c.start(priority=k)
        rows = slice(s * tb, (s + 1) * tb)
        rf = rf_ref[rows, :]
        x = (x1_ref[rows, :] + rf[:, 0:1] * _load_token_tiles(bufs[s].at[0], tb)
             + rf[:, 1:2] * _load_token_tiles(bufs[s].at[1], tb))
        inv = lax.rsqrt(jnp.mean(x * x, axis=-1, keepdims=True) + RMS_EPS)
        o_ref[rows, :] = (x * inv) * g_ref[...]

    for s in range(COMBINE_RING):
        if s + COMBINE_AHEAD < COMBINE_RING:
            sub_block(s, True)
        else:
            @pl.when(i + 1 < n_steps)
            def _():
                sub_block(s, True)

            @pl.when(i + 1 >= n_steps)
            def _():
                sub_block(s, False)


def _combine(x1, route_f, y_tiles, dest, ln_g, tb):
    T, D = x1.shape
    g2 = ln_g.reshape(1, D)
    step = COMBINE_RING * tb
    return pl.pallas_call(
        functools.partial(_combine_kernel, tb=tb),
        grid_spec=pltpu.PrefetchScalarGridSpec(
            num_scalar_prefetch=1,
            grid=(T // step,),
            in_specs=[pl.BlockSpec((step, D), lambda i, d: (i, 0)),
                      pl.BlockSpec((step, LANES), lambda i, d: (i, 0)),
                      pl.BlockSpec((1, D), lambda i, d: (0, 0)),
                      pl.BlockSpec(memory_space=pl.ANY)],
            out_specs=pl.BlockSpec((step, D), lambda i, d: (i, 0)),
            scratch_shapes=[pltpu.VMEM((TOP_K, tb * SUBLANES, LANES), F32) for _ in range(COMBINE_RING)]
                           + [pltpu.SemaphoreType.DMA((COMBINE_RING,))],
        ),
        out_shape=jax.ShapeDtypeStruct((T, D), F32),
        compiler_params=_params("arbitrary"),
        name="combine",
    )(dest, x1, route_f, g2, y_tiles)


MXU_WIDTH = 256


class _Tiles(NamedTuple):
    in_proj: int
    query: int
    mixers: int
    out_proj: int
    expert: int
    combine: int


def _tiles(n_tokens, seq_len):
    query = MXU_WIDTH
    tiles = _Tiles(in_proj=4 * MXU_WIDTH, query=query, mixers=query * (GLA_HEADS // 2),
                   out_proj=SUBLANES * LANES, expert=MXU_WIDTH, combine=MXU_WIDTH)
    assert seq_len % tiles.mixers == 0 and tiles.mixers % GLA_CHUNK == 0
    assert n_tokens % tiles.in_proj == 0 and n_tokens % tiles.out_proj == 0
    assert n_tokens % (COMBINE_RING * tiles.combine) == 0 and (TOP_K * n_tokens) % tiles.expert == 0
    return tiles


def _layer(x2, B, S, ln1_g, w_in, w2, gb, ng, w_out, ln2_g, w_group, b_group, w_expert, b_expert,
           w_gate, w_up, w_down, ln_f_g, tiles):
    q, k, v, gq, gk, gv, gr, lr = _in_proj(x2, ln1_g, w_in, tiles.in_proj)
    o_sb, o_gla = _mixers(q, k, v, gq, gk, gv, gr, lr, w2, gb, ng, B, S, tiles.mixers, tiles.query)
    x1, h2_tiles, route_i, route_f, counts = _out_proj(x2, o_sb, o_gla, w_out, ln2_g, w_group, b_group,
                                                       w_expert, b_expert, tiles.out_proj)
    T = x2.shape[0]
    dest, *tables = _route_tables(route_i, counts, TOP_K * T, tiles.expert)
    y_tiles = _expert_ffn(h2_tiles, dest, tables, w_gate, w_up, w_down, tiles.expert)
    return _combine(x1, route_f, y_tiles, dest, ln_f_g, tiles.combine)


def kernel(x, ln1_g, w_in, gla_gate_w2, gla_gate_b, gla_norm_g, w_out, ln2_g, w_group, b_group, w_expert, b_expert, exp_w_gate, exp_w_up, exp_w_down, ln_f_g):
    B, S, D = x.shape
    assert ln1_g.shape[0] == 1, "single-layer stack"
    assert D == SUBLANES * LANES, "one (SUBLANES, LANES) tile per token"
    out = _layer(x.reshape(B * S, D), B, S, ln1_g[0], w_in[0], gla_gate_w2[0], gla_gate_b[0],
                 gla_norm_g[0], w_out[0], ln2_g[0], w_group[0], b_group[0], w_expert[0], b_expert[0],
                 exp_w_gate[0], exp_w_up[0], exp_w_down[0], ln_f_g, _tiles(B * S, S))
    return out.reshape(B, S, D)
```

```python
import functools
from typing import NamedTuple

import numpy as np
import jax
import jax.numpy as jnp
from jax import lax
from jax.experimental import pallas as pl
from jax.experimental.pallas import tpu as pltpu

SB_HEADS = 8
SB_HEAD_DIM = 64
SB_WIDTH = SB_HEADS * SB_HEAD_DIM
GLA_HEADS = 4
GLA_DK = 64
GLA_DV = 128
GLA_KW = GLA_HEADS * GLA_DK
GLA_VW = GLA_HEADS * GLA_DV
GLA_GATE_RANK = 16
GLA_GATE_TAU = 16.0
N_GROUPS = 4
EXPERTS_PER_GROUP = 8
N_EXPERTS = N_GROUPS * EXPERTS_PER_GROUP
TOP_K = 2
RMS_EPS = 1e-6
LOG2_E = 1.4426950408889634

LANES = 128
VMEM_LIMIT_BYTES = 56 * 1024 * 1024

F32 = jnp.float32
BF16 = jnp.bfloat16


def _dot(a, b):
    return jnp.dot(a, b, preferred_element_type=F32)


def _dot_nt(a, b):
    return lax.dot_general(a, b, (((1,), (1,)), ((), ())), preferred_element_type=F32)


def _softplus(z):
    return jnp.maximum(z, 0.0) + jnp.log(1.0 + jnp.exp(-jnp.abs(z)))


def _params(*sem):
    return pltpu.CompilerParams(dimension_semantics=sem, vmem_limit_bytes=VMEM_LIMIT_BYTES)


def _in_proj_kernel(x_ref, g_ref, wq, wk, wv, wgq, wgk, wgv, wgr, wlr,
                    oq, ok, ov, ogq, ogk, ogv, ogr, olr):
    x = x_ref[...]
    inv = lax.rsqrt(jnp.mean(x * x, axis=-1, keepdims=True) + RMS_EPS)
    h = ((x * inv) * g_ref[...]).astype(BF16)
    oq[...] = (_dot(h, wq[...]) * (SB_HEAD_DIM ** -0.5 * LOG2_E)).astype(oq.dtype)
    ok[...] = _dot(h, wk[...]).astype(ok.dtype)
    ov[...] = _dot(h, wv[...]).astype(ov.dtype)
    ogq[...] = (_dot(h, wgq[...]) * (GLA_DK ** -0.5)).astype(ogq.dtype)
    ogk[...] = _dot(h, wgk[...]).astype(ogk.dtype)
    ogv[...] = _dot(h, wgv[...]).astype(ogv.dtype)
    ogr[...] = _dot(h, wgr[...]).astype(ogr.dtype)
    olr[...] = _dot(h, wlr[...]).astype(olr.dtype)


def _in_proj(x2, ln_g, w_in, tm):
    T, D = x2.shape
    sizes = (SB_WIDTH, SB_WIDTH, SB_WIDTH, GLA_KW, GLA_KW, GLA_VW, GLA_VW, GLA_GATE_RANK)
    offs = np.concatenate([[0], np.cumsum(sizes)])
    ws = [w_in[:, int(offs[i]):int(offs[i + 1])].astype(BF16) for i in range(len(sizes))]
    out_dtypes = (BF16, BF16, BF16, F32, F32, BF16, F32, F32)
    row = lambda n: pl.BlockSpec((tm, n), lambda i: (i, 0))
    full = lambda n: pl.BlockSpec((D, n), lambda i: (0, 0))
    return pl.pallas_call(
        _in_proj_kernel,
        grid=(T // tm,),
        in_specs=[row(D), pl.BlockSpec((1, D), lambda i: (0, 0))] + [full(n) for n in sizes],
        out_specs=[row(n) for n in sizes],
        out_shape=[jax.ShapeDtypeStruct((T, n), dt) for n, dt in zip(sizes, out_dtypes)],
        compiler_params=_params("parallel"),
        name="in_proj",
    )(x2, ln_g.reshape(1, D), *ws)


SB_ZERO_WEIGHT_BITS = 160.0


SOFTPLUS2_CLAMP = 126.0


def _softplus2(z):
    return jnp.maximum(z, jnp.log2(1.0 + jnp.exp2(jnp.minimum(z, SOFTPLUS2_CLAMP))))


def _sb_query_block(q_ref, k_ref, v_ref, o_ref, qh_ref, acc_ref, carry_ref, *, row0, qi, tq, n_pairs,
                    alongside=None):
    qrows = slice(row0, row0 + tq)
    r = lax.broadcasted_iota(jnp.int32, (tq, tq), 0)
    c = lax.broadcasted_iota(jnp.int32, (tq, tq), 1)
    causal = c < r
    later = (r > c).astype(BF16)
    head0 = lax.broadcasted_iota(jnp.int32, (1, LANES), 1) < SB_HEAD_DIM

    for p in range(n_pairs):
        q2 = q_ref[qrows, p * LANES:(p + 1) * LANES]
        zero = jnp.zeros_like(q2)
        qh_ref[p, 0:tq] = jnp.where(head0, q2, zero)
        qh_ref[p, tq:2 * tq] = jnp.where(head0, zero, q2)
    acc_ref[...] = jnp.zeros_like(acc_ref)
    carry_ref[...] = jnp.zeros_like(carry_ref)
    causal2 = jnp.concatenate([causal, causal], axis=0)

    def key_block(kb, diag):
        rows = pl.ds(pl.multiple_of(kb * tq, tq), tq)
        pairs = range(n_pairs)
        zs = {}

        def scores(p):
            zs[p] = _dot_nt(qh_ref[p], k_ref[rows, pl.ds(p * LANES, LANES)])
        scores(0)
        scores(1)

        def suffix_stage(p):
            sp = _softplus2(zs[p])
            if diag:
                sp = jnp.where(causal2, sp, 0.0)
            own = zs.pop(p) - sp
            return own, sp[:, 0:1], _dot(sp.astype(BF16), later)

        def weight_stage(p, own, sp0, run):
            v2 = v_ref[rows, pl.ds(p * LANES, LANES)]
            vzero = jnp.zeros_like(v2)
            v_stack = jnp.concatenate([jnp.where(head0, v2, vzero), jnp.where(head0, vzero, v2)], axis=0)
            carry = carry_ref[p]
            w = jnp.exp2(own - run - jnp.concatenate([carry] * (tq // LANES), axis=1))
            if diag:
                w = jnp.where(causal2, w, 0.0)
            w = w.astype(BF16)
            acc_ref[p] += _dot(jnp.concatenate([w[0:tq], w[tq:2 * tq]], axis=1), v_stack)
            carry_ref[p] = carry + jnp.broadcast_to(run[:, 0:1] + sp0, (2 * tq, LANES))

        pending = None
        for p in pairs:
            if p + 2 < n_pairs:
                scores(p + 2)
            staged = suffix_stage(p)
            if pending is not None:
                weight_stage(*pending)
            pending = (p,) + staged
        weight_stage(*pending)

    def min_carry():
        m = carry_ref[0]
        for p in range(1, n_pairs):
            m = jnp.minimum(m, carry_ref[p])
        return jnp.min(m)

    def cond(state):
        kb, m = state
        return jnp.logical_and(kb >= 0, m < SB_ZERO_WEIGHT_BITS)

    def body(state):
        kb, _ = state
        key_block(kb, False)
        return kb - 1, min_carry()

    if alongside is not None:
        alongside()
    key_block(qi, True)
    lax.while_loop(cond, body, (qi - 1, min_carry()))
    for p in range(n_pairs):
        o_ref[qrows, p * LANES:(p + 1) * LANES] = acc_ref[p].astype(o_ref.dtype)


GLA_CHUNK = 64
GLA_LEVELS = (32, 16, 8, 4, 2, 1)


def _gla_tables(C=GLA_CHUNK):
    t = np.arange(C)[:, None]
    j = np.arange(C)[None, :]
    blocks = [(j <= t), (j > t)]
    masks = []
    for half in GLA_LEVELS:
        blk = 2 * half
        u = (t // blk) * blk + half - 1
        second = (t % blk) >= half
        blocks.append((second & (j > u) & (j <= t)) | ((~second) & (j > t) & (j <= u)))
        s = j
        masks.append(second & ((s % blk) < half) & ((s // blk) == (t // blk)))
    blocks.append(np.ones((8, C), bool))
    table = np.concatenate(blocks, axis=0).astype(np.float32)
    table = np.concatenate([table, table], axis=1)
    return table, np.concatenate(masks, axis=0).astype(np.float32)


def _split_bf16(a):
    hi = a.astype(BF16)
    lo = (a - hi.astype(F32)).astype(BF16)
    return hi, lo


def _gla_head_pair(p, gq_ref, gk_ref, gv_ref, gr_ref, lr_ref, w2_ref, gb_ref, ng_ref, tab_ref, msk_ref,
                   o_ref, state_ref, *, n_chunks):
    C = GLA_CHUNK
    n_lv = len(GLA_LEVELS)

    head0 = lax.broadcasted_iota(jnp.int32, (1, LANES), 1) < GLA_DK
    eye = (lax.broadcasted_iota(jnp.int32, (C, C), 0) == lax.broadcasted_iota(jnp.int32, (C, C), 1))
    w2_hi, w2_lo = _split_bf16(w2_ref[...])
    lr_hi, lr_lo = _split_bf16(lr_ref[...])

    def stack_heads(a):
        return jnp.concatenate([jnp.where(head0, a, 0.0), jnp.where(head0, 0.0, a)], axis=0)

    cols = slice(p * LANES, (p + 1) * LANES)
    pre = (_dot(lr_hi, w2_hi[:, cols]) + _dot(lr_hi, w2_lo[:, cols])
           + _dot(lr_lo, w2_hi[:, cols])) + gb_ref[:, cols]
    log2_a = -_softplus(-pre) * (LOG2_E / GLA_GATE_TAU)
    la_hi, la_lo = _split_bf16(log2_a)
    la = jnp.concatenate(
        [jnp.concatenate([la_hi[c * C:(c + 1) * C], la_lo[c * C:(c + 1) * C]], axis=0)
         for c in range(n_chunks)], axis=1)
    group = MXU_WIDTH // LANES
    assert n_chunks % group == 0
    ex_groups = {}

    def decay_factors(c):
        g = c // group
        if g not in ex_groups:
            ex_groups[g] = jnp.exp2(_dot(tab_ref[...], la[:, g * MXU_WIDTH:(g + 1) * MXU_WIDTH]))
        return ex_groups[g][:, (c % group) * LANES:(c % group + 1) * LANES]

    st = state_ref[p]
    def score_stage(c):
        rows = slice(c * C, (c + 1) * C)
        ex = decay_factors(c)
        q2 = gq_ref[rows, cols]
        k2 = gk_ref[rows, cols]
        q_int = q2 * ex[0:C]
        k_dec = (k2 * ex[C:2 * C]).astype(BF16)
        dec_row = ex[(2 + n_lv) * C:(2 + n_lv) * C + 1]
        diag = jnp.sum(stack_heads(q2 * k2), axis=1, keepdims=True)
        scores = jnp.where(jnp.concatenate([eye, eye], axis=0), diag, 0.0)
        for l in range(n_lv):
            ex_l = ex[(2 + l) * C:(3 + l) * C]
            q_l = stack_heads(q2 * ex_l).astype(BF16)
            k_l = (k2 * ex_l).astype(BF16)
            m_l = msk_ref[l * C:(l + 1) * C, :] > 0.5
            scores = jnp.where(jnp.concatenate([m_l, m_l], axis=0), _dot_nt(q_l, k_l), scores)
        return c, q_int, k_dec, dec_row, scores.astype(BF16)

    def state_stage(st, c, q_int, k_dec, dec_row, scores):
        rows = slice(c * C, (c + 1) * C)
        o_inter = _dot_nt(stack_heads(q_int).astype(BF16), st.astype(BF16))
        upd = []
        for h in range(2):
            vcols = slice((2 * p + h) * GLA_DV, (2 * p + h + 1) * GLA_DV)
            v_h = gv_ref[rows, vcols]
            o = o_inter[h * C:(h + 1) * C] + _dot(scores[h * C:(h + 1) * C], v_h)
            inv = lax.rsqrt(jnp.mean(o * o, axis=-1, keepdims=True) + RMS_EPS)
            y = (o * inv) * ng_ref[:, vcols]
            g = gr_ref[rows, vcols]
            y = y * (g / (1.0 + jnp.exp(-g)))
            o_ref[rows, vcols] = y.astype(o_ref.dtype)
            v_t = v_h.astype(F32).T.astype(BF16)
            upd.append(_dot(v_t, k_dec))
        return st * dec_row + jnp.where(head0, upd[0], upd[1])

    pending = score_stage(0)
    for c in range(1, n_chunks):
        ahead = score_stage(c)
        st = state_stage(st, *pending)
        pending = ahead
    st = state_stage(st, *pending)
    state_ref[p] = st


def _mixers_kernel(q_ref, k_ref, v_ref, gq_ref, gk_ref, gv_ref, gr_ref, lr_ref, w2_ref, gb_ref, ng_ref,
                   tab_ref, msk_ref, osb_ref, ogla_ref, qh_ref, acc_ref, carry_ref, state_ref,
                   *, tq, n_pairs, n_chunks):
    blk = pl.program_id(1)
    n_q = q_ref.shape[0] // tq
    assert n_q == GLA_HEADS // 2

    @pl.when(blk == 0)
    def _():
        state_ref[...] = jnp.zeros_like(state_ref)

    for half in range(n_q):
        gla = functools.partial(_gla_head_pair, half, gq_ref, gk_ref, gv_ref, gr_ref, lr_ref, w2_ref, gb_ref,
                                ng_ref, tab_ref, msk_ref, ogla_ref, state_ref, n_chunks=n_chunks)
        _sb_query_block(q_ref, k_ref, v_ref, osb_ref, qh_ref, acc_ref, carry_ref, row0=half * tq,
                        qi=blk * n_q + half, tq=tq, n_pairs=n_pairs, alongside=gla)


def _mixers(q, k, v, gq, gk, gv, gr, lr, w2, gb, ng, B, S, tb, tq):
    table, masks = _gla_tables()
    table = jnp.asarray(table, BF16)
    masks = jnp.asarray(masks, F32)
    T = B * S
    W = q.shape[-1]
    n_pairs = W // LANES
    nblk = S // tb
    q3, k3, v3 = (a.reshape(B, S, W) for a in (q, k, v))
    blk3 = pl.BlockSpec((None, tb, W), lambda b, i: (b, i, 0))
    seq3 = pl.BlockSpec((None, S, W), lambda b, i: (b, 0, 0))
    row = lambda n: pl.BlockSpec((tb, n), lambda b, i: (b * nblk + i, 0))
    const = lambda a: pl.BlockSpec(a.shape, lambda b, i: (0,) * a.ndim)
    w2 = w2.astype(F32)
    gb2 = gb.reshape(1, GLA_KW)
    ng2 = ng.reshape(1, GLA_VW)
    o_sb, o_gla = pl.pallas_call(
        functools.partial(_mixers_kernel, tq=tq, n_pairs=n_pairs, n_chunks=tb // GLA_CHUNK),
        grid=(B, nblk),
        in_specs=[blk3, seq3, seq3, row(GLA_KW), row(GLA_KW), row(GLA_VW), row(GLA_VW), row(GLA_GATE_RANK),
                  const(w2), const(gb2), const(ng2), const(table), const(masks)],
        out_specs=[blk3, row(GLA_VW)],
        out_shape=[jax.ShapeDtypeStruct((B, S, W), BF16), jax.ShapeDtypeStruct((T, GLA_VW), BF16)],
        scratch_shapes=[pltpu.VMEM((n_pairs, 2 * tq, LANES), BF16),
                        pltpu.VMEM((n_pairs, tq, LANES), F32),
                        pltpu.VMEM((n_pairs, 2 * tq, LANES), F32),
                        pltpu.VMEM((GLA_HEADS // 2, GLA_DV, 2 * GLA_DK), F32)],
        compiler_params=_params("parallel", "arbitrary"),
        name="mixers",
    )(q3, k3, v3, gq, gk, gv, gr, lr, w2, gb2, ng2, table, masks)
    return o_sb.reshape(T, W), o_gla


SUBLANES = 8


def _store_token_tiles(ref, value, accumulate=False):
    n = value.shape[0]
    for s in range(SUBLANES):
        rows = pl.ds(s, n, stride=SUBLANES)
        chunk = value[:, s * LANES:(s + 1) * LANES]
        ref[rows, :] = ref[rows, :] + chunk if accumulate else chunk


def _load_token_tiles(ref, n):
    return jnp.concatenate([ref[pl.ds(s, n, stride=SUBLANES), :] for s in range(SUBLANES)], axis=1)


def _out_proj_kernel(x_ref, oa_ref, ob_ref, wa_ref, wb_ref, g_ref, wr_ref, br_ref, tri_ref,
                     x1_ref, h2_ref, ri_ref, rf_ref, cnt_ref):
    @pl.when(pl.program_id(0) == 0)
    def _():
        cnt_ref[...] = jnp.zeros_like(cnt_ref)

    half = tri_ref.shape[0]
    halves = [slice(i * half, (i + 1) * half) for i in range(x_ref.shape[0] // half)]
    w_hi, w_lo = _split_bf16(wr_ref[...])
    w_cat = jnp.concatenate([w_hi, w_lo], axis=1)
    lane = lax.broadcasted_iota(jnp.int32, (half, LANES), 1).astype(F32)
    neg = -jnp.inf

    def residual_stage(rows):
        x1 = x_ref[rows, :] + _dot(oa_ref[rows, :], wa_ref[...]) + _dot(ob_ref[rows, :], wb_ref[...])
        x1_ref[rows, :] = x1
        inv = lax.rsqrt(jnp.mean(x1 * x1, axis=-1, keepdims=True) + RMS_EPS)
        h2 = (x1 * inv) * g_ref[...]
        _store_token_tiles(h2_ref.at[pl.ds(rows.start * SUBLANES, half * SUBLANES), :], h2)
        return _split_bf16(h2)

    def logits_stage(h_hi, h_lo):
        hw = _dot(h_hi, w_cat)
        return hw[:, :LANES] + hw[:, LANES:] + _dot(h_lo, w_hi) + br_ref[...]

    def first_max(vals):
        m = jnp.max(vals, axis=1, keepdims=True)
        idx = jnp.min(jnp.where(vals == m, lane, float(LANES)), axis=1, keepdims=True)
        return m, idx

    def select_stage(rows, logits):
        is_group = lane < N_GROUPS
        g_max, g_idx = first_max(jnp.where(is_group, logits, neg))
        g_p = 1.0 / jnp.sum(jnp.where(is_group, jnp.exp(logits - g_max), 0.0), axis=1, keepdims=True)
        lo_lane = N_GROUPS + EXPERTS_PER_GROUP * g_idx
        sel = jnp.where((lane >= lo_lane) & (lane < lo_lane + EXPERTS_PER_GROUP), logits, neg)
        m1, i1 = first_max(sel)
        m2, i2 = first_max(jnp.where(lane == i1, neg, sel))
        e = jnp.exp(m2 - m1)
        gate1 = g_p / (1.0 + e)
        gate2 = g_p * e / (1.0 + e)
        rf_ref[rows, :] = jnp.where(lane == 0, gate1, jnp.where(lane == 1, gate2, 0.0))
        e1 = i1 - N_GROUPS
        e2 = i2 - N_GROUPS
        hits = (lane == e1).astype(F32) + (lane == e2).astype(F32)
        return e1, e2, hits

    pieces = [residual_stage(rows) for rows in halves]
    logits = [logits_stage(*hl) for hl in pieces]
    chosen = [select_stage(rows, lg) for rows, lg in zip(halves, logits)]

    running = cnt_ref[...]
    for i, (e1, e2, hits) in enumerate(chosen):
        before = _dot(tri_ref[...], hits.astype(BF16)) + running
        r1 = jnp.sum(jnp.where(lane == e1, before, 0.0), axis=1, keepdims=True)
        r2 = jnp.sum(jnp.where(lane == e2, before, 0.0), axis=1, keepdims=True)
        running = running + jnp.sum(hits, axis=0, keepdims=True)
        packed = jnp.where(lane == 0, e1, jnp.where(lane == 1, e2,
                           jnp.where(lane == 2, r1, jnp.where(lane == 3, r2, 0.0))))
        for j in range(half // LANES):
            rows_t = packed[j * LANES:(j + 1) * LANES, :].T
            jj = i * (half // LANES) + j
            for k in range(2 * TOP_K):
                ri_ref[k, jj:jj + 1, :] = rows_t[k:k + 1, :].astype(jnp.int32)
    cnt_ref[...] = running


def _out_proj(x2, o_sb, o_gla, w_out, ln_g, w_group, b_group, w_expert, b_expert, tm):
    T, D = x2.shape
    wa = w_out[:SB_WIDTH].astype(BF16)
    wb = w_out[SB_WIDTH:].astype(BF16)
    n_r = N_GROUPS + N_EXPERTS
    wr = jnp.zeros((D, LANES), F32).at[:, :n_r].set(jnp.concatenate([w_group, w_expert], axis=1))
    br = jnp.zeros((1, LANES), F32).at[0, :n_r].set(jnp.concatenate([b_group, b_expert]))
    row = lambda n: pl.BlockSpec((tm, n), lambda i: (i, 0))
    const = lambda a: pl.BlockSpec(a.shape, lambda i: (0,) * a.ndim)
    g2 = ln_g.reshape(1, D)
    half = tm // 2
    tri = jnp.asarray(np.tril(np.ones((half, half), np.float32), -1), BF16)
    return pl.pallas_call(
        _out_proj_kernel,
        grid=(T // tm,),
        in_specs=[row(D), row(SB_WIDTH), row(GLA_VW), const(wa), const(wb), const(g2), const(wr), const(br),
                  const(tri)],
        out_specs=[row(D), pl.BlockSpec((tm * SUBLANES, LANES), lambda i: (i, 0)),
                   pl.BlockSpec((2 * TOP_K, tm // LANES, LANES), lambda i: (0, i, 0)), row(LANES),
                   pl.BlockSpec((1, LANES), lambda i: (0, 0))],
        out_shape=[jax.ShapeDtypeStruct((T, D), F32), jax.ShapeDtypeStruct((T * SUBLANES, LANES), F32),
                   jax.ShapeDtypeStruct((2 * TOP_K, T // LANES, LANES), jnp.int32),
                   jax.ShapeDtypeStruct((T, LANES), F32), jax.ShapeDtypeStruct((1, LANES), F32)],
        compiler_params=_params("arbitrary"),
        name="out_proj",
    )(x2, o_sb, o_gla, wa, wb, g2, wr, br, tri)


def _route_tables(route, counts, n_rows, tm):
    counts = counts[0, :N_EXPERTS].astype(jnp.int32)
    ends = jnp.cumsum(counts)
    starts = ends - counts
    route = route.reshape(2 * TOP_K, -1)
    expert = route[:TOP_K, :, None] == jnp.arange(N_EXPERTS, dtype=jnp.int32)
    dest = (jnp.sum(jnp.where(expert, starts, 0), axis=-1) + route[TOP_K:]).reshape(-1)

    n_items = n_rows // tm + N_EXPERTS + EXPERT_GATHER_DEPTH
    items_e = (counts + tm - 1) // tm
    item_hi = jnp.cumsum(items_e)
    item_lo = item_hi - items_e
    i = jnp.arange(n_items, dtype=jnp.int32)
    e_of = jnp.minimum(jnp.sum(item_hi[None, :] <= i[:, None], axis=1), N_EXPERTS - 1)
    owner = e_of[:, None] == jnp.arange(N_EXPERTS, dtype=jnp.int32)
    pick = lambda tab: jnp.sum(jnp.where(owner, tab, 0), axis=-1)
    item_row = jnp.where(i < item_hi[-1], pick(starts) + (i - pick(item_lo)) * tm, n_rows)
    return (dest, item_row.astype(jnp.int32), item_lo.astype(jnp.int32), item_hi.astype(jnp.int32))


ROW_ISSUE_UNROLL = 8
EXPERT_GATHER_DEPTH = 4


def _token_tile(ref, t):
    return ref.at[pl.ds(pl.multiple_of(t * SUBLANES, SUBLANES), SUBLANES), :]


def _wait_token_gather(src_ref, dst_ref, sem):
    pltpu.make_async_copy(src_ref.at[pl.ds(0, dst_ref.shape[0]), :], dst_ref, sem).wait()


def _expert_kernel(row_ref, lo_ref, hi_ref, dest_ref,
                   h_ref, wg_ref, wu_ref, wd_ref, y_ref, *scratch, tm):
    depth = EXPERT_GATHER_DEPTH
    xbufs = scratch[:depth]
    ybuf, tok_ref, wg_bf, wu_bf, wd_bf, gsem, wsem = scratch[depth:]
    e = pl.program_id(0)
    n_rows = dest_ref.shape[0]
    n_tok = n_rows // TOP_K
    total = hi_ref[N_EXPERTS - 1]

    def gather_copy(i, r, slot):
        t = tok_ref[row_ref[i] + r]
        return pltpu.make_async_copy(_token_tile(h_ref, t), _token_tile(xbufs[slot], r), gsem.at[slot])

    def out_copy(row, slot):
        rows = pl.ds(pl.multiple_of(row * SUBLANES, SUBLANES), tm * SUBLANES)
        return pltpu.make_async_copy(ybuf.at[slot], y_ref.at[rows, :], wsem)

    @pl.when(e == 0)
    def _():
        def invert(t, _):
            for k in range(TOP_K):
                tok_ref[dest_ref[k * n_tok + t]] = t
            return 0
        lax.fori_loop(0, n_tok, invert, 0, unroll=ROW_ISSUE_UNROLL)

        def spare(r, _):
            tok_ref[n_rows + r] = 0
            return 0
        lax.fori_loop(0, tm, spare, 0, unroll=ROW_ISSUE_UNROLL)
        ybuf[1] = jnp.zeros(ybuf.shape[1:], ybuf.dtype)
        out_copy(n_rows, 1).start()
        for ahead in range(depth - 1):
            def first(r, _):
                gather_copy(ahead, r, ahead).start()
                return 0
            lax.fori_loop(0, tm, first, 0, unroll=ROW_ISSUE_UNROLL)

    wg_bf[...] = wg_ref[...].astype(BF16)
    wu_bf[...] = wu_ref[...].astype(BF16)
    wd_bf[...] = wd_ref[...].astype(BF16)

    def item_in_slot(i, slot):
        _wait_token_gather(h_ref, xbufs[slot], gsem.at[slot])
        nslot = (slot + depth - 1) % depth
        for r in range(tm):
            gather_copy(i + depth - 1, r, nslot).start(priority=1)
        xb = _load_token_tiles(xbufs[slot], tm).astype(BF16)
        g = _dot(xb, wg_bf[...])
        u = _dot(xb, wu_bf[...])
        hidden = ((g / (1.0 + jnp.exp(-g))) * u).astype(BF16)
        _store_token_tiles(ybuf.at[slot % 2], _dot(hidden, wd_bf[...]))
        out_copy(0, slot % 2).wait()
        out_copy(row_ref[i], slot % 2).start()

    def item(i, _):
        slot = lax.rem(i, depth)
        for s in range(depth):
            @pl.when(slot == s)
            def _():
                item_in_slot(i, s)
        return 0

    lax.fori_loop(lo_ref[e], hi_ref[e], item, 0)

    @pl.when(e == pl.num_programs(0) - 1)
    def _():
        out_copy(0, 0).wait()
        for ahead in range(depth - 1):
            for s in range(depth):
                @pl.when(lax.rem(total + ahead, depth) == s)
                def _():
                    _wait_token_gather(h_ref, xbufs[s], gsem.at[s])


def _expert_ffn(h2_tiles, dest, tables, w_gate, w_up, w_down, tm):
    N = dest.shape[0]
    _, D, F = w_gate.shape
    assert D == SUBLANES * LANES
    item_row, item_lo, item_hi = tables
    wspec = lambda shape: pl.BlockSpec((None,) + shape, lambda e, *_: (e, 0, 0))
    return pl.pallas_call(
        functools.partial(_expert_kernel, tm=tm),
        grid_spec=pltpu.PrefetchScalarGridSpec(
            num_scalar_prefetch=4,
            grid=(N_EXPERTS,),
            in_specs=[pl.BlockSpec(memory_space=pl.ANY), wspec((D, F)), wspec((D, F)), wspec((F, D))],
            out_specs=pl.BlockSpec(memory_space=pl.ANY),
            scratch_shapes=[pltpu.VMEM((tm * SUBLANES, LANES), F32) for _ in range(EXPERT_GATHER_DEPTH)]
                           + [pltpu.VMEM((2, tm * SUBLANES, LANES), F32),
                            pltpu.SMEM((N + tm,), jnp.int32),
                            pltpu.VMEM((D, F), BF16), pltpu.VMEM((D, F), BF16), pltpu.VMEM((F, D), BF16),
                            pltpu.SemaphoreType.DMA((EXPERT_GATHER_DEPTH,)), pltpu.SemaphoreType.DMA(())],
        ),
        out_shape=jax.ShapeDtypeStruct(((N + tm) * SUBLANES, LANES), F32),
        compiler_params=_params("arbitrary"),
        name="expert_ffn",
    )(item_row, item_lo, item_hi, dest, h2_tiles, w_gate, w_up, w_down)


COMBINE_RING = 4
COMBINE_AHEAD = 3


def _combine_kernel(dest_ref, x1_ref, rf_ref, g_ref, y_ref, o_ref, *scratch, tb):
    bufs, sem = scratch[:COMBINE_RING], scratch[COMBINE_RING]
    i = pl.program_id(0)
    n_steps = pl.num_programs(0)
    n_tok = n_steps * COMBINE_RING * tb

    def copies(j, slot, r):
        return [pltpu.make_async_copy(_token_tile(y_ref, dest_ref[k * n_tok + j * tb + r]),
                                      _token_tile(bufs[slot].at[k], r), sem.at[slot])
                for k in range(TOP_K)]

    @pl.when(i == 0)
    def _():
        for slot in range(COMBINE_AHEAD):
            def first(r, _):
                for c in copies(slot, slot, r):
                    c.start()
                return 0
            lax.fori_loop(0, tb, first, 0, unroll=ROW_ISSUE_UNROLL // TOP_K)

    def sub_block(s, fetch):
        j = i * COMBINE_RING + s
        for k in range(TOP_K):
            _wait_token_gather(y_ref, bufs[s].at[k], sem.at[s])
        if fetch:
            for r in range(tb):
                for k, c in enumerate(copies(j + COMBINE_AHEAD, (s + COMBINE_AHEAD) % COMBINE_RING, r)):
                    c.start(priority=k)
        rows = slice(s * tb, (s + 1) * tb)
        rf = rf_ref[rows, :]
        x = (x1_ref[rows, :] + rf[:, 0:1] * _load_token_tiles(bufs[s].at[0], tb)
             + rf[:, 1:2] * _load_token_tiles(bufs[s].at[1], tb))
        inv = lax.rsqrt(jnp.mean(x * x, axis=-1, keepdims=True) + RMS_EPS)
        o_ref[rows, :] = (x * inv) * g_ref[...]

    for s in range(COMBINE_RING):
        if s + COMBINE_AHEAD < COMBINE_RING:
            sub_block(s, True)
        else:
            @pl.when(i + 1 < n_steps)
            def _():
                sub_block(s, True)

            @pl.when(i + 1 >= n_steps)
            def _():
                sub_block(s, False)


def _combine(x1, route_f, y_tiles, dest, ln_g, tb):
    T, D = x1.shape
    g2 = ln_g.reshape(1, D)
    step = COMBINE_RING * tb
    return pl.pallas_call(
        functools.partial(_combine_kernel, tb=tb),
        grid_spec=pltpu.PrefetchScalarGridSpec(
            num_scalar_prefetch=1,
            grid=(T // step,),
            in_specs=[pl.BlockSpec((step, D), lambda i, d: (i, 0)),
                      pl.BlockSpec((step, LANES), lambda i, d: (i, 0)),
                      pl.BlockSpec((1, D), lambda i, d: (0, 0)),
                      pl.BlockSpec(memory_space=pl.ANY)],
            out_specs=pl.BlockSpec((step, D), lambda i, d: (i, 0)),
            scratch_shapes=[pltpu.VMEM((TOP_K, tb * SUBLANES, LANES), F32) for _ in range(COMBINE_RING)]
                           + [pltpu.SemaphoreType.DMA((COMBINE_RING,))],
        ),
        out_shape=jax.ShapeDtypeStruct((T, D), F32),
        compiler_params=_params("arbitrary"),
        name="combine",
    )(dest, x1, route_f, g2, y_tiles)


MXU_WIDTH = 256


class _Tiles(NamedTuple):
    in_proj: int
    query: int
    mixers: int
    out_proj: int
    expert: int
    combine: int


def _tiles(n_tokens, seq_len):
    query = MXU_WIDTH
    tiles = _Tiles(in_proj=4 * MXU_WIDTH, query=query, mixers=query * (GLA_HEADS // 2),
                   out_proj=SUBLANES * LANES, expert=MXU_WIDTH, combine=MXU_WIDTH)
    assert seq_len % tiles.mixers == 0 and tiles.mixers % GLA_CHUNK == 0
    assert n_tokens % tiles.in_proj == 0 and n_tokens % tiles.out_proj == 0
    assert n_tokens % (COMBINE_RING * tiles.combine) == 0 and (TOP_K * n_tokens) % tiles.expert == 0
    return tiles


def _layer(x2, B, S, ln1_g, w_in, w2, gb, ng, w_out, ln2_g, w_group, b_group, w_expert, b_expert,
           w_gate, w_up, w_down, ln_f_g, tiles):
    q, k, v, gq, gk, gv, gr, lr = _in_proj(x2, ln1_g, w_in, tiles.in_proj)
    o_sb, o_gla = _mixers(q, k, v, gq, gk, gv, gr, lr, w2, gb, ng, B, S, tiles.mixers, tiles.query)
    x1, h2_tiles, route_i, route_f, counts = _out_proj(x2, o_sb, o_gla, w_out, ln2_g, w_group, b_group,
                                                       w_expert, b_expert, tiles.out_proj)
    T = x2.shape[0]
    dest, *tables = _route_tables(route_i, counts, TOP_K * T, tiles.expert)
    y_tiles = _expert_ffn(h2_tiles, dest, tables, w_gate, w_up, w_down, tiles.expert)
    return _combine(x1, route_f, y_tiles, dest, ln_f_g, tiles.combine)


def kernel(x, ln1_g, w_in, gla_gate_w2, gla_gate_b, gla_norm_g, w_out, ln2_g, w_group, b_group, w_expert, b_expert, exp_w_gate, exp_w_up, exp_w_down, ln_f_g):
    B, S, D = x.shape
    assert ln1_g.shape[0] == 1, "single-layer stack"
    assert D == SUBLANES * LANES, "one (SUBLANES, LANES) tile per token"
    out = _layer(x.reshape(B * S, D), B, S, ln1_g[0], w_in[0], gla_gate_w2[0], gla_gate_b[0],
                 gla_norm_g[0], w_out[0], ln2_g[0], w_group[0], b_group[0], w_expert[0], b_expert[0],
                 exp_w_gate[0], exp_w_up[0], exp_w_down[0], ln_f_g, _tiles(B * S, S))
    return out.reshape(B, S, D)
```
